```python
import math
import jax, jax.numpy as jnp
from jax import lax
import numpy as np

D_MODEL = 1024
BATCH = 32
SEQ = 256
DEPTH = 2
DEC_BATCH = 2
DEC_SEQ = 4096
PAST_LEN = 512

GRID_W = 64
N_MIXERS = 2
N_HYENA_LAYERS = (DEPTH + 1) // 2
N_NAT_LAYERS = DEPTH // 2
N_MOD = 6
RMS_EPS = 1e-6
NEG_INF = -1e30
HY_ORDER = 2
HY_SHORT = 3
HY_BANDS = 16
HY_EMB = 1 + 2 * HY_BANDS
HY_FFN = 64
HY_MIN_DECAY = math.log(1e-2) / 1.5
HY_MAX_DECAY = math.log(1e-2) / 0.3
N_HEADS = 16
HEAD_DIM = D_MODEL // N_HEADS
WIN_ROWS = 8
WIN_COLS = 16
Q_COL_BLOCK = 16
K_COL_BLOCK = Q_COL_BLOCK + WIN_COLS
CTX_Q_BLOCK = 128
N_EXPERTS = 32
TOP_K = 4
D_EXPERT = D_MODEL
SWIGLU_LIMIT = 7.0
SWIGLU_ALPHA = 1.702
MOE_BLOCK = 128

kernel_name = "hyena_natten_moe_diffusion_step"


def _rmsnorm(x, g):
    xf = x.astype(jnp.float32)
    y = xf * lax.rsqrt(jnp.mean(xf * xf, axis=-1, keepdims=True) + RMS_EPS)
    return (y * g.astype(jnp.float32)).astype(x.dtype)


def _modulation(cond, w, b, dtype):
    m = jax.nn.silu(cond.astype(jnp.float32)) @ w.astype(jnp.float32) + b.astype(jnp.float32)
    m = m.astype(dtype).reshape(cond.shape[0], 1, N_MOD, D_MODEL)
    return [m[:, :, i] for i in range(N_MOD)]


def _short_conv(z, w):
    L = z.shape[1]
    pad = HY_SHORT // 2
    zp = jnp.pad(z, ((0, 0), (pad, HY_SHORT - 1 - pad), (0, 0)))
    return sum(zp[:, j:j + L] * w[j] for j in range(HY_SHORT))


def _hyena_filters(L, w1, b1, freq, w2, b2, w3):
    f32 = jnp.float32
    t = jnp.linspace(0.0, 1.0, L, dtype=f32)[:, None]
    ang = (2.0 * math.pi / L) * jnp.arange(L, dtype=f32)[:, None]
    bands = jnp.linspace(1e-4, HY_BANDS - 1, HY_BANDS, dtype=f32)[None, :]
    z = jnp.concatenate([t, jnp.cos(bands * ang), -jnp.sin(bands * ang)], axis=-1)
    fr = freq.astype(f32)
    h = jnp.sin(fr * (z @ w1.astype(f32) + b1.astype(f32)))
    h = jnp.sin(fr * (h @ w2.astype(f32) + b2.astype(f32)))
    h = (h @ w3.astype(f32)).reshape(L, 2, HY_ORDER, D_MODEL)
    deltas = jnp.abs(jnp.linspace(HY_MIN_DECAY, HY_MAX_DECAY, D_MODEL, dtype=f32))
    h = h * jnp.exp(-t * deltas)[:, None, None, :]
    k = jnp.concatenate([h[:, 0], jnp.zeros((1, HY_ORDER, D_MODEL), f32), h[:0:-1, 1]], axis=0)
    return k / jnp.sum(jnp.abs(k), axis=0, keepdims=True)


def _hyena(h, w_in, w_short, f_w1, f_b1, f_freq, f_w2, f_b2, f_w3, skip, w_out):
    B, L, _ = h.shape
    z = _short_conv(h @ w_in, w_short)
    v, x1, x2 = jnp.split(z, 3, axis=-1)
    kf = jnp.fft.rfft(_hyena_filters(L, f_w1, f_b1, f_freq, f_w2, f_b2, f_w3), axis=0)
    y = v.astype(jnp.float32)
    for o, gate in enumerate((x1, x2)):
        yc = jnp.fft.irfft(jnp.fft.rfft(y, n=2 * L, axis=1) * kf[None, :, o], n=2 * L, axis=1)[:, :L]
        y = gate.astype(jnp.float32) * (yc + y * skip[o].astype(jnp.float32))
    return y.astype(h.dtype) @ w_out


def _ctx_attention(q, k, v):
    B, H, S, Dh = q.shape
    nb = S // CTX_Q_BLOCK
    scale = HEAD_DIM ** -0.5
    qb = jnp.moveaxis(q.reshape(B, H, nb, CTX_Q_BLOCK, Dh), 2, 0)

    def block(qi):
        s = jnp.einsum('bhqd,bhkd->bhqk', qi, k).astype(jnp.float32) * scale
        p = jax.nn.softmax(s, axis=-1).astype(v.dtype)
        return jnp.einsum('bhqk,bhkd->bhqd', p, v)

    o = lax.map(block, qb)
    return jnp.moveaxis(o, 0, 2).reshape(B, H, S, Dh)


def _nat_attention(q, k, v, k_ctx, v_ctx, rpb):
    B, H, L, Dh = q.shape
    rows = L // GRID_W
    wr = min(WIN_ROWS, rows)
    n_cb = GRID_W // Q_COL_BLOCK
    nloc = wr * K_COL_BLOCK
    scale = HEAD_DIM ** -0.5
    qc = np.arange(GRID_W).reshape(n_cb, Q_COL_BLOCK)
    q_start = np.clip(qc - WIN_COLS // 2, 0, GRID_W - WIN_COLS)
    k_start = np.clip(np.arange(n_cb) * Q_COL_BLOCK - WIN_COLS // 2, 0, GRID_W - K_COL_BLOCK)
    kc = k_start[:, None] + np.arange(K_COL_BLOCK)
    in_win = (kc[:, None, :] >= q_start[:, :, None]) & (kc[:, None, :] < q_start[:, :, None] + WIN_COLS)
    mask = jnp.asarray(np.broadcast_to(in_win[:, :, None, :], (n_cb, Q_COL_BLOCK, wr, K_COL_BLOCK)).reshape(n_cb, Q_COL_BLOCK, nloc))
    idx_c = np.clip(kc[:, None, :] - qc[:, :, None] + WIN_COLS - 1, 0, 2 * WIN_COLS - 2)
    qg = jnp.moveaxis(q.reshape(B, H, rows, GRID_W, Dh), 2, 0)
    kg = k.reshape(B, H, rows, GRID_W, Dh)
    vg = v.reshape(B, H, rows, GRID_W, Dh)

    def row_block(args):
        qr, r = args
        rs = jnp.clip(r - wr // 2, 0, rows - wr)
        kband = lax.dynamic_slice_in_dim(kg, rs, wr, axis=2)
        vband = lax.dynamic_slice_in_dim(vg, rs, wr, axis=2)
        kb = jnp.stack([kband[:, :, :, int(s):int(s) + K_COL_BLOCK] for s in k_start], axis=2).reshape(B, H, n_cb, nloc, Dh)
        vb = jnp.stack([vband[:, :, :, int(s):int(s) + K_COL_BLOCK] for s in k_start], axis=2).reshape(B, H, n_cb, nloc, Dh)
        idx_r = rs + jnp.arange(wr) - r + (WIN_ROWS - 1)
        bias = rpb[:, idx_r][:, :, idx_c]
        bias = jnp.transpose(bias, (0, 2, 3, 1, 4)).reshape(H, n_cb, Q_COL_BLOCK, nloc)
        qr = qr.reshape(B, H, n_cb, Q_COL_BLOCK, Dh)
        s_loc = jnp.einsum('bhjqd,bhjkd->bhjqk', qr, kb).astype(jnp.float32) * scale + bias.astype(jnp.float32)
        s_loc = jnp.where(mask, s_loc, NEG_INF)
        s_ctx = jnp.einsum('bhjqd,bhpd->bhjqp', qr, k_ctx).astype(jnp.float32) * scale
        p = jax.nn.softmax(jnp.concatenate([s_loc, s_ctx], axis=-1), axis=-1).astype(v.dtype)
        o = jnp.einsum('bhjqk,bhjkd->bhjqd', p[..., :nloc], vb) + jnp.einsum('bhjqp,bhpd->bhjqd', p[..., nloc:], v_ctx)
        return o.reshape(B, H, GRID_W, Dh)

    o = lax.map(row_block, (qg, jnp.arange(rows)))
    return jnp.moveaxis(o, 0, 2).reshape(B, H, L, Dh)


def _moe(x, w_r, b_r, w_g, b_g, w_u, b_u, w_d, b_d):
    B, L, D = x.shape
    N = B * L
    xt = x.reshape(N, D)
    logits = xt.astype(jnp.float32) @ w_r.astype(jnp.float32) + b_r.astype(jnp.float32)
    top_v, top_e = lax.top_k(logits, TOP_K)
    gates = jax.nn.softmax(top_v, axis=-1)
    NK = N * TOP_K
    CAP = NK + N_EXPERTS * MOE_BLOCK
    NB = CAP // MOE_BLOCK
    flat_e = top_e.reshape(NK).astype(jnp.int32)
    flat_tok = jnp.arange(NK, dtype=jnp.int32) // TOP_K
    flat_g = gates.reshape(NK)
    order = jnp.argsort(flat_e, stable=True)
    se = flat_e[order]
    counts = jnp.bincount(flat_e, length=N_EXPERTS)
    padded = (counts + MOE_BLOCK - 1) // MOE_BLOCK * MOE_BLOCK
    pad_end = jnp.cumsum(padded)
    pad_start = pad_end - padded
    start = jnp.cumsum(counts) - counts
    dest = pad_start[se] + jnp.arange(NK, dtype=jnp.int32) - start[se]
    slot_tok = jnp.zeros((CAP,), jnp.int32).at[dest].set(flat_tok[order])
    slot_g = jnp.zeros((CAP,), jnp.float32).at[dest].set(flat_g[order])
    block_e = jnp.minimum(jnp.searchsorted(pad_end, jnp.arange(NB, dtype=pad_end.dtype) * MOE_BLOCK, side='right'), N_EXPERTS - 1)
    xs = xt[slot_tok].reshape(NB, MOE_BLOCK, D)

    def expert_block(args):
        xb, e = args
        g = jnp.minimum(xb @ w_g[e] + b_g[e], SWIGLU_LIMIT)
        u = jnp.clip(xb @ w_u[e] + b_u[e], -SWIGLU_LIMIT, SWIGLU_LIMIT)
        return (g * jax.nn.sigmoid(SWIGLU_ALPHA * g) * (u + 1.0)) @ w_d[e] + b_d[e]

    ys = lax.map(expert_block, (xs, block_e)).reshape(CAP, D)
    out = jnp.zeros((N, D), x.dtype).at[slot_tok].add(ys * slot_g[:, None].astype(x.dtype))
    return out.reshape(B, L, D)


def _trunk(x, cond, ctx_k, ctx_v, p):
    B, L, _ = x.shape
    new_k, new_v = [], []
    for i in range(DEPTH):
        sh1, sc1, g1, sh2, sc2, g2 = _modulation(cond, p['ada_w'][i], p['ada_b'][i], x.dtype)
        h = _rmsnorm(x, p['norm1_g'][i]) * (1.0 + sc1) + sh1
        j = i // N_MIXERS
        if i % N_MIXERS == 0:
            out = _hyena(h, p['hy_w_in'][j], p['hy_w_short'][j], p['hy_f_w1'][j], p['hy_f_b1'][j], p['hy_f_freq'][j],
                         p['hy_f_w2'][j], p['hy_f_b2'][j], p['hy_f_w3'][j], p['hy_skip'][j], p['hy_w_out'][j])
        else:
            qkv = (h @ p['na_w_qkv'][j]).reshape(B, L, 3, N_HEADS, HEAD_DIM)
            q, k, v = jnp.transpose(qkv, (2, 0, 3, 1, 4))
            if ctx_k is None:
                o = _ctx_attention(q, k, v)
                new_k.append(k)
                new_v.append(v)
            else:
                o = _nat_attention(q, k, v, ctx_k[:, j], ctx_v[:, j], p['na_rpb'][j])
            out = jnp.transpose(o, (0, 2, 1, 3)).reshape(B, L, D_MODEL) @ p['na_w_o'][j]
        x = x + g1 * out
        h = _rmsnorm(x, p['norm2_g'][i]) * (1.0 + sc2) + sh2
        x = x + g2 * _moe(h, p['moe_w_router'][i], p['moe_b_router'][i], p['moe_w_gate'][i], p['moe_b_gate'][i],
                          p['moe_w_up'][i], p['moe_b_up'][i], p['moe_w_down'][i], p['moe_b_down'][i])
    return _rmsnorm(x, p['normf_g']), new_k, new_v


def setup_inputs(seed: int = 0) -> dict:
    key = jax.random.key(seed)
    ks = iter(jax.random.split(key, 48))

    def nrm(shape, scale):
        return scale * jax.random.normal(next(ks), shape, jnp.float32)

    D, F, E = D_MODEL, D_EXPERT, N_EXPERTS
    NH, NN = N_HYENA_LAYERS, N_NAT_LAYERS
    return {
        'x_prompt': nrm((BATCH, SEQ, D), 1.0),
        'x_sample': nrm((DEC_BATCH, DEC_SEQ, D), 1.0),
        'cache_k': nrm((DEC_BATCH, NN, N_HEADS, PAST_LEN, HEAD_DIM), 1.0),
        'cache_v': nrm((DEC_BATCH, NN, N_HEADS, PAST_LEN, HEAD_DIM), 1.0),
        'c': nrm((DEC_BATCH, D), 1.0),
        'c_ctx': nrm((D,), 1.0),
        'ada_w': nrm((DEPTH, D, N_MOD * D), 0.5 * D ** -0.5),
        'ada_b': nrm((DEPTH, N_MOD * D), 0.02),
        'norm1_g': 1.0 + nrm((DEPTH, D), 0.02),
        'norm2_g': 1.0 + nrm((DEPTH, D), 0.02),
        'normf_g': 1.0 + nrm((D,), 0.02),
        'hy_w_in': nrm((NH, D, 3 * D), D ** -0.5),
        'hy_w_short': nrm((NH, HY_SHORT, 3 * D), HY_SHORT ** -0.5),
        'hy_f_w1': nrm((NH, HY_EMB, HY_FFN), HY_EMB ** -0.5),
        'hy_f_b1': nrm((NH, HY_FFN), 0.1),
        'hy_f_freq': 1.0 + nrm((NH, HY_FFN), 0.1),
        'hy_f_w2': nrm((NH, HY_FFN, HY_FFN), HY_FFN ** -0.5),
        'hy_f_b2': nrm((NH, HY_FFN), 0.1),
        'hy_f_w3': nrm((NH, HY_FFN, 2 * HY_ORDER * D), HY_FFN ** -0.5),
        'hy_skip': nrm((NH, HY_ORDER, D), 0.5),
        'hy_w_out': nrm((NH, D, D), D ** -0.5),
        'na_w_qkv': nrm((NN, D, 3 * D), D ** -0.5),
        'na_rpb': nrm((NN, N_HEADS, 2 * WIN_ROWS - 1, 2 * WIN_COLS - 1), 0.1),
        'na_w_o': nrm((NN, D, D), D ** -0.5),
        'moe_w_router': nrm((DEPTH, D, E), D ** -0.5),
        'moe_b_router': nrm((DEPTH, E), 0.01),
        'moe_w_gate': nrm((DEPTH, E, D, F), D ** -0.5),
        'moe_b_gate': nrm((DEPTH, E, F), 0.01),
        'moe_w_up': nrm((DEPTH, E, D, F), D ** -0.5),
        'moe_b_up': nrm((DEPTH, E, F), 0.01),
        'moe_w_down': nrm((DEPTH, E, F, D), F ** -0.5),
        'moe_b_down': nrm((DEPTH, E, D), 0.01),
    }


def reference(x_prompt, x_sample, cache_k, cache_v, c, c_ctx, ada_w, ada_b, norm1_g, norm2_g, normf_g,
              hy_w_in, hy_w_short, hy_f_w1, hy_f_b1, hy_f_freq, hy_f_w2, hy_f_b2, hy_f_w3, hy_skip, hy_w_out,
              na_w_qkv, na_rpb, na_w_o, moe_w_router, moe_b_router, moe_w_gate, moe_b_gate, moe_w_up, moe_b_up,
              moe_w_down, moe_b_down):
    p = dict(ada_w=ada_w, ada_b=ada_b, norm1_g=norm1_g, norm2_g=norm2_g, normf_g=normf_g,
             hy_w_in=hy_w_in, hy_w_short=hy_w_short, hy_f_w1=hy_f_w1, hy_f_b1=hy_f_b1, hy_f_freq=hy_f_freq,
             hy_f_w2=hy_f_w2, hy_f_b2=hy_f_b2, hy_f_w3=hy_f_w3, hy_skip=hy_skip, hy_w_out=hy_w_out,
             na_w_qkv=na_w_qkv, na_rpb=na_rpb, na_w_o=na_w_o,
             moe_w_router=moe_w_router, moe_b_router=moe_b_router, moe_w_gate=moe_w_gate, moe_b_gate=moe_b_gate,
             moe_w_up=moe_w_up, moe_b_up=moe_b_up, moe_w_down=moe_w_down, moe_b_down=moe_b_down)
    y_prompt, ks, vs = _trunk(x_prompt, c_ctx[None, :], None, None, p)
    y_sample, _, _ = _trunk(x_sample, c, cache_k, cache_v, p)
    new_cache_k = jnp.stack(ks, axis=1)
    new_cache_v = jnp.stack(vs, axis=1)
    return (y_prompt, y_sample, new_cache_k, new_cache_v)
```

```python
import functools
import math

import numpy as np
import jax
import jax.numpy as jnp
from jax import lax
from jax.experimental import pallas as pl
from jax.experimental.pallas import tpu as pltpu

F32 = jnp.float32
BF16 = jnp.bfloat16

D_MODEL = 1024
N_MOD = 6
RMS_EPS = 1e-6
N_EXPERTS = 32
TOP_K = 4
SWIGLU_LIMIT = 7.0
SWIGLU_ALPHA = 1.702

N_COND = 8
ROW_TILE = 256
MOE_TILE = 256
COMBINE_TILE = 128
V7X_VMEM_LIMIT = 56 * 1024 * 1024


def _cparams(*sem, vmem=V7X_VMEM_LIMIT):
    return pltpu.CompilerParams(dimension_semantics=sem, vmem_limit_bytes=vmem)


def _dot(a, b):
    return jnp.dot(a, b, preferred_element_type=F32)


def _split_bf16(x):
    hi = x.astype(BF16)
    lo = (x - hi.astype(F32)).astype(BF16)
    return hi, lo


def _dot3(a, b):
    ah, al = _split_bf16(a)
    bh, bl = _split_bf16(b)
    return _dot(ah, bh) + (_dot(al, bh) + _dot(ah, bl))


def _seg_of_tile(i, tile, n_ctx, smp_len):
    ctx_tiles = n_ctx // tile
    per_smp = smp_len // tile
    return jnp.where(i < ctx_tiles, 0, 1 + (i - ctx_tiles) // per_smp)


def _norm_mod(x, g, sc, sh):
    y = x * lax.rsqrt(jnp.mean(x * x, axis=-1, keepdims=True) + RMS_EPS)
    return (y * g) * (1.0 + sc) + sh


def _mod_kernel(c_ref, w_ref, b_ref, o_ref):
    c = c_ref[...]
    a = c * jax.nn.sigmoid(c)
    o_ref[0] = _dot3(a, w_ref[0]) + b_ref[0]


def _modulation(cond, ada_w, ada_b):
    depth, d, n_out = ada_w.shape
    tn = 1536
    m = pl.pallas_call(
        _mod_kernel,
        grid=(depth, n_out // tn),
        in_specs=[
            pl.BlockSpec((N_COND, d), lambda l, j: (0, 0)),
            pl.BlockSpec((1, d, tn), lambda l, j: (l, 0, j)),
            pl.BlockSpec((1, 1, tn), lambda l, j: (l, 0, j)),
        ],
        out_specs=pl.BlockSpec((1, N_COND, tn), lambda l, j: (l, 0, j)),
        out_shape=jax.ShapeDtypeStruct((depth, N_COND, n_out), F32),
        compiler_params=_cparams("arbitrary", "arbitrary"),
        name="adaln_modulation",
    )(cond, ada_w, ada_b.reshape(depth, 1, n_out))
    m = m.reshape(depth, N_COND, N_MOD, d)
    return jnp.transpose(m, (0, 2, 1, 3)).reshape(depth, N_MOD * N_COND, 1, d)


def _mod_spec(which, tile, n_ctx, smp_len, first_tile=0):
    return pl.BlockSpec(
        (1, 1, D_MODEL),
        lambda i, *_: (which * N_COND + _seg_of_tile(i + first_tile, tile, n_ctx, smp_len), 0, 0))


def _norm_proj_kernel(x_ref, g_ref, sc_ref, sh_ref, w_ref, o_ref):
    h = _norm_mod(x_ref[...], g_ref[...], sc_ref[0], sh_ref[0])
    o_ref[...] = _dot(h.astype(BF16), w_ref[...]).astype(o_ref.dtype)


def _norm_proj(x, g, mods, w_bf16, seg, out_dtype=F32):
    n, d = x.shape
    n_out = w_bf16.shape[1]
    n_ctx, smp_len = seg
    return pl.pallas_call(
        _norm_proj_kernel,
        grid=(n // ROW_TILE,),
        in_specs=[
            pl.BlockSpec((ROW_TILE, d), lambda i: (i, 0)),
            pl.BlockSpec((1, d), lambda i: (0, 0)),
            _mod_spec(1, ROW_TILE, n_ctx, smp_len),
            _mod_spec(0, ROW_TILE, n_ctx, smp_len),
            pl.BlockSpec((d, n_out), lambda i: (0, 0)),
        ],
        out_specs=pl.BlockSpec((ROW_TILE, n_out), lambda i: (i, 0)),
        out_shape=jax.ShapeDtypeStruct((n, n_out), out_dtype),
        compiler_params=_cparams("parallel"),
        name="norm_proj",
    )(x, g.reshape(1, d), mods, mods, w_bf16)


def _out_proj_kernel(y_ref, w_ref, x_ref, gate_ref, o_ref):
    o_ref[...] = x_ref[...] + gate_ref[0] * _dot(y_ref[...], w_ref[...])


def _out_proj(y_bf16, w_bf16, x, mods, seg):
    n, d = x.shape
    n_ctx, smp_len = seg
    return pl.pallas_call(
        _out_proj_kernel,
        grid=(n // ROW_TILE,),
        in_specs=[
            pl.BlockSpec((ROW_TILE, d), lambda i: (i, 0)),
            pl.BlockSpec((d, d), lambda i: (0, 0)),
            pl.BlockSpec((ROW_TILE, d), lambda i: (i, 0)),
            _mod_spec(2, ROW_TILE, n_ctx, smp_len),
        ],
        out_specs=pl.BlockSpec((ROW_TILE, d), lambda i: (i, 0)),
        out_shape=jax.ShapeDtypeStruct((n, d), F32),
        compiler_params=_cparams("parallel"),
        name="out_proj",
    )(y_bf16, w_bf16, x, mods)


def _columns(cols):
    t = cols[0].shape[0]
    lane = lax.broadcasted_iota(jnp.int32, (t, len(cols)), 1)
    out = jnp.broadcast_to(cols[-1], (t, len(cols)))
    for k in range(len(cols) - 2, -1, -1):
        out = jnp.where(lane == k, cols[k], out)
    return out


def _router_kernel(x_ref, g_ref, sc_ref, sh_ref, wr_ref, br_ref, tri_ref,
                   e_ref, gate_ref, rank_ref, cnt_ref, run_ref):
    i = pl.program_id(0)

    @pl.when(i == 0)
    def _():
        run_ref[...] = jnp.zeros_like(run_ref)

    h = _norm_mod(x_ref[...], g_ref[...], sc_ref[0], sh_ref[0])
    logits = _dot3(h, wr_ref[...]) + br_ref[...]
    lane = lax.broadcasted_iota(jnp.int32, logits.shape, 1)
    work = logits
    vals, idxs, hots = [], [], []
    for _ in range(TOP_K):
        m = jnp.max(work, axis=-1, keepdims=True)
        idx = jnp.min(jnp.where(work == m, lane, N_EXPERTS), axis=-1, keepdims=True)
        hot = lane == idx
        vals.append(m)
        idxs.append(idx)
        hots.append(hot)
        work = jnp.where(hot, -jnp.inf, work)
    ex = [jnp.exp(v - vals[0]) for v in vals]
    denom = ex[0] + ex[1] + ex[2] + ex[3]
    gate_ref[...] = _columns([e / denom for e in ex])
    e_ref[...] = _columns(idxs)

    chosen = (hots[0] | hots[1] | hots[2] | hots[3]).astype(F32)
    before = run_ref[...] + _dot(tri_ref[...], chosen.astype(BF16))
    ranks = [jnp.sum(jnp.where(hot, before, 0.0), axis=-1, keepdims=True) for hot in hots]
    rank_ref[...] = _columns(ranks).astype(jnp.int32)
    run_ref[...] += jnp.sum(chosen, axis=0, keepdims=True)
    cnt_ref[...] = run_ref[...].astype(jnp.int32)


def _router(x, g, mods, w_r, b_r, seg):
    n, d = x.shape
    n_ctx, smp_len = seg
    t = ROW_TILE
    tri = jnp.asarray(np.tril(np.ones((t, t), np.float32), -1), BF16)
    tok4 = lambda i: (i, 0)
    return pl.pallas_call(
        _router_kernel,
        grid=(n // t,),
        in_specs=[
            pl.BlockSpec((t, d), lambda i: (i, 0)),
            pl.BlockSpec((1, d), lambda i: (0, 0)),
            _mod_spec(4, t, n_ctx, smp_len),
            _mod_spec(3, t, n_ctx, smp_len),
            pl.BlockSpec((d, N_EXPERTS), lambda i: (0, 0)),
            pl.BlockSpec((1, N_EXPERTS), lambda i: (0, 0)),
            pl.BlockSpec((t, t), lambda i: (0, 0)),
        ],
        out_specs=[
            pl.BlockSpec((t, TOP_K), tok4),
            pl.BlockSpec((t, TOP_K), tok4),
            pl.BlockSpec((t, TOP_K), tok4),
            pl.BlockSpec((1, N_EXPERTS), lambda i: (0, 0)),
        ],
        out_shape=[
            jax.ShapeDtypeStruct((n, TOP_K), jnp.int32),
            jax.ShapeDtypeStruct((n, TOP_K), F32),
            jax.ShapeDtypeStruct((n, TOP_K), jnp.int32),
            jax.ShapeDtypeStruct((1, N_EXPERTS), jnp.int32),
        ],
        scratch_shapes=[pltpu.VMEM((1, N_EXPERTS), F32)],
        compiler_params=_cparams("arbitrary"),
        name="moe_router",
    )(x, g.reshape(1, d), mods, mods, w_r, b_r.reshape(1, N_EXPERTS), tri)


def _row_copy(src_ref, src_row, dst_ref, dst_row, sem):
    return pltpu.make_async_copy(src_ref.at[pl.ds(src_row, 1), :],
                                 dst_ref.at[pl.ds(dst_row, 1), :], sem)


def _dispatch_kernel(fill_start_ref, fill_len_ref, na_ref, dest_ref, x_ref, g_ref, sc_ref, sh_ref,
                     xs_ref, h_ref, zero_ref, sem):
    i = pl.program_id(0)
    t = x_ref.shape[0]
    h_ref[...] = _norm_mod(x_ref[...], g_ref[...], sc_ref[0], sh_ref[0])

    def issue(r, carry):
        for k in range(TOP_K):
            _row_copy(h_ref, r, xs_ref, dest_ref[r * TOP_K + k], sem).start()
        return carry

    lax.fori_loop(0, t, issue, 0)

    def drain(r, carry):
        for k in range(TOP_K):
            _row_copy(h_ref, 0, xs_ref, 0, sem).wait()
        return carry

    lax.fori_loop(0, t, drain, 0)

    @pl.when(i == pl.num_programs(0) - 1)
    def _():
        zero_ref[...] = jnp.zeros_like(zero_ref)
        for e in range(N_EXPERTS):
            start = fill_start_ref[e]
            count = fill_len_ref[e]

            def fill(r, carry):
                _row_copy(zero_ref, 0, xs_ref, start + r, sem).start()
                return carry

            lax.fori_loop(0, count, fill, 0)

            def fill_drain(r, carry):
                _row_copy(zero_ref, 0, xs_ref, 0, sem).wait()
                return carry

            lax.fori_loop(0, count, fill_drain, 0)

        def block_copy(blk):
            return pltpu.make_async_copy(
                zero_ref, xs_ref.at[pl.ds(pl.multiple_of(blk * MOE_TILE, MOE_TILE), MOE_TILE), :], sem)

        n_blocks = xs_ref.shape[0] // MOE_TILE

        def tail(blk, carry):
            block_copy(blk).start()
            return carry

        lax.fori_loop(na_ref[0], n_blocks, tail, 0)

        def tail_drain(blk, carry):
            block_copy(0).wait()
            return carry

        lax.fori_loop(na_ref[0], n_blocks, tail_drain, 0)


def _dispatch(x, g, mods, dest_flat, fill_start, fill_len, n_active, cap, seg):
    n, d = x.shape
    n_ctx, smp_len = seg
    t = ROW_TILE
    grid_spec = pltpu.PrefetchScalarGridSpec(
        num_scalar_prefetch=3,
        grid=(n // t,),
        in_specs=[
            pl.BlockSpec((t * TOP_K,), lambda i, *_: (i,), memory_space=pltpu.SMEM),
            pl.BlockSpec((t, d), lambda i, *_: (i, 0)),
            pl.BlockSpec((1, d), lambda i, *_: (0, 0)),
            _mod_spec(4, t, n_ctx, smp_len),
            _mod_spec(3, t, n_ctx, smp_len),
        ],
        out_specs=pl.BlockSpec(memory_space=pl.ANY),
        scratch_shapes=[pltpu.VMEM((t, d), F32), pltpu.VMEM((MOE_TILE, d), F32), pltpu.SemaphoreType.DMA(())],
    )
    return pl.pallas_call(
        _dispatch_kernel,
        grid_spec=grid_spec,
        out_shape=jax.ShapeDtypeStruct((cap, d), F32),
        compiler_params=_cparams("arbitrary"),
        name="moe_dispatch",
    )(fill_start, fill_len, n_active, dest_flat, x, g.reshape(1, d), mods, mods)


def _ffn_kernel(be_ref, na_ref, xs_ref, wg_ref, bg_ref, wu_ref, bu_ref, wd_ref, bd_ref,
                ys_ref, wg_s, wu_s, wd_s):
    i = pl.program_id(0)
    fresh = jnp.logical_or(i == 0, be_ref[i] != be_ref[jnp.maximum(i - 1, 0)])

    @pl.when(fresh)
    def _():
        wg_s[...] = wg_ref[0].astype(BF16)
        wu_s[...] = wu_ref[0].astype(BF16)
        wd_s[...] = wd_ref[0].astype(BF16)

    @pl.when(i < na_ref[0])
    def _():
        x = xs_ref[...].astype(BF16)
        g = jnp.minimum(_dot(x, wg_s[...]) + bg_ref[0], SWIGLU_LIMIT)
        u = jnp.clip(_dot(x, wu_s[...]) + bu_ref[0], -SWIGLU_LIMIT, SWIGLU_LIMIT)
        a = g * jax.nn.sigmoid(SWIGLU_ALPHA * g) * (u + 1.0)
        ys_ref[...] = _dot(a.astype(BF16), wd_s[...]) + bd_ref[0]

    @pl.when(i >= na_ref[0])
    def _():
        ys_ref[...] = jnp.zeros_like(ys_ref)


def _expert_ffn(xs, block_e, n_active, w_g, b_g, w_u, b_u, w_d, b_d):
    cap, d = xs.shape
    f = w_g.shape[2]
    nb = cap // MOE_TILE
    wmap = lambda i, be, na: (be[i], 0, 0)
    grid_spec = pltpu.PrefetchScalarGridSpec(
        num_scalar_prefetch=2,
        grid=(nb,),
        in_specs=[
            pl.BlockSpec((MOE_TILE, d), lambda i, be, na: (jnp.minimum(i, na[0] - 1), 0)),
            pl.BlockSpec((1, d, f), wmap),
            pl.BlockSpec((1, 1, f), wmap),
            pl.BlockSpec((1, d, f), wmap),
            pl.BlockSpec((1, 1, f), wmap),
            pl.BlockSpec((1, f, d), wmap),
            pl.BlockSpec((1, 1, d), wmap),
        ],
        out_specs=pl.BlockSpec((MOE_TILE, d), lambda i, be, na: (i, 0)),
        scratch_shapes=[pltpu.VMEM((d, f), BF16), pltpu.VMEM((d, f), BF16), pltpu.VMEM((f, d), BF16)],
    )
    ne = w_g.shape[0]
    return pl.pallas_call(
        _ffn_kernel,
        grid_spec=grid_spec,
        out_shape=jax.ShapeDtypeStruct((cap, d), F32),
        compiler_params=_cparams("arbitrary"),
        name="moe_expert_ffn",
    )(block_e, n_active, xs, w_g, b_g.reshape(ne, 1, f), w_u, b_u.reshape(ne, 1, f),
      w_d, b_d.reshape(ne, 1, d))


def _combine_kernel(dest_ref, x_ref, gate_ref, g2_ref, gf_ref, ys_ref, o_ref, buf_ref, sem, *, final_norm):
    t = x_ref.shape[0]

    def issue(r, carry):
        for k in range(TOP_K):
            _row_copy(ys_ref, dest_ref[r * TOP_K + k], buf_ref.at[k], r, sem).start()
        return carry

    lax.fori_loop(0, t, issue, 0)

    def drain(r, carry):
        for k in range(TOP_K):
            _row_copy(ys_ref, 0, buf_ref.at[0], 0, sem).wait()
        return carry

    lax.fori_loop(0, t, drain, 0)

    gates = gate_ref[...]
    acc = gates[:, 0:1] * buf_ref[0]
    for k in range(1, TOP_K):
        acc = acc + gates[:, k:k + 1] * buf_ref[k]
    y = x_ref[...] + g2_ref[0] * acc
    if final_norm:
        y = y * lax.rsqrt(jnp.mean(y * y, axis=-1, keepdims=True) + RMS_EPS) * gf_ref[...]
    o_ref[...] = y


def _combine(x, ys, dest_flat, gates, mods, normf_g, seg, final_norm):
    n, d = x.shape
    n_ctx, smp_len = seg
    t = COMBINE_TILE
    return pl.pallas_call(
        functools.partial(_combine_kernel, final_norm=final_norm),
        grid=(n // t,),
        in_specs=[
            pl.BlockSpec((t * TOP_K,), lambda i: (i,), memory_space=pltpu.SMEM),
            pl.BlockSpec((t, d), lambda i: (i, 0)),
            pl.BlockSpec((t, TOP_K), lambda i: (i, 0)),
            _mod_spec(5, t, n_ctx, smp_len),
            pl.BlockSpec((1, d), lambda i: (0, 0)),
            pl.BlockSpec(memory_space=pl.ANY),
        ],
        out_specs=pl.BlockSpec((t, d), lambda i: (i, 0)),
        out_shape=jax.ShapeDtypeStruct((n, d), F32),
        scratch_shapes=[pltpu.VMEM((TOP_K, t, d), F32), pltpu.SemaphoreType.DMA(())],
        compiler_params=_cparams("arbitrary"),
        name="moe_combine",
    )(dest_flat, x, gates, mods, normf_g.reshape(1, d), ys)


def _moe_layer(x, norm_g, mods, w_r, b_r, w_g, b_g, w_u, b_u, w_d, b_d, normf_g, seg, final_norm):
    n, d = x.shape
    top_e, gates, rank, counts = _router(x, norm_g, mods, w_r, b_r, seg)
    counts = counts[0]
    padded = (counts + MOE_TILE - 1) // MOE_TILE * MOE_TILE
    pad_end = jnp.cumsum(padded)
    pad_start = pad_end - padded
    cap = n * TOP_K + N_EXPERTS * MOE_TILE
    nb = cap // MOE_TILE
    experts = jnp.arange(N_EXPERTS, dtype=jnp.int32)
    start_of = jnp.sum(jnp.where(top_e[..., None] == experts, pad_start, 0), axis=-1)
    dest = (start_of + rank).reshape(n * TOP_K).astype(jnp.int32)
    blk_start = jnp.arange(nb, dtype=jnp.int32) * MOE_TILE
    block_e = jnp.minimum(jnp.sum(blk_start[:, None] >= pad_end[None, :], axis=1), N_EXPERTS - 1).astype(jnp.int32)
    n_active = (pad_end[-1] // MOE_TILE).astype(jnp.int32).reshape(1)
    xs = _dispatch(x, norm_g, mods, dest, (pad_start + counts).astype(jnp.int32),
                   (padded - counts).astype(jnp.int32), n_active, cap, seg)
    ys = _expert_ffn(xs, block_e, n_active, w_g, b_g, w_u, b_u, w_d, b_d)
    return _combine(x, ys, dest, gates, mods, normf_g, seg, final_norm)


HY_ORDER = 2
HY_BANDS = 16
HY_EMB = 1 + 2 * HY_BANDS
HY_FFN = 64
HY_MIN_DECAY = math.log(1e-2) / 1.5
HY_MAX_DECAY = math.log(1e-2) / 0.3
HY_EMB_PAD = 64


def _filter_mlp_kernel(z_ref, t_ref, w1_ref, b1_ref, fr_ref, w2_ref, b2_ref, w3_ref, dl_ref, o_ref):
    fr = fr_ref[...]
    h = jnp.sin(fr * (_dot3(z_ref[...], w1_ref[...]) + b1_ref[...]))
    h = jnp.sin(fr * (_dot3(h, w2_ref[...]) + b2_ref[...]))
    o_ref[...] = _dot3(h, w3_ref[...]) * jnp.exp(-t_ref[...] * dl_ref[...])


def _hyena_filters(length, w1, b1, freq, w2, b2, w3):
    t = jnp.linspace(0.0, 1.0, length, dtype=F32)[:, None]
    ang = (2.0 * math.pi / length) * jnp.arange(length, dtype=F32)[:, None]
    bands = jnp.linspace(1e-4, HY_BANDS - 1, HY_BANDS, dtype=F32)[None, :]
    z = jnp.concatenate([t, jnp.cos(bands * ang), -jnp.sin(bands * ang)], axis=-1)
    z = jnp.pad(z, ((0, 0), (0, HY_EMB_PAD - HY_EMB)))
    w1p = jnp.pad(w1, ((0, HY_EMB_PAD - HY_EMB), (0, 0)))
    n_out = w3.shape[1]
    deltas = jnp.abs(jnp.linspace(HY_MIN_DECAY, HY_MAX_DECAY, D_MODEL, dtype=F32))
    deltas = jnp.tile(deltas, n_out // D_MODEL)[None, :]
    tl, tn = 256, 1024
    row = lambda i, j: (i, 0)
    fixed = lambda i, j: (0, 0)
    return pl.pallas_call(
        _filter_mlp_kernel,
        grid=(length // tl, n_out // tn),
        in_specs=[
            pl.BlockSpec((tl, HY_EMB_PAD), row),
            pl.BlockSpec((tl, 1), row),
            pl.BlockSpec((HY_EMB_PAD, HY_FFN), fixed),
            pl.BlockSpec((1, HY_FFN), fixed),
            pl.BlockSpec((1, HY_FFN), fixed),
            pl.BlockSpec((HY_FFN, HY_FFN), fixed),
            pl.BlockSpec((1, HY_FFN), fixed),
            pl.BlockSpec((HY_FFN, tn), lambda i, j: (0, j)),
            pl.BlockSpec((1, tn), lambda i, j: (0, j)),
        ],
        out_specs=pl.BlockSpec((tl, tn), lambda i, j: (i, j)),
        out_shape=jax.ShapeDtypeStruct((length, n_out), F32),
        compiler_params=_cparams("parallel", "parallel"),
        name="hyena_filter_mlp",
    )(z, t, w1p, b1.reshape(1, -1), freq.reshape(1, -1), w2, b2.reshape(1, -1), w3, deltas)


def _short_conv(z, w):
    length = z.shape[0]
    row = lax.broadcasted_iota(jnp.int32, z.shape, 0)
    prev = jnp.where(row == 0, 0.0, pltpu.roll(z, 1, 0))
    nxt = jnp.where(row == length - 1, 0.0, pltpu.roll(z, length - 1, 0))
    return (prev * w[0:1] + z * w[1:2]) + nxt * w[2:3]


def _filter_halves(hf_ref, hb_ref):
    hf = hf_ref[...]
    hb = hb_ref[...]
    hb = jnp.where(lax.broadcasted_iota(jnp.int32, hb.shape, 0) == 0, 0.0, hb)
    norm = jnp.sum(jnp.abs(hf), axis=0, keepdims=True) + jnp.sum(jnp.abs(hb), axis=0, keepdims=True)
    return hf + hb, hf - hb, 1.0 / norm


def _direct_dft_tables(length):
    n_fft = 2 * length
    n_freq = length + 1
    mf = -(-n_freq // 16) * 16
    k = np.arange(mf)[:, None]
    n = np.arange(length)[None, :]
    ang = 2.0 * np.pi * ((k * n) % n_fft) / n_fft
    valid = k < n_freq
    cos = np.where(valid, np.cos(ang), 0.0)
    msin = np.where(valid, -np.sin(ang), 0.0)
    weight = np.where((k == 0) | (k == length), 1.0, 2.0) * valid / n_fft
    fwd = np.concatenate([cos, msin], axis=0)
    inv = np.concatenate([weight * cos, weight * msin], axis=0).T
    return jnp.asarray(fwd, BF16), jnp.asarray(inv, BF16), mf


def _hyena_direct_kernel(zv_ref, z1_ref, z2_ref, wv_ref, w1_ref, w2_ref,
                         hf0_ref, hf1_ref, hb0_ref, hb1_ref, skip_ref, fw_ref, iv_ref,
                         o_ref, kr_ref, ki_ref):
    mf = fw_ref.shape[0] // 2
    dt = o_ref.shape[-1]

    @pl.when(pl.program_id(1) == 0)
    def _():
        for o, (hf_ref, hb_ref) in enumerate(((hf0_ref, hb0_ref), (hf1_ref, hb1_ref))):
            hs, hd, inv_norm = _filter_halves(hf_ref, hb_ref)
            spec = _dot(fw_ref[...], jnp.concatenate([hs, hd], axis=1).astype(BF16))
            kr_ref[o] = spec[:mf, :dt] * inv_norm
            ki_ref[o] = spec[mf:, dt:] * inv_norm

    y = _short_conv(zv_ref[0], wv_ref[...])
    for o, (z_ref, w_ref) in enumerate(((z1_ref, w1_ref), (z2_ref, w2_ref))):
        spec = _dot(fw_ref[...], y.astype(BF16))
        yr, yi = spec[:mf], spec[mf:]
        kr, ki = kr_ref[o], ki_ref[o]
        prod = jnp.concatenate([yr * kr - yi * ki, yr * ki + yi * kr], axis=0)
        yc = _dot(iv_ref[...], prod.astype(BF16))
        y = _short_conv(z_ref[0], w_ref[...]) * (yc + y * skip_ref[o:o + 1])
    o_ref[0] = y.astype(o_ref.dtype)


def _hyena_direct(z, w_short, hfilt, skip, first=0, bsz=None):
    _, length, d3 = z.shape
    bsz = z.shape[0] if bsz is None else bsz
    d = d3 // 3
    dt = 256
    nct = d // dt
    fwd, inv, mf = _direct_dft_tables(length)
    zspec = lambda part: pl.BlockSpec((1, length, dt), lambda c, b: (b + first, 0, part * nct + c))
    wspec = lambda part: pl.BlockSpec((3, dt), lambda c, b: (0, part * nct + c))
    hspec = lambda direction, order: pl.BlockSpec(
        (length, dt), lambda c, b: (0, (direction * HY_ORDER + order) * nct + c))
    fixed = lambda c, b: (0, 0)
    return pl.pallas_call(
        _hyena_direct_kernel,
        grid=(nct, bsz),
        in_specs=[zspec(0), zspec(1), zspec(2), wspec(0), wspec(1), wspec(2),
                  hspec(0, 0), hspec(0, 1), hspec(1, 0), hspec(1, 1),
                  pl.BlockSpec((HY_ORDER, dt), lambda c, b: (0, c)),
                  pl.BlockSpec(fwd.shape, fixed), pl.BlockSpec(inv.shape, fixed)],
        out_specs=pl.BlockSpec((1, length, dt), lambda c, b: (b, 0, c)),
        out_shape=jax.ShapeDtypeStruct((bsz, length, d), BF16),
        scratch_shapes=[pltpu.VMEM((HY_ORDER, mf, dt), F32), pltpu.VMEM((HY_ORDER, mf, dt), F32)],
        compiler_params=_cparams("parallel", "arbitrary"),
        name="hyena_conv_direct",
    )(z, z, z, w_short, w_short, w_short, hfilt, hfilt, hfilt, hfilt, skip, fwd, inv)


FFT_N1 = 64
FFT_N2 = 128
FFT_LANES = 128
FFT_A_PITCH = 2 * FFT_N2 + 8
FFT_U_PITCH = 2 * FFT_N1 + 8
FFT_GROUP = 2


def _two_stage_tables():
    n_fft = FFT_N1 * FFT_N2
    half = FFT_N2 // 2
    n1 = np.arange(FFT_N1)[:, None, None]
    k2 = np.arange(FFT_N2)[None, :, None]
    n2 = np.arange(half)[None, None, :]
    ang = 2.0 * np.pi * ((k2 * (n1 + FFT_N1 * n2)) % n_fft) / n_fft
    stage_a = np.concatenate([np.cos(ang), -np.sin(ang)], axis=1)
    stage_a_inv = np.transpose(stage_a, (0, 2, 1)) / n_fft
    k1 = np.arange(FFT_N1)[:, None]
    m1 = np.arange(FFT_N1)[None, :]
    phi = 2.0 * np.pi * ((k1 * m1) % FFT_N1) / FFT_N1
    c, s = np.cos(phi), np.sin(phi)
    stage_b = np.block([[c, s], [-s, c]])
    stage_b_inv = np.block([[c, -s], [s, c]])
    return tuple(jnp.asarray(t, BF16) for t in (stage_a, stage_a_inv, stage_b, stage_b_inv))


def _fft_stage_a(y_ref, ma_ref, a_ref):
    half = FFT_N2 // 2

    def body(n1, carry):
        slab = y_ref[pl.ds(n1, half, stride=FFT_N1), :]
        a_ref[pl.ds(pl.multiple_of(n1 * FFT_A_PITCH, 8), 2 * FFT_N2), :] = _dot(ma_ref[n1], slab.astype(BF16))
        return carry

    lax.fori_loop(0, FFT_N1, body, 0)


def _fft_stage_b(a_ref, mb_ref, consume):
    def body(j, carry):
        k2 = j * FFT_GROUP
        cols = []
        for g in range(FFT_GROUP):
            re = a_ref[pl.ds(k2 + g, FFT_N1, stride=FFT_A_PITCH), :]
            im = a_ref[pl.ds(FFT_N2 + k2 + g, FFT_N1, stride=FFT_A_PITCH), :]
            cols.append(jnp.concatenate([re, im], axis=0))
        x = _dot(mb_ref[...], jnp.concatenate(cols, axis=1).astype(BF16))
        for g in range(FFT_GROUP):
            consume(k2 + g, x[:, g * FFT_LANES:(g + 1) * FFT_LANES])
        return carry

    lax.fori_loop(0, FFT_N2 // FFT_GROUP, body, 0)


def _fft_inverse(z_ref, mbi_ref, mai_ref, u_ref, out_ref):
    half = FFT_N2 // 2

    def stage_b(j, carry):
        k2 = j * FFT_GROUP
        rhs = jnp.concatenate([z_ref[k2 + g] for g in range(FFT_GROUP)], axis=1)
        u = _dot(mbi_ref[...], rhs)
        for g in range(FFT_GROUP):
            u_ref[pl.ds(pl.multiple_of((k2 + g) * FFT_U_PITCH, 8), 2 * FFT_N1), :] = (
                u[:, g * FFT_LANES:(g + 1) * FFT_LANES])
        return carry

    lax.fori_loop(0, FFT_N2 // FFT_GROUP, stage_b, 0)

    def stage_a(n1, carry):
        re = u_ref[pl.ds(n1, FFT_N2, stride=FFT_U_PITCH), :]
        im = u_ref[pl.ds(FFT_N1 + n1, FFT_N2, stride=FFT_U_PITCH), :]
        rhs = jnp.concatenate([re, im], axis=0).astype(BF16)
        out_ref[pl.ds(n1, half, stride=FFT_N1), :] = _dot(mai_ref[n1], rhs)
        return carry

    lax.fori_loop(0, FFT_N1, stage_a, 0)


FFT_WORK_ROWS = max(FFT_N1 * FFT_A_PITCH, FFT_N2 * FFT_U_PITCH)


def _hyena_spectrum_kernel(hf0_ref, hf1_ref, hb0_ref, hb1_ref, ma_ref, mb_ref, k_ref, y_ref, a_ref):
    for o, (hf_ref, hb_ref) in enumerate(((hf0_ref, hb0_ref), (hf1_ref, hb1_ref))):
        hs, hd, inv_norm = _filter_halves(hf_ref, hb_ref)

        def keep_real(k2, x):
            k_ref[o, k2, 0:FFT_N1, :] = (x[:FFT_N1] * inv_norm).astype(k_ref.dtype)

        def keep_imag(k2, x):
            k_ref[o, k2, FFT_N1:, :] = (x[FFT_N1:] * inv_norm).astype(k_ref.dtype)

        for part, keep in ((hs, keep_real), (hd, keep_imag)):
            y_ref[...] = part
            _fft_stage_a(y_ref, ma_ref, a_ref)
            _fft_stage_b(a_ref, mb_ref, keep)


def _hyena_two_stage_kernel(zv_ref, z1_ref, z2_ref, wv_ref, w1_ref, w2_ref, k_ref, skip_ref,
                            ma_ref, mai_ref, mb_ref, mbi_ref, o_ref, y_ref, c_ref, a_ref, z_ref):
    y_ref[...] = _short_conv(zv_ref[0], wv_ref[...])
    for o, (g_ref, w_ref) in enumerate(((z1_ref, w1_ref), (z2_ref, w2_ref))):
        def multiply(k2, x):
            k = k_ref[o, k2].astype(F32)
            xr, xi = x[:FFT_N1], x[FFT_N1:]
            kr, ki = k[:FFT_N1], k[FFT_N1:]
            z_ref[k2] = jnp.concatenate([xr * kr - xi * ki, xr * ki + xi * kr], axis=0).astype(z_ref.dtype)

        _fft_stage_a(y_ref, ma_ref, a_ref)
        _fft_stage_b(a_ref, mb_ref, multiply)
        _fft_inverse(z_ref, mbi_ref, mai_ref, a_ref, c_ref)
        y = _short_conv(g_ref[0], w_ref[...]) * (c_ref[...] + y_ref[...] * skip_ref[o:o + 1])
        if o + 1 < HY_ORDER:
            y_ref[...] = y
        else:
            o_ref[0] = y.astype(o_ref.dtype)


def _hyena_two_stage(z, w_short, hfilt, skip, first=0, bsz=None):
    _, length, d3 = z.shape
    bsz = z.shape[0] if bsz is None else bsz
    d = d3 // 3
    dt = FFT_LANES
    nct = d // dt
    assert 2 * length == FFT_N1 * FFT_N2
    ma, mai, mb, mbi = _two_stage_tables()
    once = pl.Buffered(1)
    hspec = lambda direction, order: pl.BlockSpec(
        (length, dt), lambda c: (0, (direction * HY_ORDER + order) * nct + c))
    spectrum = pl.pallas_call(
        _hyena_spectrum_kernel,
        grid=(nct,),
        in_specs=[hspec(0, 0), hspec(0, 1), hspec(1, 0), hspec(1, 1),
                  pl.BlockSpec(ma.shape, lambda c: (0, 0, 0), pipeline_mode=once),
                  pl.BlockSpec(mb.shape, lambda c: (0, 0), pipeline_mode=once)],
        out_specs=pl.BlockSpec((HY_ORDER, FFT_N2, 2 * FFT_N1, dt), lambda c: (0, 0, 0, c)),
        out_shape=jax.ShapeDtypeStruct((HY_ORDER, FFT_N2, 2 * FFT_N1, d), BF16),
        scratch_shapes=[pltpu.VMEM((length, dt), F32), pltpu.VMEM((FFT_WORK_ROWS, dt), F32)],
        compiler_params=_cparams("parallel"),
        name="hyena_filter_spectrum",
    )(hfilt, hfilt, hfilt, hfilt, ma, mb)

    zspec = lambda part: pl.BlockSpec((1, length, dt), lambda c, b: (b + first, 0, part * nct + c))
    wspec = lambda part: pl.BlockSpec((3, dt), lambda c, b: (0, part * nct + c))
    fixed3 = lambda c, b: (0, 0, 0)
    fixed2 = lambda c, b: (0, 0)
    return pl.pallas_call(
        _hyena_two_stage_kernel,
        grid=(nct, bsz),
        in_specs=[zspec(0), zspec(1), zspec(2), wspec(0), wspec(1), wspec(2),
                  pl.BlockSpec((HY_ORDER, FFT_N2, 2 * FFT_N1, dt), lambda c, b: (0, 0, 0, c), pipeline_mode=once),
                  pl.BlockSpec((HY_ORDER, dt), lambda c, b: (0, c)),
                  pl.BlockSpec(ma.shape, fixed3, pipeline_mode=once),
                  pl.BlockSpec(mai.shape, fixed3, pipeline_mode=once),
                  pl.BlockSpec(mb.shape, fixed2, pipeline_mode=once),
                  pl.BlockSpec(mbi.shape, fixed2, pipeline_mode=once)],
        out_specs=pl.BlockSpec((1, length, dt), lambda c, b: (b, 0, c)),
        out_shape=jax.ShapeDtypeStruct((bsz, length, d), BF16),
        scratch_shapes=[pltpu.VMEM((length, dt), F32), pltpu.VMEM((length, dt), F32),
                        pltpu.VMEM((FFT_WORK_ROWS, dt), F32), pltpu.VMEM((FFT_N2, 2 * FFT_N1, dt), BF16)],
        compiler_params=_cparams("parallel", "arbitrary"),
        name="hyena_conv_two_stage",
    )(z, z, z, w_short, w_short, w_short, spectrum, skip, ma, mai, mb, mbi)


N_HEADS = 16
HEAD_DIM = D_MODEL // N_HEADS
HEADS_PER_STEP = 2
GRID_W = 64
WIN_ROWS = 8
WIN_COLS = 16
NEG_INF = -1e30
NAT_Q_ROWS = 4
NAT_K_ROWS = NAT_Q_ROWS + WIN_ROWS - 1
ATTN_SCALE = HEAD_DIM ** -0.5


def _dot_nt(a, b):
    return lax.dot_general(a, b, (((1,), (1,)), ((), ())), preferred_element_type=F32)


def _qkv_kernel(x_ref, g_ref, sc_ref, sh_ref, w_ref, q_ref, k_ref, v_ref):
    h = _norm_mod(x_ref[...], g_ref[...], sc_ref[0], sh_ref[0])
    qkv = _dot(h.astype(BF16), w_ref[...])
    for part, ref in enumerate((q_ref, k_ref, v_ref)):
        for head in range(N_HEADS):
            lo = part * D_MODEL + head * HEAD_DIM
            ref[0, head] = qkv[:, lo:lo + HEAD_DIM].astype(ref.dtype)


def _qkv_proj(x, g, mods, w_bf16, seg, first_row, n_seq, seq_len, kv_dtype):
    d = x.shape[1]
    n_ctx, smp_len = seg
    t = ROW_TILE
    first_tile = first_row // t
    per_seq = seq_len // t
    out_spec = pl.BlockSpec((1, N_HEADS, t, HEAD_DIM), lambda i: (i // per_seq, 0, i % per_seq, 0))
    shape = (n_seq, N_HEADS, seq_len, HEAD_DIM)
    return pl.pallas_call(
        _qkv_kernel,
        grid=(n_seq * per_seq,),
        in_specs=[
            pl.BlockSpec((t, d), lambda i: (i + first_tile, 0)),
            pl.BlockSpec((1, d), lambda i: (0, 0)),
            _mod_spec(1, t, n_ctx, smp_len, first_tile),
            _mod_spec(0, t, n_ctx, smp_len, first_tile),
            pl.BlockSpec(w_bf16.shape, lambda i: (0, 0)),
        ],
        out_specs=[out_spec, out_spec, out_spec],
        out_shape=[jax.ShapeDtypeStruct(shape, BF16), jax.ShapeDtypeStruct(shape, kv_dtype),
                   jax.ShapeDtypeStruct(shape, kv_dtype)],
        compiler_params=_cparams("parallel"),
        name="qkv_proj",
    )(x, g.reshape(1, d), mods, mods, w_bf16)


def _ctx_attn_kernel(q_ref, k_ref, v_ref, o_ref):
    for j in range(HEADS_PER_STEP):
        q = q_ref[0, j]
        k = k_ref[0, j].astype(BF16)
        v = v_ref[0, j].astype(BF16)
        s = _dot_nt(q, k) * ATTN_SCALE
        p = jnp.exp(s - jnp.max(s, axis=-1, keepdims=True))
        o = _dot(p.astype(BF16), v) / jnp.sum(p, axis=-1, keepdims=True)
        o_ref[:, j * HEAD_DIM:(j + 1) * HEAD_DIM] = o.astype(o_ref.dtype)


def _ctx_attention(q, k, v):
    bsz, _, s, _ = q.shape
    hp = HEADS_PER_STEP
    spec = pl.BlockSpec((1, hp, s, HEAD_DIM), lambda b, h: (b, h, 0, 0))
    return pl.pallas_call(
        _ctx_attn_kernel,
        grid=(bsz, N_HEADS // hp),
        in_specs=[spec, spec, spec],
        out_specs=pl.BlockSpec((s, hp * HEAD_DIM), lambda b, h: (b, h)),
        out_shape=jax.ShapeDtypeStruct((bsz * s, D_MODEL), BF16),
        compiler_params=_cparams("parallel", "parallel"),
        name="ctx_attention",
    )(q, k, v)


def _rpb_toeplitz_kernel(r_ref, e_ref, o_ref):
    o_ref[...] = _dot3(r_ref[...], e_ref[...])


def _nat_bias(rpb):
    n_heads, n_r, n_c = rpb.shape
    n_cp = 32
    qc = np.arange(GRID_W)[:, None]
    kc = np.arange(GRID_W)[None, :]
    onehot = (np.clip(kc - qc + WIN_COLS - 1, 0, n_c - 1)[None] == np.arange(n_cp)[:, None, None])
    onehot = jnp.asarray(onehot.reshape(n_cp, GRID_W * GRID_W), F32)
    rows = jnp.pad(rpb.reshape(n_heads * n_r, n_c), ((0, 0), (0, n_cp - n_c)))
    toep = pl.pallas_call(
        _rpb_toeplitz_kernel,
        out_shape=jax.ShapeDtypeStruct((n_heads * n_r, GRID_W * GRID_W), F32),
        compiler_params=_cparams(),
        name="nat_bias_toeplitz",
    )(rows, onehot).reshape(n_heads, n_r, GRID_W, GRID_W)
    q_start = np.clip(qc - WIN_COLS // 2, 0, GRID_W - WIN_COLS)
    col_ok = jnp.asarray((kc >= q_start) & (kc < q_start + WIN_COLS))
    toep = jnp.where(col_ok, toep, NEG_INF)
    toep = jnp.concatenate([toep, jnp.full((n_heads, 1, GRID_W, GRID_W), NEG_INF, F32)], axis=1)
    rows_total = GRID_W
    idx = np.full((3, NAT_Q_ROWS, NAT_K_ROWS), n_r, np.int32)
    for case, r0 in enumerate((0, NAT_Q_ROWS, rows_total - NAT_Q_ROWS)):
        ks = int(np.clip(r0 - WIN_ROWS // 2, 0, rows_total - NAT_K_ROWS))
        for dr in range(NAT_Q_ROWS):
            r = r0 + dr
            rs = int(np.clip(r - WIN_ROWS // 2, 0, rows_total - WIN_ROWS))
            for dk in range(NAT_K_ROWS):
                kr = ks + dk
                if rs <= kr < rs + WIN_ROWS:
                    idx[case, dr, dk] = kr - r + WIN_ROWS - 1
    blocks = toep[:, jnp.asarray(idx)]
    blocks = jnp.transpose(blocks, (0, 1, 2, 4, 3, 5))
    return blocks.reshape(n_heads, 3, NAT_Q_ROWS * GRID_W, NAT_K_ROWS * GRID_W)


def _nat_kernel(q_ref, k_ref, v_ref, kc_ref, vc_ref, bias_ref, o_ref):
    n_blocks = q_ref.shape[2] // (NAT_Q_ROWS * GRID_W)
    rows_total = q_ref.shape[2] // GRID_W
    nq = NAT_Q_ROWS * GRID_W
    nk = NAT_K_ROWS * GRID_W
    for j in range(HEADS_PER_STEP):
        k_ctx = kc_ref[0, 0, j].astype(BF16)
        v_ctx = vc_ref[0, 0, j].astype(BF16)

        def block(blk, carry):
            ks = jnp.clip(blk * NAT_Q_ROWS - WIN_ROWS // 2, 0, rows_total - NAT_K_ROWS)
            case = jnp.where(blk == 0, 0, jnp.where(blk == n_blocks - 1, 2, 1))
            q = q_ref[0, j, pl.ds(pl.multiple_of(blk * nq, nq), nq), :]
            k_loc = k_ref[0, j, pl.ds(pl.multiple_of(ks * GRID_W, GRID_W), nk), :]
            v_loc = v_ref[0, j, pl.ds(pl.multiple_of(ks * GRID_W, GRID_W), nk), :]
            s_loc = _dot_nt(q, k_loc) * ATTN_SCALE + bias_ref[j, case]
            s_ctx = _dot_nt(q, k_ctx) * ATTN_SCALE
            m = jnp.maximum(jnp.max(s_loc, axis=-1, keepdims=True), jnp.max(s_ctx, axis=-1, keepdims=True))
            p_loc = jnp.exp(s_loc - m)
            p_ctx = jnp.exp(s_ctx - m)
            denom = jnp.sum(p_loc, axis=-1, keepdims=True) + jnp.sum(p_ctx, axis=-1, keepdims=True)
            o = (_dot(p_loc.astype(BF16), v_loc) + _dot(p_ctx.astype(BF16), v_ctx)) / denom
            o_ref[pl.ds(pl.multiple_of(blk * nq, nq), nq), j * HEAD_DIM:(j + 1) * HEAD_DIM] = o.astype(o_ref.dtype)
            return carry

        lax.fori_loop(0, n_blocks, block, 0)


def _nat_attention(q, k, v, cache_k, cache_v, bias):
    bsz, _, length, _ = q.shape
    hp = HEADS_PER_STEP
    past = cache_k.shape[3]
    spec = pl.BlockSpec((1, hp, length, HEAD_DIM), lambda b, h: (b, h, 0, 0))
    cspec = pl.BlockSpec((1, 1, hp, past, HEAD_DIM), lambda b, h: (b, 0, h, 0, 0))
    return pl.pallas_call(
        _nat_kernel,
        grid=(bsz, N_HEADS // hp),
        in_specs=[spec, spec, spec, cspec, cspec,
                  pl.BlockSpec((hp,) + bias.shape[1:], lambda b, h: (h, 0, 0, 0))],
        out_specs=pl.BlockSpec((length, hp * HEAD_DIM), lambda b, h: (b, h)),
        out_shape=jax.ShapeDtypeStruct((bsz * length, D_MODEL), BF16),
        compiler_params=_cparams("parallel", "parallel"),
        name="nat_attention",
    )(q, k, v, cache_k, cache_v, bias)


def kernel(x_prompt, x_sample, cache_k, cache_v, c, c_ctx, ada_w, ada_b, norm1_g, norm2_g, normf_g, hy_w_in, hy_w_short, hy_f_w1, hy_f_b1, hy_f_freq, hy_f_w2, hy_f_b2, hy_f_w3, hy_skip, hy_w_out, na_w_qkv, na_rpb, na_w_o, moe_w_router, moe_b_router, moe_w_gate, moe_b_gate, moe_w_up, moe_b_up, moe_w_down, moe_b_down):
    b, s, d = x_prompt.shape
    bd, sd, _ = x_sample.shape
    n_ctx = b * s
    n_tok = n_ctx + bd * sd
    assert n_ctx % sd == 0 and s % ROW_TILE == 0 and sd % ROW_TILE == 0
    seg = (n_ctx, sd)
    x = jnp.concatenate([x_prompt.reshape(n_ctx, d), x_sample.reshape(bd * sd, d)], axis=0)
    cond = jnp.zeros((N_COND, d), F32).at[0].set(c_ctx).at[1:1 + bd].set(c)
    mods = _modulation(cond, ada_w, ada_b)

    def moe(x, i, final_norm):
        return _moe_layer(x, norm2_g[i], mods[i], moe_w_router[i], moe_b_router[i], moe_w_gate[i], moe_b_gate[i],
                          moe_w_up[i], moe_b_up[i], moe_w_down[i], moe_b_down[i], normf_g, seg, final_norm)

    z = _norm_proj(x, norm1_g[0], mods[0], hy_w_in[0].astype(BF16), seg)
    fargs = (hy_f_w1[0], hy_f_b1[0], hy_f_freq[0], hy_f_w2[0], hy_f_b2[0], hy_f_w3[0])
    y_ctx = _hyena_direct(z.reshape(n_tok // s, s, 3 * d), hy_w_short[0], _hyena_filters(s, *fargs),
                          hy_skip[0], first=0, bsz=b)
    y_smp = _hyena_two_stage(z.reshape(n_tok // sd, sd, 3 * d), hy_w_short[0], _hyena_filters(sd, *fargs),
                             hy_skip[0], first=n_ctx // sd, bsz=bd)
    y = jnp.concatenate([y_ctx.reshape(n_ctx, d), y_smp.reshape(bd * sd, d)], axis=0)
    x = _out_proj(y, hy_w_out[0].astype(BF16), x, mods[0], seg)
    x = moe(x, 0, False)

    w_qkv = na_w_qkv[0].astype(BF16)
    q_c, k_c, v_c = _qkv_proj(x, norm1_g[1], mods[1], w_qkv, seg, 0, b, s, F32)
    q_s, k_s, v_s = _qkv_proj(x, norm1_g[1], mods[1], w_qkv, seg, n_ctx, bd, sd, BF16)
    o_ctx = _ctx_attention(q_c, k_c, v_c)
    o_smp = _nat_attention(q_s, k_s, v_s, cache_k, cache_v, _nat_bias(na_rpb[0]))
    x = _out_proj(jnp.concatenate([o_ctx, o_smp], axis=0), na_w_o[0].astype(BF16), x, mods[1], seg)
    x = moe(x, 1, True)

    nh, hd = k_c.shape[1], k_c.shape[3]
    return (x[:n_ctx].reshape(b, s, d), x[n_ctx:].reshape(bd, sd, d),
            k_c.reshape(b, 1, nh, s, hd), v_c.reshape(b, 1, nh, s, hd))
```

```python
import functools
import math

import numpy as np
import jax
import jax.numpy as jnp
from jax import lax
from jax.experimental import pallas as pl
from jax.experimental.pallas import tpu as pltpu

F32 = jnp.float32
BF16 = jnp.bfloat16

D_MODEL = 1024
N_MOD = 6
RMS_EPS = 1e-6
N_EXPERTS = 32
TOP_K = 4
SWIGLU_LIMIT = 7.0
SWIGLU_ALPHA = 1.702

N_COND = 8
ROW_TILE = 256
MOE_TILE = 256
COMBINE_TILE = 128
V7X_VMEM_LIMIT = 56 * 1024 * 1024


def _cparams(*sem, vmem=V7X_VMEM_LIMIT):
    return pltpu.CompilerParams(dimension_semantics=sem, vmem_limit_bytes=vmem)


def _dot(a, b):
    return jnp.dot(a, b, preferred_element_type=F32)


def _split_bf16(x):
    hi = x.astype(BF16)
    lo = (x - hi.astype(F32)).astype(BF16)
    return hi, lo


def _dot3(a, b):
    ah, al = _split_bf16(a)
    bh, bl = _split_bf16(b)
    return _dot(ah, bh) + (_dot(al, bh) + _dot(ah, bl))


def _seg_of_tile(i, tile, n_ctx, smp_len):
    ctx_tiles = n_ctx // tile
    per_smp = smp_len // tile
    return jnp.where(i < ctx_tiles, 0, 1 + (i - ctx_tiles) // per_smp)


def _norm_mod(x, g, sc, sh):
    y = x * lax.rsqrt(jnp.mean(x * x, axis=-1, keepdims=True) + RMS_EPS)
    return (y * g) * (1.0 + sc) + sh


def _mod_kernel(c_ref, w_ref, b_ref, o_ref):
    c = c_ref[...]
    a = c * jax.nn.sigmoid(c)
    o_ref[0] = _dot3(a, w_ref[0]) + b_ref[0]


def _modulation(cond, ada_w, ada_b):
    depth, d, n_out = ada_w.shape
    tn = 1536
    m = pl.pallas_call(
        _mod_kernel,
        grid=(depth, n_out // tn),
        in_specs=[
            pl.BlockSpec((N_COND, d), lambda l, j: (0, 0)),
            pl.BlockSpec((1, d, tn), lambda l, j: (l, 0, j)),
            pl.BlockSpec((1, 1, tn), lambda l, j: (l, 0, j)),
        ],
        out_specs=pl.BlockSpec((1, N_COND, tn), lambda l, j: (l, 0, j)),
        out_shape=jax.ShapeDtypeStruct((depth, N_COND, n_out), F32),
        compiler_params=_cparams("arbitrary", "arbitrary"),
        name="adaln_modulation",
    )(cond, ada_w, ada_b.reshape(depth, 1, n_out))
    m = m.reshape(depth, N_COND, N_MOD, d)
    return jnp.transpose(m, (0, 2, 1, 3)).reshape(depth, N_MOD * N_COND, 1, d)


def _mod_spec(which, tile, n_ctx, smp_len, first_tile=0):
    return pl.BlockSpec(
        (1, 1, D_MODEL),
        lambda i, *_: (which * N_COND + _seg_of_tile(i + first_tile, tile, n_ctx, smp_len), 0, 0))


def _norm_proj_kernel(x_ref, g_ref, sc_ref, sh_ref, w_ref, o_ref):
    h = _norm_mod(x_ref[...], g_ref[...], sc_ref[0], sh_ref[0])
    o_ref[...] = _dot(h.astype(BF16), w_ref[...]).astype(o_ref.dtype)


def _norm_proj(x, g, mods, w_bf16, seg, out_dtype=F32):
    n, d = x.shape
    n_out = w_bf16.shape[1]
    n_ctx, smp_len = seg
    return pl.pallas_call(
        _norm_proj_kernel,
        grid=(n // ROW_TILE,),
        in_specs=[
            pl.BlockSpec((ROW_TILE, d), lambda i: (i, 0)),
            pl.BlockSpec((1, d), lambda i: (0, 0)),
            _mod_spec(1, ROW_TILE, n_ctx, smp_len),
            _mod_spec(0, ROW_TILE, n_ctx, smp_len),
            pl.BlockSpec((d, n_out), lambda i: (0, 0)),
        ],
        out_specs=pl.BlockSpec((ROW_TILE, n_out), lambda i: (i, 0)),
        out_shape=jax.ShapeDtypeStruct((n, n_out), out_dtype),
        compiler_params=_cparams("parallel"),
        name="norm_proj",
    )(x, g.reshape(1, d), mods, mods, w_bf16)


def _out_proj_kernel(y_ref, w_ref, x_ref, gate_ref, o_ref):
    o_ref[...] = x_ref[...] + gate_ref[0] * _dot(y_ref[...], w_ref[...])


def _out_proj(y_bf16, w_bf16, x, mods, seg):
    n, d = x.shape
    n_ctx, smp_len = seg
    return pl.pallas_call(
        _out_proj_kernel,
        grid=(n // ROW_TILE,),
        in_specs=[
            pl.BlockSpec((ROW_TILE, d), lambda i: (i, 0)),
            pl.BlockSpec((d, d), lambda i: (0, 0)),
            pl.BlockSpec((ROW_TILE, d), lambda i: (i, 0)),
            _mod_spec(2, ROW_TILE, n_ctx, smp_len),
        ],
        out_specs=pl.BlockSpec((ROW_TILE, d), lambda i: (i, 0)),
        out_shape=jax.ShapeDtypeStruct((n, d), F32),
        compiler_params=_cparams("parallel"),
        name="out_proj",
    )(y_bf16, w_bf16, x, mods)


def _columns(cols):
    t = cols[0].shape[0]
    lane = lax.broadcasted_iota(jnp.int32, (t, len(cols)), 1)
    out = jnp.broadcast_to(cols[-1], (t, len(cols)))
    for k in range(len(cols) - 2, -1, -1):
        out = jnp.where(lane == k, cols[k], out)
    return out


def _router_kernel(x_ref, g_ref, sc_ref, sh_ref, wr_ref, br_ref, tri_ref,
                   e_ref, gate_ref, rank_ref, cnt_ref, run_ref):
    i = pl.program_id(0)

    @pl.when(i == 0)
    def _():
        run_ref[...] = jnp.zeros_like(run_ref)

    h = _norm_mod(x_ref[...], g_ref[...], sc_ref[0], sh_ref[0])
    logits = _dot3(h, wr_ref[...]) + br_ref[...]
    lane = lax.broadcasted_iota(jnp.int32, logits.shape, 1)
    work = logits
    vals, idxs, hots = [], [], []
    for _ in range(TOP_K):
        m = jnp.max(work, axis=-1, keepdims=True)
        idx = jnp.min(jnp.where(work == m, lane, N_EXPERTS), axis=-1, keepdims=True)
        hot = lane == idx
        vals.append(m)
        idxs.append(idx)
        hots.append(hot)
        work = jnp.where(hot, -jnp.inf, work)
    ex = [jnp.exp(v - vals[0]) for v in vals]
    denom = ex[0] + ex[1] + ex[2] + ex[3]
    gate_ref[...] = _columns([e / denom for e in ex])
    e_ref[...] = _columns(idxs)

    chosen = (hots[0] | hots[1] | hots[2] | hots[3]).astype(F32)
    before = run_ref[...] + _dot(tri_ref[...], chosen.astype(BF16))
    ranks = [jnp.sum(jnp.where(hot, before, 0.0), axis=-1, keepdims=True) for hot in hots]
    rank_ref[...] = _columns(ranks).astype(jnp.int32)
    run_ref[...] += jnp.sum(chosen, axis=0, keepdims=True)
    cnt_ref[...] = run_ref[...].astype(jnp.int32)


def _router(x, g, mods, w_r, b_r, seg):
    n, d = x.shape
    n_ctx, smp_len = seg
    t = ROW_TILE
    tri = jnp.asarray(np.tril(np.ones((t, t), np.float32), -1), BF16)
    tok4 = lambda i: (i, 0)
    return pl.pallas_call(
        _router_kernel,
        grid=(n // t,),
        in_specs=[
            pl.BlockSpec((t, d), lambda i: (i, 0)),
            pl.BlockSpec((1, d), lambda i: (0, 0)),
            _mod_spec(4, t, n_ctx, smp_len),
            _mod_spec(3, t, n_ctx, smp_len),
            pl.BlockSpec((d, N_EXPERTS), lambda i: (0, 0)),
            pl.BlockSpec((1, N_EXPERTS), lambda i: (0, 0)),
            pl.BlockSpec((t, t), lambda i: (0, 0)),
        ],
        out_specs=[
            pl.BlockSpec((t, TOP_K), tok4),
            pl.BlockSpec((t, TOP_K), tok4),
            pl.BlockSpec((t, TOP_K), tok4),
            pl.BlockSpec((1, N_EXPERTS), lambda i: (0, 0)),
        ],
        out_shape=[
            jax.ShapeDtypeStruct((n, TOP_K), jnp.int32),
            jax.ShapeDtypeStruct((n, TOP_K), F32),
            jax.ShapeDtypeStruct((n, TOP_K), jnp.int32),
            jax.ShapeDtypeStruct((1, N_EXPERTS), jnp.int32),
        ],
        scratch_shapes=[pltpu.VMEM((1, N_EXPERTS), F32)],
        compiler_params=_cparams("arbitrary"),
        name="moe_router",
    )(x, g.reshape(1, d), mods, mods, w_r, b_r.reshape(1, N_EXPERTS), tri)


def _row_copy(src_ref, src_row, dst_ref, dst_row, sem):
    return pltpu.make_async_copy(src_ref.at[pl.ds(src_row, 1), :],
                                 dst_ref.at[pl.ds(dst_row, 1), :], sem)


def _dispatch_kernel(fill_start_ref, fill_len_ref, na_ref, dest_ref, x_ref, g_ref, sc_ref, sh_ref,
                     xs_ref, h_ref, zero_ref, sem):
    i = pl.program_id(0)
    t = x_ref.shape[0]
    h_ref[...] = _norm_mod(x_ref[...], g_ref[...], sc_ref[0], sh_ref[0])

    def issue(r, carry):
        for k in range(TOP_K):
            _row_copy(h_ref, r, xs_ref, dest_ref[r * TOP_K + k], sem).start()
        return carry

    lax.fori_loop(0, t, issue, 0)

    def drain(r, carry):
        for k in range(TOP_K):
            _row_copy(h_ref, 0, xs_ref, 0, sem).wait()
        return carry

    lax.fori_loop(0, t, drain, 0)

    @pl.when(i == pl.num_programs(0) - 1)
    def _():
        zero_ref[...] = jnp.zeros_like(zero_ref)
        for e in range(N_EXPERTS):
            start = fill_start_ref[e]
            count = fill_len_ref[e]

            def fill(r, carry):
                _row_copy(zero_ref, 0, xs_ref, start + r, sem).start()
                return carry

            lax.fori_loop(0, count, fill, 0)

            def fill_drain(r, carry):
                _row_copy(zero_ref, 0, xs_ref, 0, sem).wait()
                return carry

            lax.fori_loop(0, count, fill_drain, 0)

        def block_copy(blk):
            return pltpu.make_async_copy(
                zero_ref, xs_ref.at[pl.ds(pl.multiple_of(blk * MOE_TILE, MOE_TILE), MOE_TILE), :], sem)

        n_blocks = xs_ref.shape[0] // MOE_TILE

        def tail(blk, carry):
            block_copy(blk).start()
            return carry

        lax.fori_loop(na_ref[0], n_blocks, tail, 0)

        def tail_drain(blk, carry):
            block_copy(0).wait()
            return carry

        lax.fori_loop(na_ref[0], n_blocks, tail_drain, 0)


def _dispatch(x, g, mods, dest_flat, fill_start, fill_len, n_active, cap, seg):
    n, d = x.shape
    n_ctx, smp_len = seg
    t = ROW_TILE
    grid_spec = pltpu.PrefetchScalarGridSpec(
        num_scalar_prefetch=3,
        grid=(n // t,),
        in_specs=[
            pl.BlockSpec((t * TOP_K,), lambda i, *_: (i,), memory_space=pltpu.SMEM),
            pl.BlockSpec((t, d), lambda i, *_: (i, 0)),
            pl.BlockSpec((1, d), lambda i, *_: (0, 0)),
            _mod_spec(4, t, n_ctx, smp_len),
            _mod_spec(3, t, n_ctx, smp_len),
        ],
        out_specs=pl.BlockSpec(memory_space=pl.ANY),
        scratch_shapes=[pltpu.VMEM((t, d), F32), pltpu.VMEM((MOE_TILE, d), F32), pltpu.SemaphoreType.DMA(())],
    )
    return pl.pallas_call(
        _dispatch_kernel,
        grid_spec=grid_spec,
        out_shape=jax.ShapeDtypeStruct((cap, d), F32),
        compiler_params=_cparams("arbitrary"),
        name="moe_dispatch",
    )(fill_start, fill_len, n_active, dest_flat, x, g.reshape(1, d), mods, mods)


def _ffn_kernel(be_ref, na_ref, xs_ref, wg_ref, bg_ref, wu_ref, bu_ref, wd_ref, bd_ref,
                ys_ref, wg_s, wu_s, wd_s):
    i = pl.program_id(0)
    fresh = jnp.logical_or(i == 0, be_ref[i] != be_ref[jnp.maximum(i - 1, 0)])

    @pl.when(fresh)
    def _():
        wg_s[...] = wg_ref[0, 0].astype(BF16)
        wu_s[...] = wu_ref[0, 0].astype(BF16)
        wd_s[...] = wd_ref[0, 0].astype(BF16)

    @pl.when(i < na_ref[0])
    def _():
        x = xs_ref[...].astype(BF16)
        g = jnp.minimum(_dot(x, wg_s[...]) + bg_ref[0, 0], SWIGLU_LIMIT)
        u = jnp.clip(_dot(x, wu_s[...]) + bu_ref[0, 0], -SWIGLU_LIMIT, SWIGLU_LIMIT)
        a = g * jax.nn.sigmoid(SWIGLU_ALPHA * g) * (u + 1.0)
        ys_ref[...] = _dot(a.astype(BF16), wd_s[...]) + bd_ref[0, 0]

    @pl.when(i >= na_ref[0])
    def _():
        ys_ref[...] = jnp.zeros_like(ys_ref)


def _expert_ffn(xs, block_e, n_active, layer, w_g, b_g, w_u, b_u, w_d, b_d):
    cap, d = xs.shape
    depth, ne, _, f = w_g.shape
    nb = cap // MOE_TILE
    wmap = lambda i, be, na: (layer, be[i], 0, 0)
    grid_spec = pltpu.PrefetchScalarGridSpec(
        num_scalar_prefetch=2,
        grid=(nb,),
        in_specs=[
            pl.BlockSpec((MOE_TILE, d), lambda i, be, na: (jnp.minimum(i, na[0] - 1), 0)),
            pl.BlockSpec((1, 1, d, f), wmap),
            pl.BlockSpec((1, 1, 1, f), wmap),
            pl.BlockSpec((1, 1, d, f), wmap),
            pl.BlockSpec((1, 1, 1, f), wmap),
            pl.BlockSpec((1, 1, f, d), wmap),
            pl.BlockSpec((1, 1, 1, d), wmap),
        ],
        out_specs=pl.BlockSpec((MOE_TILE, d), lambda i, be, na: (i, 0)),
        scratch_shapes=[pltpu.VMEM((d, f), BF16), pltpu.VMEM((d, f), BF16), pltpu.VMEM((f, d), BF16)],
    )
    return pl.pallas_call(
        _ffn_kernel,
        grid_spec=grid_spec,
        out_shape=jax.ShapeDtypeStruct((cap, d), F32),
        compiler_params=_cparams("arbitrary"),
        name="moe_expert_ffn",
    )(block_e, n_active, xs, w_g, b_g.reshape(depth, ne, 1, f), w_u, b_u.reshape(depth, ne, 1, f),
      w_d, b_d.reshape(depth, ne, 1, d))


def _combine_kernel(dest_ref, x_ref, gate_ref, g2_ref, gf_ref, ys_ref, o_ref, buf_ref, sem, *, final_norm):
    t = x_ref.shape[0]

    def issue(r, carry):
        for k in range(TOP_K):
            _row_copy(ys_ref, dest_ref[r * TOP_K + k], buf_ref.at[k], r, sem).start()
        return carry

    lax.fori_loop(0, t, issue, 0)

    def drain(r, carry):
        for k in range(TOP_K):
            _row_copy(ys_ref, 0, buf_ref.at[0], 0, sem).wait()
        return carry

    lax.fori_loop(0, t, drain, 0)

    gates = gate_ref[...]
    acc = gates[:, 0:1] * buf_ref[0]
    for k in range(1, TOP_K):
        acc = acc + gates[:, k:k + 1] * buf_ref[k]
    y = x_ref[...] + g2_ref[0] * acc
    if final_norm:
        y = y * lax.rsqrt(jnp.mean(y * y, axis=-1, keepdims=True) + RMS_EPS) * gf_ref[...]
    o_ref[...] = y


def _combine(x, ys, dest_flat, gates, mods, normf_g, seg, final_norm):
    n, d = x.shape
    n_ctx, smp_len = seg
    t = COMBINE_TILE
    return pl.pallas_call(
        functools.partial(_combine_kernel, final_norm=final_norm),
        grid=(n // t,),
        in_specs=[
            pl.BlockSpec((t * TOP_K,), lambda i: (i,), memory_space=pltpu.SMEM),
            pl.BlockSpec((t, d), lambda i: (i, 0)),
            pl.BlockSpec((t, TOP_K), lambda i: (i, 0)),
            _mod_spec(5, t, n_ctx, smp_len),
            pl.BlockSpec((1, d), lambda i: (0, 0)),
            pl.BlockSpec(memory_space=pl.ANY),
        ],
        out_specs=pl.BlockSpec((t, d), lambda i: (i, 0)),
        out_shape=jax.ShapeDtypeStruct((n, d), F32),
        scratch_shapes=[pltpu.VMEM((TOP_K, t, d), F32), pltpu.SemaphoreType.DMA(())],
        compiler_params=_cparams("arbitrary"),
        name="moe_combine",
    )(dest_flat, x, gates, mods, normf_g.reshape(1, d), ys)


def _moe_layer(x, norm_g, mods, w_r, b_r, layer, w_g, b_g, w_u, b_u, w_d, b_d, normf_g, seg, final_norm):
    n, d = x.shape
    top_e, gates, rank, counts = _router(x, norm_g, mods, w_r, b_r, seg)
    counts = counts[0]
    padded = (counts + MOE_TILE - 1) // MOE_TILE * MOE_TILE
    pad_end = jnp.cumsum(padded)
    pad_start = pad_end - padded
    cap = n * TOP_K + N_EXPERTS * MOE_TILE
    nb = cap // MOE_TILE
    experts = jnp.arange(N_EXPERTS, dtype=jnp.int32)
    start_of = jnp.sum(jnp.where(top_e[..., None] == experts, pad_start, 0), axis=-1)
    dest = (start_of + rank).reshape(n * TOP_K).astype(jnp.int32)
    blk_start = jnp.arange(nb, dtype=jnp.int32) * MOE_TILE
    block_e = jnp.minimum(jnp.sum(blk_start[:, None] >= pad_end[None, :], axis=1), N_EXPERTS - 1).astype(jnp.int32)
    n_active = (pad_end[-1] // MOE_TILE).astype(jnp.int32).reshape(1)
    xs = _dispatch(x, norm_g, mods, dest, (pad_start + counts).astype(jnp.int32),
                   (padded - counts).astype(jnp.int32), n_active, cap, seg)
    ys = _expert_ffn(xs, block_e, n_active, layer, w_g, b_g, w_u, b_u, w_d, b_d)
    return _combine(x, ys, dest, gates, mods, normf_g, seg, final_norm)


HY_ORDER = 2
HY_BANDS = 16
HY_EMB = 1 + 2 * HY_BANDS
HY_FFN = 64
HY_MIN_DECAY = math.log(1e-2) / 1.5
HY_MAX_DECAY = math.log(1e-2) / 0.3
HY_EMB_PAD = 64


def _filter_mlp_kernel(z_ref, t_ref, w1_ref, b1_ref, fr_ref, w2_ref, b2_ref, w3_ref, dl_ref, o_ref):
    fr = fr_ref[...]
    h = jnp.sin(fr * (_dot3(z_ref[...], w1_ref[...]) + b1_ref[...]))
    h = jnp.sin(fr * (_dot3(h, w2_ref[...]) + b2_ref[...]))
    o_ref[...] = _dot3(h, w3_ref[...]) * jnp.exp(-t_ref[...] * dl_ref[...])


def _hyena_filters(length, w1, b1, freq, w2, b2, w3):
    t = jnp.linspace(0.0, 1.0, length, dtype=F32)[:, None]
    ang = (2.0 * math.pi / length) * jnp.arange(length, dtype=F32)[:, None]
    bands = jnp.linspace(1e-4, HY_BANDS - 1, HY_BANDS, dtype=F32)[None, :]
    z = jnp.concatenate([t, jnp.cos(bands * ang), -jnp.sin(bands * ang)], axis=-1)
    z = jnp.pad(z, ((0, 0), (0, HY_EMB_PAD - HY_EMB)))
    w1p = jnp.pad(w1, ((0, HY_EMB_PAD - HY_EMB), (0, 0)))
    n_out = w3.shape[1]
    deltas = jnp.abs(jnp.linspace(HY_MIN_DECAY, HY_MAX_DECAY, D_MODEL, dtype=F32))
    deltas = jnp.tile(deltas, n_out // D_MODEL)[None, :]
    tl, tn = 256, 1024
    row = lambda i, j: (i, 0)
    fixed = lambda i, j: (0, 0)
    return pl.pallas_call(
        _filter_mlp_kernel,
        grid=(length // tl, n_out // tn),
        in_specs=[
            pl.BlockSpec((tl, HY_EMB_PAD), row),
            pl.BlockSpec((tl, 1), row),
            pl.BlockSpec((HY_EMB_PAD, HY_FFN), fixed),
            pl.BlockSpec((1, HY_FFN), fixed),
            pl.BlockSpec((1, HY_FFN), fixed),
            pl.BlockSpec((HY_FFN, HY_FFN), fixed),
            pl.BlockSpec((1, HY_FFN), fixed),
            pl.BlockSpec((HY_FFN, tn), lambda i, j: (0, j)),
            pl.BlockSpec((1, tn), lambda i, j: (0, j)),
        ],
        out_specs=pl.BlockSpec((tl, tn), lambda i, j: (i, j)),
        out_shape=jax.ShapeDtypeStruct((length, n_out), F32),
        compiler_params=_cparams("parallel", "parallel"),
        name="hyena_filter_mlp",
    )(z, t, w1p, b1.reshape(1, -1), freq.reshape(1, -1), w2, b2.reshape(1, -1), w3, deltas)


def _short_conv(z, w):
    length = z.shape[0]
    row = lax.broadcasted_iota(jnp.int32, z.shape, 0)
    prev = jnp.where(row == 0, 0.0, pltpu.roll(z, 1, 0))
    nxt = jnp.where(row == length - 1, 0.0, pltpu.roll(z, length - 1, 0))
    return (prev * w[0:1] + z * w[1:2]) + nxt * w[2:3]


def _filter_halves(hf_ref, hb_ref):
    hf = hf_ref[...]
    hb = hb_ref[...]
    hb = jnp.where(lax.broadcasted_iota(jnp.int32, hb.shape, 0) == 0, 0.0, hb)
    norm = jnp.sum(jnp.abs(hf), axis=0, keepdims=True) + jnp.sum(jnp.abs(hb), axis=0, keepdims=True)
    return hf + hb, hf - hb, 1.0 / norm


def _direct_dft_tables(length):
    n_fft = 2 * length
    n_freq = length + 1
    mf = -(-n_freq // 16) * 16
    k = np.arange(mf)[:, None]
    n = np.arange(length)[None, :]
    ang = 2.0 * np.pi * ((k * n) % n_fft) / n_fft
    valid = k < n_freq
    cos = np.where(valid, np.cos(ang), 0.0)
    msin = np.where(valid, -np.sin(ang), 0.0)
    weight = np.where((k == 0) | (k == length), 1.0, 2.0) * valid / n_fft
    fwd = np.concatenate([cos, msin], axis=0)
    inv = np.concatenate([weight * cos, weight * msin], axis=0).T
    return jnp.asarray(fwd, BF16), jnp.asarray(inv, BF16), mf


def _hyena_direct_kernel(zv_ref, z1_ref, z2_ref, wv_ref, w1_ref, w2_ref,
                         hf0_ref, hf1_ref, hb0_ref, hb1_ref, skip_ref, fw_ref, iv_ref,
                         o_ref, kr_ref, ki_ref):
    mf = fw_ref.shape[0] // 2
    dt = o_ref.shape[-1]

    @pl.when(pl.program_id(1) == 0)
    def _():
        for o, (hf_ref, hb_ref) in enumerate(((hf0_ref, hb0_ref), (hf1_ref, hb1_ref))):
            hs, hd, inv_norm = _filter_halves(hf_ref, hb_ref)
            spec = _dot(fw_ref[...], jnp.concatenate([hs, hd], axis=1).astype(BF16))
            kr_ref[o] = spec[:mf, :dt] * inv_norm
            ki_ref[o] = spec[mf:, dt:] * inv_norm

    y = _short_conv(zv_ref[0], wv_ref[...])
    for o, (z_ref, w_ref) in enumerate(((z1_ref, w1_ref), (z2_ref, w2_ref))):
        spec = _dot(fw_ref[...], y.astype(BF16))
        yr, yi = spec[:mf], spec[mf:]
        kr, ki = kr_ref[o], ki_ref[o]
        prod = jnp.concatenate([yr * kr - yi * ki, yr * ki + yi * kr], axis=0)
        yc = _dot(iv_ref[...], prod.astype(BF16))
        y = _short_conv(z_ref[0], w_ref[...]) * (yc + y * skip_ref[o:o + 1])
    o_ref[0] = y.astype(o_ref.dtype)


def _hyena_direct(z, w_short, hfilt, skip, first=0, bsz=None):
    _, length, d3 = z.shape
    bsz = z.shape[0] if bsz is None else bsz
    d = d3 // 3
    dt = 256
    nct = d // dt
    fwd, inv, mf = _direct_dft_tables(length)
    zspec = lambda part: pl.BlockSpec((1, length, dt), lambda c, b: (b + first, 0, part * nct + c))
    wspec = lambda part: pl.BlockSpec((3, dt), lambda c, b: (0, part * nct + c))
    hspec = lambda direction, order: pl.BlockSpec(
        (length, dt), lambda c, b: (0, (direction * HY_ORDER + order) * nct + c))
    fixed = lambda c, b: (0, 0)
    return pl.pallas_call(
        _hyena_direct_kernel,
        grid=(nct, bsz),
        in_specs=[zspec(0), zspec(1), zspec(2), wspec(0), wspec(1), wspec(2),
                  hspec(0, 0), hspec(0, 1), hspec(1, 0), hspec(1, 1),
                  pl.BlockSpec((HY_ORDER, dt), lambda c, b: (0, c)),
                  pl.BlockSpec(fwd.shape, fixed), pl.BlockSpec(inv.shape, fixed)],
        out_specs=pl.BlockSpec((1, length, dt), lambda c, b: (b, 0, c)),
        out_shape=jax.ShapeDtypeStruct((bsz, length, d), BF16),
        scratch_shapes=[pltpu.VMEM((HY_ORDER, mf, dt), F32), pltpu.VMEM((HY_ORDER, mf, dt), F32)],
        compiler_params=_cparams("parallel", "arbitrary"),
        name="hyena_conv_direct",
    )(z, z, z, w_short, w_short, w_short, hfilt, hfilt, hfilt, hfilt, skip, fwd, inv)


FFT_N1 = 64
FFT_N2 = 128
FFT_LANES = 128
FFT_A_PITCH = 2 * FFT_N2 + 8
FFT_U_PITCH = 2 * FFT_N1 + 8
FFT_GROUP = 2
FFT_UNROLL = 8


def _two_stage_tables():
    n_fft = FFT_N1 * FFT_N2
    half = FFT_N2 // 2
    n1 = np.arange(FFT_N1)[:, None, None]
    k2 = np.arange(FFT_N2)[None, :, None]
    n2 = np.arange(half)[None, None, :]
    ang = 2.0 * np.pi * ((k2 * (n1 + FFT_N1 * n2)) % n_fft) / n_fft
    stage_a = np.concatenate([np.cos(ang), -np.sin(ang)], axis=1)
    stage_a_inv = np.transpose(stage_a, (0, 2, 1)) / n_fft
    k1 = np.arange(FFT_N1)[:, None]
    m1 = np.arange(FFT_N1)[None, :]
    phi = 2.0 * np.pi * ((k1 * m1) % FFT_N1) / FFT_N1
    c, s = np.cos(phi), np.sin(phi)
    stage_b = np.block([[c, s], [-s, c]])
    stage_b_inv = np.block([[c, -s], [s, c]])
    return tuple(jnp.asarray(t, BF16) for t in (stage_a, stage_a_inv, stage_b, stage_b_inv))


def _fft_stage_a(y_ref, ma_ref, a_ref):
    half = FFT_N2 // 2

    def body(n1, carry):
        slab = y_ref[pl.ds(n1, half, stride=FFT_N1), :]
        a_ref[pl.ds(pl.multiple_of(n1 * FFT_A_PITCH, 8), 2 * FFT_N2), :] = _dot(ma_ref[n1], slab.astype(BF16))
        return carry

    lax.fori_loop(0, FFT_N1, body, 0, unroll=FFT_UNROLL)


def _fft_stage_b(a_ref, mb_ref, consume):
    def body(j, carry):
        k2 = j * FFT_GROUP
        cols = []
        for g in range(FFT_GROUP):
            re = a_ref[pl.ds(k2 + g, FFT_N1, stride=FFT_A_PITCH), :]
            im = a_ref[pl.ds(FFT_N2 + k2 + g, FFT_N1, stride=FFT_A_PITCH), :]
            cols.append(jnp.concatenate([re, im], axis=0))
        x = _dot(mb_ref[...], jnp.concatenate(cols, axis=1).astype(BF16))
        for g in range(FFT_GROUP):
            consume(k2 + g, x[:, g * FFT_LANES:(g + 1) * FFT_LANES])
        return carry

    lax.fori_loop(0, FFT_N2 // FFT_GROUP, body, 0, unroll=FFT_UNROLL)


def _fft_inverse(z_ref, mbi_ref, mai_ref, u_ref, out_ref):
    half = FFT_N2 // 2

    def stage_b(j, carry):
        k2 = j * FFT_GROUP
        rhs = jnp.concatenate([z_ref[k2 + g] for g in range(FFT_GROUP)], axis=1)
        u = _dot(mbi_ref[...], rhs)
        for g in range(FFT_GROUP):
            u_ref[pl.ds(pl.multiple_of((k2 + g) * FFT_U_PITCH, 8), 2 * FFT_N1), :] = (
                u[:, g * FFT_LANES:(g + 1) * FFT_LANES])
        return carry

    lax.fori_loop(0, FFT_N2 // FFT_GROUP, stage_b, 0, unroll=FFT_UNROLL)

    def stage_a(n1, carry):
        re = u_ref[pl.ds(n1, FFT_N2, stride=FFT_U_PITCH), :]
        im = u_ref[pl.ds(FFT_N1 + n1, FFT_N2, stride=FFT_U_PITCH), :]
        rhs = jnp.concatenate([re, im], axis=0).astype(BF16)
        out_ref[pl.ds(n1, half, stride=FFT_N1), :] = _dot(mai_ref[n1], rhs)
        return carry

    lax.fori_loop(0, FFT_N1, stage_a, 0, unroll=FFT_UNROLL)


FFT_WORK_ROWS = max(FFT_N1 * FFT_A_PITCH, FFT_N2 * FFT_U_PITCH)


def _hyena_spectrum_kernel(hf0_ref, hf1_ref, hb0_ref, hb1_ref, ma_ref, mb_ref, k_ref, y_ref, a_ref):
    for o, (hf_ref, hb_ref) in enumerate(((hf0_ref, hb0_ref), (hf1_ref, hb1_ref))):
        hs, hd, inv_norm = _filter_halves(hf_ref, hb_ref)

        def keep_real(k2, x):
            k_ref[o, k2, 0:FFT_N1, :] = (x[:FFT_N1] * inv_norm).astype(k_ref.dtype)

        def keep_imag(k2, x):
            k_ref[o, k2, FFT_N1:, :] = (x[FFT_N1:] * inv_norm).astype(k_ref.dtype)

        for part, keep in ((hs, keep_real), (hd, keep_imag)):
            y_ref[...] = part
            _fft_stage_a(y_ref, ma_ref, a_ref)
            _fft_stage_b(a_ref, mb_ref, keep)


def _hyena_two_stage_kernel(zv_ref, z1_ref, z2_ref, wv_ref, w1_ref, w2_ref, k_ref, skip_ref,
                            ma_ref, mai_ref, mb_ref, mbi_ref, o_ref, y_ref, c_ref, a_ref, z_ref):
    y_ref[...] = _short_conv(zv_ref[0], wv_ref[...])
    for o, (g_ref, w_ref) in enumerate(((z1_ref, w1_ref), (z2_ref, w2_ref))):
        def multiply(k2, x):
            k = k_ref[o, k2].astype(F32)
            xr, xi = x[:FFT_N1], x[FFT_N1:]
            kr, ki = k[:FFT_N1], k[FFT_N1:]
            z_ref[k2] = jnp.concatenate([xr * kr - xi * ki, xr * ki + xi * kr], axis=0).astype(z_ref.dtype)

        _fft_stage_a(y_ref, ma_ref, a_ref)
        _fft_stage_b(a_ref, mb_ref, multiply)
        _fft_inverse(z_ref, mbi_ref, mai_ref, a_ref, c_ref)
        y = _short_conv(g_ref[0], w_ref[...]) * (c_ref[...] + y_ref[...] * skip_ref[o:o + 1])
        if o + 1 < HY_ORDER:
            y_ref[...] = y
        else:
            o_ref[0] = y.astype(o_ref.dtype)


def _hyena_two_stage(z, w_short, hfilt, skip, first=0, bsz=None):
    _, length, d3 = z.shape
    bsz = z.shape[0] if bsz is None else bsz
    d = d3 // 3
    dt = FFT_LANES
    nct = d // dt
    assert 2 * length == FFT_N1 * FFT_N2
    ma, mai, mb, mbi = _two_stage_tables()
    once = pl.Buffered(1)
    hspec = lambda direction, order: pl.BlockSpec(
        (length, dt), lambda c: (0, (direction * HY_ORDER + order) * nct + c))
    spectrum = pl.pallas_call(
        _hyena_spectrum_kernel,
        grid=(nct,),
        in_specs=[hspec(0, 0), hspec(0, 1), hspec(1, 0), hspec(1, 1),
                  pl.BlockSpec(ma.shape, lambda c: (0, 0, 0), pipeline_mode=once),
                  pl.BlockSpec(mb.shape, lambda c: (0, 0), pipeline_mode=once)],
        out_specs=pl.BlockSpec((HY_ORDER, FFT_N2, 2 * FFT_N1, dt), lambda c: (0, 0, 0, c)),
        out_shape=jax.ShapeDtypeStruct((HY_ORDER, FFT_N2, 2 * FFT_N1, d), BF16),
        scratch_shapes=[pltpu.VMEM((length, dt), F32), pltpu.VMEM((FFT_WORK_ROWS, dt), F32)],
        compiler_params=_cparams("parallel"),
        name="hyena_filter_spectrum",
    )(hfilt, hfilt, hfilt, hfilt, ma, mb)

    zspec = lambda part: pl.BlockSpec((1, length, dt), lambda c, b: (b + first, 0, part * nct + c))
    wspec = lambda part: pl.BlockSpec((3, dt), lambda c, b: (0, part * nct + c))
    fixed3 = lambda c, b: (0, 0, 0)
    fixed2 = lambda c, b: (0, 0)
    return pl.pallas_call(
        _hyena_two_stage_kernel,
        grid=(nct, bsz),
        in_specs=[zspec(0), zspec(1), zspec(2), wspec(0), wspec(1), wspec(2),
                  pl.BlockSpec((HY_ORDER, FFT_N2, 2 * FFT_N1, dt), lambda c, b: (0, 0, 0, c), pipeline_mode=once),
                  pl.BlockSpec((HY_ORDER, dt), lambda c, b: (0, c)),
                  pl.BlockSpec(ma.shape, fixed3, pipeline_mode=once),
                  pl.BlockSpec(mai.shape, fixed3, pipeline_mode=once),
                  pl.BlockSpec(mb.shape, fixed2, pipeline_mode=once),
                  pl.BlockSpec(mbi.shape, fixed2, pipeline_mode=once)],
        out_specs=pl.BlockSpec((1, length, dt), lambda c, b: (b, 0, c)),
        out_shape=jax.ShapeDtypeStruct((bsz, length, d), BF16),
        scratch_shapes=[pltpu.VMEM((length, dt), F32), pltpu.VMEM((length, dt), F32),
                        pltpu.VMEM((FFT_WORK_ROWS, dt), F32), pltpu.VMEM((FFT_N2, 2 * FFT_N1, dt), BF16)],
        compiler_params=_cparams("parallel", "arbitrary"),
        name="hyena_conv_two_stage",
    )(z, z, z, w_short, w_short, w_short, spectrum, skip, ma, mai, mb, mbi)


N_HEADS = 16
HEAD_DIM = D_MODEL // N_HEADS
HEADS_PER_STEP = 2
GRID_W = 64
WIN_ROWS = 8
WIN_COLS = 16
NEG_INF = -1e30
NAT_Q_ROWS = 4
NAT_K_ROWS = NAT_Q_ROWS + WIN_ROWS - 1
ATTN_SCALE = HEAD_DIM ** -0.5


def _dot_nt(a, b):
    return lax.dot_general(a, b, (((1,), (1,)), ((), ())), preferred_element_type=F32)


def _qkv_kernel(x_ref, g_ref, sc_ref, sh_ref, w_ref, q_ref, k_ref, v_ref):
    h = _norm_mod(x_ref[...], g_ref[...], sc_ref[0], sh_ref[0])
    qkv = _dot(h.astype(BF16), w_ref[...])
    for part, ref in enumerate((q_ref, k_ref, v_ref)):
        for head in range(N_HEADS):
            lo = part * D_MODEL + head * HEAD_DIM
            ref[0, head] = qkv[:, lo:lo + HEAD_DIM].astype(ref.dtype)


def _qkv_proj(x, g, mods, w_bf16, seg, first_row, n_seq, seq_len, kv_dtype):
    d = x.shape[1]
    n_ctx, smp_len = seg
    t = ROW_TILE
    first_tile = first_row // t
    per_seq = seq_len // t
    out_spec = pl.BlockSpec((1, N_HEADS, t, HEAD_DIM), lambda i: (i // per_seq, 0, i % per_seq, 0))
    shape = (n_seq, N_HEADS, seq_len, HEAD_DIM)
    return pl.pallas_call(
        _qkv_kernel,
        grid=(n_seq * per_seq,),
        in_specs=[
            pl.BlockSpec((t, d), lambda i: (i + first_tile, 0)),
            pl.BlockSpec((1, d), lambda i: (0, 0)),
            _mod_spec(1, t, n_ctx, smp_len, first_tile),
            _mod_spec(0, t, n_ctx, smp_len, first_tile),
            pl.BlockSpec(w_bf16.shape, lambda i: (0, 0)),
        ],
        out_specs=[out_spec, out_spec, out_spec],
        out_shape=[jax.ShapeDtypeStruct(shape, BF16), jax.ShapeDtypeStruct(shape, kv_dtype),
                   jax.ShapeDtypeStruct(shape, kv_dtype)],
        compiler_params=_cparams("parallel"),
        name="qkv_proj",
    )(x, g.reshape(1, d), mods, mods, w_bf16)


def _ctx_attn_kernel(q_ref, k_ref, v_ref, o_ref):
    for j in range(HEADS_PER_STEP):
        q = q_ref[0, j]
        k = k_ref[0, j].astype(BF16)
        v = v_ref[0, j].astype(BF16)
        s = _dot_nt(q, k) * ATTN_SCALE
        p = jnp.exp(s - jnp.max(s, axis=-1, keepdims=True))
        o = _dot(p.astype(BF16), v) / jnp.sum(p, axis=-1, keepdims=True)
        o_ref[:, j * HEAD_DIM:(j + 1) * HEAD_DIM] = o.astype(o_ref.dtype)


def _ctx_attention(q, k, v):
    bsz, _, s, _ = q.shape
    hp = HEADS_PER_STEP
    spec = pl.BlockSpec((1, hp, s, HEAD_DIM), lambda b, h: (b, h, 0, 0))
    return pl.pallas_call(
        _ctx_attn_kernel,
        grid=(bsz, N_HEADS // hp),
        in_specs=[spec, spec, spec],
        out_specs=pl.BlockSpec((s, hp * HEAD_DIM), lambda b, h: (b, h)),
        out_shape=jax.ShapeDtypeStruct((bsz * s, D_MODEL), BF16),
        compiler_params=_cparams("parallel", "parallel"),
        name="ctx_attention",
    )(q, k, v)


def _rpb_toeplitz_kernel(r_ref, e_ref, o_ref):
    o_ref[...] = _dot3(r_ref[...], e_ref[...])


def _nat_bias(rpb):
    n_heads, n_r, n_c = rpb.shape
    n_cp = 32
    qc = np.arange(GRID_W)[:, None]
    kc = np.arange(GRID_W)[None, :]
    onehot = (np.clip(kc - qc + WIN_COLS - 1, 0, n_c - 1)[None] == np.arange(n_cp)[:, None, None])
    onehot = jnp.asarray(onehot.reshape(n_cp, GRID_W * GRID_W), F32)
    rows = jnp.pad(rpb.reshape(n_heads * n_r, n_c), ((0, 0), (0, n_cp - n_c)))
    toep = pl.pallas_call(
        _rpb_toeplitz_kernel,
        out_shape=jax.ShapeDtypeStruct((n_heads * n_r, GRID_W * GRID_W), F32),
        compiler_params=_cparams(),
        name="nat_bias_toeplitz",
    )(rows, onehot).reshape(n_heads, n_r, GRID_W, GRID_W)
    q_start = np.clip(qc - WIN_COLS // 2, 0, GRID_W - WIN_COLS)
    col_ok = jnp.asarray((kc >= q_start) & (kc < q_start + WIN_COLS))
    toep = jnp.where(col_ok, toep, NEG_INF)
    toep = jnp.concatenate([toep, jnp.full((n_heads, 1, GRID_W, GRID_W), NEG_INF, F32)], axis=1)
    rows_total = GRID_W
    idx = np.full((3, NAT_Q_ROWS, NAT_K_ROWS), n_r, np.int32)
    for case, r0 in enumerate((0, NAT_Q_ROWS, rows_total - NAT_Q_ROWS)):
        ks = int(np.clip(r0 - WIN_ROWS // 2, 0, rows_total - NAT_K_ROWS))
        for dr in range(NAT_Q_ROWS):
            r = r0 + dr
            rs = int(np.clip(r - WIN_ROWS // 2, 0, rows_total - WIN_ROWS))
            for dk in range(NAT_K_ROWS):
                kr = ks + dk
                if rs <= kr < rs + WIN_ROWS:
                    idx[case, dr, dk] = kr - r + WIN_ROWS - 1
    blocks = toep[:, jnp.asarray(idx)]
    blocks = jnp.transpose(blocks, (0, 1, 2, 4, 3, 5))
    return blocks.reshape(n_heads, 3, NAT_Q_ROWS * GRID_W, NAT_K_ROWS * GRID_W)


def _nat_kernel(q_ref, k_ref, v_ref, kc_ref, vc_ref, bias_ref, o_ref):
    n_blocks = q_ref.shape[2] // (NAT_Q_ROWS * GRID_W)
    rows_total = q_ref.shape[2] // GRID_W
    nq = NAT_Q_ROWS * GRID_W
    nk = NAT_K_ROWS * GRID_W
    for j in range(HEADS_PER_STEP):
        k_ctx = kc_ref[0, 0, j].astype(BF16)
        v_ctx = vc_ref[0, 0, j].astype(BF16)

        def block(blk, carry):
            ks = jnp.clip(blk * NAT_Q_ROWS - WIN_ROWS // 2, 0, rows_total - NAT_K_ROWS)
            case = jnp.where(blk == 0, 0, jnp.where(blk == n_blocks - 1, 2, 1))
            q = q_ref[0, j, pl.ds(pl.multiple_of(blk * nq, nq), nq), :]
            k_loc = k_ref[0, j, pl.ds(pl.multiple_of(ks * GRID_W, GRID_W), nk), :]
            v_loc = v_ref[0, j, pl.ds(pl.multiple_of(ks * GRID_W, GRID_W), nk), :]
            s_loc = _dot_nt(q, k_loc) * ATTN_SCALE + bias_ref[j, case]
            s_ctx = _dot_nt(q, k_ctx) * ATTN_SCALE
            m = jnp.maximum(jnp.max(s_loc, axis=-1, keepdims=True), jnp.max(s_ctx, axis=-1, keepdims=True))
            p_loc = jnp.exp(s_loc - m)
            p_ctx = jnp.exp(s_ctx - m)
            denom = jnp.sum(p_loc, axis=-1, keepdims=True) + jnp.sum(p_ctx, axis=-1, keepdims=True)
            o = (_dot(p_loc.astype(BF16), v_loc) + _dot(p_ctx.astype(BF16), v_ctx)) / denom
            o_ref[pl.ds(pl.multiple_of(blk * nq, nq), nq), j * HEAD_DIM:(j + 1) * HEAD_DIM] = o.astype(o_ref.dtype)
            return carry

        lax.fori_loop(0, n_blocks, block, 0)


def _nat_attention(q, k, v, cache_k, cache_v, bias):
    bsz, _, length, _ = q.shape
    hp = HEADS_PER_STEP
    past = cache_k.shape[3]
    spec = pl.BlockSpec((1, hp, length, HEAD_DIM), lambda b, h: (b, h, 0, 0))
    cspec = pl.BlockSpec((1, 1, hp, past, HEAD_DIM), lambda b, h: (b, 0, h, 0, 0))
    return pl.pallas_call(
        _nat_kernel,
        grid=(bsz, N_HEADS // hp),
        in_specs=[spec, spec, spec, cspec, cspec,
                  pl.BlockSpec((hp,) + bias.shape[1:], lambda b, h: (h, 0, 0, 0))],
        out_specs=pl.BlockSpec((length, hp * HEAD_DIM), lambda b, h: (b, h)),
        out_shape=jax.ShapeDtypeStruct((bsz * length, D_MODEL), BF16),
        compiler_params=_cparams("parallel", "parallel"),
        name="nat_attention",
    )(q, k, v, cache_k, cache_v, bias)


def kernel(x_prompt, x_sample, cache_k, cache_v, c, c_ctx, ada_w, ada_b, norm1_g, norm2_g, normf_g, hy_w_in, hy_w_short, hy_f_w1, hy_f_b1, hy_f_freq, hy_f_w2, hy_f_b2, hy_f_w3, hy_skip, hy_w_out, na_w_qkv, na_rpb, na_w_o, moe_w_router, moe_b_router, moe_w_gate, moe_b_gate, moe_w_up, moe_b_up, moe_w_down, moe_b_down):
    b, s, d = x_prompt.shape
    bd, sd, _ = x_sample.shape
    n_ctx = b * s
    n_tok = n_ctx + bd * sd
    assert n_ctx % sd == 0 and s % ROW_TILE == 0 and sd % ROW_TILE == 0
    seg = (n_ctx, sd)
    x = jnp.concatenate([x_prompt.reshape(n_ctx, d), x_sample.reshape(bd * sd, d)], axis=0)
    cond = jnp.zeros((N_COND, d), F32).at[0].set(c_ctx).at[1:1 + bd].set(c)
    mods = _modulation(cond, ada_w, ada_b)

    def moe(x, i, final_norm):
        return _moe_layer(x, norm2_g[i], mods[i], moe_w_router[i], moe_b_router[i], i, moe_w_gate, moe_b_gate,
                          moe_w_up, moe_b_up, moe_w_down, moe_b_down, normf_g, seg, final_norm)

    z = _norm_proj(x, norm1_g[0], mods[0], hy_w_in[0].astype(BF16), seg)
    fargs = (hy_f_w1[0], hy_f_b1[0], hy_f_freq[0], hy_f_w2[0], hy_f_b2[0], hy_f_w3[0])
    y_ctx = _hyena_direct(z.reshape(n_tok // s, s, 3 * d), hy_w_short[0], _hyena_filters(s, *fargs),
                          hy_skip[0], first=0, bsz=b)
    y_smp = _hyena_two_stage(z.reshape(n_tok // sd, sd, 3 * d), hy_w_short[0], _hyena_filters(sd, *fargs),
                             hy_skip[0], first=n_ctx // sd, bsz=bd)
    y = jnp.concatenate([y_ctx.reshape(n_ctx, d), y_smp.reshape(bd * sd, d)], axis=0)
    x = _out_proj(y, hy_w_out[0].astype(BF16), x, mods[0], seg)
    x = moe(x, 0, False)

    w_qkv = na_w_qkv[0].astype(BF16)
    q_c, k_c, v_c = _qkv_proj(x, norm1_g[1], mods[1], w_qkv, seg, 0, b, s, F32)
    q_s, k_s, v_s = _qkv_proj(x, norm1_g[1], mods[1], w_qkv, seg, n_ctx, bd, sd, BF16)
    o_ctx = _ctx_attention(q_c, k_c, v_c)
    o_smp = _nat_attention(q_s, k_s, v_s, cache_k, cache_v, _nat_bias(na_rpb[0]))
    x = _out_proj(jnp.concatenate([o_ctx, o_smp], axis=0), na_w_o[0].astype(BF16), x, mods[1], seg)
    x = moe(x, 1, True)

    nh, hd = k_c.shape[1], k_c.shape[3]
    return (x[:n_ctx].reshape(b, s, d), x[n_ctx:].reshape(bd, sd, d),
            k_c.reshape(b, 1, nh, s, hd), v_c.reshape(b, 1, nh, s, hd))
```

```python
import functools
import math

import numpy as np
import jax
import jax.numpy as jnp
from jax import lax
from jax.experimental import pallas as pl
from jax.experimental.pallas import tpu as pltpu

F32 = jnp.float32
BF16 = jnp.bfloat16

D_MODEL = 1024
N_MOD = 6
RMS_EPS = 1e-6
N_EXPERTS = 32
TOP_K = 4
SWIGLU_LIMIT = 7.0
SWIGLU_ALPHA = 1.702

N_COND = 8
ROW_TILE = 256
MOE_TILE = 256
V7X_VMEM_LIMIT = 56 * 1024 * 1024


def _cparams(*sem, vmem=V7X_VMEM_LIMIT):
    return pltpu.CompilerParams(dimension_semantics=sem, vmem_limit_bytes=vmem)


def _dot(a, b):
    return jnp.dot(a, b, preferred_element_type=F32)


def _split_bf16(x):
    hi = x.astype(BF16)
    lo = (x - hi.astype(F32)).astype(BF16)
    return hi, lo


def _dot3(a, b):
    ah, al = _split_bf16(a)
    bh, bl = _split_bf16(b)
    return _dot(ah, bh) + (_dot(al, bh) + _dot(ah, bl))


def _seg_of_tile(i, tile, n_ctx, smp_len):
    ctx_tiles = n_ctx // tile
    per_smp = smp_len // tile
    return jnp.where(i < ctx_tiles, 0, 1 + (i - ctx_tiles) // per_smp)


def _norm_mod(x, g, sc, sh):
    y = x * lax.rsqrt(jnp.mean(x * x, axis=-1, keepdims=True) + RMS_EPS)
    return (y * g) * (1.0 + sc) + sh


def _mod_kernel(c_ref, w_ref, b_ref, o_ref):
    c = c_ref[...]
    a = c * jax.nn.sigmoid(c)
    o_ref[0] = _dot3(a, w_ref[0]) + b_ref[0]


def _modulation(cond, ada_w, ada_b):
    depth, d, n_out = ada_w.shape
    tn = 1536
    m = pl.pallas_call(
        _mod_kernel,
        grid=(depth, n_out // tn),
        in_specs=[
            pl.BlockSpec((N_COND, d), lambda l, j: (0, 0)),
            pl.BlockSpec((1, d, tn), lambda l, j: (l, 0, j)),
            pl.BlockSpec((1, 1, tn), lambda l, j: (l, 0, j)),
        ],
        out_specs=pl.BlockSpec((1, N_COND, tn), lambda l, j: (l, 0, j)),
        out_shape=jax.ShapeDtypeStruct((depth, N_COND, n_out), F32),
        compiler_params=_cparams("arbitrary", "arbitrary"),
        name="adaln_modulation",
    )(cond, ada_w, ada_b.reshape(depth, 1, n_out))
    m = m.reshape(depth, N_COND, N_MOD, d)
    return jnp.transpose(m, (0, 2, 1, 3)).reshape(depth, N_MOD * N_COND, 1, d)


def _mod_spec(which, tile, n_ctx, smp_len, first_tile=0):
    return pl.BlockSpec(
        (1, 1, D_MODEL),
        lambda i, *_: (which * N_COND + _seg_of_tile(i + first_tile, tile, n_ctx, smp_len), 0, 0))


def _norm_proj_kernel(x_ref, g_ref, sc_ref, sh_ref, w_ref, o_ref):
    h = _norm_mod(x_ref[...], g_ref[...], sc_ref[0], sh_ref[0])
    o_ref[...] = _dot(h.astype(BF16), w_ref[...]).astype(o_ref.dtype)


def _norm_proj(x, g, mods, w_bf16, seg, out_dtype=F32):
    n, d = x.shape
    n_out = w_bf16.shape[1]
    n_ctx, smp_len = seg
    return pl.pallas_call(
        _norm_proj_kernel,
        grid=(n // ROW_TILE,),
        in_specs=[
            pl.BlockSpec((ROW_TILE, d), lambda i: (i, 0)),
            pl.BlockSpec((1, d), lambda i: (0, 0)),
            _mod_spec(1, ROW_TILE, n_ctx, smp_len),
            _mod_spec(0, ROW_TILE, n_ctx, smp_len),
            pl.BlockSpec((d, n_out), lambda i: (0, 0)),
        ],
        out_specs=pl.BlockSpec((ROW_TILE, n_out), lambda i: (i, 0)),
        out_shape=jax.ShapeDtypeStruct((n, n_out), out_dtype),
        compiler_params=_cparams("parallel"),
        name="norm_proj",
    )(x, g.reshape(1, d), mods, mods, w_bf16)


def _out_proj_kernel(ya_ref, yb_ref, w_ref, x_ref, gate_ref, o_ref, *, ctx_tiles):
    y = jnp.where(pl.program_id(0) < ctx_tiles, ya_ref[...], yb_ref[...])
    o_ref[...] = x_ref[...] + gate_ref[0] * _dot(y, w_ref[...])


def _out_proj(y_ctx, y_smp, w_bf16, x, mods, seg):
    n, d = x.shape
    n_ctx, smp_len = seg
    ctx_tiles = n_ctx // ROW_TILE
    return pl.pallas_call(
        functools.partial(_out_proj_kernel, ctx_tiles=ctx_tiles),
        grid=(n // ROW_TILE,),
        in_specs=[
            pl.BlockSpec((ROW_TILE, d), lambda i: (jnp.minimum(i, ctx_tiles - 1), 0)),
            pl.BlockSpec((ROW_TILE, d), lambda i: (jnp.maximum(i - ctx_tiles, 0), 0)),
            pl.BlockSpec((d, d), lambda i: (0, 0)),
            pl.BlockSpec((ROW_TILE, d), lambda i: (i, 0)),
            _mod_spec(2, ROW_TILE, n_ctx, smp_len),
        ],
        out_specs=pl.BlockSpec((ROW_TILE, d), lambda i: (i, 0)),
        out_shape=jax.ShapeDtypeStruct((n, d), F32),
        compiler_params=_cparams("parallel"),
        name="out_proj",
    )(y_ctx, y_smp, w_bf16, x, mods)


LANES = 128
ROW_SUBLANES = D_MODEL // LANES


def _store_row_tiles(ref, x):
    t = x.shape[0]
    for s in range(ROW_SUBLANES):
        ref[pl.ds(s, t, stride=ROW_SUBLANES), :] = x[:, s * LANES:(s + 1) * LANES]


def _load_row_tiles(ref, t):
    return jnp.concatenate([ref[pl.ds(s, t, stride=ROW_SUBLANES), :] for s in range(ROW_SUBLANES)], axis=1)


def _columns(cols):
    t = cols[0].shape[0]
    lane = lax.broadcasted_iota(jnp.int32, (t, len(cols)), 1)
    out = jnp.broadcast_to(cols[-1], (t, len(cols)))
    for k in range(len(cols) - 2, -1, -1):
        out = jnp.where(lane == k, cols[k], out)
    return out


def _router_kernel(x_ref, g_ref, sc_ref, sh_ref, wr_ref, br_ref, tri_ref,
                   h_ref, e_ref, gate_ref, rank_ref, cnt_ref, run_ref):
    i = pl.program_id(0)

    @pl.when(i == 0)
    def _():
        run_ref[...] = jnp.zeros_like(run_ref)

    h = _norm_mod(x_ref[...], g_ref[...], sc_ref[0], sh_ref[0])
    _store_row_tiles(h_ref, h)
    logits = _dot3(h, wr_ref[...]) + br_ref[...]
    lane = lax.broadcasted_iota(jnp.int32, logits.shape, 1)
    work = logits
    vals, idxs, hots = [], [], []
    for _ in range(TOP_K):
        m = jnp.max(work, axis=-1, keepdims=True)
        idx = jnp.min(jnp.where(work == m, lane, N_EXPERTS), axis=-1, keepdims=True)
        hot = lane == idx
        vals.append(m)
        idxs.append(idx)
        hots.append(hot)
        work = jnp.where(hot, -jnp.inf, work)
    ex = [jnp.exp(v - vals[0]) for v in vals]
    denom = ex[0] + ex[1] + ex[2] + ex[3]
    gate_ref[...] = _columns([e / denom for e in ex])
    e_ref[...] = _columns(idxs)

    chosen = (hots[0] | hots[1] | hots[2] | hots[3]).astype(F32)
    before = run_ref[...] + _dot(tri_ref[...], chosen.astype(BF16))
    ranks = [jnp.sum(jnp.where(hot, before, 0.0), axis=-1, keepdims=True) for hot in hots]
    rank_ref[...] = _columns(ranks).astype(jnp.int32)
    run_ref[...] += jnp.sum(chosen, axis=0, keepdims=True)
    cnt_ref[...] = run_ref[...].astype(jnp.int32)


def _router(x, g, mods, w_r, b_r, seg):
    n, d = x.shape
    n_ctx, smp_len = seg
    t = ROW_TILE
    tri = jnp.asarray(np.tril(np.ones((t, t), np.float32), -1), BF16)
    tok4 = lambda i: (i, 0)
    return pl.pallas_call(
        _router_kernel,
        grid=(n // t,),
        in_specs=[
            pl.BlockSpec((t, d), lambda i: (i, 0)),
            pl.BlockSpec((1, d), lambda i: (0, 0)),
            _mod_spec(4, t, n_ctx, smp_len),
            _mod_spec(3, t, n_ctx, smp_len),
            pl.BlockSpec((d, N_EXPERTS), lambda i: (0, 0)),
            pl.BlockSpec((1, N_EXPERTS), lambda i: (0, 0)),
            pl.BlockSpec((t, t), lambda i: (0, 0)),
        ],
        out_specs=[
            pl.BlockSpec((t * ROW_SUBLANES, LANES), tok4),
            pl.BlockSpec((t, TOP_K), tok4),
            pl.BlockSpec((t, TOP_K), tok4),
            pl.BlockSpec((t, TOP_K), tok4),
            pl.BlockSpec((1, N_EXPERTS), lambda i: (0, 0)),
        ],
        out_shape=[
            jax.ShapeDtypeStruct((n * ROW_SUBLANES, LANES), F32),
            jax.ShapeDtypeStruct((n, TOP_K), jnp.int32),
            jax.ShapeDtypeStruct((n, TOP_K), F32),
            jax.ShapeDtypeStruct((n, TOP_K), jnp.int32),
            jax.ShapeDtypeStruct((1, N_EXPERTS), jnp.int32),
        ],
        scratch_shapes=[pltpu.VMEM((1, N_EXPERTS), F32)],
        compiler_params=_cparams("arbitrary"),
        name="moe_router",
    )(x, g.reshape(1, d), mods, mods, w_r, b_r.reshape(1, N_EXPERTS), tri)


def _tile_copy(src_ref, src_row, dst_ref, dst_row, sem):
    rows = lambda r: pl.ds(pl.multiple_of(r, ROW_SUBLANES), ROW_SUBLANES)
    return pltpu.make_async_copy(src_ref.at[rows(src_row), :], dst_ref.at[rows(dst_row), :], sem)


def _ffn_kernel(be_ref, na_ref, src_ref, src_next_ref, dst_ref, h_ref,
                wg_ref, bg_ref, wu_ref, bu_ref, wd_ref, bd_ref,
                out_ref, wg_s, wu_s, wd_s, xbuf, obuf, sem_in, sem_out):
    i = pl.program_id(0)
    last = pl.num_programs(0) - 1
    n_active = na_ref[0]
    slot = lax.rem(i, 2)
    tm = xbuf.shape[1] // ROW_SUBLANES

    def start_gather(idx_ref, buf):
        def body(r, carry):
            _tile_copy(h_ref, idx_ref[r], xbuf.at[buf], r * ROW_SUBLANES, sem_in.at[buf]).start()
            return carry

        lax.fori_loop(0, tm, body, 0, unroll=8)

    def wait_block(buf_ref, sem):
        pltpu.make_async_copy(h_ref.at[pl.ds(0, tm * ROW_SUBLANES), :], buf_ref, sem).wait()

    @pl.when(i == 0)
    def _():
        obuf[1] = jnp.zeros(obuf.shape[1:], obuf.dtype)
        spare = out_ref.at[pl.ds(out_ref.shape[0] - tm * ROW_SUBLANES, tm * ROW_SUBLANES), :]
        fill = pltpu.make_async_copy(obuf.at[1], spare, sem_out.at[1])
        fill.start()
        fill.wait()

    @pl.when(jnp.logical_and(i == 0, n_active > 0))
    def _():
        start_gather(src_ref, 0)

    @pl.when(i + 1 < n_active)
    def _():
        start_gather(src_next_ref, 1 - slot)

    @pl.when(jnp.logical_and(i >= 2, i - 2 < n_active))
    def _():
        wait_block(obuf.at[slot], sem_out.at[slot])

    fresh = jnp.logical_or(i == 0, be_ref[i] != be_ref[jnp.maximum(i - 1, 0)])

    @pl.when(fresh)
    def _():
        wg_s[...] = wg_ref[0, 0].astype(BF16)
        wu_s[...] = wu_ref[0, 0].astype(BF16)
        wd_s[...] = wd_ref[0, 0].astype(BF16)

    @pl.when(i < n_active)
    def _():
        wait_block(xbuf.at[slot], sem_in.at[slot])
        x = _load_row_tiles(xbuf.at[slot], tm).astype(BF16)
        g = jnp.minimum(_dot(x, wg_s[...]) + bg_ref[0, 0], SWIGLU_LIMIT)
        u = jnp.clip(_dot(x, wu_s[...]) + bu_ref[0, 0], -SWIGLU_LIMIT, SWIGLU_LIMIT)
        a = g * jax.nn.sigmoid(SWIGLU_ALPHA * g) * (u + 1.0)
        _store_row_tiles(obuf.at[slot], _dot(a.astype(BF16), wd_s[...]) + bd_ref[0, 0])

        def scatter(r, carry):
            _tile_copy(obuf.at[slot], r * ROW_SUBLANES, out_ref, dst_ref[r], sem_out.at[slot]).start()
            return carry

        lax.fori_loop(0, tm, scatter, 0, unroll=8)

    @pl.when(i == last)
    def _():
        @pl.when(jnp.logical_and(i >= 1, i - 1 < n_active))
        def _():
            wait_block(obuf.at[1 - slot], sem_out.at[1 - slot])

        @pl.when(i < n_active)
        def _():
            wait_block(obuf.at[slot], sem_out.at[slot])


def _expert_ffn(h_tiles, src_rows, dst_rows, block_e, n_active, layer, w_g, b_g, w_u, b_u, w_d, b_d):
    d = D_MODEL
    n = h_tiles.shape[0] // ROW_SUBLANES
    depth, ne, _, f = w_g.shape
    nb = block_e.shape[0]
    tm = MOE_TILE
    wmap = lambda i, be, na: (layer, be[i], 0, 0)
    grid_spec = pltpu.PrefetchScalarGridSpec(
        num_scalar_prefetch=2,
        grid=(nb,),
        in_specs=[
            pl.BlockSpec((tm,), lambda i, *_: (i,), memory_space=pltpu.SMEM),
            pl.BlockSpec((tm,), lambda i, *_: (jnp.minimum(i + 1, nb - 1),), memory_space=pltpu.SMEM),
            pl.BlockSpec((tm,), lambda i, *_: (i,), memory_space=pltpu.SMEM),
            pl.BlockSpec(memory_space=pl.ANY),
            pl.BlockSpec((1, 1, d, f), wmap),
            pl.BlockSpec((1, 1, 1, f), wmap),
            pl.BlockSpec((1, 1, d, f), wmap),
            pl.BlockSpec((1, 1, 1, f), wmap),
            pl.BlockSpec((1, 1, f, d), wmap),
            pl.BlockSpec((1, 1, 1, d), wmap),
        ],
        out_specs=pl.BlockSpec(memory_space=pl.ANY),
        scratch_shapes=[pltpu.VMEM((d, f), BF16), pltpu.VMEM((d, f), BF16), pltpu.VMEM((f, d), BF16),
                        pltpu.VMEM((2, tm * ROW_SUBLANES, LANES), F32),
                        pltpu.VMEM((2, tm * ROW_SUBLANES, LANES), F32),
                        pltpu.SemaphoreType.DMA((2,)), pltpu.SemaphoreType.DMA((2,))],
    )
    return pl.pallas_call(
        _ffn_kernel,
        grid_spec=grid_spec,
        out_shape=jax.ShapeDtypeStruct(((TOP_K * n + tm) * ROW_SUBLANES, LANES), F32),
        compiler_params=_cparams("arbitrary"),
        name="moe_expert_ffn",
    )(block_e, n_active, src_rows, src_rows, dst_rows, h_tiles,
      w_g, b_g.reshape(depth, ne, 1, f), w_u, b_u.reshape(depth, ne, 1, f), w_d, b_d.reshape(depth, ne, 1, d))


def _combine_kernel(x_ref, gate_ref, g2_ref, gf_ref, y0_ref, y1_ref, y2_ref, y3_ref, o_ref, *, final_norm):
    t = x_ref.shape[0]
    gates = gate_ref[...]
    acc = gates[:, 0:1] * _load_row_tiles(y0_ref, t)
    for k, y_ref in enumerate((y1_ref, y2_ref, y3_ref), start=1):
        acc = acc + gates[:, k:k + 1] * _load_row_tiles(y_ref, t)
    y = x_ref[...] + g2_ref[0] * acc
    if final_norm:
        y = y * lax.rsqrt(jnp.mean(y * y, axis=-1, keepdims=True) + RMS_EPS) * gf_ref[...]
    o_ref[...] = y


def _combine(x, ys, gates, mods, normf_g, seg, final_norm, first_row=0, n_rows=None):
    n, d = x.shape
    n_rows = n if n_rows is None else n_rows
    n_ctx, smp_len = seg
    t = ROW_TILE
    first = first_row // t
    rows = lambda i: (i + first, 0)
    plane = lambda k: pl.BlockSpec((t * ROW_SUBLANES, LANES), lambda i: (k * (n // t) + i + first, 0))
    return pl.pallas_call(
        functools.partial(_combine_kernel, final_norm=final_norm),
        grid=(n_rows // t,),
        in_specs=[
            pl.BlockSpec((t, d), rows),
            pl.BlockSpec((t, TOP_K), rows),
            _mod_spec(5, t, n_ctx, smp_len, first),
            pl.BlockSpec((1, d), lambda i: (0, 0)),
        ] + [plane(k) for k in range(TOP_K)],
        out_specs=pl.BlockSpec((t, d), lambda i: (i, 0)),
        out_shape=jax.ShapeDtypeStruct((n_rows, d), F32),
        compiler_params=_cparams("parallel"),
        name="moe_combine",
    )(x, gates, mods, normf_g.reshape(1, d), ys, ys, ys, ys)


def _moe_layer(x, norm_g, mods, w_r, b_r, layer, w_g, b_g, w_u, b_u, w_d, b_d, normf_g, seg, final_norm):
    n, d = x.shape
    nk = n * TOP_K
    h, top_e, gates, rank, counts = _router(x, norm_g, mods, w_r, b_r, seg)
    counts = counts[0]
    padded = (counts + MOE_TILE - 1) // MOE_TILE * MOE_TILE
    pad_end = jnp.cumsum(padded)
    pad_start = pad_end - padded
    cap = nk + N_EXPERTS * MOE_TILE
    nb = cap // MOE_TILE
    experts = jnp.arange(N_EXPERTS, dtype=jnp.int32)
    start_of = jnp.sum(jnp.where(top_e[..., None] == experts, pad_start, 0), axis=-1)
    dest = (start_of + rank).reshape(nk).astype(jnp.int32)
    slot_src = jnp.full((cap,), -1, jnp.int32).at[dest].set(jnp.arange(nk, dtype=jnp.int32))
    blk_start = jnp.arange(nb, dtype=jnp.int32) * MOE_TILE
    block_e = jnp.minimum(jnp.sum(blk_start[:, None] >= pad_end[None, :], axis=1), N_EXPERTS - 1).astype(jnp.int32)
    n_active = (pad_end[-1] // MOE_TILE).astype(jnp.int32).reshape(1)
    entry = jnp.maximum(slot_src, 0)
    src_rows = (entry // TOP_K) * ROW_SUBLANES
    spare = TOP_K * n + jnp.arange(cap, dtype=jnp.int32) % MOE_TILE
    dst_rows = jnp.where(slot_src >= 0, (entry % TOP_K) * n + entry // TOP_K, spare) * ROW_SUBLANES
    ys = _expert_ffn(h, src_rows, dst_rows, block_e, n_active, layer, w_g, b_g, w_u, b_u, w_d, b_d)
    if not final_norm:
        return _combine(x, ys, gates, mods, normf_g, seg, False)
    n_ctx = seg[0]
    return (_combine(x, ys, gates, mods, normf_g, seg, True, 0, n_ctx),
            _combine(x, ys, gates, mods, normf_g, seg, True, n_ctx, n - n_ctx))


HY_ORDER = 2
HY_BANDS = 16
HY_EMB = 1 + 2 * HY_BANDS
HY_FFN = 64
HY_MIN_DECAY = math.log(1e-2) / 1.5
HY_MAX_DECAY = math.log(1e-2) / 0.3
HY_EMB_PAD = 64


def _filter_mlp_kernel(z_ref, t_ref, w1_ref, b1_ref, fr_ref, w2_ref, b2_ref, w3_ref, dl_ref, o_ref):
    fr = fr_ref[...]
    h = jnp.sin(fr * (_dot3(z_ref[...], w1_ref[...]) + b1_ref[...]))
    h = jnp.sin(fr * (_dot3(h, w2_ref[...]) + b2_ref[...]))
    o_ref[...] = _dot3(h, w3_ref[...]) * jnp.exp(-t_ref[...] * dl_ref[...])


def _hyena_filters(length, w1, b1, freq, w2, b2, w3):
    t = jnp.linspace(0.0, 1.0, length, dtype=F32)[:, None]
    ang = (2.0 * math.pi / length) * jnp.arange(length, dtype=F32)[:, None]
    bands = jnp.linspace(1e-4, HY_BANDS - 1, HY_BANDS, dtype=F32)[None, :]
    z = jnp.concatenate([t, jnp.cos(bands * ang), -jnp.sin(bands * ang)], axis=-1)
    z = jnp.pad(z, ((0, 0), (0, HY_EMB_PAD - HY_EMB)))
    w1p = jnp.pad(w1, ((0, HY_EMB_PAD - HY_EMB), (0, 0)))
    n_out = w3.shape[1]
    deltas = jnp.abs(jnp.linspace(HY_MIN_DECAY, HY_MAX_DECAY, D_MODEL, dtype=F32))
    deltas = jnp.tile(deltas, n_out // D_MODEL)[None, :]
    tl, tn = 256, 1024
    row = lambda i, j: (i, 0)
    fixed = lambda i, j: (0, 0)
    return pl.pallas_call(
        _filter_mlp_kernel,
        grid=(length // tl, n_out // tn),
        in_specs=[
            pl.BlockSpec((tl, HY_EMB_PAD), row),
            pl.BlockSpec((tl, 1), row),
            pl.BlockSpec((HY_EMB_PAD, HY_FFN), fixed),
            pl.BlockSpec((1, HY_FFN), fixed),
            pl.BlockSpec((1, HY_FFN), fixed),
            pl.BlockSpec((HY_FFN, HY_FFN), fixed),
            pl.BlockSpec((1, HY_FFN), fixed),
            pl.BlockSpec((HY_FFN, tn), lambda i, j: (0, j)),
            pl.BlockSpec((1, tn), lambda i, j: (0, j)),
        ],
        out_specs=pl.BlockSpec((tl, tn), lambda i, j: (i, j)),
        out_shape=jax.ShapeDtypeStruct((length, n_out), F32),
        compiler_params=_cparams("parallel", "parallel"),
        name="hyena_filter_mlp",
    )(z, t, w1p, b1.reshape(1, -1), freq.reshape(1, -1), w2, b2.reshape(1, -1), w3, deltas)


def _short_conv(z, w):
    length = z.shape[0]
    row = lax.broadcasted_iota(jnp.int32, z.shape, 0)
    prev = jnp.where(row == 0, 0.0, pltpu.roll(z, 1, 0))
    nxt = jnp.where(row == length - 1, 0.0, pltpu.roll(z, length - 1, 0))
    return (prev * w[0:1] + z * w[1:2]) + nxt * w[2:3]


def _filter_halves(hf_ref, hb_ref):
    hf = hf_ref[...]
    hb = hb_ref[...]
    hb = jnp.where(lax.broadcasted_iota(jnp.int32, hb.shape, 0) == 0, 0.0, hb)
    norm = jnp.sum(jnp.abs(hf), axis=0, keepdims=True) + jnp.sum(jnp.abs(hb), axis=0, keepdims=True)
    return hf + hb, hf - hb, 1.0 / norm


def _direct_dft_tables(length):
    n_fft = 2 * length
    n_freq = length + 1
    mf = -(-n_freq // 16) * 16
    k = np.arange(mf)[:, None]
    n = np.arange(length)[None, :]
    ang = 2.0 * np.pi * ((k * n) % n_fft) / n_fft
    valid = k < n_freq
    cos = np.where(valid, np.cos(ang), 0.0)
    msin = np.where(valid, -np.sin(ang), 0.0)
    weight = np.where((k == 0) | (k == length), 1.0, 2.0) * valid / n_fft
    fwd = np.concatenate([cos, msin], axis=0)
    inv = np.concatenate([weight * cos, weight * msin], axis=0).T
    return jnp.asarray(fwd, BF16), jnp.asarray(inv, BF16), mf


def _hyena_direct_kernel(zv_ref, z1_ref, z2_ref, wv_ref, w1_ref, w2_ref,
                         hf0_ref, hf1_ref, hb0_ref, hb1_ref, skip_ref, fw_ref, iv_ref,
                         o_ref, kr_ref, ki_ref):
    mf = fw_ref.shape[0] // 2
    dt = o_ref.shape[-1]

    @pl.when(pl.program_id(1) == 0)
    def _():
        for o, (hf_ref, hb_ref) in enumerate(((hf0_ref, hb0_ref), (hf1_ref, hb1_ref))):
            hs, hd, inv_norm = _filter_halves(hf_ref, hb_ref)
            spec = _dot(fw_ref[...], jnp.concatenate([hs, hd], axis=1).astype(BF16))
            kr_ref[o] = spec[:mf, :dt] * inv_norm
            ki_ref[o] = spec[mf:, dt:] * inv_norm

    y = _short_conv(zv_ref[0], wv_ref[...])
    for o, (z_ref, w_ref) in enumerate(((z1_ref, w1_ref), (z2_ref, w2_ref))):
        spec = _dot(fw_ref[...], y.astype(BF16))
        yr, yi = spec[:mf], spec[mf:]
        kr, ki = kr_ref[o], ki_ref[o]
        prod = jnp.concatenate([yr * kr - yi * ki, yr * ki + yi * kr], axis=0)
        yc = _dot(iv_ref[...], prod.astype(BF16))
        y = _short_conv(z_ref[0], w_ref[...]) * (yc + y * skip_ref[o:o + 1])
    o_ref[0] = y.astype(o_ref.dtype)


def _hyena_direct(z, w_short, hfilt, skip, first=0, bsz=None):
    _, length, d3 = z.shape
    bsz = z.shape[0] if bsz is None else bsz
    d = d3 // 3
    dt = 256
    nct = d // dt
    fwd, inv, mf = _direct_dft_tables(length)
    zspec = lambda part: pl.BlockSpec((1, length, dt), lambda c, b: (b + first, 0, part * nct + c))
    wspec = lambda part: pl.BlockSpec((3, dt), lambda c, b: (0, part * nct + c))
    hspec = lambda direction, order: pl.BlockSpec(
        (length, dt), lambda c, b: (0, (direction * HY_ORDER + order) * nct + c))
    fixed = lambda c, b: (0, 0)
    return pl.pallas_call(
        _hyena_direct_kernel,
        grid=(nct, bsz),
        in_specs=[zspec(0), zspec(1), zspec(2), wspec(0), wspec(1), wspec(2),
                  hspec(0, 0), hspec(0, 1), hspec(1, 0), hspec(1, 1),
                  pl.BlockSpec((HY_ORDER, dt), lambda c, b: (0, c)),
                  pl.BlockSpec(fwd.shape, fixed), pl.BlockSpec(inv.shape, fixed)],
        out_specs=pl.BlockSpec((1, length, dt), lambda c, b: (b, 0, c)),
        out_shape=jax.ShapeDtypeStruct((bsz, length, d), BF16),
        scratch_shapes=[pltpu.VMEM((HY_ORDER, mf, dt), F32), pltpu.VMEM((HY_ORDER, mf, dt), F32)],
        compiler_params=_cparams("parallel", "arbitrary"),
        name="hyena_conv_direct",
    )(z, z, z, w_short, w_short, w_short, hfilt, hfilt, hfilt, hfilt, skip, fwd, inv)


FFT_N1 = 64
FFT_N2 = 128
FFT_LANES = 128
FFT_A_PITCH = 2 * FFT_N2 + 8
FFT_U_PITCH = 2 * FFT_N1 + 8
FFT_GROUP = 2
FFT_UNROLL = 8


def _two_stage_tables():
    n_fft = FFT_N1 * FFT_N2
    half = FFT_N2 // 2
    n1 = np.arange(FFT_N1)[:, None, None]
    k2 = np.arange(FFT_N2)[None, :, None]
    n2 = np.arange(half)[None, None, :]
    ang = 2.0 * np.pi * ((k2 * (n1 + FFT_N1 * n2)) % n_fft) / n_fft
    stage_a = np.concatenate([np.cos(ang), -np.sin(ang)], axis=1)
    stage_a_inv = np.transpose(stage_a, (0, 2, 1)) / n_fft
    k1 = np.arange(FFT_N1)[:, None]
    m1 = np.arange(FFT_N1)[None, :]
    phi = 2.0 * np.pi * ((k1 * m1) % FFT_N1) / FFT_N1
    c, s = np.cos(phi), np.sin(phi)
    stage_b = np.block([[c, s], [-s, c]])
    stage_b_inv = np.block([[c, -s], [s, c]])
    return tuple(jnp.asarray(t, BF16) for t in (stage_a, stage_a_inv, stage_b, stage_b_inv))


def _fft_stage_a(y_ref, ma_ref, a_ref):
    half = FFT_N2 // 2

    def body(n1, carry):
        slab = y_ref[pl.ds(n1, half, stride=FFT_N1), :]
        a_ref[pl.ds(pl.multiple_of(n1 * FFT_A_PITCH, 8), 2 * FFT_N2), :] = _dot(ma_ref[n1], slab.astype(BF16))
        return carry

    lax.fori_loop(0, FFT_N1, body, 0, unroll=FFT_UNROLL)


def _fft_stage_b(a_ref, mb_ref, consume):
    def body(j, carry):
        k2 = j * FFT_GROUP
        cols = []
        for g in range(FFT_GROUP):
            re = a_ref[pl.ds(k2 + g, FFT_N1, stride=FFT_A_PITCH), :]
            im = a_ref[pl.ds(FFT_N2 + k2 + g, FFT_N1, stride=FFT_A_PITCH), :]
            cols.append(jnp.concatenate([re, im], axis=0))
        x = _dot(mb_ref[...], jnp.concatenate(cols, axis=1).astype(BF16))
        for g in range(FFT_GROUP):
            consume(k2 + g, x[:, g * FFT_LANES:(g + 1) * FFT_LANES])
        return carry

    lax.fori_loop(0, FFT_N2 // FFT_GROUP, body, 0, unroll=FFT_UNROLL)


def _fft_inverse(z_ref, mbi_ref, mai_ref, u_ref, out_ref):
    half = FFT_N2 // 2

    def stage_b(j, carry):
        k2 = j * FFT_GROUP
        rhs = jnp.concatenate([z_ref[k2 + g] for g in range(FFT_GROUP)], axis=1)
        u = _dot(mbi_ref[...], rhs)
        for g in range(FFT_GROUP):
            u_ref[pl.ds(pl.multiple_of((k2 + g) * FFT_U_PITCH, 8), 2 * FFT_N1), :] = (
                u[:, g * FFT_LANES:(g + 1) * FFT_LANES])
        return carry

    lax.fori_loop(0, FFT_N2 // FFT_GROUP, stage_b, 0, unroll=FFT_UNROLL)

    def stage_a(n1, carry):
        re = u_ref[pl.ds(n1, FFT_N2, stride=FFT_U_PITCH), :]
        im = u_ref[pl.ds(FFT_N1 + n1, FFT_N2, stride=FFT_U_PITCH), :]
        rhs = jnp.concatenate([re, im], axis=0).astype(BF16)
        out_ref[pl.ds(n1, half, stride=FFT_N1), :] = _dot(mai_ref[n1], rhs)
        return carry

    lax.fori_loop(0, FFT_N1, stage_a, 0, unroll=FFT_UNROLL)


FFT_WORK_ROWS = max(FFT_N1 * FFT_A_PITCH, FFT_N2 * FFT_U_PITCH)


def _hyena_spectrum_kernel(hf0_ref, hf1_ref, hb0_ref, hb1_ref, ma_ref, mb_ref, k_ref, y_ref, a_ref):
    for o, (hf_ref, hb_ref) in enumerate(((hf0_ref, hb0_ref), (hf1_ref, hb1_ref))):
        hs, hd, inv_norm = _filter_halves(hf_ref, hb_ref)

        def keep_real(k2, x):
            k_ref[o, k2, 0:FFT_N1, :] = (x[:FFT_N1] * inv_norm).astype(k_ref.dtype)

        def keep_imag(k2, x):
            k_ref[o, k2, FFT_N1:, :] = (x[FFT_N1:] * inv_norm).astype(k_ref.dtype)

        for part, keep in ((hs, keep_real), (hd, keep_imag)):
            y_ref[...] = part
            _fft_stage_a(y_ref, ma_ref, a_ref)
            _fft_stage_b(a_ref, mb_ref, keep)


def _hyena_two_stage_kernel(zv_ref, z1_ref, z2_ref, wv_ref, w1_ref, w2_ref, k_ref, skip_ref,
                            ma_ref, mai_ref, mb_ref, mbi_ref, o_ref, y_ref, c_ref, a_ref, z_ref):
    y_ref[...] = _short_conv(zv_ref[0], wv_ref[...])
    for o, (g_ref, w_ref) in enumerate(((z1_ref, w1_ref), (z2_ref, w2_ref))):
        def multiply(k2, x):
            k = k_ref[o, k2].astype(F32)
            xr, xi = x[:FFT_N1], x[FFT_N1:]
            kr, ki = k[:FFT_N1], k[FFT_N1:]
            z_ref[k2] = jnp.concatenate([xr * kr - xi * ki, xr * ki + xi * kr], axis=0).astype(z_ref.dtype)

        _fft_stage_a(y_ref, ma_ref, a_ref)
        _fft_stage_b(a_ref, mb_ref, multiply)
        _fft_inverse(z_ref, mbi_ref, mai_ref, a_ref, c_ref)
        y = _short_conv(g_ref[0], w_ref[...]) * (c_ref[...] + y_ref[...] * skip_ref[o:o + 1])
        if o + 1 < HY_ORDER:
            y_ref[...] = y
        else:
            o_ref[0] = y.astype(o_ref.dtype)


def _hyena_two_stage(z, w_short, hfilt, skip, first=0, bsz=None):
    _, length, d3 = z.shape
    bsz = z.shape[0] if bsz is None else bsz
    d = d3 // 3
    dt = FFT_LANES
    nct = d // dt
    assert 2 * length == FFT_N1 * FFT_N2
    ma, mai, mb, mbi = _two_stage_tables()
    once = pl.Buffered(1)
    hspec = lambda direction, order: pl.BlockSpec(
        (length, dt), lambda c: (0, (direction * HY_ORDER + order) * nct + c))
    spectrum = pl.pallas_call(
        _hyena_spectrum_kernel,
        grid=(nct,),
        in_specs=[hspec(0, 0), hspec(0, 1), hspec(1, 0), hspec(1, 1),
                  pl.BlockSpec(ma.shape, lambda c: (0, 0, 0), pipeline_mode=once),
                  pl.BlockSpec(mb.shape, lambda c: (0, 0), pipeline_mode=once)],
        out_specs=pl.BlockSpec((HY_ORDER, FFT_N2, 2 * FFT_N1, dt), lambda c: (0, 0, 0, c)),
        out_shape=jax.ShapeDtypeStruct((HY_ORDER, FFT_N2, 2 * FFT_N1, d), BF16),
        scratch_shapes=[pltpu.VMEM((length, dt), F32), pltpu.VMEM((FFT_WORK_ROWS, dt), F32)],
        compiler_params=_cparams("parallel"),
        name="hyena_filter_spectrum",
    )(hfilt, hfilt, hfilt, hfilt, ma, mb)

    zspec = lambda part: pl.BlockSpec((1, length, dt), lambda c, b: (b + first, 0, part * nct + c))
    wspec = lambda part: pl.BlockSpec((3, dt), lambda c, b: (0, part * nct + c))
    fixed3 = lambda c, b: (0, 0, 0)
    fixed2 = lambda c, b: (0, 0)
    return pl.pallas_call(
        _hyena_two_stage_kernel,
        grid=(nct, bsz),
        in_specs=[zspec(0), zspec(1), zspec(2), wspec(0), wspec(1), wspec(2),
                  pl.BlockSpec((HY_ORDER, FFT_N2, 2 * FFT_N1, dt), lambda c, b: (0, 0, 0, c), pipeline_mode=once),
                  pl.BlockSpec((HY_ORDER, dt), lambda c, b: (0, c)),
                  pl.BlockSpec(ma.shape, fixed3, pipeline_mode=once),
                  pl.BlockSpec(mai.shape, fixed3, pipeline_mode=once),
                  pl.BlockSpec(mb.shape, fixed2, pipeline_mode=once),
                  pl.BlockSpec(mbi.shape, fixed2, pipeline_mode=once)],
        out_specs=pl.BlockSpec((1, length, dt), lambda c, b: (b, 0, c)),
        out_shape=jax.ShapeDtypeStruct((bsz, length, d), BF16),
        scratch_shapes=[pltpu.VMEM((length, dt), F32), pltpu.VMEM((length, dt), F32),
                        pltpu.VMEM((FFT_WORK_ROWS, dt), F32), pltpu.VMEM((FFT_N2, 2 * FFT_N1, dt), BF16)],
        compiler_params=_cparams("parallel", "arbitrary"),
        name="hyena_conv_two_stage",
    )(z, z, z, w_short, w_short, w_short, spectrum, skip, ma, mai, mb, mbi)


N_HEADS = 16
HEAD_DIM = D_MODEL // N_HEADS
HEADS_PER_STEP = 2
CTX_HEADS_PER_STEP = 8
GRID_W = 64
WIN_ROWS = 8
WIN_COLS = 16
NEG_INF = -1e30
NAT_Q_ROWS = 4
NAT_K_ROWS = NAT_Q_ROWS + WIN_ROWS - 1
ATTN_SCALE = HEAD_DIM ** -0.5


def _dot_nt(a, b):
    return lax.dot_general(a, b, (((1,), (1,)), ((), ())), preferred_element_type=F32)


def _qkv_kernel(x_ref, g_ref, sc_ref, sh_ref, w_ref, q_ref, k_ref, v_ref):
    h = _norm_mod(x_ref[...], g_ref[...], sc_ref[0], sh_ref[0])
    qkv = _dot(h.astype(BF16), w_ref[...])
    for part, ref in enumerate((q_ref, k_ref, v_ref)):
        for head in range(N_HEADS):
            lo = part * D_MODEL + head * HEAD_DIM
            ref[0, head] = qkv[:, lo:lo + HEAD_DIM].astype(ref.dtype)


def _qkv_proj(x, g, mods, w_bf16, seg, first_row, n_seq, seq_len, kv_dtype):
    d = x.shape[1]
    n_ctx, smp_len = seg
    t = ROW_TILE
    first_tile = first_row // t
    per_seq = seq_len // t
    out_spec = pl.BlockSpec((1, N_HEADS, t, HEAD_DIM), lambda i: (i // per_seq, 0, i % per_seq, 0))
    shape = (n_seq, N_HEADS, seq_len, HEAD_DIM)
    return pl.pallas_call(
        _qkv_kernel,
        grid=(n_seq * per_seq,),
        in_specs=[
            pl.BlockSpec((t, d), lambda i: (i + first_tile, 0)),
            pl.BlockSpec((1, d), lambda i: (0, 0)),
            _mod_spec(1, t, n_ctx, smp_len, first_tile),
            _mod_spec(0, t, n_ctx, smp_len, first_tile),
            pl.BlockSpec(w_bf16.shape, lambda i: (0, 0)),
        ],
        out_specs=[out_spec, out_spec, out_spec],
        out_shape=[jax.ShapeDtypeStruct(shape, BF16), jax.ShapeDtypeStruct(shape, kv_dtype),
                   jax.ShapeDtypeStruct(shape, kv_dtype)],
        compiler_params=_cparams("parallel"),
        name="qkv_proj",
    )(x, g.reshape(1, d), mods, mods, w_bf16)


def _ctx_attn_kernel(q_ref, k_ref, v_ref, o_ref):
    for j in range(q_ref.shape[1]):
        q = q_ref[0, j]
        k = k_ref[0, j].astype(BF16)
        v = v_ref[0, j].astype(BF16)
        s = _dot_nt(q, k) * ATTN_SCALE
        p = jnp.exp(s - jnp.max(s, axis=-1, keepdims=True))
        o = _dot(p.astype(BF16), v) / jnp.sum(p, axis=-1, keepdims=True)
        o_ref[:, j * HEAD_DIM:(j + 1) * HEAD_DIM] = o.astype(o_ref.dtype)


def _ctx_attention(q, k, v):
    bsz, _, s, _ = q.shape
    hp = CTX_HEADS_PER_STEP
    spec = pl.BlockSpec((1, hp, s, HEAD_DIM), lambda b, h: (b, h, 0, 0))
    return pl.pallas_call(
        _ctx_attn_kernel,
        grid=(bsz, N_HEADS // hp),
        in_specs=[spec, spec, spec],
        out_specs=pl.BlockSpec((s, hp * HEAD_DIM), lambda b, h: (b, h)),
        out_shape=jax.ShapeDtypeStruct((bsz * s, D_MODEL), BF16),
        compiler_params=_cparams("parallel", "parallel"),
        name="ctx_attention",
    )(q, k, v)


def _rpb_toeplitz_kernel(r_ref, e_ref, o_ref):
    o_ref[...] = _dot3(r_ref[...], e_ref[...])


def _nat_bias(rpb):
    n_heads, n_r, n_c = rpb.shape
    n_cp = 32
    qc = np.arange(GRID_W)[:, None]
    kc = np.arange(GRID_W)[None, :]
    onehot = (np.clip(kc - qc + WIN_COLS - 1, 0, n_c - 1)[None] == np.arange(n_cp)[:, None, None])
    onehot = jnp.asarray(onehot.reshape(n_cp, GRID_W * GRID_W), F32)
    rows = jnp.pad(rpb.reshape(n_heads * n_r, n_c), ((0, 0), (0, n_cp - n_c)))
    toep = pl.pallas_call(
        _rpb_toeplitz_kernel,
        out_shape=jax.ShapeDtypeStruct((n_heads * n_r, GRID_W * GRID_W), F32),
        compiler_params=_cparams(),
        name="nat_bias_toeplitz",
    )(rows, onehot).reshape(n_heads, n_r, GRID_W, GRID_W)
    q_start = np.clip(qc - WIN_COLS // 2, 0, GRID_W - WIN_COLS)
    col_ok = jnp.asarray((kc >= q_start) & (kc < q_start + WIN_COLS))
    toep = jnp.where(col_ok, toep, NEG_INF)
    toep = jnp.concatenate([toep, jnp.full((n_heads, 1, GRID_W, GRID_W), NEG_INF, F32)], axis=1)
    rows_total = GRID_W
    idx = np.full((3, NAT_Q_ROWS, NAT_K_ROWS), n_r, np.int32)
    for case, r0 in enumerate((0, NAT_Q_ROWS, rows_total - NAT_Q_ROWS)):
        ks = int(np.clip(r0 - WIN_ROWS // 2, 0, rows_total - NAT_K_ROWS))
        for dr in range(NAT_Q_ROWS):
            r = r0 + dr
            rs = int(np.clip(r - WIN_ROWS // 2, 0, rows_total - WIN_ROWS))
            for dk in range(NAT_K_ROWS):
                kr = ks + dk
                if rs <= kr < rs + WIN_ROWS:
                    idx[case, dr, dk] = kr - r + WIN_ROWS - 1
    def assemble(t_ref, o_ref):
        for case in range(3):
            for dr in range(NAT_Q_ROWS):
                for dk in range(NAT_K_ROWS):
                    o_ref[0, case, dr * GRID_W:(dr + 1) * GRID_W, dk * GRID_W:(dk + 1) * GRID_W] = (
                        t_ref[0, int(idx[case, dr, dk])])

    return pl.pallas_call(
        assemble,
        grid=(n_heads,),
        in_specs=[pl.BlockSpec((1, n_r + 1, GRID_W, GRID_W), lambda h: (h, 0, 0, 0))],
        out_specs=pl.BlockSpec((1, 3, NAT_Q_ROWS * GRID_W, NAT_K_ROWS * GRID_W), lambda h: (h, 0, 0, 0)),
        out_shape=jax.ShapeDtypeStruct((n_heads, 3, NAT_Q_ROWS * GRID_W, NAT_K_ROWS * GRID_W), F32),
        compiler_params=_cparams("parallel"),
        name="nat_bias_assemble",
    )(toep)


def _nat_kernel(q_ref, k_ref, v_ref, kc_ref, vc_ref, bias_ref, o_ref):
    n_blocks = q_ref.shape[2] // (NAT_Q_ROWS * GRID_W)
    rows_total = q_ref.shape[2] // GRID_W
    nq = NAT_Q_ROWS * GRID_W
    nk = NAT_K_ROWS * GRID_W
    ctx = [(kc_ref[0, 0, j].astype(BF16), vc_ref[0, 0, j].astype(BF16)) for j in range(HEADS_PER_STEP)]

    def block(blk, carry):
        ks = jnp.clip(blk * NAT_Q_ROWS - WIN_ROWS // 2, 0, rows_total - NAT_K_ROWS)
        case = jnp.where(blk == 0, 0, jnp.where(blk == n_blocks - 1, 2, 1))
        q_rows = pl.ds(pl.multiple_of(blk * nq, nq), nq)
        k_rows = pl.ds(pl.multiple_of(ks * GRID_W, GRID_W), nk)
        for j, (k_ctx, v_ctx) in enumerate(ctx):
            q = q_ref[0, j, q_rows, :]
            s_loc = _dot_nt(q, k_ref[0, j, k_rows, :]) * ATTN_SCALE + bias_ref[j, case]
            s_ctx = _dot_nt(q, k_ctx) * ATTN_SCALE
            m = jnp.maximum(jnp.max(s_loc, axis=-1, keepdims=True), jnp.max(s_ctx, axis=-1, keepdims=True))
            p_loc = jnp.exp(s_loc - m)
            p_ctx = jnp.exp(s_ctx - m)
            denom = jnp.sum(p_loc, axis=-1, keepdims=True) + jnp.sum(p_ctx, axis=-1, keepdims=True)
            o = (_dot(p_loc.astype(BF16), v_ref[0, j, k_rows, :]) + _dot(p_ctx.astype(BF16), v_ctx)) / denom
            o_ref[q_rows, j * HEAD_DIM:(j + 1) * HEAD_DIM] = o.astype(o_ref.dtype)
        return carry

    lax.fori_loop(0, n_blocks, block, 0, unroll=2)


def _nat_attention(q, k, v, cache_k, cache_v, bias):
    bsz, _, length, _ = q.shape
    hp = HEADS_PER_STEP
    past = cache_k.shape[3]
    spec = pl.BlockSpec((1, hp, length, HEAD_DIM), lambda b, h: (b, h, 0, 0))
    cspec = pl.BlockSpec((1, 1, hp, past, HEAD_DIM), lambda b, h: (b, 0, h, 0, 0))
    return pl.pallas_call(
        _nat_kernel,
        grid=(bsz, N_HEADS // hp),
        in_specs=[spec, spec, spec, cspec, cspec,
                  pl.BlockSpec((hp,) + bias.shape[1:], lambda b, h: (h, 0, 0, 0))],
        out_specs=pl.BlockSpec((length, hp * HEAD_DIM), lambda b, h: (b, h)),
        out_shape=jax.ShapeDtypeStruct((bsz * length, D_MODEL), BF16),
        compiler_params=_cparams("parallel", "parallel"),
        name="nat_attention",
    )(q, k, v, cache_k, cache_v, bias)


def kernel(x_prompt, x_sample, cache_k, cache_v, c, c_ctx, ada_w, ada_b, norm1_g, norm2_g, normf_g, hy_w_in, hy_w_short, hy_f_w1, hy_f_b1, hy_f_freq, hy_f_w2, hy_f_b2, hy_f_w3, hy_skip, hy_w_out, na_w_qkv, na_rpb, na_w_o, moe_w_router, moe_b_router, moe_w_gate, moe_b_gate, moe_w_up, moe_b_up, moe_w_down, moe_b_down):
    b, s, d = x_prompt.shape
    bd, sd, _ = x_sample.shape
    n_ctx = b * s
    n_tok = n_ctx + bd * sd
    assert n_ctx % sd == 0 and s % ROW_TILE == 0 and sd % ROW_TILE == 0
    seg = (n_ctx, sd)
    x = jnp.concatenate([x_prompt.reshape(n_ctx, d), x_sample.reshape(bd * sd, d)], axis=0)
    cond = jnp.zeros((N_COND, d), F32).at[0].set(c_ctx).at[1:1 + bd].set(c)
    mods = _modulation(cond, ada_w, ada_b)

    def moe(x, i, final_norm):
        return _moe_layer(x, norm2_g[i], mods[i], moe_w_router[i], moe_b_router[i], i, moe_w_gate, moe_b_gate,
                          moe_w_up, moe_b_up, moe_w_down, moe_b_down, normf_g, seg, final_norm)

    z = _norm_proj(x, norm1_g[0], mods[0], hy_w_in[0].astype(BF16), seg)
    fargs = (hy_f_w1[0], hy_f_b1[0], hy_f_freq[0], hy_f_w2[0], hy_f_b2[0], hy_f_w3[0])
    y_ctx = _hyena_direct(z.reshape(n_tok // s, s, 3 * d), hy_w_short[0], _hyena_filters(s, *fargs),
                          hy_skip[0], first=0, bsz=b)
    y_smp = _hyena_two_stage(z.reshape(n_tok // sd, sd, 3 * d), hy_w_short[0], _hyena_filters(sd, *fargs),
                             hy_skip[0], first=n_ctx // sd, bsz=bd)
    x = _out_proj(y_ctx.reshape(n_ctx, d), y_smp.reshape(bd * sd, d), hy_w_out[0].astype(BF16), x, mods[0], seg)
    x = moe(x, 0, False)

    w_qkv = na_w_qkv[0].astype(BF16)
    q_c, k_c, v_c = _qkv_proj(x, norm1_g[1], mods[1], w_qkv, seg, 0, b, s, F32)
    q_s, k_s, v_s = _qkv_proj(x, norm1_g[1], mods[1], w_qkv, seg, n_ctx, bd, sd, BF16)
    o_ctx = _ctx_attention(q_c, k_c, v_c)
    o_smp = _nat_attention(q_s, k_s, v_s, cache_k, cache_v, _nat_bias(na_rpb[0]))
    x = _out_proj(o_ctx, o_smp, na_w_o[0].astype(BF16), x, mods[1], seg)
    y_prompt, y_sample = moe(x, 1, True)

    nh, hd = k_c.shape[1], k_c.shape[3]
    return (y_prompt.reshape(b, s, d), y_sample.reshape(bd, sd, d),
            k_c.reshape(b, 1, nh, s, hd), v_c.reshape(b, 1, nh, s, hd))
```

```python
import functools
import math

import numpy as np
import jax
import jax.numpy as jnp
from jax import lax
from jax.experimental import pallas as pl
from jax.experimental.pallas import tpu as pltpu

F32 = jnp.float32
BF16 = jnp.bfloat16

D_MODEL = 1024
N_MOD = 6
RMS_EPS = 1e-6
N_EXPERTS = 32
TOP_K = 4
SWIGLU_LIMIT = 7.0
SWIGLU_ALPHA = 1.702

N_COND = 8
ROW_TILE = 256
MOE_TILE = 256
V7X_VMEM_LIMIT = 56 * 1024 * 1024


def _cparams(*sem, vmem=V7X_VMEM_LIMIT):
    return pltpu.CompilerParams(dimension_semantics=sem, vmem_limit_bytes=vmem)


def _dot(a, b):
    return jnp.dot(a, b, preferred_element_type=F32)


def _split_bf16(x):
    hi = x.astype(BF16)
    lo = (x - hi.astype(F32)).astype(BF16)
    return hi, lo


def _dot3(a, b):
    ah, al = _split_bf16(a)
    bh, bl = _split_bf16(b)
    return _dot(ah, bh) + (_dot(al, bh) + _dot(ah, bl))


def _seg_of_tile(i, tile, n_ctx, smp_len):
    ctx_tiles = n_ctx // tile
    per_smp = smp_len // tile
    return jnp.where(i < ctx_tiles, 0, 1 + (i - ctx_tiles) // per_smp)


def _norm_mod(x, g, sc, sh):
    y = x * lax.rsqrt(jnp.mean(x * x, axis=-1, keepdims=True) + RMS_EPS)
    return (y * g) * (1.0 + sc) + sh


def _mod_kernel(c_ref, w_ref, b_ref, o_ref):
    c = c_ref[...]
    a = c * jax.nn.sigmoid(c)
    o_ref[0] = _dot3(a, w_ref[0]) + b_ref[0]


def _modulation(cond, ada_w, ada_b):
    depth, d, n_out = ada_w.shape
    tn = 1536
    m = pl.pallas_call(
        _mod_kernel,
        grid=(depth, n_out // tn),
        in_specs=[
            pl.BlockSpec((N_COND, d), lambda l, j: (0, 0)),
            pl.BlockSpec((1, d, tn), lambda l, j: (l, 0, j)),
            pl.BlockSpec((1, 1, tn), lambda l, j: (l, 0, j)),
        ],
        out_specs=pl.BlockSpec((1, N_COND, tn), lambda l, j: (l, 0, j)),
        out_shape=jax.ShapeDtypeStruct((depth, N_COND, n_out), F32),
        compiler_params=_cparams("arbitrary", "arbitrary"),
        name="adaln_modulation",
    )(cond, ada_w, ada_b.reshape(depth, 1, n_out))
    m = m.reshape(depth, N_COND, N_MOD, d)
    return jnp.transpose(m, (0, 2, 1, 3)).reshape(depth, N_MOD * N_COND, 1, d)


def _mod_spec(which, tile, n_ctx, smp_len, first_tile=0):
    return pl.BlockSpec(
        (1, 1, D_MODEL),
        lambda i, *_: (which * N_COND + _seg_of_tile(i + first_tile, tile, n_ctx, smp_len), 0, 0))


def _norm_proj_kernel(x_ref, g_ref, sc_ref, sh_ref, w_ref, o_ref):
    h = _norm_mod(x_ref[...], g_ref[...], sc_ref[0], sh_ref[0])
    o_ref[...] = _dot(h.astype(BF16), w_ref[...]).astype(o_ref.dtype)


def _norm_proj(x, g, mods, w_bf16, seg, out_dtype=F32):
    n, d = x.shape
    n_out = w_bf16.shape[1]
    n_ctx, smp_len = seg
    return pl.pallas_call(
        _norm_proj_kernel,
        grid=(n // ROW_TILE,),
        in_specs=[
            pl.BlockSpec((ROW_TILE, d), lambda i: (i, 0)),
            pl.BlockSpec((1, d), lambda i: (0, 0)),
            _mod_spec(1, ROW_TILE, n_ctx, smp_len),
            _mod_spec(0, ROW_TILE, n_ctx, smp_len),
            pl.BlockSpec((d, n_out), lambda i: (0, 0)),
        ],
        out_specs=pl.BlockSpec((ROW_TILE, n_out), lambda i: (i, 0)),
        out_shape=jax.ShapeDtypeStruct((n, n_out), out_dtype),
        compiler_params=_cparams("parallel"),
        name="norm_proj",
    )(x, g.reshape(1, d), mods, mods, w_bf16)


def _out_proj_kernel(ya_ref, yb_ref, w_ref, x_ref, gate_ref, o_ref, *, ctx_tiles):
    y = jnp.where(pl.program_id(0) < ctx_tiles, ya_ref[...], yb_ref[...])
    o_ref[...] = x_ref[...] + gate_ref[0] * _dot(y, w_ref[...])


def _out_proj(y_ctx, y_smp, w_bf16, x, mods, seg):
    n, d = x.shape
    n_ctx, smp_len = seg
    ctx_tiles = n_ctx // ROW_TILE
    return pl.pallas_call(
        functools.partial(_out_proj_kernel, ctx_tiles=ctx_tiles),
        grid=(n // ROW_TILE,),
        in_specs=[
            pl.BlockSpec((ROW_TILE, d), lambda i: (jnp.minimum(i, ctx_tiles - 1), 0)),
            pl.BlockSpec((ROW_TILE, d), lambda i: (jnp.maximum(i - ctx_tiles, 0), 0)),
            pl.BlockSpec((d, d), lambda i: (0, 0)),
            pl.BlockSpec((ROW_TILE, d), lambda i: (i, 0)),
            _mod_spec(2, ROW_TILE, n_ctx, smp_len),
        ],
        out_specs=pl.BlockSpec((ROW_TILE, d), lambda i: (i, 0)),
        out_shape=jax.ShapeDtypeStruct((n, d), F32),
        compiler_params=_cparams("parallel"),
        name="out_proj",
    )(y_ctx, y_smp, w_bf16, x, mods)


LANES = 128
ROW_SUBLANES = D_MODEL // LANES


def _store_row_tiles(ref, x):
    t = x.shape[0]
    for s in range(ROW_SUBLANES):
        ref[pl.ds(s, t, stride=ROW_SUBLANES), :] = x[:, s * LANES:(s + 1) * LANES]


def _load_row_tiles(ref, t):
    return jnp.concatenate([ref[pl.ds(s, t, stride=ROW_SUBLANES), :] for s in range(ROW_SUBLANES)], axis=1)


def _columns(cols):
    t = cols[0].shape[0]
    lane = lax.broadcasted_iota(jnp.int32, (t, len(cols)), 1)
    out = jnp.broadcast_to(cols[-1], (t, len(cols)))
    for k in range(len(cols) - 2, -1, -1):
        out = jnp.where(lane == k, cols[k], out)
    return out


def _router_kernel(x_ref, g_ref, sc_ref, sh_ref, wr_ref, br_ref, tri_ref,
                   h_ref, e_ref, gate_ref, rank_ref, cnt_ref, run_ref):
    i = pl.program_id(0)

    @pl.when(i == 0)
    def _():
        run_ref[...] = jnp.zeros_like(run_ref)

    h = _norm_mod(x_ref[...], g_ref[...], sc_ref[0], sh_ref[0])
    _store_row_tiles(h_ref, h)
    logits = _dot3(h, wr_ref[...]) + br_ref[...]
    lane = lax.broadcasted_iota(jnp.int32, logits.shape, 1)
    work = logits
    vals, idxs, hots = [], [], []
    for _ in range(TOP_K):
        m = jnp.max(work, axis=-1, keepdims=True)
        idx = jnp.min(jnp.where(work == m, lane, N_EXPERTS), axis=-1, keepdims=True)
        hot = lane == idx
        vals.append(m)
        idxs.append(idx)
        hots.append(hot)
        work = jnp.where(hot, -jnp.inf, work)
    ex = [jnp.exp(v - vals[0]) for v in vals]
    denom = ex[0] + ex[1] + ex[2] + ex[3]
    gate_ref[...] = _columns([e / denom for e in ex])
    e_ref[...] = _columns(idxs)

    chosen = (hots[0] | hots[1] | hots[2] | hots[3]).astype(F32)
    before = run_ref[...] + _dot(tri_ref[...], chosen.astype(BF16))
    ranks = [jnp.sum(jnp.where(hot, before, 0.0), axis=-1, keepdims=True) for hot in hots]
    rank_ref[...] = _columns(ranks).astype(jnp.int32)
    run_ref[...] += jnp.sum(chosen, axis=0, keepdims=True)
    cnt_ref[...] = run_ref[...].astype(jnp.int32)


def _router(x, g, mods, w_r, b_r, seg):
    n, d = x.shape
    n_ctx, smp_len = seg
    t = ROW_TILE
    tri = jnp.asarray(np.tril(np.ones((t, t), np.float32), -1), BF16)
    tok4 = lambda i: (i, 0)
    return pl.pallas_call(
        _router_kernel,
        grid=(n // t,),
        in_specs=[
            pl.BlockSpec((t, d), lambda i: (i, 0)),
            pl.BlockSpec((1, d), lambda i: (0, 0)),
            _mod_spec(4, t, n_ctx, smp_len),
            _mod_spec(3, t, n_ctx, smp_len),
            pl.BlockSpec((d, N_EXPERTS), lambda i: (0, 0)),
            pl.BlockSpec((1, N_EXPERTS), lambda i: (0, 0)),
            pl.BlockSpec((t, t), lambda i: (0, 0)),
        ],
        out_specs=[
            pl.BlockSpec((t * ROW_SUBLANES, LANES), tok4),
            pl.BlockSpec((t, TOP_K), tok4),
            pl.BlockSpec((t, TOP_K), tok4),
            pl.BlockSpec((t, TOP_K), tok4),
            pl.BlockSpec((1, N_EXPERTS), lambda i: (0, 0)),
        ],
        out_shape=[
            jax.ShapeDtypeStruct((n * ROW_SUBLANES, LANES), F32),
            jax.ShapeDtypeStruct((n, TOP_K), jnp.int32),
            jax.ShapeDtypeStruct((n, TOP_K), F32),
            jax.ShapeDtypeStruct((n, TOP_K), jnp.int32),
            jax.ShapeDtypeStruct((1, N_EXPERTS), jnp.int32),
        ],
        scratch_shapes=[pltpu.VMEM((1, N_EXPERTS), F32)],
        compiler_params=_cparams("arbitrary"),
        name="moe_router",
    )(x, g.reshape(1, d), mods, mods, w_r, b_r.reshape(1, N_EXPERTS), tri)


def _tile_copy(src_ref, src_row, dst_ref, dst_row, sem):
    rows = lambda r: pl.ds(pl.multiple_of(r, ROW_SUBLANES), ROW_SUBLANES)
    return pltpu.make_async_copy(src_ref.at[rows(src_row), :], dst_ref.at[rows(dst_row), :], sem)


def _tiles_wait(src_ref, dst_ref, n_tiles, sem):
    rows = pl.ds(0, n_tiles * ROW_SUBLANES)
    pltpu.make_async_copy(src_ref.at[rows, :], dst_ref.at[rows, :], sem).wait()


ENTRIES_PER_TILE = ROW_TILE * TOP_K


def _dispatch_kernel(fill_start_ref, fill_len_ref, na_ref, dest_ref, h_ref, xs_ref, hbuf, zero_ref, sem):
    i = pl.program_id(0)
    last = pl.num_programs(0) - 1
    slot = lax.rem(i, 2)
    hbuf[slot] = h_ref[...]

    def issue(t, carry):
        for k in range(TOP_K):
            _tile_copy(hbuf.at[slot], t * ROW_SUBLANES, xs_ref, dest_ref[t * TOP_K + k], sem.at[slot]).start()
        return carry

    lax.fori_loop(0, ROW_TILE, issue, 0, unroll=4)

    @pl.when(i >= 1)
    def _():
        _tiles_wait(xs_ref, xs_ref, ENTRIES_PER_TILE, sem.at[1 - slot])

    @pl.when(i == last)
    def _():
        _tiles_wait(xs_ref, xs_ref, ENTRIES_PER_TILE, sem.at[slot])
        zero_ref[...] = jnp.zeros_like(zero_ref)
        for e in range(N_EXPERTS):
            start = fill_start_ref[e]
            count = fill_len_ref[e]

            def fill(r, carry):
                _tile_copy(zero_ref, 0, xs_ref, (start + r) * ROW_SUBLANES, sem.at[0]).start()
                return carry

            lax.fori_loop(0, count, fill, 0)

            def fill_drain(r, carry):
                _tile_copy(zero_ref, 0, xs_ref, 0, sem.at[0]).wait()
                return carry

            lax.fori_loop(0, count, fill_drain, 0)

        block_rows = MOE_TILE * ROW_SUBLANES

        def block_copy(blk):
            return pltpu.make_async_copy(
                zero_ref, xs_ref.at[pl.ds(pl.multiple_of(blk * block_rows, block_rows), block_rows), :], sem.at[0])

        n_blocks = xs_ref.shape[0] // block_rows

        def tail(blk, carry):
            block_copy(blk).start()
            return carry

        lax.fori_loop(na_ref[0], n_blocks, tail, 0)

        def tail_drain(blk, carry):
            block_copy(0).wait()
            return carry

        lax.fori_loop(na_ref[0], n_blocks, tail_drain, 0)


def _dispatch(h_tiles, dest_rows, fill_start, fill_len, n_active, cap):
    n = h_tiles.shape[0] // ROW_SUBLANES
    grid_spec = pltpu.PrefetchScalarGridSpec(
        num_scalar_prefetch=3,
        grid=(n // ROW_TILE,),
        in_specs=[
            pl.BlockSpec((ENTRIES_PER_TILE,), lambda i, *_: (i,), memory_space=pltpu.SMEM),
            pl.BlockSpec((ROW_TILE * ROW_SUBLANES, LANES), lambda i, *_: (i, 0)),
        ],
        out_specs=pl.BlockSpec(memory_space=pl.ANY),
        scratch_shapes=[pltpu.VMEM((2, ROW_TILE * ROW_SUBLANES, LANES), F32),
                        pltpu.VMEM((MOE_TILE * ROW_SUBLANES, LANES), F32), pltpu.SemaphoreType.DMA((2,))],
    )
    return pl.pallas_call(
        _dispatch_kernel,
        grid_spec=grid_spec,
        out_shape=jax.ShapeDtypeStruct((cap * ROW_SUBLANES, LANES), F32),
        compiler_params=_cparams("arbitrary"),
        name="moe_dispatch",
    )(fill_start, fill_len, n_active, dest_rows, h_tiles)


def _ffn_kernel(be_ref, na_ref, xs_ref, wg_ref, bg_ref, wu_ref, bu_ref, wd_ref, bd_ref,
                ys_ref, wg_s, wu_s, wd_s):
    i = pl.program_id(0)
    tm = xs_ref.shape[0] // ROW_SUBLANES
    fresh = jnp.logical_or(i == 0, be_ref[i] != be_ref[jnp.maximum(i - 1, 0)])

    @pl.when(fresh)
    def _():
        wg_s[...] = wg_ref[0, 0].astype(BF16)
        wu_s[...] = wu_ref[0, 0].astype(BF16)
        wd_s[...] = wd_ref[0, 0].astype(BF16)

    @pl.when(i < na_ref[0])
    def _():
        x = _load_row_tiles(xs_ref, tm).astype(BF16)
        g = jnp.minimum(_dot(x, wg_s[...]) + bg_ref[0, 0], SWIGLU_LIMIT)
        u = jnp.clip(_dot(x, wu_s[...]) + bu_ref[0, 0], -SWIGLU_LIMIT, SWIGLU_LIMIT)
        a = g * jax.nn.sigmoid(SWIGLU_ALPHA * g) * (u + 1.0)
        _store_row_tiles(ys_ref, _dot(a.astype(BF16), wd_s[...]) + bd_ref[0, 0])

    @pl.when(i >= na_ref[0])
    def _():
        ys_ref[...] = jnp.zeros_like(ys_ref)


def _expert_ffn(xs, block_e, n_active, layer, w_g, b_g, w_u, b_u, w_d, b_d):
    d = D_MODEL
    depth, ne, _, f = w_g.shape
    nb = block_e.shape[0]
    block = (MOE_TILE * ROW_SUBLANES, LANES)
    wmap = lambda i, be, na: (layer, be[i], 0, 0)
    grid_spec = pltpu.PrefetchScalarGridSpec(
        num_scalar_prefetch=2,
        grid=(nb,),
        in_specs=[
            pl.BlockSpec(block, lambda i, be, na: (jnp.minimum(i, na[0] - 1), 0)),
            pl.BlockSpec((1, 1, d, f), wmap),
            pl.BlockSpec((1, 1, 1, f), wmap),
            pl.BlockSpec((1, 1, d, f), wmap),
            pl.BlockSpec((1, 1, 1, f), wmap),
            pl.BlockSpec((1, 1, f, d), wmap),
            pl.BlockSpec((1, 1, 1, d), wmap),
        ],
        out_specs=pl.BlockSpec(block, lambda i, be, na: (i, 0)),
        scratch_shapes=[pltpu.VMEM((d, f), BF16), pltpu.VMEM((d, f), BF16), pltpu.VMEM((f, d), BF16)],
    )
    return pl.pallas_call(
        _ffn_kernel,
        grid_spec=grid_spec,
        out_shape=jax.ShapeDtypeStruct(xs.shape, F32),
        compiler_params=_cparams("arbitrary"),
        name="moe_expert_ffn",
    )(block_e, n_active, xs, w_g, b_g.reshape(depth, ne, 1, f), w_u, b_u.reshape(depth, ne, 1, f),
      w_d, b_d.reshape(depth, ne, 1, d))


def _combine_kernel(dest_ref, dest_next_ref, x_ref, gate_ref, g2_ref, gf_ref, ys_ref, o_ref, buf, sem, *,
                    final_norm):
    i = pl.program_id(0)
    slot = lax.rem(i, 2)
    t = x_ref.shape[0]
    plane = t * ROW_SUBLANES

    def start_gather(idx_ref, b):
        def body(r, carry):
            for k in range(TOP_K):
                _tile_copy(ys_ref, idx_ref[r * TOP_K + k], buf.at[b], k * plane + r * ROW_SUBLANES, sem.at[b]).start()
            return carry

        lax.fori_loop(0, t, body, 0, unroll=4)

    @pl.when(i == 0)
    def _():
        start_gather(dest_ref, 0)

    @pl.when(i + 1 < pl.num_programs(0))
    def _():
        start_gather(dest_next_ref, 1 - slot)

    _tiles_wait(ys_ref, buf.at[slot], TOP_K * t, sem.at[slot])
    gates = gate_ref[...]
    acc = None
    for k in range(TOP_K):
        term = gates[:, k:k + 1] * _load_row_tiles(buf.at[slot, pl.ds(k * plane, plane), :], t)
        acc = term if acc is None else acc + term
    y = x_ref[...] + g2_ref[0] * acc
    if final_norm:
        y = y * lax.rsqrt(jnp.mean(y * y, axis=-1, keepdims=True) + RMS_EPS) * gf_ref[...]
    o_ref[...] = y


def _combine(x, ys, dest_rows, gates, mods, normf_g, seg, final_norm, first_row=0, n_rows=None):
    n, d = x.shape
    n_rows = n if n_rows is None else n_rows
    n_ctx, smp_len = seg
    t = ROW_TILE
    first = first_row // t
    steps = n_rows // t
    rows = lambda i: (i + first, 0)
    return pl.pallas_call(
        functools.partial(_combine_kernel, final_norm=final_norm),
        grid=(steps,),
        in_specs=[
            pl.BlockSpec((ENTRIES_PER_TILE,), lambda i: (i + first,), memory_space=pltpu.SMEM),
            pl.BlockSpec((ENTRIES_PER_TILE,), lambda i: (jnp.minimum(i + 1, steps - 1) + first,),
                         memory_space=pltpu.SMEM),
            pl.BlockSpec((t, d), rows),
            pl.BlockSpec((t, TOP_K), rows),
            _mod_spec(5, t, n_ctx, smp_len, first),
            pl.BlockSpec((1, d), lambda i: (0, 0)),
            pl.BlockSpec(memory_space=pl.ANY),
        ],
        out_specs=pl.BlockSpec((t, d), lambda i: (i, 0)),
        out_shape=jax.ShapeDtypeStruct((n_rows, d), F32),
        scratch_shapes=[pltpu.VMEM((2, TOP_K * t * ROW_SUBLANES, LANES), F32), pltpu.SemaphoreType.DMA((2,))],
        compiler_params=_cparams("arbitrary"),
        name="moe_combine",
    )(dest_rows, dest_rows, x, gates, mods, normf_g.reshape(1, d), ys)


def _moe_layer(x, norm_g, mods, w_r, b_r, layer, w_g, b_g, w_u, b_u, w_d, b_d, normf_g, seg, final_norm):
    n, d = x.shape
    nk = n * TOP_K
    h, top_e, gates, rank, counts = _router(x, norm_g, mods, w_r, b_r, seg)
    counts = counts[0]
    padded = (counts + MOE_TILE - 1) // MOE_TILE * MOE_TILE
    pad_end = jnp.cumsum(padded)
    pad_start = pad_end - padded
    cap = nk + N_EXPERTS * MOE_TILE
    nb = cap // MOE_TILE
    experts = jnp.arange(N_EXPERTS, dtype=jnp.int32)
    start_of = jnp.sum(jnp.where(top_e[..., None] == experts, pad_start, 0), axis=-1)
    dest_rows = ((start_of + rank) * ROW_SUBLANES).reshape(nk).astype(jnp.int32)
    blk_start = jnp.arange(nb, dtype=jnp.int32) * MOE_TILE
    block_e = jnp.minimum(jnp.sum(blk_start[:, None] >= pad_end[None, :], axis=1), N_EXPERTS - 1).astype(jnp.int32)
    n_active = (pad_end[-1] // MOE_TILE).astype(jnp.int32).reshape(1)
    xs = _dispatch(h, dest_rows, (pad_start + counts).astype(jnp.int32), (padded - counts).astype(jnp.int32),
                   n_active, cap)
    ys = _expert_ffn(xs, block_e, n_active, layer, w_g, b_g, w_u, b_u, w_d, b_d)
    if not final_norm:
        return _combine(x, ys, dest_rows, gates, mods, normf_g, seg, False)
    n_ctx = seg[0]
    return (_combine(x, ys, dest_rows, gates, mods, normf_g, seg, True, 0, n_ctx),
            _combine(x, ys, dest_rows, gates, mods, normf_g, seg, True, n_ctx, n - n_ctx))


HY_ORDER = 2
HY_BANDS = 16
HY_EMB = 1 + 2 * HY_BANDS
HY_FFN = 64
HY_MIN_DECAY = math.log(1e-2) / 1.5
HY_MAX_DECAY = math.log(1e-2) / 0.3
HY_EMB_PAD = 64


def _filter_mlp_kernel(z_ref, t_ref, w1_ref, b1_ref, fr_ref, w2_ref, b2_ref, w3_ref, dl_ref, o_ref):
    fr = fr_ref[...]
    h = jnp.sin(fr * (_dot3(z_ref[...], w1_ref[...]) + b1_ref[...]))
    h = jnp.sin(fr * (_dot3(h, w2_ref[...]) + b2_ref[...]))
    o_ref[...] = _dot3(h, w3_ref[...]) * jnp.exp(-t_ref[...] * dl_ref[...])


def _hyena_filters(length, w1, b1, freq, w2, b2, w3):
    t = jnp.linspace(0.0, 1.0, length, dtype=F32)[:, None]
    ang = (2.0 * math.pi / length) * jnp.arange(length, dtype=F32)[:, None]
    bands = jnp.linspace(1e-4, HY_BANDS - 1, HY_BANDS, dtype=F32)[None, :]
    z = jnp.concatenate([t, jnp.cos(bands * ang), -jnp.sin(bands * ang)], axis=-1)
    z = jnp.pad(z, ((0, 0), (0, HY_EMB_PAD - HY_EMB)))
    w1p = jnp.pad(w1, ((0, HY_EMB_PAD - HY_EMB), (0, 0)))
    n_out = w3.shape[1]
    deltas = jnp.abs(jnp.linspace(HY_MIN_DECAY, HY_MAX_DECAY, D_MODEL, dtype=F32))
    deltas = jnp.tile(deltas, n_out // D_MODEL)[None, :]
    tl, tn = 256, 1024
    row = lambda i, j: (i, 0)
    fixed = lambda i, j: (0, 0)
    return pl.pallas_call(
        _filter_mlp_kernel,
        grid=(length // tl, n_out // tn),
        in_specs=[
            pl.BlockSpec((tl, HY_EMB_PAD), row),
            pl.BlockSpec((tl, 1), row),
            pl.BlockSpec((HY_EMB_PAD, HY_FFN), fixed),
            pl.BlockSpec((1, HY_FFN), fixed),
            pl.BlockSpec((1, HY_FFN), fixed),
            pl.BlockSpec((HY_FFN, HY_FFN), fixed),
            pl.BlockSpec((1, HY_FFN), fixed),
            pl.BlockSpec((HY_FFN, tn), lambda i, j: (0, j)),
            pl.BlockSpec((1, tn), lambda i, j: (0, j)),
        ],
        out_specs=pl.BlockSpec((tl, tn), lambda i, j: (i, j)),
        out_shape=jax.ShapeDtypeStruct((length, n_out), F32),
        compiler_params=_cparams("parallel", "parallel"),
        name="hyena_filter_mlp",
    )(z, t, w1p, b1.reshape(1, -1), freq.reshape(1, -1), w2, b2.reshape(1, -1), w3, deltas)


def _short_conv(z, w):
    length = z.shape[0]
    row = lax.broadcasted_iota(jnp.int32, z.shape, 0)
    prev = jnp.where(row == 0, 0.0, pltpu.roll(z, 1, 0))
    nxt = jnp.where(row == length - 1, 0.0, pltpu.roll(z, length - 1, 0))
    return (prev * w[0:1] + z * w[1:2]) + nxt * w[2:3]


def _filter_halves(hf_ref, hb_ref):
    hf = hf_ref[...]
    hb = hb_ref[...]
    hb = jnp.where(lax.broadcasted_iota(jnp.int32, hb.shape, 0) == 0, 0.0, hb)
    norm = jnp.sum(jnp.abs(hf), axis=0, keepdims=True) + jnp.sum(jnp.abs(hb), axis=0, keepdims=True)
    return hf + hb, hf - hb, 1.0 / norm


def _direct_dft_tables(length):
    n_fft = 2 * length
    n_freq = length + 1
    mf = -(-n_freq // 16) * 16
    k = np.arange(mf)[:, None]
    n = np.arange(length)[None, :]
    ang = 2.0 * np.pi * ((k * n) % n_fft) / n_fft
    valid = k < n_freq
    cos = np.where(valid, np.cos(ang), 0.0)
    msin = np.where(valid, -np.sin(ang), 0.0)
    weight = np.where((k == 0) | (k == length), 1.0, 2.0) * valid / n_fft
    fwd = np.concatenate([cos, msin], axis=0)
    inv = np.concatenate([weight * cos, weight * msin], axis=0).T
    return jnp.asarray(fwd, BF16), jnp.asarray(inv, BF16), mf


def _hyena_direct_kernel(zv_ref, z1_ref, z2_ref, wv_ref, w1_ref, w2_ref,
                         hf0_ref, hf1_ref, hb0_ref, hb1_ref, skip_ref, fw_ref, iv_ref,
                         o_ref, kr_ref, ki_ref):
    mf = fw_ref.shape[0] // 2
    dt = o_ref.shape[-1]

    @pl.when(pl.program_id(1) == 0)
    def _():
        for o, (hf_ref, hb_ref) in enumerate(((hf0_ref, hb0_ref), (hf1_ref, hb1_ref))):
            hs, hd, inv_norm = _filter_halves(hf_ref, hb_ref)
            spec = _dot(fw_ref[...], jnp.concatenate([hs, hd], axis=1).astype(BF16))
            kr_ref[o] = spec[:mf, :dt] * inv_norm
            ki_ref[o] = spec[mf:, dt:] * inv_norm

    y = _short_conv(zv_ref[0], wv_ref[...])
    for o, (z_ref, w_ref) in enumerate(((z1_ref, w1_ref), (z2_ref, w2_ref))):
        spec = _dot(fw_ref[...], y.astype(BF16))
        yr, yi = spec[:mf], spec[mf:]
        kr, ki = kr_ref[o], ki_ref[o]
        prod = jnp.concatenate([yr * kr - yi * ki, yr * ki + yi * kr], axis=0)
        yc = _dot(iv_ref[...], prod.astype(BF16))
        y = _short_conv(z_ref[0], w_ref[...]) * (yc + y * skip_ref[o:o + 1])
    o_ref[0] = y.astype(o_ref.dtype)


def _hyena_direct(z, w_short, hfilt, skip, first=0, bsz=None):
    _, length, d3 = z.shape
    bsz = z.shape[0] if bsz is None else bsz
    d = d3 // 3
    dt = 256
    nct = d // dt
    fwd, inv, mf = _direct_dft_tables(length)
    zspec = lambda part: pl.BlockSpec((1, length, dt), lambda c, b: (b + first, 0, part * nct + c))
    wspec = lambda part: pl.BlockSpec((3, dt), lambda c, b: (0, part * nct + c))
    hspec = lambda direction, order: pl.BlockSpec(
        (length, dt), lambda c, b: (0, (direction * HY_ORDER + order) * nct + c))
    fixed = lambda c, b: (0, 0)
    return pl.pallas_call(
        _hyena_direct_kernel,
        grid=(nct, bsz),
        in_specs=[zspec(0), zspec(1), zspec(2), wspec(0), wspec(1), wspec(2),
                  hspec(0, 0), hspec(0, 1), hspec(1, 0), hspec(1, 1),
                  pl.BlockSpec((HY_ORDER, dt), lambda c, b: (0, c)),
                  pl.BlockSpec(fwd.shape, fixed), pl.BlockSpec(inv.shape, fixed)],
        out_specs=pl.BlockSpec((1, length, dt), lambda c, b: (b, 0, c)),
        out_shape=jax.ShapeDtypeStruct((bsz, length, d), BF16),
        scratch_shapes=[pltpu.VMEM((HY_ORDER, mf, dt), F32), pltpu.VMEM((HY_ORDER, mf, dt), F32)],
        compiler_params=_cparams("parallel", "arbitrary"),
        name="hyena_conv_direct",
    )(z, z, z, w_short, w_short, w_short, hfilt, hfilt, hfilt, hfilt, skip, fwd, inv)


FFT_N1 = 64
FFT_N2 = 128
FFT_LANES = 128
FFT_A_PITCH = 2 * FFT_N2 + 8
FFT_U_PITCH = 2 * FFT_N1 + 8
FFT_GROUP = 2
FFT_UNROLL = 8


def _two_stage_tables():
    n_fft = FFT_N1 * FFT_N2
    half = FFT_N2 // 2
    n1 = np.arange(FFT_N1)[:, None, None]
    k2 = np.arange(FFT_N2)[None, :, None]
    n2 = np.arange(half)[None, None, :]
    ang = 2.0 * np.pi * ((k2 * (n1 + FFT_N1 * n2)) % n_fft) / n_fft
    stage_a = np.concatenate([np.cos(ang), -np.sin(ang)], axis=1)
    stage_a_inv = np.transpose(stage_a, (0, 2, 1)) / n_fft
    k1 = np.arange(FFT_N1)[:, None]
    m1 = np.arange(FFT_N1)[None, :]
    phi = 2.0 * np.pi * ((k1 * m1) % FFT_N1) / FFT_N1
    c, s = np.cos(phi), np.sin(phi)
    stage_b = np.block([[c, s], [-s, c]])
    stage_b_inv = np.block([[c, -s], [s, c]])
    return tuple(jnp.asarray(t, BF16) for t in (stage_a, stage_a_inv, stage_b, stage_b_inv))


def _fft_stage_a(y_ref, ma_ref, a_ref):
    half = FFT_N2 // 2

    def body(n1, carry):
        slab = y_ref[pl.ds(n1, half, stride=FFT_N1), :]
        a_ref[pl.ds(pl.multiple_of(n1 * FFT_A_PITCH, 8), 2 * FFT_N2), :] = _dot(ma_ref[n1], slab.astype(BF16))
        return carry

    lax.fori_loop(0, FFT_N1, body, 0, unroll=FFT_UNROLL)


def _fft_stage_b(a_ref, mb_ref, consume):
    def body(j, carry):
        k2 = j * FFT_GROUP
        cols = []
        for g in range(FFT_GROUP):
            re = a_ref[pl.ds(k2 + g, FFT_N1, stride=FFT_A_PITCH), :]
            im = a_ref[pl.ds(FFT_N2 + k2 + g, FFT_N1, stride=FFT_A_PITCH), :]
            cols.append(jnp.concatenate([re, im], axis=0))
        x = _dot(mb_ref[...], jnp.concatenate(cols, axis=1).astype(BF16))
        for g in range(FFT_GROUP):
            consume(k2 + g, x[:, g * FFT_LANES:(g + 1) * FFT_LANES])
        return carry

    lax.fori_loop(0, FFT_N2 // FFT_GROUP, body, 0, unroll=FFT_UNROLL)


def _fft_inverse(z_ref, mbi_ref, mai_ref, u_ref, out_ref):
    half = FFT_N2 // 2

    def stage_b(j, carry):
        k2 = j * FFT_GROUP
        rhs = jnp.concatenate([z_ref[k2 + g] for g in range(FFT_GROUP)], axis=1)
        u = _dot(mbi_ref[...], rhs)
        for g in range(FFT_GROUP):
            u_ref[pl.ds(pl.multiple_of((k2 + g) * FFT_U_PITCH, 8), 2 * FFT_N1), :] = (
                u[:, g * FFT_LANES:(g + 1) * FFT_LANES])
        return carry

    lax.fori_loop(0, FFT_N2 // FFT_GROUP, stage_b, 0, unroll=FFT_UNROLL)

    def stage_a(n1, carry):
        re = u_ref[pl.ds(n1, FFT_N2, stride=FFT_U_PITCH), :]
        im = u_ref[pl.ds(FFT_N1 + n1, FFT_N2, stride=FFT_U_PITCH), :]
        rhs = jnp.concatenate([re, im], axis=0).astype(BF16)
        out_ref[pl.ds(n1, half, stride=FFT_N1), :] = _dot(mai_ref[n1], rhs)
        return carry

    lax.fori_loop(0, FFT_N1, stage_a, 0, unroll=FFT_UNROLL)


FFT_WORK_ROWS = max(FFT_N1 * FFT_A_PITCH, FFT_N2 * FFT_U_PITCH)


def _hyena_spectrum_kernel(hf0_ref, hf1_ref, hb0_ref, hb1_ref, ma_ref, mb_ref, k_ref, y_ref, a_ref):
    for o, (hf_ref, hb_ref) in enumerate(((hf0_ref, hb0_ref), (hf1_ref, hb1_ref))):
        hs, hd, inv_norm = _filter_halves(hf_ref, hb_ref)

        def keep_real(k2, x):
            k_ref[o, k2, 0:FFT_N1, :] = (x[:FFT_N1] * inv_norm).astype(k_ref.dtype)

        def keep_imag(k2, x):
            k_ref[o, k2, FFT_N1:, :] = (x[FFT_N1:] * inv_norm).astype(k_ref.dtype)

        for part, keep in ((hs, keep_real), (hd, keep_imag)):
            y_ref[...] = part
            _fft_stage_a(y_ref, ma_ref, a_ref)
            _fft_stage_b(a_ref, mb_ref, keep)


def _hyena_two_stage_kernel(zv_ref, z1_ref, z2_ref, wv_ref, w1_ref, w2_ref, k_ref, skip_ref,
                            ma_ref, mai_ref, mb_ref, mbi_ref, o_ref, y_ref, c_ref, a_ref, z_ref):
    y_ref[...] = _short_conv(zv_ref[0], wv_ref[...])
    for o, (g_ref, w_ref) in enumerate(((z1_ref, w1_ref), (z2_ref, w2_ref))):
        def multiply(k2, x):
            k = k_ref[o, k2].astype(F32)
            xr, xi = x[:FFT_N1], x[FFT_N1:]
            kr, ki = k[:FFT_N1], k[FFT_N1:]
            z_ref[k2] = jnp.concatenate([xr * kr - xi * ki, xr * ki + xi * kr], axis=0).astype(z_ref.dtype)

        _fft_stage_a(y_ref, ma_ref, a_ref)
        _fft_stage_b(a_ref, mb_ref, multiply)
        _fft_inverse(z_ref, mbi_ref, mai_ref, a_ref, c_ref)
        y = _short_conv(g_ref[0], w_ref[...]) * (c_ref[...] + y_ref[...] * skip_ref[o:o + 1])
        if o + 1 < HY_ORDER:
            y_ref[...] = y
        else:
            o_ref[0] = y.astype(o_ref.dtype)


def _hyena_two_stage(z, w_short, hfilt, skip, first=0, bsz=None):
    _, length, d3 = z.shape
    bsz = z.shape[0] if bsz is None else bsz
    d = d3 // 3
    dt = FFT_LANES
    nct = d // dt
    assert 2 * length == FFT_N1 * FFT_N2
    ma, mai, mb, mbi = _two_stage_tables()
    once = pl.Buffered(1)
    hspec = lambda direction, order: pl.BlockSpec(
        (length, dt), lambda c: (0, (direction * HY_ORDER + order) * nct + c))
    spectrum = pl.pallas_call(
        _hyena_spectrum_kernel,
        grid=(nct,),
        in_specs=[hspec(0, 0), hspec(0, 1), hspec(1, 0), hspec(1, 1),
                  pl.BlockSpec(ma.shape, lambda c: (0, 0, 0), pipeline_mode=once),
                  pl.BlockSpec(mb.shape, lambda c: (0, 0), pipeline_mode=once)],
        out_specs=pl.BlockSpec((HY_ORDER, FFT_N2, 2 * FFT_N1, dt), lambda c: (0, 0, 0, c)),
        out_shape=jax.ShapeDtypeStruct((HY_ORDER, FFT_N2, 2 * FFT_N1, d), BF16),
        scratch_shapes=[pltpu.VMEM((length, dt), F32), pltpu.VMEM((FFT_WORK_ROWS, dt), F32)],
        compiler_params=_cparams("parallel"),
        name="hyena_filter_spectrum",
    )(hfilt, hfilt, hfilt, hfilt, ma, mb)

    zspec = lambda part: pl.BlockSpec((1, length, dt), lambda c, b: (b + first, 0, part * nct + c))
    wspec = lambda part: pl.BlockSpec((3, dt), lambda c, b: (0, part * nct + c))
    fixed3 = lambda c, b: (0, 0, 0)
    fixed2 = lambda c, b: (0, 0)
    return pl.pallas_call(
        _hyena_two_stage_kernel,
        grid=(nct, bsz),
        in_specs=[zspec(0), zspec(1), zspec(2), wspec(0), wspec(1), wspec(2),
                  pl.BlockSpec((HY_ORDER, FFT_N2, 2 * FFT_N1, dt), lambda c, b: (0, 0, 0, c), pipeline_mode=once),
                  pl.BlockSpec((HY_ORDER, dt), lambda c, b: (0, c)),
                  pl.BlockSpec(ma.shape, fixed3, pipeline_mode=once),
                  pl.BlockSpec(mai.shape, fixed3, pipeline_mode=once),
                  pl.BlockSpec(mb.shape, fixed2, pipeline_mode=once),
                  pl.BlockSpec(mbi.shape, fixed2, pipeline_mode=once)],
        out_specs=pl.BlockSpec((1, length, dt), lambda c, b: (b, 0, c)),
        out_shape=jax.ShapeDtypeStruct((bsz, length, d), BF16),
        scratch_shapes=[pltpu.VMEM((length, dt), F32), pltpu.VMEM((length, dt), F32),
                        pltpu.VMEM((FFT_WORK_ROWS, dt), F32), pltpu.VMEM((FFT_N2, 2 * FFT_N1, dt), BF16)],
        compiler_params=_cparams("parallel", "arbitrary"),
        name="hyena_conv_two_stage",
    )(z, z, z, w_short, w_short, w_short, spectrum, skip, ma, mai, mb, mbi)


N_HEADS = 16
HEAD_DIM = D_MODEL // N_HEADS
HEADS_PER_STEP = 2
CTX_HEADS_PER_STEP = 8
GRID_W = 64
WIN_ROWS = 8
WIN_COLS = 16
NEG_INF = -1e30
NAT_Q_ROWS = 4
NAT_K_ROWS = NAT_Q_ROWS + WIN_ROWS - 1
ATTN_SCALE = HEAD_DIM ** -0.5


def _dot_nt(a, b):
    return lax.dot_general(a, b, (((1,), (1,)), ((), ())), preferred_element_type=F32)


def _qkv_kernel(x_ref, g_ref, sc_ref, sh_ref, w_ref, q_ref, k_ref, v_ref):
    h = _norm_mod(x_ref[...], g_ref[...], sc_ref[0], sh_ref[0])
    qkv = _dot(h.astype(BF16), w_ref[...])
    for part, ref in enumerate((q_ref, k_ref, v_ref)):
        for head in range(N_HEADS):
            lo = part * D_MODEL + head * HEAD_DIM
            ref[0, head] = qkv[:, lo:lo + HEAD_DIM].astype(ref.dtype)


def _qkv_proj(x, g, mods, w_bf16, seg, first_row, n_seq, seq_len, kv_dtype):
    d = x.shape[1]
    n_ctx, smp_len = seg
    t = ROW_TILE
    first_tile = first_row // t
    per_seq = seq_len // t
    out_spec = pl.BlockSpec((1, N_HEADS, t, HEAD_DIM), lambda i: (i // per_seq, 0, i % per_seq, 0))
    shape = (n_seq, N_HEADS, seq_len, HEAD_DIM)
    return pl.pallas_call(
        _qkv_kernel,
        grid=(n_seq * per_seq,),
        in_specs=[
            pl.BlockSpec((t, d), lambda i: (i + first_tile, 0)),
            pl.BlockSpec((1, d), lambda i: (0, 0)),
            _mod_spec(1, t, n_ctx, smp_len, first_tile),
            _mod_spec(0, t, n_ctx, smp_len, first_tile),
            pl.BlockSpec(w_bf16.shape, lambda i: (0, 0)),
        ],
        out_specs=[out_spec, out_spec, out_spec],
        out_shape=[jax.ShapeDtypeStruct(shape, BF16), jax.ShapeDtypeStruct(shape, kv_dtype),
                   jax.ShapeDtypeStruct(shape, kv_dtype)],
        compiler_params=_cparams("parallel"),
        name="qkv_proj",
    )(x, g.reshape(1, d), mods, mods, w_bf16)


def _ctx_attn_kernel(q_ref, k_ref, v_ref, o_ref):
    for j in range(q_ref.shape[1]):
        q = q_ref[0, j]
        k = k_ref[0, j].astype(BF16)
        v = v_ref[0, j].astype(BF16)
        s = _dot_nt(q, k) * ATTN_SCALE
        p = jnp.exp(s - jnp.max(s, axis=-1, keepdims=True))
        o = _dot(p.astype(BF16), v) / jnp.sum(p, axis=-1, keepdims=True)
        o_ref[:, j * HEAD_DIM:(j + 1) * HEAD_DIM] = o.astype(o_ref.dtype)


def _ctx_attention(q, k, v):
    bsz, _, s, _ = q.shape
    hp = CTX_HEADS_PER_STEP
    spec = pl.BlockSpec((1, hp, s, HEAD_DIM), lambda b, h: (b, h, 0, 0))
    return pl.pallas_call(
        _ctx_attn_kernel,
        grid=(bsz, N_HEADS // hp),
        in_specs=[spec, spec, spec],
        out_specs=pl.BlockSpec((s, hp * HEAD_DIM), lambda b, h: (b, h)),
        out_shape=jax.ShapeDtypeStruct((bsz * s, D_MODEL), BF16),
        compiler_params=_cparams("parallel", "parallel"),
        name="ctx_attention",
    )(q, k, v)


def _rpb_toeplitz_kernel(r_ref, e_ref, o_ref):
    o_ref[...] = _dot3(r_ref[...], e_ref[...])


def _nat_bias(rpb):
    n_heads, n_r, n_c = rpb.shape
    n_cp = 32
    qc = np.arange(GRID_W)[:, None]
    kc = np.arange(GRID_W)[None, :]
    onehot = (np.clip(kc - qc + WIN_COLS - 1, 0, n_c - 1)[None] == np.arange(n_cp)[:, None, None])
    onehot = jnp.asarray(onehot.reshape(n_cp, GRID_W * GRID_W), F32)
    rows = jnp.pad(rpb.reshape(n_heads * n_r, n_c), ((0, 0), (0, n_cp - n_c)))
    toep = pl.pallas_call(
        _rpb_toeplitz_kernel,
        out_shape=jax.ShapeDtypeStruct((n_heads * n_r, GRID_W * GRID_W), F32),
        compiler_params=_cparams(),
        name="nat_bias_toeplitz",
    )(rows, onehot).reshape(n_heads, n_r, GRID_W, GRID_W)
    q_start = np.clip(qc - WIN_COLS // 2, 0, GRID_W - WIN_COLS)
    col_ok = jnp.asarray((kc >= q_start) & (kc < q_start + WIN_COLS))
    toep = jnp.where(col_ok, toep, NEG_INF)
    toep = jnp.concatenate([toep, jnp.full((n_heads, 1, GRID_W, GRID_W), NEG_INF, F32)], axis=1)
    rows_total = GRID_W
    idx = np.full((3, NAT_Q_ROWS, NAT_K_ROWS), n_r, np.int32)
    for case, r0 in enumerate((0, NAT_Q_ROWS, rows_total - NAT_Q_ROWS)):
        ks = int(np.clip(r0 - WIN_ROWS // 2, 0, rows_total - NAT_K_ROWS))
        for dr in range(NAT_Q_ROWS):
            r = r0 + dr
            rs = int(np.clip(r - WIN_ROWS // 2, 0, rows_total - WIN_ROWS))
            for dk in range(NAT_K_ROWS):
                kr = ks + dk
                if rs <= kr < rs + WIN_ROWS:
                    idx[case, dr, dk] = kr - r + WIN_ROWS - 1
    def assemble(t_ref, o_ref):
        for case in range(3):
            for dr in range(NAT_Q_ROWS):
                for dk in range(NAT_K_ROWS):
                    o_ref[0, case, dr * GRID_W:(dr + 1) * GRID_W, dk * GRID_W:(dk + 1) * GRID_W] = (
                        t_ref[0, int(idx[case, dr, dk])])

    return pl.pallas_call(
        assemble,
        grid=(n_heads,),
        in_specs=[pl.BlockSpec((1, n_r + 1, GRID_W, GRID_W), lambda h: (h, 0, 0, 0))],
        out_specs=pl.BlockSpec((1, 3, NAT_Q_ROWS * GRID_W, NAT_K_ROWS * GRID_W), lambda h: (h, 0, 0, 0)),
        out_shape=jax.ShapeDtypeStruct((n_heads, 3, NAT_Q_ROWS * GRID_W, NAT_K_ROWS * GRID_W), F32),
        compiler_params=_cparams("parallel"),
        name="nat_bias_assemble",
    )(toep)


def _nat_kernel(q_ref, k_ref, v_ref, kc_ref, vc_ref, bias_ref, o_ref):
    n_blocks = q_ref.shape[2] // (NAT_Q_ROWS * GRID_W)
    rows_total = q_ref.shape[2] // GRID_W
    nq = NAT_Q_ROWS * GRID_W
    nk = NAT_K_ROWS * GRID_W
    ctx = [(kc_ref[0, 0, j].astype(BF16), vc_ref[0, 0, j].astype(BF16)) for j in range(HEADS_PER_STEP)]

    def block(blk, carry):
        ks = jnp.clip(blk * NAT_Q_ROWS - WIN_ROWS // 2, 0, rows_total - NAT_K_ROWS)
        case = jnp.where(blk == 0, 0, jnp.where(blk == n_blocks - 1, 2, 1))
        q_rows = pl.ds(pl.multiple_of(blk * nq, nq), nq)
        k_rows = pl.ds(pl.multiple_of(ks * GRID_W, GRID_W), nk)
        for j, (k_ctx, v_ctx) in enumerate(ctx):
            q = q_ref[0, j, q_rows, :]
            s_loc = _dot_nt(q, k_ref[0, j, k_rows, :]) * ATTN_SCALE + bias_ref[j, case]
            s_ctx = _dot_nt(q, k_ctx) * ATTN_SCALE
            m = jnp.maximum(jnp.max(s_loc, axis=-1, keepdims=True), jnp.max(s_ctx, axis=-1, keepdims=True))
            p_loc = jnp.exp(s_loc - m)
            p_ctx = jnp.exp(s_ctx - m)
            denom = jnp.sum(p_loc, axis=-1, keepdims=True) + jnp.sum(p_ctx, axis=-1, keepdims=True)
            o = (_dot(p_loc.astype(BF16), v_ref[0, j, k_rows, :]) + _dot(p_ctx.astype(BF16), v_ctx)) / denom
            o_ref[q_rows, j * HEAD_DIM:(j + 1) * HEAD_DIM] = o.astype(o_ref.dtype)
        return carry

    lax.fori_loop(0, n_blocks, block, 0, unroll=2)


def _nat_attention(q, k, v, cache_k, cache_v, bias):
    bsz, _, length, _ = q.shape
    hp = HEADS_PER_STEP
    past = cache_k.shape[3]
    spec = pl.BlockSpec((1, hp, length, HEAD_DIM), lambda b, h: (b, h, 0, 0))
    cspec = pl.BlockSpec((1, 1, hp, past, HEAD_DIM), lambda b, h: (b, 0, h, 0, 0))
    return pl.pallas_call(
        _nat_kernel,
        grid=(bsz, N_HEADS // hp),
        in_specs=[spec, spec, spec, cspec, cspec,
                  pl.BlockSpec((hp,) + bias.shape[1:], lambda b, h: (h, 0, 0, 0))],
        out_specs=pl.BlockSpec((length, hp * HEAD_DIM), lambda b, h: (b, h)),
        out_shape=jax.ShapeDtypeStruct((bsz * length, D_MODEL), BF16),
        compiler_params=_cparams("parallel", "parallel"),
        name="nat_attention",
    )(q, k, v, cache_k, cache_v, bias)


def kernel(x_prompt, x_sample, cache_k, cache_v, c, c_ctx, ada_w, ada_b, norm1_g, norm2_g, normf_g, hy_w_in, hy_w_short, hy_f_w1, hy_f_b1, hy_f_freq, hy_f_w2, hy_f_b2, hy_f_w3, hy_skip, hy_w_out, na_w_qkv, na_rpb, na_w_o, moe_w_router, moe_b_router, moe_w_gate, moe_b_gate, moe_w_up, moe_b_up, moe_w_down, moe_b_down):
    b, s, d = x_prompt.shape
    bd, sd, _ = x_sample.shape
    n_ctx = b * s
    n_tok = n_ctx + bd * sd
    assert n_ctx % sd == 0 and s % ROW_TILE == 0 and sd % ROW_TILE == 0
    seg = (n_ctx, sd)
    x = jnp.concatenate([x_prompt.reshape(n_ctx, d), x_sample.reshape(bd * sd, d)], axis=0)
    cond = jnp.zeros((N_COND, d), F32).at[0].set(c_ctx).at[1:1 + bd].set(c)
    mods = _modulation(cond, ada_w, ada_b)

    def moe(x, i, final_norm):
        return _moe_layer(x, norm2_g[i], mods[i], moe_w_router[i], moe_b_router[i], i, moe_w_gate, moe_b_gate,
                          moe_w_up, moe_b_up, moe_w_down, moe_b_down, normf_g, seg, final_norm)

    z = _norm_proj(x, norm1_g[0], mods[0], hy_w_in[0].astype(BF16), seg)
    fargs = (hy_f_w1[0], hy_f_b1[0], hy_f_freq[0], hy_f_w2[0], hy_f_b2[0], hy_f_w3[0])
    y_ctx = _hyena_direct(z.reshape(n_tok // s, s, 3 * d), hy_w_short[0], _hyena_filters(s, *fargs),
                          hy_skip[0], first=0, bsz=b)
    y_smp = _hyena_two_stage(z.reshape(n_tok // sd, sd, 3 * d), hy_w_short[0], _hyena_filters(sd, *fargs),
                             hy_skip[0], first=n_ctx // sd, bsz=bd)
    x = _out_proj(y_ctx.reshape(n_ctx, d), y_smp.reshape(bd * sd, d), hy_w_out[0].astype(BF16), x, mods[0], seg)
    x = moe(x, 0, False)

    w_qkv = na_w_qkv[0].astype(BF16)
    q_c, k_c, v_c = _qkv_proj(x, norm1_g[1], mods[1], w_qkv, seg, 0, b, s, F32)
    q_s, k_s, v_s = _qkv_proj(x, norm1_g[1], mods[1], w_qkv, seg, n_ctx, bd, sd, BF16)
    o_ctx = _ctx_attention(q_c, k_c, v_c)
    o_smp = _nat_attention(q_s, k_s, v_s, cache_k, cache_v, _nat_bias(na_rpb[0]))
    x = _out_proj(o_ctx, o_smp, na_w_o[0].astype(BF16), x, mods[1], seg)
    y_prompt, y_sample = moe(x, 1, True)

    nh, hd = k_c.shape[1], k_c.shape[3]
    return (y_prompt.reshape(b, s, d), y_sample.reshape(bd, sd, d),
            k_c.reshape(b, 1, nh, s, hd), v_c.reshape(b, 1, nh, s, hd))
```

```python
import functools
import math

import numpy as np
import jax
import jax.numpy as jnp
from jax import lax
from jax.experimental import pallas as pl
from jax.experimental.pallas import tpu as pltpu

F32 = jnp.float32
BF16 = jnp.bfloat16

D_MODEL = 1024
N_MOD = 6
RMS_EPS = 1e-6
N_EXPERTS = 32
TOP_K = 4
SWIGLU_LIMIT = 7.0
SWIGLU_ALPHA = 1.702

N_COND = 8
ROW_TILE = 256
MOE_TILE = 256
V7X_VMEM_LIMIT = 56 * 1024 * 1024


def _cparams(*sem, vmem=V7X_VMEM_LIMIT):
    return pltpu.CompilerParams(dimension_semantics=sem, vmem_limit_bytes=vmem)


def _dot(a, b):
    return jnp.dot(a, b, preferred_element_type=F32)


def _split_bf16(x):
    hi = x.astype(BF16)
    lo = (x - hi.astype(F32)).astype(BF16)
    return hi, lo


def _dot3(a, b):
    ah, al = _split_bf16(a)
    bh, bl = _split_bf16(b)
    return _dot(ah, bh) + (_dot(al, bh) + _dot(ah, bl))


def _seg_of_tile(i, tile, n_ctx, smp_len):
    ctx_tiles = n_ctx // tile
    per_smp = smp_len // tile
    return jnp.where(i < ctx_tiles, 0, 1 + (i - ctx_tiles) // per_smp)


def _norm_mod(x, g, sc, sh):
    y = x * lax.rsqrt(jnp.mean(x * x, axis=-1, keepdims=True) + RMS_EPS)
    return (y * g) * (1.0 + sc) + sh


def _mod_kernel(c_ref, w_ref, b_ref, o_ref):
    c = c_ref[...]
    a = c * jax.nn.sigmoid(c)
    o_ref[0] = _dot3(a, w_ref[0]) + b_ref[0]


def _modulation(cond, ada_w, ada_b):
    depth, d, n_out = ada_w.shape
    tn = 1536
    m = pl.pallas_call(
        _mod_kernel,
        grid=(depth, n_out // tn),
        in_specs=[
            pl.BlockSpec((N_COND, d), lambda l, j: (0, 0)),
            pl.BlockSpec((1, d, tn), lambda l, j: (l, 0, j)),
            pl.BlockSpec((1, 1, tn), lambda l, j: (l, 0, j)),
        ],
        out_specs=pl.BlockSpec((1, N_COND, tn), lambda l, j: (l, 0, j)),
        out_shape=jax.ShapeDtypeStruct((depth, N_COND, n_out), F32),
        compiler_params=_cparams("arbitrary", "arbitrary"),
        name="adaln_modulation",
    )(cond, ada_w, ada_b.reshape(depth, 1, n_out))
    m = m.reshape(depth, N_COND, N_MOD, d)
    return jnp.transpose(m, (0, 2, 1, 3)).reshape(depth, N_MOD * N_COND, 1, d)


def _mod_spec(which, tile, n_ctx, smp_len, first_tile=0):
    return pl.BlockSpec(
        (1, 1, D_MODEL),
        lambda i, *_: (which * N_COND + _seg_of_tile(i + first_tile, tile, n_ctx, smp_len), 0, 0))


def _norm_proj_kernel(x_ref, g_ref, sc_ref, sh_ref, w_ref, o_ref):
    h = _norm_mod(x_ref[...], g_ref[...], sc_ref[0], sh_ref[0])
    o_ref[...] = _dot(h.astype(BF16), w_ref[...]).astype(o_ref.dtype)


def _norm_proj(x, g, mods, w_bf16, seg, out_dtype=F32):
    n, d = x.shape
    n_out = w_bf16.shape[1]
    n_ctx, smp_len = seg
    return pl.pallas_call(
        _norm_proj_kernel,
        grid=(n // ROW_TILE,),
        in_specs=[
            pl.BlockSpec((ROW_TILE, d), lambda i: (i, 0)),
            pl.BlockSpec((1, d), lambda i: (0, 0)),
            _mod_spec(1, ROW_TILE, n_ctx, smp_len),
            _mod_spec(0, ROW_TILE, n_ctx, smp_len),
            pl.BlockSpec((d, n_out), lambda i: (0, 0)),
        ],
        out_specs=pl.BlockSpec((ROW_TILE, n_out), lambda i: (i, 0)),
        out_shape=jax.ShapeDtypeStruct((n, n_out), out_dtype),
        compiler_params=_cparams("parallel"),
        name="norm_proj",
    )(x, g.reshape(1, d), mods, mods, w_bf16)


def _out_proj_kernel(ya_ref, yb_ref, w_ref, x_ref, gate_ref, o_ref, *, ctx_tiles):
    y = jnp.where(pl.program_id(0) < ctx_tiles, ya_ref[...], yb_ref[...])
    o_ref[...] = x_ref[...] + gate_ref[0] * _dot(y, w_ref[...])


def _out_proj(y_ctx, y_smp, w_bf16, x, mods, seg):
    n, d = x.shape
    n_ctx, smp_len = seg
    ctx_tiles = n_ctx // ROW_TILE
    return pl.pallas_call(
        functools.partial(_out_proj_kernel, ctx_tiles=ctx_tiles),
        grid=(n // ROW_TILE,),
        in_specs=[
            pl.BlockSpec((ROW_TILE, d), lambda i: (jnp.minimum(i, ctx_tiles - 1), 0)),
            pl.BlockSpec((ROW_TILE, d), lambda i: (jnp.maximum(i - ctx_tiles, 0), 0)),
            pl.BlockSpec((d, d), lambda i: (0, 0)),
            pl.BlockSpec((ROW_TILE, d), lambda i: (i, 0)),
            _mod_spec(2, ROW_TILE, n_ctx, smp_len),
        ],
        out_specs=pl.BlockSpec((ROW_TILE, d), lambda i: (i, 0)),
        out_shape=jax.ShapeDtypeStruct((n, d), F32),
        compiler_params=_cparams("parallel"),
        name="out_proj",
    )(y_ctx, y_smp, w_bf16, x, mods)


LANES = 128
ROW_SUBLANES = D_MODEL // LANES


def _store_row_tiles(ref, x):
    t = x.shape[0]
    for s in range(ROW_SUBLANES):
        ref[pl.ds(s, t, stride=ROW_SUBLANES), :] = x[:, s * LANES:(s + 1) * LANES]


def _load_row_tiles(ref, t):
    return jnp.concatenate([ref[pl.ds(s, t, stride=ROW_SUBLANES), :] for s in range(ROW_SUBLANES)], axis=1)


def _columns(cols):
    t = cols[0].shape[0]
    lane = lax.broadcasted_iota(jnp.int32, (t, len(cols)), 1)
    out = jnp.broadcast_to(cols[-1], (t, len(cols)))
    for k in range(len(cols) - 2, -1, -1):
        out = jnp.where(lane == k, cols[k], out)
    return out


def _router_kernel(x_ref, g_ref, sc_ref, sh_ref, wr_ref, br_ref, tri_ref,
                   h_ref, e_ref, gate_ref, rank_ref, cnt_ref, run_ref):
    i = pl.program_id(0)

    @pl.when(i == 0)
    def _():
        run_ref[...] = jnp.zeros_like(run_ref)

    h = _norm_mod(x_ref[...], g_ref[...], sc_ref[0], sh_ref[0])
    _store_row_tiles(h_ref, h)
    logits = _dot3(h, wr_ref[...]) + br_ref[...]
    lane = lax.broadcasted_iota(jnp.int32, logits.shape, 1)
    work = logits
    vals, idxs, hots = [], [], []
    for _ in range(TOP_K):
        m = jnp.max(work, axis=-1, keepdims=True)
        idx = jnp.min(jnp.where(work == m, lane, N_EXPERTS), axis=-1, keepdims=True)
        hot = lane == idx
        vals.append(m)
        idxs.append(idx)
        hots.append(hot)
        work = jnp.where(hot, -jnp.inf, work)
    ex = [jnp.exp(v - vals[0]) for v in vals]
    denom = ex[0] + ex[1] + ex[2] + ex[3]
    gate_ref[...] = _columns([e / denom for e in ex])
    e_ref[...] = _columns(idxs)

    chosen = (hots[0] | hots[1] | hots[2] | hots[3]).astype(F32)
    before = run_ref[...] + _dot(tri_ref[...], chosen.astype(BF16))
    ranks = [jnp.sum(jnp.where(hot, before, 0.0), axis=-1, keepdims=True) for hot in hots]
    rank_ref[...] = _columns(ranks).astype(jnp.int32)
    run_ref[...] += jnp.sum(chosen, axis=0, keepdims=True)
    cnt_ref[...] = run_ref[...].astype(jnp.int32)


def _router(x, g, mods, w_r, b_r, seg):
    n, d = x.shape
    n_ctx, smp_len = seg
    t = ROW_TILE
    tri = jnp.asarray(np.tril(np.ones((t, t), np.float32), -1), BF16)
    tok4 = lambda i: (i, 0)
    return pl.pallas_call(
        _router_kernel,
        grid=(n // t,),
        in_specs=[
            pl.BlockSpec((t, d), lambda i: (i, 0)),
            pl.BlockSpec((1, d), lambda i: (0, 0)),
            _mod_spec(4, t, n_ctx, smp_len),
            _mod_spec(3, t, n_ctx, smp_len),
            pl.BlockSpec((d, N_EXPERTS), lambda i: (0, 0)),
            pl.BlockSpec((1, N_EXPERTS), lambda i: (0, 0)),
            pl.BlockSpec((t, t), lambda i: (0, 0)),
        ],
        out_specs=[
            pl.BlockSpec((t * ROW_SUBLANES, LANES), tok4),
            pl.BlockSpec((t, TOP_K), tok4),
            pl.BlockSpec((t, TOP_K), tok4),
            pl.BlockSpec((t, TOP_K), tok4),
            pl.BlockSpec((1, N_EXPERTS), lambda i: (0, 0)),
        ],
        out_shape=[
            jax.ShapeDtypeStruct((n * ROW_SUBLANES, LANES), F32),
            jax.ShapeDtypeStruct((n, TOP_K), jnp.int32),
            jax.ShapeDtypeStruct((n, TOP_K), F32),
            jax.ShapeDtypeStruct((n, TOP_K), jnp.int32),
            jax.ShapeDtypeStruct((1, N_EXPERTS), jnp.int32),
        ],
        scratch_shapes=[pltpu.VMEM((1, N_EXPERTS), F32)],
        compiler_params=_cparams("arbitrary"),
        name="moe_router",
    )(x, g.reshape(1, d), mods, mods, w_r, b_r.reshape(1, N_EXPERTS), tri)


def _tile_copy(src_ref, src_row, dst_ref, dst_row, sem):
    rows = lambda r: pl.ds(pl.multiple_of(r, ROW_SUBLANES), ROW_SUBLANES)
    return pltpu.make_async_copy(src_ref.at[rows(src_row), :], dst_ref.at[rows(dst_row), :], sem)


def _tiles_wait(src_ref, dst_ref, n_tiles, sem):
    rows = pl.ds(0, n_tiles * ROW_SUBLANES)
    pltpu.make_async_copy(src_ref.at[rows, :], dst_ref.at[rows, :], sem).wait()


ENTRIES_PER_TILE = ROW_TILE * TOP_K
DMA_PRIORITIES = 2


def _dispatch_kernel(fill_start_ref, fill_len_ref, na_ref, dest_ref, h_ref, xs_ref, hbuf, zero_ref, sem):
    i = pl.program_id(0)
    last = pl.num_programs(0) - 1
    slot = lax.rem(i, 2)
    hbuf[slot] = h_ref[...]

    def issue(t, carry):
        for k in range(TOP_K):
            _tile_copy(hbuf.at[slot], t * ROW_SUBLANES, xs_ref, dest_ref[t * TOP_K + k],
                       sem.at[slot]).start(priority=k % DMA_PRIORITIES)
        return carry

    lax.fori_loop(0, ROW_TILE, issue, 0, unroll=4)

    @pl.when(i >= 1)
    def _():
        _tiles_wait(xs_ref, xs_ref, ENTRIES_PER_TILE, sem.at[1 - slot])

    @pl.when(i == last)
    def _():
        _tiles_wait(xs_ref, xs_ref, ENTRIES_PER_TILE, sem.at[slot])
        zero_ref[...] = jnp.zeros_like(zero_ref)
        for e in range(N_EXPERTS):
            start = fill_start_ref[e]
            count = fill_len_ref[e]

            def fill(r, carry):
                _tile_copy(zero_ref, 0, xs_ref, (start + r) * ROW_SUBLANES, sem.at[0]).start()
                return carry

            lax.fori_loop(0, count, fill, 0)

            def fill_drain(r, carry):
                _tile_copy(zero_ref, 0, xs_ref, 0, sem.at[0]).wait()
                return carry

            lax.fori_loop(0, count, fill_drain, 0)

        block_rows = MOE_TILE * ROW_SUBLANES

        def block_copy(blk):
            return pltpu.make_async_copy(
                zero_ref, xs_ref.at[pl.ds(pl.multiple_of(blk * block_rows, block_rows), block_rows), :], sem.at[0])

        n_blocks = xs_ref.shape[0] // block_rows

        def tail(blk, carry):
            block_copy(blk).start()
            return carry

        lax.fori_loop(na_ref[0], n_blocks, tail, 0)

        def tail_drain(blk, carry):
            block_copy(0).wait()
            return carry

        lax.fori_loop(na_ref[0], n_blocks, tail_drain, 0)


def _dispatch(h_tiles, dest_rows, fill_start, fill_len, n_active, cap):
    n = h_tiles.shape[0] // ROW_SUBLANES
    grid_spec = pltpu.PrefetchScalarGridSpec(
        num_scalar_prefetch=3,
        grid=(n // ROW_TILE,),
        in_specs=[
            pl.BlockSpec((ENTRIES_PER_TILE,), lambda i, *_: (i,), memory_space=pltpu.SMEM),
            pl.BlockSpec((ROW_TILE * ROW_SUBLANES, LANES), lambda i, *_: (i, 0)),
        ],
        out_specs=pl.BlockSpec(memory_space=pl.ANY),
        scratch_shapes=[pltpu.VMEM((2, ROW_TILE * ROW_SUBLANES, LANES), F32),
                        pltpu.VMEM((MOE_TILE * ROW_SUBLANES, LANES), F32), pltpu.SemaphoreType.DMA((2,))],
    )
    return pl.pallas_call(
        _dispatch_kernel,
        grid_spec=grid_spec,
        out_shape=jax.ShapeDtypeStruct((cap * ROW_SUBLANES, LANES), F32),
        compiler_params=_cparams("arbitrary"),
        name="moe_dispatch",
    )(fill_start, fill_len, n_active, dest_rows, h_tiles)


def _ffn_kernel(be_ref, na_ref, wslot_ref, next_e_ref, xs_ref, wg_ref, bg_ref, wu_ref, bu_ref, wd_ref, bd_ref,
                ys_ref, wbuf, wg_s, wu_s, wd_s, wsem, *, layer):
    i = pl.program_id(0)
    n_active = na_ref[0]
    tm = xs_ref.shape[0] // ROW_SUBLANES
    changed = jnp.logical_or(i == 0, be_ref[i] != be_ref[jnp.maximum(i - 1, 0)])
    fresh = jnp.logical_and(i < n_active, changed)

    def weight_copies(expert, b):
        return [pltpu.make_async_copy(w_ref.at[layer, expert], wbuf.at[b, m], wsem.at[b])
                for m, w_ref in enumerate((wg_ref, wu_ref, wd_ref))]

    @pl.when(jnp.logical_and(i == 0, n_active > 0))
    def _():
        for copy in weight_copies(be_ref[0], wslot_ref[0]):
            copy.start()

    @pl.when(fresh)
    def _():
        b = wslot_ref[i]
        for copy in weight_copies(be_ref[i], b):
            copy.wait()
        wg_s[...] = wbuf[b, 0].astype(BF16)
        wu_s[...] = wbuf[b, 1].astype(BF16)
        wd_s[...] = wbuf[b, 2].astype(BF16)
        upcoming = next_e_ref[i]

        @pl.when(upcoming >= 0)
        def _():
            for copy in weight_copies(upcoming, 1 - b):
                copy.start()

    @pl.when(i < n_active)
    def _():
        x = _load_row_tiles(xs_ref, tm).astype(BF16)
        g = jnp.minimum(_dot(x, wg_s[...]) + bg_ref[0, 0], SWIGLU_LIMIT)
        u = jnp.clip(_dot(x, wu_s[...]) + bu_ref[0, 0], -SWIGLU_LIMIT, SWIGLU_LIMIT)
        a = g * jax.nn.sigmoid(SWIGLU_ALPHA * g) * (u + 1.0)
        _store_row_tiles(ys_ref, _dot(a.astype(BF16), wd_s[...]) + bd_ref[0, 0])

    @pl.when(i >= na_ref[0])
    def _():
        ys_ref[...] = jnp.zeros_like(ys_ref)


def _expert_ffn(xs, block_e, n_active, weight_slot, next_expert, layer, w_g, b_g, w_u, b_u, w_d, b_d):
    d = D_MODEL
    depth, ne, _, f = w_g.shape
    assert f == d
    nb = block_e.shape[0]
    block = (MOE_TILE * ROW_SUBLANES, LANES)
    bmap = lambda i, be, *_: (layer, be[i], 0, 0)
    hbm = pl.BlockSpec(memory_space=pl.ANY)
    grid_spec = pltpu.PrefetchScalarGridSpec(
        num_scalar_prefetch=4,
        grid=(nb,),
        in_specs=[
            pl.BlockSpec(block, lambda i, be, na, *_: (jnp.minimum(i, na[0] - 1), 0)),
            hbm, pl.BlockSpec((1, 1, 1, f), bmap),
            hbm, pl.BlockSpec((1, 1, 1, f), bmap),
            hbm, pl.BlockSpec((1, 1, 1, d), bmap),
        ],
        out_specs=pl.BlockSpec(block, lambda i, *_: (i, 0)),
        scratch_shapes=[pltpu.VMEM((2, 3, d, f), F32),
                        pltpu.VMEM((d, f), BF16), pltpu.VMEM((d, f), BF16), pltpu.VMEM((f, d), BF16),
                        pltpu.SemaphoreType.DMA((2,))],
    )
    return pl.pallas_call(
        functools.partial(_ffn_kernel, layer=layer),
        grid_spec=grid_spec,
        out_shape=jax.ShapeDtypeStruct(xs.shape, F32),
        compiler_params=_cparams("arbitrary"),
        name="moe_expert_ffn",
    )(block_e, n_active, weight_slot, next_expert, xs,
      w_g, b_g.reshape(depth, ne, 1, f), w_u, b_u.reshape(depth, ne, 1, f), w_d, b_d.reshape(depth, ne, 1, d))


def _combine_kernel(dest_ref, dest_next_ref, x_ref, gate_ref, g2_ref, gf_ref, ys_ref, o_ref, buf, sem, *,
                    final_norm):
    i = pl.program_id(0)
    slot = lax.rem(i, 2)
    t = x_ref.shape[0]
    plane = t * ROW_SUBLANES

    def start_gather(idx_ref, b):
        def body(r, carry):
            for k in range(TOP_K):
                _tile_copy(ys_ref, idx_ref[r * TOP_K + k], buf.at[b], k * plane + r * ROW_SUBLANES,
                           sem.at[b]).start(priority=k % DMA_PRIORITIES)
            return carry

        lax.fori_loop(0, t, body, 0, unroll=4)

    @pl.when(i == 0)
    def _():
        start_gather(dest_ref, 0)

    @pl.when(i + 1 < pl.num_programs(0))
    def _():
        start_gather(dest_next_ref, 1 - slot)

    _tiles_wait(ys_ref, buf.at[slot], TOP_K * t, sem.at[slot])
    gates = gate_ref[...]
    acc = None
    for k in range(TOP_K):
        term = gates[:, k:k + 1] * _load_row_tiles(buf.at[slot, pl.ds(k * plane, plane), :], t)
        acc = term if acc is None else acc + term
    y = x_ref[...] + g2_ref[0] * acc
    if final_norm:
        y = y * lax.rsqrt(jnp.mean(y * y, axis=-1, keepdims=True) + RMS_EPS) * gf_ref[...]
    o_ref[...] = y


def _combine(x, ys, dest_rows, gates, mods, normf_g, seg, final_norm, first_row=0, n_rows=None):
    n, d = x.shape
    n_rows = n if n_rows is None else n_rows
    n_ctx, smp_len = seg
    t = ROW_TILE
    first = first_row // t
    steps = n_rows // t
    rows = lambda i: (i + first, 0)
    return pl.pallas_call(
        functools.partial(_combine_kernel, final_norm=final_norm),
        grid=(steps,),
        in_specs=[
            pl.BlockSpec((ENTRIES_PER_TILE,), lambda i: (i + first,), memory_space=pltpu.SMEM),
            pl.BlockSpec((ENTRIES_PER_TILE,), lambda i: (jnp.minimum(i + 1, steps - 1) + first,),
                         memory_space=pltpu.SMEM),
            pl.BlockSpec((t, d), rows),
            pl.BlockSpec((t, TOP_K), rows),
            _mod_spec(5, t, n_ctx, smp_len, first),
            pl.BlockSpec((1, d), lambda i: (0, 0)),
            pl.BlockSpec(memory_space=pl.ANY),
        ],
        out_specs=pl.BlockSpec((t, d), lambda i: (i, 0)),
        out_shape=jax.ShapeDtypeStruct((n_rows, d), F32),
        scratch_shapes=[pltpu.VMEM((2, TOP_K * t * ROW_SUBLANES, LANES), F32), pltpu.SemaphoreType.DMA((2,))],
        compiler_params=_cparams("arbitrary"),
        name="moe_combine",
    )(dest_rows, dest_rows, x, gates, mods, normf_g.reshape(1, d), ys)


def _moe_layer(x, norm_g, mods, w_r, b_r, layer, w_g, b_g, w_u, b_u, w_d, b_d, normf_g, seg, final_norm):
    n, d = x.shape
    nk = n * TOP_K
    h, top_e, gates, rank, counts = _router(x, norm_g, mods, w_r, b_r, seg)
    counts = counts[0]
    padded = (counts + MOE_TILE - 1) // MOE_TILE * MOE_TILE
    pad_end = jnp.cumsum(padded)
    pad_start = pad_end - padded
    cap = nk + N_EXPERTS * MOE_TILE
    nb = cap // MOE_TILE
    experts = jnp.arange(N_EXPERTS, dtype=jnp.int32)
    start_of = jnp.sum(jnp.where(top_e[..., None] == experts, pad_start, 0), axis=-1)
    dest_rows = ((start_of + rank) * ROW_SUBLANES).reshape(nk).astype(jnp.int32)
    blk_start = jnp.arange(nb, dtype=jnp.int32) * MOE_TILE
    block_e = jnp.minimum(jnp.sum(blk_start[:, None] >= pad_end[None, :], axis=1), N_EXPERTS - 1).astype(jnp.int32)
    n_active = (pad_end[-1] // MOE_TILE).astype(jnp.int32).reshape(1)
    xs = _dispatch(h, dest_rows, (pad_start + counts).astype(jnp.int32), (padded - counts).astype(jnp.int32),
                   n_active, cap)
    changes = jnp.concatenate([jnp.zeros((1,), jnp.int32), (block_e[1:] != block_e[:-1]).astype(jnp.int32)])
    weight_slot = (jnp.cumsum(changes) % 2).astype(jnp.int32)
    later = (experts[None, :] > experts[:, None]) & (counts[None, :] > 0)
    next_of = jnp.min(jnp.where(later, experts[None, :], N_EXPERTS), axis=1)
    next_expert = jnp.where(next_of < N_EXPERTS, next_of, -1).astype(jnp.int32)[block_e]
    ys = _expert_ffn(xs, block_e, n_active, weight_slot, next_expert, layer, w_g, b_g, w_u, b_u, w_d, b_d)
    if not final_norm:
        return _combine(x, ys, dest_rows, gates, mods, normf_g, seg, False)
    n_ctx = seg[0]
    return (_combine(x, ys, dest_rows, gates, mods, normf_g, seg, True, 0, n_ctx),
            _combine(x, ys, dest_rows, gates, mods, normf_g, seg, True, n_ctx, n - n_ctx))


HY_ORDER = 2
HY_BANDS = 16
HY_EMB = 1 + 2 * HY_BANDS
HY_FFN = 64
HY_MIN_DECAY = math.log(1e-2) / 1.5
HY_MAX_DECAY = math.log(1e-2) / 0.3
HY_EMB_PAD = 64


def _filter_mlp_kernel(z_ref, t_ref, w1_ref, b1_ref, fr_ref, w2_ref, b2_ref, w3_ref, dl_ref, o_ref, h_ref):
    @pl.when(pl.program_id(1) == 0)
    def _():
        fr = fr_ref[...]
        h = jnp.sin(fr * (_dot3(z_ref[...], w1_ref[...]) + b1_ref[...]))
        h_ref[...] = jnp.sin(fr * (_dot3(h, w2_ref[...]) + b2_ref[...]))

    o_ref[...] = _dot3(h_ref[...], w3_ref[...]) * jnp.exp(-t_ref[...] * dl_ref[...])


def _hyena_filters(length, w1, b1, freq, w2, b2, w3):
    t = jnp.linspace(0.0, 1.0, length, dtype=F32)[:, None]
    ang = (2.0 * math.pi / length) * jnp.arange(length, dtype=F32)[:, None]
    bands = jnp.linspace(1e-4, HY_BANDS - 1, HY_BANDS, dtype=F32)[None, :]
    z = jnp.concatenate([t, jnp.cos(bands * ang), -jnp.sin(bands * ang)], axis=-1)
    z = jnp.pad(z, ((0, 0), (0, HY_EMB_PAD - HY_EMB)))
    w1p = jnp.pad(w1, ((0, HY_EMB_PAD - HY_EMB), (0, 0)))
    n_out = w3.shape[1]
    deltas = jnp.abs(jnp.linspace(HY_MIN_DECAY, HY_MAX_DECAY, D_MODEL, dtype=F32))
    deltas = jnp.tile(deltas, n_out // D_MODEL)[None, :]
    tl, tn = 256, 1024
    row = lambda i, j: (i, 0)
    fixed = lambda i, j: (0, 0)
    return pl.pallas_call(
        _filter_mlp_kernel,
        grid=(length // tl, n_out // tn),
        in_specs=[
            pl.BlockSpec((tl, HY_EMB_PAD), row),
            pl.BlockSpec((tl, 1), row),
            pl.BlockSpec((HY_EMB_PAD, HY_FFN), fixed),
            pl.BlockSpec((1, HY_FFN), fixed),
            pl.BlockSpec((1, HY_FFN), fixed),
            pl.BlockSpec((HY_FFN, HY_FFN), fixed),
            pl.BlockSpec((1, HY_FFN), fixed),
            pl.BlockSpec((HY_FFN, tn), lambda i, j: (0, j)),
            pl.BlockSpec((1, tn), lambda i, j: (0, j)),
        ],
        out_specs=pl.BlockSpec((tl, tn), lambda i, j: (i, j)),
        out_shape=jax.ShapeDtypeStruct((length, n_out), F32),
        scratch_shapes=[pltpu.VMEM((tl, HY_FFN), F32)],
        compiler_params=_cparams("parallel", "arbitrary"),
        name="hyena_filter_mlp",
    )(z, t, w1p, b1.reshape(1, -1), freq.reshape(1, -1), w2, b2.reshape(1, -1), w3, deltas)


def _short_conv(z, w):
    length = z.shape[0]
    row = lax.broadcasted_iota(jnp.int32, z.shape, 0)
    prev = jnp.where(row == 0, 0.0, pltpu.roll(z, 1, 0))
    nxt = jnp.where(row == length - 1, 0.0, pltpu.roll(z, length - 1, 0))
    return (prev * w[0:1] + z * w[1:2]) + nxt * w[2:3]


def _filter_halves(hf_ref, hb_ref):
    hf = hf_ref[...]
    hb = hb_ref[...]
    hb = jnp.where(lax.broadcasted_iota(jnp.int32, hb.shape, 0) == 0, 0.0, hb)
    norm = jnp.sum(jnp.abs(hf), axis=0, keepdims=True) + jnp.sum(jnp.abs(hb), axis=0, keepdims=True)
    return hf + hb, hf - hb, 1.0 / norm


def _direct_dft_tables(length):
    n_fft = 2 * length
    n_freq = length + 1
    mf = -(-n_freq // 16) * 16
    k = np.arange(mf)[:, None]
    n = np.arange(length)[None, :]
    ang = 2.0 * np.pi * ((k * n) % n_fft) / n_fft
    valid = k < n_freq
    cos = np.where(valid, np.cos(ang), 0.0)
    msin = np.where(valid, -np.sin(ang), 0.0)
    weight = np.where((k == 0) | (k == length), 1.0, 2.0) * valid / n_fft
    fwd = np.concatenate([cos, msin], axis=0)
    inv = np.concatenate([weight * cos, weight * msin], axis=0).T
    return jnp.asarray(fwd, BF16), jnp.asarray(inv, BF16), mf


def _hyena_direct_kernel(zv_ref, z1_ref, z2_ref, wv_ref, w1_ref, w2_ref,
                         hf0_ref, hf1_ref, hb0_ref, hb1_ref, skip_ref, fw_ref, iv_ref,
                         o_ref, kr_ref, ki_ref):
    mf = fw_ref.shape[0] // 2
    dt = o_ref.shape[-1]

    @pl.when(pl.program_id(1) == 0)
    def _():
        for o, (hf_ref, hb_ref) in enumerate(((hf0_ref, hb0_ref), (hf1_ref, hb1_ref))):
            hs, hd, inv_norm = _filter_halves(hf_ref, hb_ref)
            spec = _dot(fw_ref[...], jnp.concatenate([hs, hd], axis=1).astype(BF16))
            kr_ref[o] = spec[:mf, :dt] * inv_norm
            ki_ref[o] = spec[mf:, dt:] * inv_norm

    y = _short_conv(zv_ref[0], wv_ref[...])
    for o, (z_ref, w_ref) in enumerate(((z1_ref, w1_ref), (z2_ref, w2_ref))):
        spec = _dot(fw_ref[...], y.astype(BF16))
        yr, yi = spec[:mf], spec[mf:]
        kr, ki = kr_ref[o], ki_ref[o]
        prod = jnp.concatenate([yr * kr - yi * ki, yr * ki + yi * kr], axis=0)
        yc = _dot(iv_ref[...], prod.astype(BF16))
        y = _short_conv(z_ref[0], w_ref[...]) * (yc + y * skip_ref[o:o + 1])
    o_ref[0] = y.astype(o_ref.dtype)


def _hyena_direct(z, w_short, hfilt, skip, first=0, bsz=None):
    _, length, d3 = z.shape
    bsz = z.shape[0] if bsz is None else bsz
    d = d3 // 3
    dt = 256
    nct = d // dt
    fwd, inv, mf = _direct_dft_tables(length)
    zspec = lambda part: pl.BlockSpec((1, length, dt), lambda c, b: (b + first, 0, part * nct + c))
    wspec = lambda part: pl.BlockSpec((3, dt), lambda c, b: (0, part * nct + c))
    hspec = lambda direction, order: pl.BlockSpec(
        (length, dt), lambda c, b: (0, (direction * HY_ORDER + order) * nct + c))
    fixed = lambda c, b: (0, 0)
    return pl.pallas_call(
        _hyena_direct_kernel,
        grid=(nct, bsz),
        in_specs=[zspec(0), zspec(1), zspec(2), wspec(0), wspec(1), wspec(2),
                  hspec(0, 0), hspec(0, 1), hspec(1, 0), hspec(1, 1),
                  pl.BlockSpec((HY_ORDER, dt), lambda c, b: (0, c)),
                  pl.BlockSpec(fwd.shape, fixed), pl.BlockSpec(inv.shape, fixed)],
        out_specs=pl.BlockSpec((1, length, dt), lambda c, b: (b, 0, c)),
        out_shape=jax.ShapeDtypeStruct((bsz, length, d), BF16),
        scratch_shapes=[pltpu.VMEM((HY_ORDER, mf, dt), F32), pltpu.VMEM((HY_ORDER, mf, dt), F32)],
        compiler_params=_cparams("parallel", "arbitrary"),
        name="hyena_conv_direct",
    )(z, z, z, w_short, w_short, w_short, hfilt, hfilt, hfilt, hfilt, skip, fwd, inv)


FFT_N1 = 64
FFT_N2 = 128
FFT_LANES = 128
FFT_A_PITCH = 2 * FFT_N2 + 8
FFT_U_PITCH = 2 * FFT_N1 + 8
FFT_GROUP = 2
FFT_UNROLL = 8


def _two_stage_tables():
    n_fft = FFT_N1 * FFT_N2
    half = FFT_N2 // 2
    n1 = np.arange(FFT_N1)[:, None, None]
    k2 = np.arange(FFT_N2)[None, :, None]
    n2 = np.arange(half)[None, None, :]
    ang = 2.0 * np.pi * ((k2 * (n1 + FFT_N1 * n2)) % n_fft) / n_fft
    stage_a = np.concatenate([np.cos(ang), -np.sin(ang)], axis=1)
    stage_a_inv = np.transpose(stage_a, (0, 2, 1)) / n_fft
    k1 = np.arange(FFT_N1)[:, None]
    m1 = np.arange(FFT_N1)[None, :]
    phi = 2.0 * np.pi * ((k1 * m1) % FFT_N1) / FFT_N1
    c, s = np.cos(phi), np.sin(phi)
    stage_b = np.block([[c, s], [-s, c]])
    stage_b_inv = np.block([[c, -s], [s, c]])
    return tuple(jnp.asarray(t, BF16) for t in (stage_a, stage_a_inv, stage_b, stage_b_inv))


def _fft_stage_a(y_ref, ma_ref, a_ref):
    half = FFT_N2 // 2

    def body(n1, carry):
        slab = y_ref[pl.ds(n1, half, stride=FFT_N1), :]
        a_ref[pl.ds(pl.multiple_of(n1 * FFT_A_PITCH, 8), 2 * FFT_N2), :] = _dot(ma_ref[n1], slab.astype(BF16))
        return carry

    lax.fori_loop(0, FFT_N1, body, 0, unroll=FFT_UNROLL)


def _fft_stage_b(a_ref, mb_ref, consume):
    def body(j, carry):
        k2 = j * FFT_GROUP
        cols = []
        for g in range(FFT_GROUP):
            re = a_ref[pl.ds(k2 + g, FFT_N1, stride=FFT_A_PITCH), :]
            im = a_ref[pl.ds(FFT_N2 + k2 + g, FFT_N1, stride=FFT_A_PITCH), :]
            cols.append(jnp.concatenate([re, im], axis=0))
        x = _dot(mb_ref[...], jnp.concatenate(cols, axis=1).astype(BF16))
        for g in range(FFT_GROUP):
            consume(k2 + g, x[:, g * FFT_LANES:(g + 1) * FFT_LANES])
        return carry

    lax.fori_loop(0, FFT_N2 // FFT_GROUP, body, 0, unroll=FFT_UNROLL)


def _fft_inverse(z_ref, mbi_ref, mai_ref, u_ref, out_ref):
    half = FFT_N2 // 2

    def stage_b(j, carry):
        k2 = j * FFT_GROUP
        rhs = jnp.concatenate([z_ref[k2 + g] for g in range(FFT_GROUP)], axis=1)
        u = _dot(mbi_ref[...], rhs)
        for g in range(FFT_GROUP):
            u_ref[pl.ds(pl.multiple_of((k2 + g) * FFT_U_PITCH, 8), 2 * FFT_N1), :] = (
                u[:, g * FFT_LANES:(g + 1) * FFT_LANES])
        return carry

    lax.fori_loop(0, FFT_N2 // FFT_GROUP, stage_b, 0, unroll=FFT_UNROLL)

    def stage_a(n1, carry):
        re = u_ref[pl.ds(n1, FFT_N2, stride=FFT_U_PITCH), :]
        im = u_ref[pl.ds(FFT_N1 + n1, FFT_N2, stride=FFT_U_PITCH), :]
        rhs = jnp.concatenate([re, im], axis=0).astype(BF16)
        out_ref[pl.ds(n1, half, stride=FFT_N1), :] = _dot(mai_ref[n1], rhs)
        return carry

    lax.fori_loop(0, FFT_N1, stage_a, 0, unroll=FFT_UNROLL)


FFT_WORK_ROWS = max(FFT_N1 * FFT_A_PITCH, FFT_N2 * FFT_U_PITCH)


def _hyena_spectrum_kernel(hf0_ref, hf1_ref, hb0_ref, hb1_ref, ma_ref, mb_ref, k_ref, y_ref, a_ref):
    for o, (hf_ref, hb_ref) in enumerate(((hf0_ref, hb0_ref), (hf1_ref, hb1_ref))):
        hs, hd, inv_norm = _filter_halves(hf_ref, hb_ref)

        def keep_real(k2, x):
            k_ref[o, k2, 0:FFT_N1, :] = (x[:FFT_N1] * inv_norm).astype(k_ref.dtype)

        def keep_imag(k2, x):
            k_ref[o, k2, FFT_N1:, :] = (x[FFT_N1:] * inv_norm).astype(k_ref.dtype)

        for part, keep in ((hs, keep_real), (hd, keep_imag)):
            y_ref[...] = part
            _fft_stage_a(y_ref, ma_ref, a_ref)
            _fft_stage_b(a_ref, mb_ref, keep)


def _hyena_two_stage_kernel(zv_ref, z1_ref, z2_ref, wv_ref, w1_ref, w2_ref, k_ref, skip_ref,
                            ma_ref, mai_ref, mb_ref, mbi_ref, o_ref, y_ref, c_ref, a_ref, z_ref):
    y_ref[...] = _short_conv(zv_ref[0], wv_ref[...])
    for o, (g_ref, w_ref) in enumerate(((z1_ref, w1_ref), (z2_ref, w2_ref))):
        def multiply(k2, x):
            k = k_ref[o, k2].astype(F32)
            xr, xi = x[:FFT_N1], x[FFT_N1:]
            kr, ki = k[:FFT_N1], k[FFT_N1:]
            z_ref[k2] = jnp.concatenate([xr * kr - xi * ki, xr * ki + xi * kr], axis=0).astype(z_ref.dtype)

        _fft_stage_a(y_ref, ma_ref, a_ref)
        _fft_stage_b(a_ref, mb_ref, multiply)
        _fft_inverse(z_ref, mbi_ref, mai_ref, a_ref, c_ref)
        y = _short_conv(g_ref[0], w_ref[...]) * (c_ref[...] + y_ref[...] * skip_ref[o:o + 1])
        if o + 1 < HY_ORDER:
            y_ref[...] = y
        else:
            o_ref[0] = y.astype(o_ref.dtype)


def _hyena_two_stage(z, w_short, hfilt, skip, first=0, bsz=None):
    _, length, d3 = z.shape
    bsz = z.shape[0] if bsz is None else bsz
    d = d3 // 3
    dt = FFT_LANES
    nct = d // dt
    assert 2 * length == FFT_N1 * FFT_N2
    ma, mai, mb, mbi = _two_stage_tables()
    once = pl.Buffered(1)
    hspec = lambda direction, order: pl.BlockSpec(
        (length, dt), lambda c: (0, (direction * HY_ORDER + order) * nct + c))
    spectrum = pl.pallas_call(
        _hyena_spectrum_kernel,
        grid=(nct,),
        in_specs=[hspec(0, 0), hspec(0, 1), hspec(1, 0), hspec(1, 1),
                  pl.BlockSpec(ma.shape, lambda c: (0, 0, 0), pipeline_mode=once),
                  pl.BlockSpec(mb.shape, lambda c: (0, 0), pipeline_mode=once)],
        out_specs=pl.BlockSpec((HY_ORDER, FFT_N2, 2 * FFT_N1, dt), lambda c: (0, 0, 0, c)),
        out_shape=jax.ShapeDtypeStruct((HY_ORDER, FFT_N2, 2 * FFT_N1, d), BF16),
        scratch_shapes=[pltpu.VMEM((length, dt), F32), pltpu.VMEM((FFT_WORK_ROWS, dt), F32)],
        compiler_params=_cparams("parallel"),
        name="hyena_filter_spectrum",
    )(hfilt, hfilt, hfilt, hfilt, ma, mb)

    zspec = lambda part: pl.BlockSpec((1, length, dt), lambda c, b: (b + first, 0, part * nct + c))
    wspec = lambda part: pl.BlockSpec((3, dt), lambda c, b: (0, part * nct + c))
    fixed3 = lambda c, b: (0, 0, 0)
    fixed2 = lambda c, b: (0, 0)
    return pl.pallas_call(
        _hyena_two_stage_kernel,
        grid=(nct, bsz),
        in_specs=[zspec(0), zspec(1), zspec(2), wspec(0), wspec(1), wspec(2),
                  pl.BlockSpec((HY_ORDER, FFT_N2, 2 * FFT_N1, dt), lambda c, b: (0, 0, 0, c), pipeline_mode=once),
                  pl.BlockSpec((HY_ORDER, dt), lambda c, b: (0, c)),
                  pl.BlockSpec(ma.shape, fixed3, pipeline_mode=once),
                  pl.BlockSpec(mai.shape, fixed3, pipeline_mode=once),
                  pl.BlockSpec(mb.shape, fixed2, pipeline_mode=once),
                  pl.BlockSpec(mbi.shape, fixed2, pipeline_mode=once)],
        out_specs=pl.BlockSpec((1, length, dt), lambda c, b: (b, 0, c)),
        out_shape=jax.ShapeDtypeStruct((bsz, length, d), BF16),
        scratch_shapes=[pltpu.VMEM((length, dt), F32), pltpu.VMEM((length, dt), F32),
                        pltpu.VMEM((FFT_WORK_ROWS, dt), F32), pltpu.VMEM((FFT_N2, 2 * FFT_N1, dt), BF16)],
        compiler_params=_cparams("parallel", "arbitrary"),
        name="hyena_conv_two_stage",
    )(z, z, z, w_short, w_short, w_short, spectrum, skip, ma, mai, mb, mbi)


N_HEADS = 16
HEAD_DIM = D_MODEL // N_HEADS
HEADS_PER_STEP = 2
CTX_HEADS_PER_STEP = 8
GRID_W = 64
WIN_ROWS = 8
WIN_COLS = 16
NEG_INF = -1e30
NAT_Q_ROWS = 4
NAT_K_ROWS = NAT_Q_ROWS + WIN_ROWS - 1
ATTN_SCALE = HEAD_DIM ** -0.5


def _dot_nt(a, b):
    return lax.dot_general(a, b, (((1,), (1,)), ((), ())), preferred_element_type=F32)


def _qkv_kernel(x_ref, g_ref, sc_ref, sh_ref, w_ref, q_ref, k_ref, v_ref):
    h = _norm_mod(x_ref[...], g_ref[...], sc_ref[0], sh_ref[0])
    qkv = _dot(h.astype(BF16), w_ref[...])
    for part, ref in enumerate((q_ref, k_ref, v_ref)):
        for head in range(N_HEADS):
            lo = part * D_MODEL + head * HEAD_DIM
            ref[0, head] = qkv[:, lo:lo + HEAD_DIM].astype(ref.dtype)


def _qkv_proj(x, g, mods, w_bf16, seg, first_row, n_seq, seq_len, kv_dtype):
    d = x.shape[1]
    n_ctx, smp_len = seg
    t = ROW_TILE
    first_tile = first_row // t
    per_seq = seq_len // t
    out_spec = pl.BlockSpec((1, N_HEADS, t, HEAD_DIM), lambda i: (i // per_seq, 0, i % per_seq, 0))
    shape = (n_seq, N_HEADS, seq_len, HEAD_DIM)
    return pl.pallas_call(
        _qkv_kernel,
        grid=(n_seq * per_seq,),
        in_specs=[
            pl.BlockSpec((t, d), lambda i: (i + first_tile, 0)),
            pl.BlockSpec((1, d), lambda i: (0, 0)),
            _mod_spec(1, t, n_ctx, smp_len, first_tile),
            _mod_spec(0, t, n_ctx, smp_len, first_tile),
            pl.BlockSpec(w_bf16.shape, lambda i: (0, 0)),
        ],
        out_specs=[out_spec, out_spec, out_spec],
        out_shape=[jax.ShapeDtypeStruct(shape, BF16), jax.ShapeDtypeStruct(shape, kv_dtype),
                   jax.ShapeDtypeStruct(shape, kv_dtype)],
        compiler_params=_cparams("parallel"),
        name="qkv_proj",
    )(x, g.reshape(1, d), mods, mods, w_bf16)


def _ctx_attn_kernel(q_ref, k_ref, v_ref, o_ref):
    for j in range(q_ref.shape[1]):
        q = q_ref[0, j]
        k = k_ref[0, j].astype(BF16)
        v = v_ref[0, j].astype(BF16)
        s = _dot_nt(q, k) * ATTN_SCALE
        p = jnp.exp(s - jnp.max(s, axis=-1, keepdims=True))
        o = _dot(p.astype(BF16), v) / jnp.sum(p, axis=-1, keepdims=True)
        o_ref[:, j * HEAD_DIM:(j + 1) * HEAD_DIM] = o.astype(o_ref.dtype)


def _ctx_attention(q, k, v):
    bsz, _, s, _ = q.shape
    hp = CTX_HEADS_PER_STEP
    spec = pl.BlockSpec((1, hp, s, HEAD_DIM), lambda b, h: (b, h, 0, 0))
    return pl.pallas_call(
        _ctx_attn_kernel,
        grid=(bsz, N_HEADS // hp),
        in_specs=[spec, spec, spec],
        out_specs=pl.BlockSpec((s, hp * HEAD_DIM), lambda b, h: (b, h)),
        out_shape=jax.ShapeDtypeStruct((bsz * s, D_MODEL), BF16),
        compiler_params=_cparams("parallel", "parallel"),
        name="ctx_attention",
    )(q, k, v)


def _rpb_toeplitz_kernel(r_ref, e_ref, o_ref):
    o_ref[...] = _dot3(r_ref[...], e_ref[...])


def _nat_bias(rpb):
    n_heads, n_r, n_c = rpb.shape
    n_cp = 32
    qc = np.arange(GRID_W)[:, None]
    kc = np.arange(GRID_W)[None, :]
    onehot = (np.clip(kc - qc + WIN_COLS - 1, 0, n_c - 1)[None] == np.arange(n_cp)[:, None, None])
    onehot = jnp.asarray(onehot.reshape(n_cp, GRID_W * GRID_W), F32)
    rows = jnp.pad(rpb.reshape(n_heads * n_r, n_c), ((0, 0), (0, n_cp - n_c)))
    toep = pl.pallas_call(
        _rpb_toeplitz_kernel,
        out_shape=jax.ShapeDtypeStruct((n_heads * n_r, GRID_W * GRID_W), F32),
        compiler_params=_cparams(),
        name="nat_bias_toeplitz",
    )(rows, onehot).reshape(n_heads, n_r, GRID_W, GRID_W)
    q_start = np.clip(qc - WIN_COLS // 2, 0, GRID_W - WIN_COLS)
    col_ok = jnp.asarray((kc >= q_start) & (kc < q_start + WIN_COLS))
    toep = jnp.where(col_ok, toep, NEG_INF)
    toep = jnp.concatenate([toep, jnp.full((n_heads, 1, GRID_W, GRID_W), NEG_INF, F32)], axis=1)
    rows_total = GRID_W
    idx = np.full((3, NAT_Q_ROWS, NAT_K_ROWS), n_r, np.int32)
    for case, r0 in enumerate((0, NAT_Q_ROWS, rows_total - NAT_Q_ROWS)):
        ks = int(np.clip(r0 - WIN_ROWS // 2, 0, rows_total - NAT_K_ROWS))
        for dr in range(NAT_Q_ROWS):
            r = r0 + dr
            rs = int(np.clip(r - WIN_ROWS // 2, 0, rows_total - WIN_ROWS))
            for dk in range(NAT_K_ROWS):
                kr = ks + dk
                if rs <= kr < rs + WIN_ROWS:
                    idx[case, dr, dk] = kr - r + WIN_ROWS - 1
    def assemble(t_ref, o_ref):
        for case in range(3):
            for dr in range(NAT_Q_ROWS):
                for dk in range(NAT_K_ROWS):
                    o_ref[0, case, dr * GRID_W:(dr + 1) * GRID_W, dk * GRID_W:(dk + 1) * GRID_W] = (
                        t_ref[0, int(idx[case, dr, dk])])

    return pl.pallas_call(
        assemble,
        grid=(n_heads,),
        in_specs=[pl.BlockSpec((1, n_r + 1, GRID_W, GRID_W), lambda h: (h, 0, 0, 0))],
        out_specs=pl.BlockSpec((1, 3, NAT_Q_ROWS * GRID_W, NAT_K_ROWS * GRID_W), lambda h: (h, 0, 0, 0)),
        out_shape=jax.ShapeDtypeStruct((n_heads, 3, NAT_Q_ROWS * GRID_W, NAT_K_ROWS * GRID_W), F32),
        compiler_params=_cparams("parallel"),
        name="nat_bias_assemble",
    )(toep)


def _nat_kernel(q_ref, k_ref, v_ref, kc_ref, vc_ref, bias_ref, o_ref):
    n_blocks = q_ref.shape[2] // (NAT_Q_ROWS * GRID_W)
    rows_total = q_ref.shape[2] // GRID_W
    nq = NAT_Q_ROWS * GRID_W
    nk = NAT_K_ROWS * GRID_W
    ctx = [(kc_ref[0, 0, j].astype(BF16), vc_ref[0, 0, j].astype(BF16)) for j in range(HEADS_PER_STEP)]

    def block(blk, carry):
        ks = jnp.clip(blk * NAT_Q_ROWS - WIN_ROWS // 2, 0, rows_total - NAT_K_ROWS)
        case = jnp.where(blk == 0, 0, jnp.where(blk == n_blocks - 1, 2, 1))
        q_rows = pl.ds(pl.multiple_of(blk * nq, nq), nq)
        k_rows = pl.ds(pl.multiple_of(ks * GRID_W, GRID_W), nk)
        for j, (k_ctx, v_ctx) in enumerate(ctx):
            q = q_ref[0, j, q_rows, :]
            s_loc = _dot_nt(q, k_ref[0, j, k_rows, :]) * ATTN_SCALE + bias_ref[j, case]
            s_ctx = _dot_nt(q, k_ctx) * ATTN_SCALE
            m = jnp.maximum(jnp.max(s_loc, axis=-1, keepdims=True), jnp.max(s_ctx, axis=-1, keepdims=True))
            p_loc = jnp.exp(s_loc - m)
            p_ctx = jnp.exp(s_ctx - m)
            denom = jnp.sum(p_loc, axis=-1, keepdims=True) + jnp.sum(p_ctx, axis=-1, keepdims=True)
            o = (_dot(p_loc.astype(BF16), v_ref[0, j, k_rows, :]) + _dot(p_ctx.astype(BF16), v_ctx)) / denom
            o_ref[q_rows, j * HEAD_DIM:(j + 1) * HEAD_DIM] = o.astype(o_ref.dtype)
        return carry

    lax.fori_loop(0, n_blocks, block, 0, unroll=2)


def _nat_attention(q, k, v, cache_k, cache_v, bias):
    bsz, _, length, _ = q.shape
    hp = HEADS_PER_STEP
    past = cache_k.shape[3]
    spec = pl.BlockSpec((1, hp, length, HEAD_DIM), lambda b, h: (b, h, 0, 0))
    cspec = pl.BlockSpec((1, 1, hp, past, HEAD_DIM), lambda b, h: (b, 0, h, 0, 0))
    return pl.pallas_call(
        _nat_kernel,
        grid=(bsz, N_HEADS // hp),
        in_specs=[spec, spec, spec, cspec, cspec,
                  pl.BlockSpec((hp,) + bias.shape[1:], lambda b, h: (h, 0, 0, 0))],
        out_specs=pl.BlockSpec((length, hp * HEAD_DIM), lambda b, h: (b, h)),
        out_shape=jax.ShapeDtypeStruct((bsz * length, D_MODEL), BF16),
        compiler_params=_cparams("parallel", "parallel"),
        name="nat_attention",
    )(q, k, v, cache_k, cache_v, bias)


def kernel(x_prompt, x_sample, cache_k, cache_v, c, c_ctx, ada_w, ada_b, norm1_g, norm2_g, normf_g, hy_w_in, hy_w_short, hy_f_w1, hy_f_b1, hy_f_freq, hy_f_w2, hy_f_b2, hy_f_w3, hy_skip, hy_w_out, na_w_qkv, na_rpb, na_w_o, moe_w_router, moe_b_router, moe_w_gate, moe_b_gate, moe_w_up, moe_b_up, moe_w_down, moe_b_down):
    b, s, d = x_prompt.shape
    bd, sd, _ = x_sample.shape
    n_ctx = b * s
    n_tok = n_ctx + bd * sd
    assert n_ctx % sd == 0 and s % ROW_TILE == 0 and sd % ROW_TILE == 0
    seg = (n_ctx, sd)
    x = jnp.concatenate([x_prompt.reshape(n_ctx, d), x_sample.reshape(bd * sd, d)], axis=0)
    cond = jnp.zeros((N_COND, d), F32).at[0].set(c_ctx).at[1:1 + bd].set(c)
    mods = _modulation(cond, ada_w, ada_b)

    def moe(x, i, final_norm):
        return _moe_layer(x, norm2_g[i], mods[i], moe_w_router[i], moe_b_router[i], i, moe_w_gate, moe_b_gate,
                          moe_w_up, moe_b_up, moe_w_down, moe_b_down, normf_g, seg, final_norm)

    z = _norm_proj(x, norm1_g[0], mods[0], hy_w_in[0].astype(BF16), seg)
    fargs = (hy_f_w1[0], hy_f_b1[0], hy_f_freq[0], hy_f_w2[0], hy_f_b2[0], hy_f_w3[0])
    y_ctx = _hyena_direct(z.reshape(n_tok // s, s, 3 * d), hy_w_short[0], _hyena_filters(s, *fargs),
                          hy_skip[0], first=0, bsz=b)
    y_smp = _hyena_two_stage(z.reshape(n_tok // sd, sd, 3 * d), hy_w_short[0], _hyena_filters(sd, *fargs),
                             hy_skip[0], first=n_ctx // sd, bsz=bd)
    x = _out_proj(y_ctx.reshape(n_ctx, d), y_smp.reshape(bd * sd, d), hy_w_out[0].astype(BF16), x, mods[0], seg)
    x = moe(x, 0, False)

    w_qkv = na_w_qkv[0].astype(BF16)
    q_c, k_c, v_c = _qkv_proj(x, norm1_g[1], mods[1], w_qkv, seg, 0, b, s, F32)
    q_s, k_s, v_s = _qkv_proj(x, norm1_g[1], mods[1], w_qkv, seg, n_ctx, bd, sd, BF16)
    o_ctx = _ctx_attention(q_c, k_c, v_c)
    o_smp = _nat_attention(q_s, k_s, v_s, cache_k, cache_v, _nat_bias(na_rpb[0]))
    x = _out_proj(o_ctx, o_smp, na_w_o[0].astype(BF16), x, mods[1], seg)
    y_prompt, y_sample = moe(x, 1, True)

    nh, hd = k_c.shape[1], k_c.shape[3]
    return (y_prompt.reshape(b, s, d), y_sample.reshape(bd, sd, d),
            k_c.reshape(b, 1, nh, s, hd), v_c.reshape(b, 1, nh, s, hd))
```

```python
import functools
import math

import numpy as np
import jax
import jax.numpy as jnp
from jax import lax
from jax.experimental import pallas as pl
from jax.experimental.pallas import tpu as pltpu

F32 = jnp.float32
BF16 = jnp.bfloat16

D_MODEL = 1024
N_MOD = 6
RMS_EPS = 1e-6
N_EXPERTS = 32
TOP_K = 4
SWIGLU_LIMIT = 7.0
SWIGLU_ALPHA = 1.702

N_COND = 8
ROW_TILE = 256
MOE_TILE = 256
V7X_VMEM_LIMIT = 56 * 1024 * 1024


def _cparams(*sem, vmem=V7X_VMEM_LIMIT):
    return pltpu.CompilerParams(dimension_semantics=sem, vmem_limit_bytes=vmem)


def _dot(a, b):
    return jnp.dot(a, b, preferred_element_type=F32)


def _split_bf16(x):
    hi = x.astype(BF16)
    lo = (x - hi.astype(F32)).astype(BF16)
    return hi, lo


def _dot3(a, b):
    ah, al = _split_bf16(a)
    bh, bl = _split_bf16(b)
    return _dot(ah, bh) + (_dot(al, bh) + _dot(ah, bl))


def _seg_of_tile(i, tile, n_ctx, smp_len):
    ctx_tiles = n_ctx // tile
    per_smp = smp_len // tile
    return jnp.where(i < ctx_tiles, 0, 1 + (i - ctx_tiles) // per_smp)


def _norm_mod(x, g, sc, sh):
    y = x * lax.rsqrt(jnp.mean(x * x, axis=-1, keepdims=True) + RMS_EPS)
    return (y * g) * (1.0 + sc) + sh


def _mod_kernel(c_ref, w_ref, b_ref, o_ref):
    c = c_ref[...]
    a = c * jax.nn.sigmoid(c)
    o_ref[0] = _dot3(a, w_ref[0]) + b_ref[0]


def _modulation(cond, ada_w, ada_b):
    depth, d, n_out = ada_w.shape
    tn = 1536
    m = pl.pallas_call(
        _mod_kernel,
        grid=(depth, n_out // tn),
        in_specs=[
            pl.BlockSpec((N_COND, d), lambda l, j: (0, 0)),
            pl.BlockSpec((1, d, tn), lambda l, j: (l, 0, j)),
            pl.BlockSpec((1, 1, tn), lambda l, j: (l, 0, j)),
        ],
        out_specs=pl.BlockSpec((1, N_COND, tn), lambda l, j: (l, 0, j)),
        out_shape=jax.ShapeDtypeStruct((depth, N_COND, n_out), F32),
        compiler_params=_cparams("arbitrary", "arbitrary"),
        name="adaln_modulation",
    )(cond, ada_w, ada_b.reshape(depth, 1, n_out))
    m = m.reshape(depth, N_COND, N_MOD, d)
    return jnp.transpose(m, (0, 2, 1, 3)).reshape(depth, N_MOD * N_COND, 1, d)


def _mod_spec(which, tile, n_ctx, smp_len, first_tile=0):
    return pl.BlockSpec(
        (1, 1, D_MODEL),
        lambda i, *_: (which * N_COND + _seg_of_tile(i + first_tile, tile, n_ctx, smp_len), 0, 0))


def _group_specs(block_cols, ctx_tiles, stacked=False):
    base = ctx_tiles if stacked else 0
    return (pl.BlockSpec((ROW_TILE, block_cols), lambda i: (jnp.minimum(i, ctx_tiles - 1), 0)),
            pl.BlockSpec((ROW_TILE, block_cols), lambda i: (jnp.maximum(i - ctx_tiles, 0) + base, 0)))


def _pick_group(a_ref, b_ref, ctx_tiles):
    return jnp.where(pl.program_id(0) < ctx_tiles, a_ref[...], b_ref[...])


def _norm_proj_kernel(xa_ref, xb_ref, g_ref, sc_ref, sh_ref, w_ref, o_ref, *, ctx_tiles):
    h = _norm_mod(_pick_group(xa_ref, xb_ref, ctx_tiles), g_ref[...], sc_ref[0], sh_ref[0])
    o_ref[...] = _dot(h.astype(BF16), w_ref[...]).astype(o_ref.dtype)


def _norm_proj(x_ctx, x_smp, g, mods, w_bf16, seg, out_dtype=F32):
    d = x_ctx.shape[1]
    n = x_ctx.shape[0] + x_smp.shape[0]
    n_out = w_bf16.shape[1]
    n_ctx, smp_len = seg
    ctx_tiles = n_ctx // ROW_TILE
    return pl.pallas_call(
        functools.partial(_norm_proj_kernel, ctx_tiles=ctx_tiles),
        grid=(n // ROW_TILE,),
        in_specs=[
            *_group_specs(d, ctx_tiles),
            pl.BlockSpec((1, d), lambda i: (0, 0)),
            _mod_spec(1, ROW_TILE, n_ctx, smp_len),
            _mod_spec(0, ROW_TILE, n_ctx, smp_len),
            pl.BlockSpec((d, n_out), lambda i: (0, 0)),
        ],
        out_specs=pl.BlockSpec((ROW_TILE, n_out), lambda i: (i, 0)),
        out_shape=jax.ShapeDtypeStruct((n, n_out), out_dtype),
        compiler_params=_cparams("parallel"),
        name="norm_proj",
    )(x_ctx, x_smp, g.reshape(1, d), mods, mods, w_bf16)


def _out_proj_kernel(ya_ref, yb_ref, w_ref, xa_ref, xb_ref, gate_ref, o_ref, *, ctx_tiles):
    y = _pick_group(ya_ref, yb_ref, ctx_tiles)
    o_ref[...] = _pick_group(xa_ref, xb_ref, ctx_tiles) + gate_ref[0] * _dot(y, w_ref[...])


def _out_proj(y_ctx, y_smp, w_bf16, x_ctx, x_smp, mods, seg):
    d = x_ctx.shape[1]
    n_ctx, smp_len = seg
    n = n_ctx + y_smp.shape[0]
    ctx_tiles = n_ctx // ROW_TILE
    stacked = x_smp is None
    x_smp = x_ctx if stacked else x_smp
    return pl.pallas_call(
        functools.partial(_out_proj_kernel, ctx_tiles=ctx_tiles),
        grid=(n // ROW_TILE,),
        in_specs=[
            *_group_specs(d, ctx_tiles),
            pl.BlockSpec((d, d), lambda i: (0, 0)),
            *_group_specs(d, ctx_tiles, stacked),
            _mod_spec(2, ROW_TILE, n_ctx, smp_len),
        ],
        out_specs=pl.BlockSpec((ROW_TILE, d), lambda i: (i, 0)),
        out_shape=jax.ShapeDtypeStruct((n, d), F32),
        compiler_params=_cparams("parallel"),
        name="out_proj",
    )(y_ctx, y_smp, w_bf16, x_ctx, x_smp, mods)


LANES = 128
ROW_SUBLANES = D_MODEL // LANES


def _store_row_tiles(ref, x):
    t = x.shape[0]
    for s in range(ROW_SUBLANES):
        ref[pl.ds(s, t, stride=ROW_SUBLANES), :] = x[:, s * LANES:(s + 1) * LANES]


def _load_row_tiles(ref, t):
    return jnp.concatenate([ref[pl.ds(s, t, stride=ROW_SUBLANES), :] for s in range(ROW_SUBLANES)], axis=1)


def _columns(cols):
    t = cols[0].shape[0]
    lane = lax.broadcasted_iota(jnp.int32, (t, len(cols)), 1)
    out = jnp.broadcast_to(cols[-1], (t, len(cols)))
    for k in range(len(cols) - 2, -1, -1):
        out = jnp.where(lane == k, cols[k], out)
    return out


def _router_kernel(x_ref, g_ref, sc_ref, sh_ref, wr_ref, br_ref, tri_ref,
                   h_ref, e_ref, gate_ref, rank_ref, cnt_ref, run_ref):
    i = pl.program_id(0)

    @pl.when(i == 0)
    def _():
        run_ref[...] = jnp.zeros_like(run_ref)

    h = _norm_mod(x_ref[...], g_ref[...], sc_ref[0], sh_ref[0])
    _store_row_tiles(h_ref, h)
    logits = _dot3(h, wr_ref[...]) + br_ref[...]
    lane = lax.broadcasted_iota(jnp.int32, logits.shape, 1).astype(F32)
    work = logits
    vals, idxs, hots = [], [], []
    for _ in range(TOP_K):
        m = jnp.max(work, axis=-1, keepdims=True)
        idx = jnp.min(jnp.where(work == m, lane, float(N_EXPERTS)), axis=-1, keepdims=True)
        hot = lane == idx
        vals.append(m)
        idxs.append(idx)
        hots.append(hot)
        work = jnp.where(hot, -jnp.inf, work)
    ex = [jnp.exp(v - vals[0]) for v in vals]
    denom = ex[0] + ex[1] + ex[2] + ex[3]
    gate_ref[...] = _columns([e / denom for e in ex])
    e_ref[...] = _columns(idxs).astype(jnp.int32)

    chosen = (hots[0] | hots[1] | hots[2] | hots[3]).astype(F32)
    before = run_ref[...] + _dot(tri_ref[...], chosen.astype(BF16))
    ranks = [jnp.sum(jnp.where(hot, before, 0.0), axis=-1, keepdims=True) for hot in hots]
    rank_ref[...] = _columns(ranks).astype(jnp.int32)
    run_ref[...] += jnp.sum(chosen, axis=0, keepdims=True)
    cnt_ref[...] = run_ref[...].astype(jnp.int32)


def _router(x, g, mods, w_r, b_r, seg):
    n, d = x.shape
    n_ctx, smp_len = seg
    t = ROW_TILE
    tri = jnp.asarray(np.tril(np.ones((t, t), np.float32), -1), BF16)
    tok4 = lambda i: (i, 0)
    return pl.pallas_call(
        _router_kernel,
        grid=(n // t,),
        in_specs=[
            pl.BlockSpec((t, d), lambda i: (i, 0)),
            pl.BlockSpec((1, d), lambda i: (0, 0)),
            _mod_spec(4, t, n_ctx, smp_len),
            _mod_spec(3, t, n_ctx, smp_len),
            pl.BlockSpec((d, N_EXPERTS), lambda i: (0, 0)),
            pl.BlockSpec((1, N_EXPERTS), lambda i: (0, 0)),
            pl.BlockSpec((t, t), lambda i: (0, 0)),
        ],
        out_specs=[
            pl.BlockSpec((t * ROW_SUBLANES, LANES), tok4),
            pl.BlockSpec((t, TOP_K), tok4),
            pl.BlockSpec((t, TOP_K), tok4),
            pl.BlockSpec((t, TOP_K), tok4),
            pl.BlockSpec((1, N_EXPERTS), lambda i: (0, 0)),
        ],
        out_shape=[
            jax.ShapeDtypeStruct((n * ROW_SUBLANES, LANES), F32),
            jax.ShapeDtypeStruct((n, TOP_K), jnp.int32),
            jax.ShapeDtypeStruct((n, TOP_K), F32),
            jax.ShapeDtypeStruct((n, TOP_K), jnp.int32),
            jax.ShapeDtypeStruct((1, N_EXPERTS), jnp.int32),
        ],
        scratch_shapes=[pltpu.VMEM((1, N_EXPERTS), F32)],
        compiler_params=_cparams("arbitrary"),
        name="moe_router",
    )(x, g.reshape(1, d), mods, mods, w_r, b_r.reshape(1, N_EXPERTS), tri)


def _tile_copy(src_ref, src_row, dst_ref, dst_row, sem):
    rows = lambda r: pl.ds(pl.multiple_of(r, ROW_SUBLANES), ROW_SUBLANES)
    return pltpu.make_async_copy(src_ref.at[rows(src_row), :], dst_ref.at[rows(dst_row), :], sem)


def _tiles_wait(src_ref, dst_ref, n_tiles, sem):
    rows = pl.ds(0, n_tiles * ROW_SUBLANES)
    pltpu.make_async_copy(src_ref.at[rows, :], dst_ref.at[rows, :], sem).wait()


ENTRIES_PER_TILE = ROW_TILE * TOP_K
DMA_PRIORITIES = 2


def _dispatch_kernel(fill_start_ref, fill_len_ref, na_ref, dest_ref, h_ref, xs_ref, hbuf, zero_ref, sem):
    i = pl.program_id(0)
    last = pl.num_programs(0) - 1
    slot = lax.rem(i, 2)
    hbuf[slot] = h_ref[...]

    def issue(t, carry):
        for k in range(TOP_K):
            _tile_copy(hbuf.at[slot], t * ROW_SUBLANES, xs_ref, dest_ref[t * TOP_K + k],
                       sem.at[slot]).start(priority=k % DMA_PRIORITIES)
        return carry

    lax.fori_loop(0, ROW_TILE, issue, 0, unroll=4)

    @pl.when(i >= 1)
    def _():
        _tiles_wait(xs_ref, xs_ref, ENTRIES_PER_TILE, sem.at[1 - slot])

    @pl.when(i == last)
    def _():
        _tiles_wait(xs_ref, xs_ref, ENTRIES_PER_TILE, sem.at[slot])
        zero_ref[...] = jnp.zeros_like(zero_ref)
        for e in range(N_EXPERTS):
            start = fill_start_ref[e]
            count = fill_len_ref[e]

            def fill(r, carry):
                _tile_copy(zero_ref, 0, xs_ref, (start + r) * ROW_SUBLANES, sem.at[0]).start()
                return carry

            lax.fori_loop(0, count, fill, 0)

            def fill_drain(r, carry):
                _tile_copy(zero_ref, 0, xs_ref, 0, sem.at[0]).wait()
                return carry

            lax.fori_loop(0, count, fill_drain, 0)

        block_rows = MOE_TILE * ROW_SUBLANES

        def block_copy(blk):
            return pltpu.make_async_copy(
                zero_ref, xs_ref.at[pl.ds(pl.multiple_of(blk * block_rows, block_rows), block_rows), :], sem.at[0])

        n_blocks = xs_ref.shape[0] // block_rows

        def tail(blk, carry):
            block_copy(blk).start()
            return carry

        lax.fori_loop(na_ref[0], n_blocks, tail, 0)

        def tail_drain(blk, carry):
            block_copy(0).wait()
            return carry

        lax.fori_loop(na_ref[0], n_blocks, tail_drain, 0)


def _dispatch(h_tiles, dest_rows, fill_start, fill_len, n_active, cap):
    n = h_tiles.shape[0] // ROW_SUBLANES
    grid_spec = pltpu.PrefetchScalarGridSpec(
        num_scalar_prefetch=3,
        grid=(n // ROW_TILE,),
        in_specs=[
            pl.BlockSpec((ENTRIES_PER_TILE,), lambda i, *_: (i,), memory_space=pltpu.SMEM),
            pl.BlockSpec((ROW_TILE * ROW_SUBLANES, LANES), lambda i, *_: (i, 0)),
        ],
        out_specs=pl.BlockSpec(memory_space=pl.ANY),
        scratch_shapes=[pltpu.VMEM((2, ROW_TILE * ROW_SUBLANES, LANES), F32),
                        pltpu.VMEM((MOE_TILE * ROW_SUBLANES, LANES), F32), pltpu.SemaphoreType.DMA((2,))],
    )
    return pl.pallas_call(
        _dispatch_kernel,
        grid_spec=grid_spec,
        out_shape=jax.ShapeDtypeStruct((cap * ROW_SUBLANES, LANES), F32),
        compiler_params=_cparams("arbitrary"),
        name="moe_dispatch",
    )(fill_start, fill_len, n_active, dest_rows, h_tiles)


def _ffn_kernel(be_ref, na_ref, wslot_ref, next_e_ref, xs_ref, wg_ref, bg_ref, wu_ref, bu_ref, wd_ref, bd_ref,
                ys_ref, wbuf, wg_s, wu_s, wd_s, wsem, *, layer):
    i = pl.program_id(0)
    n_active = na_ref[0]
    tm = xs_ref.shape[0] // ROW_SUBLANES
    changed = jnp.logical_or(i == 0, be_ref[i] != be_ref[jnp.maximum(i - 1, 0)])
    fresh = jnp.logical_and(i < n_active, changed)

    def weight_copies(expert, b):
        return [pltpu.make_async_copy(w_ref.at[layer, expert], wbuf.at[b, m], wsem.at[b])
                for m, w_ref in enumerate((wg_ref, wu_ref, wd_ref))]

    @pl.when(jnp.logical_and(i == 0, n_active > 0))
    def _():
        for copy in weight_copies(be_ref[0], wslot_ref[0]):
            copy.start()

    @pl.when(fresh)
    def _():
        b = wslot_ref[i]
        for copy in weight_copies(be_ref[i], b):
            copy.wait()
        wg_s[...] = wbuf[b, 0].astype(BF16)
        wu_s[...] = wbuf[b, 1].astype(BF16)
        wd_s[...] = wbuf[b, 2].astype(BF16)
        upcoming = next_e_ref[i]

        @pl.when(upcoming >= 0)
        def _():
            for copy in weight_copies(upcoming, 1 - b):
                copy.start()

    @pl.when(i < n_active)
    def _():
        x = _load_row_tiles(xs_ref, tm).astype(BF16)
        g = jnp.minimum(_dot(x, wg_s[...]) + bg_ref[0, 0], SWIGLU_LIMIT)
        u = jnp.clip(_dot(x, wu_s[...]) + bu_ref[0, 0], -SWIGLU_LIMIT, SWIGLU_LIMIT)
        a = g * jax.nn.sigmoid(SWIGLU_ALPHA * g) * (u + 1.0)
        _store_row_tiles(ys_ref, _dot(a.astype(BF16), wd_s[...]) + bd_ref[0, 0])

    @pl.when(i >= na_ref[0])
    def _():
        ys_ref[...] = jnp.zeros_like(ys_ref)


def _expert_ffn(xs, block_e, n_active, weight_slot, next_expert, layer, w_g, b_g, w_u, b_u, w_d, b_d):
    d = D_MODEL
    depth, ne, _, f = w_g.shape
    assert f == d
    nb = block_e.shape[0]
    block = (MOE_TILE * ROW_SUBLANES, LANES)
    bmap = lambda i, be, *_: (layer, be[i], 0, 0)
    hbm = pl.BlockSpec(memory_space=pl.ANY)
    grid_spec = pltpu.PrefetchScalarGridSpec(
        num_scalar_prefetch=4,
        grid=(nb,),
        in_specs=[
            pl.BlockSpec(block, lambda i, be, na, *_: (jnp.minimum(i, na[0] - 1), 0)),
            hbm, pl.BlockSpec((1, 1, 1, f), bmap),
            hbm, pl.BlockSpec((1, 1, 1, f), bmap),
            hbm, pl.BlockSpec((1, 1, 1, d), bmap),
        ],
        out_specs=pl.BlockSpec(block, lambda i, *_: (i, 0)),
        scratch_shapes=[pltpu.VMEM((2, 3, d, f), F32),
                        pltpu.VMEM((d, f), BF16), pltpu.VMEM((d, f), BF16), pltpu.VMEM((f, d), BF16),
                        pltpu.SemaphoreType.DMA((2,))],
    )
    return pl.pallas_call(
        functools.partial(_ffn_kernel, layer=layer),
        grid_spec=grid_spec,
        out_shape=jax.ShapeDtypeStruct(xs.shape, F32),
        compiler_params=_cparams("arbitrary"),
        name="moe_expert_ffn",
    )(block_e, n_active, weight_slot, next_expert, xs,
      w_g, b_g.reshape(depth, ne, 1, f), w_u, b_u.reshape(depth, ne, 1, f), w_d, b_d.reshape(depth, ne, 1, d))


def _combine_kernel(dest_ref, dest_next_ref, x_ref, gate_ref, g2_ref, gf_ref, ys_ref, o_ref, buf, sem, *,
                    final_norm):
    i = pl.program_id(0)
    slot = lax.rem(i, 2)
    t = x_ref.shape[0]
    plane = t * ROW_SUBLANES

    def start_gather(idx_ref, b):
        def body(r, carry):
            for k in range(TOP_K):
                _tile_copy(ys_ref, idx_ref[r * TOP_K + k], buf.at[b], k * plane + r * ROW_SUBLANES,
                           sem.at[b]).start(priority=k % DMA_PRIORITIES)
            return carry

        lax.fori_loop(0, t, body, 0, unroll=4)

    @pl.when(i == 0)
    def _():
        start_gather(dest_ref, 0)

    @pl.when(i + 1 < pl.num_programs(0))
    def _():
        start_gather(dest_next_ref, 1 - slot)

    _tiles_wait(ys_ref, buf.at[slot], TOP_K * t, sem.at[slot])
    gates = gate_ref[...]
    acc = None
    for k in range(TOP_K):
        term = gates[:, k:k + 1] * _load_row_tiles(buf.at[slot, pl.ds(k * plane, plane), :], t)
        acc = term if acc is None else acc + term
    y = x_ref[...] + g2_ref[0] * acc
    if final_norm:
        y = y * lax.rsqrt(jnp.mean(y * y, axis=-1, keepdims=True) + RMS_EPS) * gf_ref[...]
    o_ref[...] = y


def _combine(x, ys, dest_rows, gates, mods, normf_g, seg, final_norm, first_row=0, n_rows=None):
    n, d = x.shape
    n_rows = n if n_rows is None else n_rows
    n_ctx, smp_len = seg
    t = ROW_TILE
    first = first_row // t
    steps = n_rows // t
    rows = lambda i: (i + first, 0)
    return pl.pallas_call(
        functools.partial(_combine_kernel, final_norm=final_norm),
        grid=(steps,),
        in_specs=[
            pl.BlockSpec((ENTRIES_PER_TILE,), lambda i: (i + first,), memory_space=pltpu.SMEM),
            pl.BlockSpec((ENTRIES_PER_TILE,), lambda i: (jnp.minimum(i + 1, steps - 1) + first,),
                         memory_space=pltpu.SMEM),
            pl.BlockSpec((t, d), rows),
            pl.BlockSpec((t, TOP_K), rows),
            _mod_spec(5, t, n_ctx, smp_len, first),
            pl.BlockSpec((1, d), lambda i: (0, 0)),
            pl.BlockSpec(memory_space=pl.ANY),
        ],
        out_specs=pl.BlockSpec((t, d), lambda i: (i, 0)),
        out_shape=jax.ShapeDtypeStruct((n_rows, d), F32),
        scratch_shapes=[pltpu.VMEM((2, TOP_K * t * ROW_SUBLANES, LANES), F32), pltpu.SemaphoreType.DMA((2,))],
        compiler_params=_cparams("arbitrary"),
        name="moe_combine",
    )(dest_rows, dest_rows, x, gates, mods, normf_g.reshape(1, d), ys)


def _moe_layer(x, norm_g, mods, w_r, b_r, layer, w_g, b_g, w_u, b_u, w_d, b_d, normf_g, seg, final_norm):
    n, d = x.shape
    nk = n * TOP_K
    h, top_e, gates, rank, counts = _router(x, norm_g, mods, w_r, b_r, seg)
    counts = counts[0]
    padded = (counts + MOE_TILE - 1) // MOE_TILE * MOE_TILE
    pad_end = jnp.cumsum(padded)
    pad_start = pad_end - padded
    cap = nk + N_EXPERTS * MOE_TILE
    nb = cap // MOE_TILE
    experts = jnp.arange(N_EXPERTS, dtype=jnp.int32)
    start_of = jnp.sum(jnp.where(top_e[..., None] == experts, pad_start, 0), axis=-1)
    dest_rows = ((start_of + rank) * ROW_SUBLANES).reshape(nk).astype(jnp.int32)
    blk_start = jnp.arange(nb, dtype=jnp.int32) * MOE_TILE
    block_e = jnp.minimum(jnp.sum(blk_start[:, None] >= pad_end[None, :], axis=1), N_EXPERTS - 1).astype(jnp.int32)
    n_active = (pad_end[-1] // MOE_TILE).astype(jnp.int32).reshape(1)
    xs = _dispatch(h, dest_rows, (pad_start + counts).astype(jnp.int32), (padded - counts).astype(jnp.int32),
                   n_active, cap)
    changes = jnp.concatenate([jnp.zeros((1,), jnp.int32), (block_e[1:] != block_e[:-1]).astype(jnp.int32)])
    weight_slot = (jnp.cumsum(changes) % 2).astype(jnp.int32)
    later = (experts[None, :] > experts[:, None]) & (counts[None, :] > 0)
    next_of = jnp.min(jnp.where(later, experts[None, :], N_EXPERTS), axis=1)
    next_expert = jnp.where(next_of < N_EXPERTS, next_of, -1).astype(jnp.int32)[block_e]
    ys = _expert_ffn(xs, block_e, n_active, weight_slot, next_expert, layer, w_g, b_g, w_u, b_u, w_d, b_d)
    if not final_norm:
        return _combine(x, ys, dest_rows, gates, mods, normf_g, seg, False)
    n_ctx = seg[0]
    return (_combine(x, ys, dest_rows, gates, mods, normf_g, seg, True, 0, n_ctx),
            _combine(x, ys, dest_rows, gates, mods, normf_g, seg, True, n_ctx, n - n_ctx))


HY_ORDER = 2
HY_BANDS = 16
HY_EMB = 1 + 2 * HY_BANDS
HY_FFN = 64
HY_MIN_DECAY = math.log(1e-2) / 1.5
HY_MAX_DECAY = math.log(1e-2) / 0.3
HY_EMB_PAD = 64


def _filter_mlp_kernel(z_ref, t_ref, w1_ref, b1_ref, fr_ref, w2_ref, b2_ref, w3_ref, dl_ref, o_ref, h_ref):
    @pl.when(pl.program_id(1) == 0)
    def _():
        fr = fr_ref[...]
        h = jnp.sin(fr * (_dot3(z_ref[...], w1_ref[...]) + b1_ref[...]))
        h_ref[...] = jnp.sin(fr * (_dot3(h, w2_ref[...]) + b2_ref[...]))

    o_ref[...] = _dot3(h_ref[...], w3_ref[...]) * jnp.exp(-t_ref[...] * dl_ref[...])


def _hyena_filters(length, w1, b1, freq, w2, b2, w3):
    t = jnp.linspace(0.0, 1.0, length, dtype=F32)[:, None]
    ang = (2.0 * math.pi / length) * jnp.arange(length, dtype=F32)[:, None]
    bands = jnp.linspace(1e-4, HY_BANDS - 1, HY_BANDS, dtype=F32)[None, :]
    z = jnp.concatenate([t, jnp.cos(bands * ang), -jnp.sin(bands * ang)], axis=-1)
    z = jnp.pad(z, ((0, 0), (0, HY_EMB_PAD - HY_EMB)))
    w1p = jnp.pad(w1, ((0, HY_EMB_PAD - HY_EMB), (0, 0)))
    n_out = w3.shape[1]
    deltas = jnp.abs(jnp.linspace(HY_MIN_DECAY, HY_MAX_DECAY, D_MODEL, dtype=F32))
    deltas = jnp.tile(deltas, n_out // D_MODEL)[None, :]
    tl, tn = 256, 1024
    row = lambda i, j: (i, 0)
    fixed = lambda i, j: (0, 0)
    return pl.pallas_call(
        _filter_mlp_kernel,
        grid=(length // tl, n_out // tn),
        in_specs=[
            pl.BlockSpec((tl, HY_EMB_PAD), row),
            pl.BlockSpec((tl, 1), row),
            pl.BlockSpec((HY_EMB_PAD, HY_FFN), fixed),
            pl.BlockSpec((1, HY_FFN), fixed),
            pl.BlockSpec((1, HY_FFN), fixed),
            pl.BlockSpec((HY_FFN, HY_FFN), fixed),
            pl.BlockSpec((1, HY_FFN), fixed),
            pl.BlockSpec((HY_FFN, tn), lambda i, j: (0, j)),
            pl.BlockSpec((1, tn), lambda i, j: (0, j)),
        ],
        out_specs=pl.BlockSpec((tl, tn), lambda i, j: (i, j)),
        out_shape=jax.ShapeDtypeStruct((length, n_out), F32),
        scratch_shapes=[pltpu.VMEM((tl, HY_FFN), F32)],
        compiler_params=_cparams("parallel", "arbitrary"),
        name="hyena_filter_mlp",
    )(z, t, w1p, b1.reshape(1, -1), freq.reshape(1, -1), w2, b2.reshape(1, -1), w3, deltas)


def _short_conv(z, w):
    length = z.shape[0]
    row = lax.broadcasted_iota(jnp.int32, z.shape, 0)
    prev = jnp.where(row == 0, 0.0, pltpu.roll(z, 1, 0))
    nxt = jnp.where(row == length - 1, 0.0, pltpu.roll(z, length - 1, 0))
    return (prev * w[0:1] + z * w[1:2]) + nxt * w[2:3]


def _filter_halves(hf_ref, hb_ref):
    hf = hf_ref[...]
    hb = hb_ref[...]
    hb = jnp.where(lax.broadcasted_iota(jnp.int32, hb.shape, 0) == 0, 0.0, hb)
    norm = jnp.sum(jnp.abs(hf), axis=0, keepdims=True) + jnp.sum(jnp.abs(hb), axis=0, keepdims=True)
    return hf + hb, hf - hb, 1.0 / norm


def _direct_dft_tables(length):
    n_fft = 2 * length
    n_freq = length + 1
    mf = -(-n_freq // 16) * 16
    k = np.arange(mf)[:, None]
    n = np.arange(length)[None, :]
    ang = 2.0 * np.pi * ((k * n) % n_fft) / n_fft
    valid = k < n_freq
    cos = np.where(valid, np.cos(ang), 0.0)
    msin = np.where(valid, -np.sin(ang), 0.0)
    weight = np.where((k == 0) | (k == length), 1.0, 2.0) * valid / n_fft
    fwd = np.concatenate([cos, msin], axis=0)
    inv = np.concatenate([weight * cos, weight * msin], axis=0).T
    return jnp.asarray(fwd, BF16), jnp.asarray(inv, BF16), mf


def _hyena_direct_kernel(zv_ref, z1_ref, z2_ref, wv_ref, w1_ref, w2_ref,
                         hf0_ref, hf1_ref, hb0_ref, hb1_ref, skip_ref, fw_ref, iv_ref,
                         o_ref, kr_ref, ki_ref):
    mf = fw_ref.shape[0] // 2
    dt = o_ref.shape[-1]

    @pl.when(pl.program_id(1) == 0)
    def _():
        for o, (hf_ref, hb_ref) in enumerate(((hf0_ref, hb0_ref), (hf1_ref, hb1_ref))):
            hs, hd, inv_norm = _filter_halves(hf_ref, hb_ref)
            spec = _dot(fw_ref[...], jnp.concatenate([hs, hd], axis=1).astype(BF16))
            kr_ref[o] = spec[:mf, :dt] * inv_norm
            ki_ref[o] = spec[mf:, dt:] * inv_norm

    y = _short_conv(zv_ref[0], wv_ref[...])
    for o, (z_ref, w_ref) in enumerate(((z1_ref, w1_ref), (z2_ref, w2_ref))):
        spec = _dot(fw_ref[...], y.astype(BF16))
        yr, yi = spec[:mf], spec[mf:]
        kr, ki = kr_ref[o], ki_ref[o]
        prod = jnp.concatenate([yr * kr - yi * ki, yr * ki + yi * kr], axis=0)
        yc = _dot(iv_ref[...], prod.astype(BF16))
        y = _short_conv(z_ref[0], w_ref[...]) * (yc + y * skip_ref[o:o + 1])
    o_ref[0] = y.astype(o_ref.dtype)


def _hyena_direct(z, w_short, hfilt, skip, first=0, bsz=None):
    _, length, d3 = z.shape
    bsz = z.shape[0] if bsz is None else bsz
    d = d3 // 3
    dt = 512
    nct = d // dt
    fwd, inv, mf = _direct_dft_tables(length)
    zspec = lambda part: pl.BlockSpec((1, length, dt), lambda c, b: (b + first, 0, part * nct + c))
    wspec = lambda part: pl.BlockSpec((3, dt), lambda c, b: (0, part * nct + c))
    hspec = lambda direction, order: pl.BlockSpec(
        (length, dt), lambda c, b: (0, (direction * HY_ORDER + order) * nct + c))
    fixed = lambda c, b: (0, 0)
    return pl.pallas_call(
        _hyena_direct_kernel,
        grid=(nct, bsz),
        in_specs=[zspec(0), zspec(1), zspec(2), wspec(0), wspec(1), wspec(2),
                  hspec(0, 0), hspec(0, 1), hspec(1, 0), hspec(1, 1),
                  pl.BlockSpec((HY_ORDER, dt), lambda c, b: (0, c)),
                  pl.BlockSpec(fwd.shape, fixed), pl.BlockSpec(inv.shape, fixed)],
        out_specs=pl.BlockSpec((1, length, dt), lambda c, b: (b, 0, c)),
        out_shape=jax.ShapeDtypeStruct((bsz, length, d), BF16),
        scratch_shapes=[pltpu.VMEM((HY_ORDER, mf, dt), F32), pltpu.VMEM((HY_ORDER, mf, dt), F32)],
        compiler_params=_cparams("parallel", "arbitrary"),
        name="hyena_conv_direct",
    )(z, z, z, w_short, w_short, w_short, hfilt, hfilt, hfilt, hfilt, skip, fwd, inv)


FFT_N1 = 64
FFT_N2 = 128
FFT_LANES = 128
FFT_A_PITCH = 2 * FFT_N2 + 8
FFT_U_PITCH = 2 * FFT_N1 + 8
FFT_GROUP = 2
FFT_UNROLL = 16


def _two_stage_tables():
    n_fft = FFT_N1 * FFT_N2
    half = FFT_N2 // 2
    n1 = np.arange(FFT_N1)[:, None, None]
    k2 = np.arange(FFT_N2)[None, :, None]
    n2 = np.arange(half)[None, None, :]
    ang = 2.0 * np.pi * ((k2 * (n1 + FFT_N1 * n2)) % n_fft) / n_fft
    stage_a = np.concatenate([np.cos(ang), -np.sin(ang)], axis=1)
    stage_a_inv = np.transpose(stage_a, (0, 2, 1)) / n_fft
    k1 = np.arange(FFT_N1)[:, None]
    m1 = np.arange(FFT_N1)[None, :]
    phi = 2.0 * np.pi * ((k1 * m1) % FFT_N1) / FFT_N1
    c, s = np.cos(phi), np.sin(phi)
    stage_b = np.block([[c, s], [-s, c]])
    stage_b_inv = np.block([[c, -s], [s, c]])
    return tuple(jnp.asarray(t, BF16) for t in (stage_a, stage_a_inv, stage_b, stage_b_inv))


def _fft_stage_a(y_ref, ma_ref, a_ref):
    half = FFT_N2 // 2

    def body(n1, carry):
        slab = y_ref[pl.ds(n1, half, stride=FFT_N1), :]
        a_ref[pl.ds(pl.multiple_of(n1 * FFT_A_PITCH, 8), 2 * FFT_N2), :] = _dot(ma_ref[n1], slab.astype(BF16))
        return carry

    lax.fori_loop(0, FFT_N1, body, 0, unroll=FFT_UNROLL)


def _fft_stage_b(a_ref, mb_ref, consume):
    def body(j, carry):
        k2 = j * FFT_GROUP
        cols = []
        for g in range(FFT_GROUP):
            re = a_ref[pl.ds(k2 + g, FFT_N1, stride=FFT_A_PITCH), :]
            im = a_ref[pl.ds(FFT_N2 + k2 + g, FFT_N1, stride=FFT_A_PITCH), :]
            cols.append(jnp.concatenate([re, im], axis=0))
        x = _dot(mb_ref[...], jnp.concatenate(cols, axis=1).astype(BF16))
        for g in range(FFT_GROUP):
            consume(k2 + g, x[:, g * FFT_LANES:(g + 1) * FFT_LANES])
        return carry

    lax.fori_loop(0, FFT_N2 // FFT_GROUP, body, 0, unroll=FFT_UNROLL)


def _fft_inverse(z_ref, mbi_ref, mai_ref, u_ref, out_ref):
    half = FFT_N2 // 2

    def stage_b(j, carry):
        k2 = j * FFT_GROUP
        rhs = jnp.concatenate([z_ref[k2 + g] for g in range(FFT_GROUP)], axis=1)
        u = _dot(mbi_ref[...], rhs)
        for g in range(FFT_GROUP):
            u_ref[pl.ds(pl.multiple_of((k2 + g) * FFT_U_PITCH, 8), 2 * FFT_N1), :] = (
                u[:, g * FFT_LANES:(g + 1) * FFT_LANES])
        return carry

    lax.fori_loop(0, FFT_N2 // FFT_GROUP, stage_b, 0, unroll=FFT_UNROLL)

    def stage_a(n1, carry):
        re = u_ref[pl.ds(n1, FFT_N2, stride=FFT_U_PITCH), :]
        im = u_ref[pl.ds(FFT_N1 + n1, FFT_N2, stride=FFT_U_PITCH), :]
        rhs = jnp.concatenate([re, im], axis=0).astype(BF16)
        out_ref[pl.ds(n1, half, stride=FFT_N1), :] = _dot(mai_ref[n1], rhs)
        return carry

    lax.fori_loop(0, FFT_N1, stage_a, 0, unroll=FFT_UNROLL)


FFT_WORK_ROWS = max(FFT_N1 * FFT_A_PITCH, FFT_N2 * FFT_U_PITCH)


def _hyena_spectrum_kernel(hf0_ref, hf1_ref, hb0_ref, hb1_ref, ma_ref, mb_ref, k_ref, y_ref, a_ref):
    for o, (hf_ref, hb_ref) in enumerate(((hf0_ref, hb0_ref), (hf1_ref, hb1_ref))):
        hs, hd, inv_norm = _filter_halves(hf_ref, hb_ref)

        def keep_real(k2, x):
            k_ref[o, k2, 0:FFT_N1, :] = (x[:FFT_N1] * inv_norm).astype(k_ref.dtype)

        def keep_imag(k2, x):
            k_ref[o, k2, FFT_N1:, :] = (x[FFT_N1:] * inv_norm).astype(k_ref.dtype)

        for part, keep in ((hs, keep_real), (hd, keep_imag)):
            y_ref[...] = part
            _fft_stage_a(y_ref, ma_ref, a_ref)
            _fft_stage_b(a_ref, mb_ref, keep)


def _hyena_two_stage_kernel(zv_ref, z1_ref, z2_ref, wv_ref, w1_ref, w2_ref, k_ref, skip_ref,
                            ma_ref, mai_ref, mb_ref, mbi_ref, o_ref, y_ref, c_ref, a_ref, z_ref):
    y_ref[...] = _short_conv(zv_ref[0], wv_ref[...])
    for o, (g_ref, w_ref) in enumerate(((z1_ref, w1_ref), (z2_ref, w2_ref))):
        def multiply(k2, x):
            k = k_ref[o, k2].astype(F32)
            xr, xi = x[:FFT_N1], x[FFT_N1:]
            kr, ki = k[:FFT_N1], k[FFT_N1:]
            z_ref[k2] = jnp.concatenate([xr * kr - xi * ki, xr * ki + xi * kr], axis=0).astype(z_ref.dtype)

        _fft_stage_a(y_ref, ma_ref, a_ref)
        _fft_stage_b(a_ref, mb_ref, multiply)
        _fft_inverse(z_ref, mbi_ref, mai_ref, a_ref, c_ref)
        y = _short_conv(g_ref[0], w_ref[...]) * (c_ref[...] + y_ref[...] * skip_ref[o:o + 1])
        if o + 1 < HY_ORDER:
            y_ref[...] = y
        else:
            o_ref[0] = y.astype(o_ref.dtype)


def _hyena_two_stage(z, w_short, hfilt, skip, first=0, bsz=None):
    _, length, d3 = z.shape
    bsz = z.shape[0] if bsz is None else bsz
    d = d3 // 3
    dt = FFT_LANES
    nct = d // dt
    assert 2 * length == FFT_N1 * FFT_N2
    ma, mai, mb, mbi = _two_stage_tables()
    once = pl.Buffered(1)
    hspec = lambda direction, order: pl.BlockSpec(
        (length, dt), lambda c: (0, (direction * HY_ORDER + order) * nct + c))
    spectrum = pl.pallas_call(
        _hyena_spectrum_kernel,
        grid=(nct,),
        in_specs=[hspec(0, 0), hspec(0, 1), hspec(1, 0), hspec(1, 1),
                  pl.BlockSpec(ma.shape, lambda c: (0, 0, 0), pipeline_mode=once),
                  pl.BlockSpec(mb.shape, lambda c: (0, 0), pipeline_mode=once)],
        out_specs=pl.BlockSpec((HY_ORDER, FFT_N2, 2 * FFT_N1, dt), lambda c: (0, 0, 0, c)),
        out_shape=jax.ShapeDtypeStruct((HY_ORDER, FFT_N2, 2 * FFT_N1, d), BF16),
        scratch_shapes=[pltpu.VMEM((length, dt), F32), pltpu.VMEM((FFT_WORK_ROWS, dt), F32)],
        compiler_params=_cparams("parallel"),
        name="hyena_filter_spectrum",
    )(hfilt, hfilt, hfilt, hfilt, ma, mb)

    zspec = lambda part: pl.BlockSpec((1, length, dt), lambda c, b: (b + first, 0, part * nct + c))
    wspec = lambda part: pl.BlockSpec((3, dt), lambda c, b: (0, part * nct + c))
    fixed3 = lambda c, b: (0, 0, 0)
    fixed2 = lambda c, b: (0, 0)
    return pl.pallas_call(
        _hyena_two_stage_kernel,
        grid=(nct, bsz),
        in_specs=[zspec(0), zspec(1), zspec(2), wspec(0), wspec(1), wspec(2),
                  pl.BlockSpec((HY_ORDER, FFT_N2, 2 * FFT_N1, dt), lambda c, b: (0, 0, 0, c), pipeline_mode=once),
                  pl.BlockSpec((HY_ORDER, dt), lambda c, b: (0, c)),
                  pl.BlockSpec(ma.shape, fixed3, pipeline_mode=once),
                  pl.BlockSpec(mai.shape, fixed3, pipeline_mode=once),
                  pl.BlockSpec(mb.shape, fixed2, pipeline_mode=once),
                  pl.BlockSpec(mbi.shape, fixed2, pipeline_mode=once)],
        out_specs=pl.BlockSpec((1, length, dt), lambda c, b: (b, 0, c)),
        out_shape=jax.ShapeDtypeStruct((bsz, length, d), BF16),
        scratch_shapes=[pltpu.VMEM((length, dt), F32), pltpu.VMEM((length, dt), F32),
                        pltpu.VMEM((FFT_WORK_ROWS, dt), F32), pltpu.VMEM((FFT_N2, 2 * FFT_N1, dt), BF16)],
        compiler_params=_cparams("parallel", "arbitrary"),
        name="hyena_conv_two_stage",
    )(z, z, z, w_short, w_short, w_short, spectrum, skip, ma, mai, mb, mbi)


N_HEADS = 16
HEAD_DIM = D_MODEL // N_HEADS
HEADS_PER_STEP = 2
CTX_HEADS_PER_STEP = 8
GRID_W = 64
WIN_ROWS = 8
WIN_COLS = 16
NEG_INF = -1e30
NAT_Q_ROWS = 4
NAT_K_ROWS = NAT_Q_ROWS + WIN_ROWS - 1
ATTN_SCALE = HEAD_DIM ** -0.5


def _dot_nt(a, b):
    return lax.dot_general(a, b, (((1,), (1,)), ((), ())), preferred_element_type=F32)


def _qkv_kernel(x_ref, g_ref, sc_ref, sh_ref, w_ref, q_ref, k_ref, v_ref):
    h = _norm_mod(x_ref[...], g_ref[...], sc_ref[0], sh_ref[0])
    qkv = _dot(h.astype(BF16), w_ref[...])
    for part, ref in enumerate((q_ref, k_ref, v_ref)):
        for head in range(N_HEADS):
            lo = part * D_MODEL + head * HEAD_DIM
            ref[0, head] = qkv[:, lo:lo + HEAD_DIM].astype(ref.dtype)


def _qkv_proj(x, g, mods, w_bf16, seg, first_row, n_seq, seq_len, kv_dtype):
    d = x.shape[1]
    n_ctx, smp_len = seg
    t = ROW_TILE
    first_tile = first_row // t
    per_seq = seq_len // t
    out_spec = pl.BlockSpec((1, N_HEADS, t, HEAD_DIM), lambda i: (i // per_seq, 0, i % per_seq, 0))
    shape = (n_seq, N_HEADS, seq_len, HEAD_DIM)
    return pl.pallas_call(
        _qkv_kernel,
        grid=(n_seq * per_seq,),
        in_specs=[
            pl.BlockSpec((t, d), lambda i: (i + first_tile, 0)),
            pl.BlockSpec((1, d), lambda i: (0, 0)),
            _mod_spec(1, t, n_ctx, smp_len, first_tile),
            _mod_spec(0, t, n_ctx, smp_len, first_tile),
            pl.BlockSpec(w_bf16.shape, lambda i: (0, 0)),
        ],
        out_specs=[out_spec, out_spec, out_spec],
        out_shape=[jax.ShapeDtypeStruct(shape, BF16), jax.ShapeDtypeStruct(shape, kv_dtype),
                   jax.ShapeDtypeStruct(shape, kv_dtype)],
        compiler_params=_cparams("parallel"),
        name="qkv_proj",
    )(x, g.reshape(1, d), mods, mods, w_bf16)


def _ctx_attn_kernel(q_ref, k_ref, v_ref, o_ref):
    for j in range(q_ref.shape[1]):
        q = q_ref[0, j]
        k = k_ref[0, j].astype(BF16)
        v = v_ref[0, j].astype(BF16)
        s = _dot_nt(q, k) * ATTN_SCALE
        p = jnp.exp(s - jnp.max(s, axis=-1, keepdims=True))
        o = _dot(p.astype(BF16), v) / jnp.sum(p, axis=-1, keepdims=True)
        o_ref[:, j * HEAD_DIM:(j + 1) * HEAD_DIM] = o.astype(o_ref.dtype)


def _ctx_attention(q, k, v):
    bsz, _, s, _ = q.shape
    hp = CTX_HEADS_PER_STEP
    spec = pl.BlockSpec((1, hp, s, HEAD_DIM), lambda b, h: (b, h, 0, 0))
    return pl.pallas_call(
        _ctx_attn_kernel,
        grid=(bsz, N_HEADS // hp),
        in_specs=[spec, spec, spec],
        out_specs=pl.BlockSpec((s, hp * HEAD_DIM), lambda b, h: (b, h)),
        out_shape=jax.ShapeDtypeStruct((bsz * s, D_MODEL), BF16),
        compiler_params=_cparams("parallel", "parallel"),
        name="ctx_attention",
    )(q, k, v)


def _rpb_toeplitz_kernel(r_ref, e_ref, o_ref):
    o_ref[...] = _dot3(r_ref[...], e_ref[...])


def _nat_bias(rpb):
    n_heads, n_r, n_c = rpb.shape
    n_cp = 32
    qc = np.arange(GRID_W)[:, None]
    kc = np.arange(GRID_W)[None, :]
    onehot = (np.clip(kc - qc + WIN_COLS - 1, 0, n_c - 1)[None] == np.arange(n_cp)[:, None, None])
    onehot = jnp.asarray(onehot.reshape(n_cp, GRID_W * GRID_W), F32)
    rows = jnp.pad(rpb.reshape(n_heads * n_r, n_c), ((0, 0), (0, n_cp - n_c)))
    toep = pl.pallas_call(
        _rpb_toeplitz_kernel,
        out_shape=jax.ShapeDtypeStruct((n_heads * n_r, GRID_W * GRID_W), F32),
        compiler_params=_cparams(),
        name="nat_bias_toeplitz",
    )(rows, onehot).reshape(n_heads, n_r, GRID_W, GRID_W)
    q_start = np.clip(qc - WIN_COLS // 2, 0, GRID_W - WIN_COLS)
    col_ok = jnp.asarray((kc >= q_start) & (kc < q_start + WIN_COLS))
    toep = jnp.where(col_ok, toep, NEG_INF)
    toep = jnp.concatenate([toep, jnp.full((n_heads, 1, GRID_W, GRID_W), NEG_INF, F32)], axis=1)
    rows_total = GRID_W
    idx = np.full((3, NAT_Q_ROWS, NAT_K_ROWS), n_r, np.int32)
    for case, r0 in enumerate((0, NAT_Q_ROWS, rows_total - NAT_Q_ROWS)):
        ks = int(np.clip(r0 - WIN_ROWS // 2, 0, rows_total - NAT_K_ROWS))
        for dr in range(NAT_Q_ROWS):
            r = r0 + dr
            rs = int(np.clip(r - WIN_ROWS // 2, 0, rows_total - WIN_ROWS))
            for dk in range(NAT_K_ROWS):
                kr = ks + dk
                if rs <= kr < rs + WIN_ROWS:
                    idx[case, dr, dk] = kr - r + WIN_ROWS - 1
    def assemble(t_ref, o_ref):
        for case in range(3):
            for dr in range(NAT_Q_ROWS):
                for dk in range(NAT_K_ROWS):
                    o_ref[0, case, dr * GRID_W:(dr + 1) * GRID_W, dk * GRID_W:(dk + 1) * GRID_W] = (
                        t_ref[0, int(idx[case, dr, dk])])

    return pl.pallas_call(
        assemble,
        grid=(n_heads,),
        in_specs=[pl.BlockSpec((1, n_r + 1, GRID_W, GRID_W), lambda h: (h, 0, 0, 0))],
        out_specs=pl.BlockSpec((1, 3, NAT_Q_ROWS * GRID_W, NAT_K_ROWS * GRID_W), lambda h: (h, 0, 0, 0)),
        out_shape=jax.ShapeDtypeStruct((n_heads, 3, NAT_Q_ROWS * GRID_W, NAT_K_ROWS * GRID_W), F32),
        compiler_params=_cparams("parallel"),
        name="nat_bias_assemble",
    )(toep)


def _nat_kernel(q_ref, k_ref, v_ref, kc_ref, vc_ref, bias_ref, o_ref):
    n_blocks = q_ref.shape[2] // (NAT_Q_ROWS * GRID_W)
    rows_total = q_ref.shape[2] // GRID_W
    nq = NAT_Q_ROWS * GRID_W
    nk = NAT_K_ROWS * GRID_W
    ctx = [(kc_ref[0, 0, j].astype(BF16), vc_ref[0, 0, j].astype(BF16)) for j in range(HEADS_PER_STEP)]

    def block(blk, carry):
        ks = jnp.clip(blk * NAT_Q_ROWS - WIN_ROWS // 2, 0, rows_total - NAT_K_ROWS)
        case = jnp.where(blk == 0, 0, jnp.where(blk == n_blocks - 1, 2, 1))
        q_rows = pl.ds(pl.multiple_of(blk * nq, nq), nq)
        k_rows = pl.ds(pl.multiple_of(ks * GRID_W, GRID_W), nk)
        for j, (k_ctx, v_ctx) in enumerate(ctx):
            q = q_ref[0, j, q_rows, :]
            s_loc = _dot_nt(q, k_ref[0, j, k_rows, :]) * ATTN_SCALE + bias_ref[j, case]
            s_ctx = _dot_nt(q, k_ctx) * ATTN_SCALE
            m = jnp.maximum(jnp.max(s_loc, axis=-1, keepdims=True), jnp.max(s_ctx, axis=-1, keepdims=True))
            p_loc = jnp.exp(s_loc - m)
            p_ctx = jnp.exp(s_ctx - m)
            denom = jnp.sum(p_loc, axis=-1, keepdims=True) + jnp.sum(p_ctx, axis=-1, keepdims=True)
            o = (_dot(p_loc.astype(BF16), v_ref[0, j, k_rows, :]) + _dot(p_ctx.astype(BF16), v_ctx)) / denom
            o_ref[q_rows, j * HEAD_DIM:(j + 1) * HEAD_DIM] = o.astype(o_ref.dtype)
        return carry

    lax.fori_loop(0, n_blocks, block, 0, unroll=2)


def _nat_attention(q, k, v, cache_k, cache_v, bias):
    bsz, _, length, _ = q.shape
    hp = HEADS_PER_STEP
    past = cache_k.shape[3]
    spec = pl.BlockSpec((1, hp, length, HEAD_DIM), lambda b, h: (b, h, 0, 0))
    cspec = pl.BlockSpec((1, 1, hp, past, HEAD_DIM), lambda b, h: (b, 0, h, 0, 0))
    return pl.pallas_call(
        _nat_kernel,
        grid=(bsz, N_HEADS // hp),
        in_specs=[spec, spec, spec, cspec, cspec,
                  pl.BlockSpec((hp,) + bias.shape[1:], lambda b, h: (h, 0, 0, 0))],
        out_specs=pl.BlockSpec((length, hp * HEAD_DIM), lambda b, h: (b, h)),
        out_shape=jax.ShapeDtypeStruct((bsz * length, D_MODEL), BF16),
        compiler_params=_cparams("parallel", "parallel"),
        name="nat_attention",
    )(q, k, v, cache_k, cache_v, bias)


def kernel(x_prompt, x_sample, cache_k, cache_v, c, c_ctx, ada_w, ada_b, norm1_g, norm2_g, normf_g, hy_w_in, hy_w_short, hy_f_w1, hy_f_b1, hy_f_freq, hy_f_w2, hy_f_b2, hy_f_w3, hy_skip, hy_w_out, na_w_qkv, na_rpb, na_w_o, moe_w_router, moe_b_router, moe_w_gate, moe_b_gate, moe_w_up, moe_b_up, moe_w_down, moe_b_down):
    b, s, d = x_prompt.shape
    bd, sd, _ = x_sample.shape
    n_ctx = b * s
    n_tok = n_ctx + bd * sd
    assert n_ctx % sd == 0 and s % ROW_TILE == 0 and sd % ROW_TILE == 0
    seg = (n_ctx, sd)
    x_ctx = x_prompt.reshape(n_ctx, d)
    x_smp = x_sample.reshape(bd * sd, d)
    cond = jnp.zeros((N_COND, d), F32).at[0].set(c_ctx).at[1:1 + bd].set(c)
    mods = _modulation(cond, ada_w, ada_b)

    def moe(x, i, final_norm):
        return _moe_layer(x, norm2_g[i], mods[i], moe_w_router[i], moe_b_router[i], i, moe_w_gate, moe_b_gate,
                          moe_w_up, moe_b_up, moe_w_down, moe_b_down, normf_g, seg, final_norm)

    z = _norm_proj(x_ctx, x_smp, norm1_g[0], mods[0], hy_w_in[0].astype(BF16), seg)
    fargs = (hy_f_w1[0], hy_f_b1[0], hy_f_freq[0], hy_f_w2[0], hy_f_b2[0], hy_f_w3[0])
    y_ctx = _hyena_direct(z.reshape(n_tok // s, s, 3 * d), hy_w_short[0], _hyena_filters(s, *fargs),
                          hy_skip[0], first=0, bsz=b)
    y_smp = _hyena_two_stage(z.reshape(n_tok // sd, sd, 3 * d), hy_w_short[0], _hyena_filters(sd, *fargs),
                             hy_skip[0], first=n_ctx // sd, bsz=bd)
    x = _out_proj(y_ctx.reshape(n_ctx, d), y_smp.reshape(bd * sd, d), hy_w_out[0].astype(BF16), x_ctx, x_smp,
                  mods[0], seg)
    x = moe(x, 0, False)

    w_qkv = na_w_qkv[0].astype(BF16)
    q_c, k_c, v_c = _qkv_proj(x, norm1_g[1], mods[1], w_qkv, seg, 0, b, s, F32)
    q_s, k_s, v_s = _qkv_proj(x, norm1_g[1], mods[1], w_qkv, seg, n_ctx, bd, sd, BF16)
    o_ctx = _ctx_attention(q_c, k_c, v_c)
    o_smp = _nat_attention(q_s, k_s, v_s, cache_k, cache_v, _nat_bias(na_rpb[0]))
    x = _out_proj(o_ctx, o_smp, na_w_o[0].astype(BF16), x, None, mods[1], seg)
    y_prompt, y_sample = moe(x, 1, True)

    nh, hd = k_c.shape[1], k_c.shape[3]
    return (y_prompt.reshape(b, s, d), y_sample.reshape(bd, sd, d),
            k_c.reshape(b, 1, nh, s, hd), v_c.reshape(b, 1, nh, s, hd))
```

```python
import functools
import math

import numpy as np
import jax
import jax.numpy as jnp
from jax import lax
from jax.experimental import pallas as pl
from jax.experimental.pallas import tpu as pltpu

F32 = jnp.float32
BF16 = jnp.bfloat16

D_MODEL = 1024
N_MOD = 6
RMS_EPS = 1e-6
N_EXPERTS = 32
TOP_K = 4
SWIGLU_LIMIT = 7.0
SWIGLU_ALPHA = 1.702

N_COND = 8
ROW_TILE = 256
MOE_TILE = 512
V7X_VMEM_LIMIT = 56 * 1024 * 1024


def _cparams(*sem, vmem=V7X_VMEM_LIMIT):
    return pltpu.CompilerParams(dimension_semantics=sem, vmem_limit_bytes=vmem)


def _dot(a, b):
    return jnp.dot(a, b, preferred_element_type=F32)


def _split_bf16(x):
    hi = x.astype(BF16)
    lo = (x - hi.astype(F32)).astype(BF16)
    return hi, lo


def _dot3(a, b):
    ah, al = _split_bf16(a)
    bh, bl = _split_bf16(b)
    return _dot(ah, bh) + (_dot(al, bh) + _dot(ah, bl))


def _seg_of_tile(i, tile, n_ctx, smp_len):
    ctx_tiles = n_ctx // tile
    per_smp = smp_len // tile
    return jnp.where(i < ctx_tiles, 0, 1 + (i - ctx_tiles) // per_smp)


def _norm_mod(x, g, sc, sh):
    y = x * lax.rsqrt(jnp.mean(x * x, axis=-1, keepdims=True) + RMS_EPS)
    return (y * g) * (1.0 + sc) + sh


def _mod_kernel(c_ref, w_ref, b_ref, o_ref):
    c = c_ref[...]
    a = c * jax.nn.sigmoid(c)
    o_ref[0] = _dot3(a, w_ref[0]) + b_ref[0]


def _modulation(cond, ada_w, ada_b):
    depth, d, n_out = ada_w.shape
    tn = 1536
    m = pl.pallas_call(
        _mod_kernel,
        grid=(depth, n_out // tn),
        in_specs=[
            pl.BlockSpec((N_COND, d), lambda l, j: (0, 0)),
            pl.BlockSpec((1, d, tn), lambda l, j: (l, 0, j)),
            pl.BlockSpec((1, 1, tn), lambda l, j: (l, 0, j)),
        ],
        out_specs=pl.BlockSpec((1, N_COND, tn), lambda l, j: (l, 0, j)),
        out_shape=jax.ShapeDtypeStruct((depth, N_COND, n_out), F32),
        compiler_params=_cparams("arbitrary", "arbitrary"),
        name="adaln_modulation",
    )(cond, ada_w, ada_b.reshape(depth, 1, n_out))
    m = m.reshape(depth, N_COND, N_MOD, d)
    return jnp.transpose(m, (0, 2, 1, 3)).reshape(depth, N_MOD * N_COND, 1, d)


def _mod_spec(which, tile, n_ctx, smp_len, first_tile=0):
    return pl.BlockSpec(
        (1, 1, D_MODEL),
        lambda i, *_: (which * N_COND + _seg_of_tile(i + first_tile, tile, n_ctx, smp_len), 0, 0))


def _group_specs(block_cols, ctx_tiles, stacked=False):
    base = ctx_tiles if stacked else 0
    return (pl.BlockSpec((ROW_TILE, block_cols), lambda i: (jnp.minimum(i, ctx_tiles - 1), 0)),
            pl.BlockSpec((ROW_TILE, block_cols), lambda i: (jnp.maximum(i - ctx_tiles, 0) + base, 0)))


def _pick_group(a_ref, b_ref, ctx_tiles):
    return jnp.where(pl.program_id(0) < ctx_tiles, a_ref[...], b_ref[...])


def _norm_proj_kernel(xa_ref, xb_ref, g_ref, sc_ref, sh_ref, w_ref, o_ref, *, ctx_tiles):
    h = _norm_mod(_pick_group(xa_ref, xb_ref, ctx_tiles), g_ref[...], sc_ref[0], sh_ref[0])
    o_ref[...] = _dot(h.astype(BF16), w_ref[...]).astype(o_ref.dtype)


def _norm_proj(x_ctx, x_smp, g, mods, w_bf16, seg, out_dtype=F32):
    d = x_ctx.shape[1]
    n = x_ctx.shape[0] + x_smp.shape[0]
    n_out = w_bf16.shape[1]
    n_ctx, smp_len = seg
    ctx_tiles = n_ctx // ROW_TILE
    return pl.pallas_call(
        functools.partial(_norm_proj_kernel, ctx_tiles=ctx_tiles),
        grid=(n // ROW_TILE,),
        in_specs=[
            *_group_specs(d, ctx_tiles),
            pl.BlockSpec((1, d), lambda i: (0, 0)),
            _mod_spec(1, ROW_TILE, n_ctx, smp_len),
            _mod_spec(0, ROW_TILE, n_ctx, smp_len),
            pl.BlockSpec((d, n_out), lambda i: (0, 0)),
        ],
        out_specs=pl.BlockSpec((ROW_TILE, n_out), lambda i: (i, 0)),
        out_shape=jax.ShapeDtypeStruct((n, n_out), out_dtype),
        compiler_params=_cparams("parallel"),
        name="norm_proj",
    )(x_ctx, x_smp, g.reshape(1, d), mods, mods, w_bf16)


def _out_proj_kernel(ya_ref, yb_ref, w_ref, xa_ref, xb_ref, gate_ref, o_ref, *, ctx_tiles):
    y = _pick_group(ya_ref, yb_ref, ctx_tiles)
    o_ref[...] = _pick_group(xa_ref, xb_ref, ctx_tiles) + gate_ref[0] * _dot(y, w_ref[...])


def _out_proj(y_ctx, y_smp, w_bf16, x_ctx, x_smp, mods, seg):
    d = x_ctx.shape[1]
    n_ctx, smp_len = seg
    n = n_ctx + y_smp.shape[0]
    ctx_tiles = n_ctx // ROW_TILE
    stacked = x_smp is None
    x_smp = x_ctx if stacked else x_smp
    return pl.pallas_call(
        functools.partial(_out_proj_kernel, ctx_tiles=ctx_tiles),
        grid=(n // ROW_TILE,),
        in_specs=[
            *_group_specs(d, ctx_tiles),
            pl.BlockSpec((d, d), lambda i: (0, 0)),
            *_group_specs(d, ctx_tiles, stacked),
            _mod_spec(2, ROW_TILE, n_ctx, smp_len),
        ],
        out_specs=pl.BlockSpec((ROW_TILE, d), lambda i: (i, 0)),
        out_shape=jax.ShapeDtypeStruct((n, d), F32),
        compiler_params=_cparams("parallel"),
        name="out_proj",
    )(y_ctx, y_smp, w_bf16, x_ctx, x_smp, mods)


LANES = 128
ROW_SUBLANES = D_MODEL // LANES


def _store_row_tiles(ref, x):
    t = x.shape[0]
    for s in range(ROW_SUBLANES):
        ref[pl.ds(s, t, stride=ROW_SUBLANES), :] = x[:, s * LANES:(s + 1) * LANES]


def _load_row_tiles(ref, t):
    return jnp.concatenate([ref[pl.ds(s, t, stride=ROW_SUBLANES), :] for s in range(ROW_SUBLANES)], axis=1)


def _columns(cols):
    t = cols[0].shape[0]
    lane = lax.broadcasted_iota(jnp.int32, (t, len(cols)), 1)
    out = jnp.broadcast_to(cols[-1], (t, len(cols)))
    for k in range(len(cols) - 2, -1, -1):
        out = jnp.where(lane == k, cols[k], out)
    return out


def _router_kernel(x_ref, g_ref, sc_ref, sh_ref, wr_ref, br_ref, tri_ref,
                   h_ref, e_ref, gate_ref, rank_ref, cnt_ref, run_ref):
    i = pl.program_id(0)

    @pl.when(i == 0)
    def _():
        run_ref[...] = jnp.zeros_like(run_ref)

    h = _norm_mod(x_ref[...], g_ref[...], sc_ref[0], sh_ref[0])
    _store_row_tiles(h_ref, h)
    logits = _dot3(h, wr_ref[...]) + br_ref[...]
    lane = lax.broadcasted_iota(jnp.int32, logits.shape, 1).astype(F32)
    work = logits
    vals, idxs, hots = [], [], []
    for _ in range(TOP_K):
        m = jnp.max(work, axis=-1, keepdims=True)
        idx = jnp.min(jnp.where(work == m, lane, float(N_EXPERTS)), axis=-1, keepdims=True)
        hot = lane == idx
        vals.append(m)
        idxs.append(idx)
        hots.append(hot)
        work = jnp.where(hot, -jnp.inf, work)
    ex = [jnp.exp(v - vals[0]) for v in vals]
    denom = ex[0] + ex[1] + ex[2] + ex[3]
    gate_ref[...] = _columns([e / denom for e in ex])
    e_ref[...] = _columns(idxs).astype(jnp.int32)

    chosen = (hots[0] | hots[1] | hots[2] | hots[3]).astype(F32)
    before = run_ref[...] + _dot(tri_ref[...], chosen.astype(BF16))
    ranks = [jnp.sum(jnp.where(hot, before, 0.0), axis=-1, keepdims=True) for hot in hots]
    rank_ref[...] = _columns(ranks).astype(jnp.int32)
    run_ref[...] += jnp.sum(chosen, axis=0, keepdims=True)
    cnt_ref[...] = run_ref[...].astype(jnp.int32)


def _router(x, g, mods, w_r, b_r, seg):
    n, d = x.shape
    n_ctx, smp_len = seg
    t = ROW_TILE
    tri = jnp.asarray(np.tril(np.ones((t, t), np.float32), -1), BF16)
    tok4 = lambda i: (i, 0)
    return pl.pallas_call(
        _router_kernel,
        grid=(n // t,),
        in_specs=[
            pl.BlockSpec((t, d), lambda i: (i, 0)),
            pl.BlockSpec((1, d), lambda i: (0, 0)),
            _mod_spec(4, t, n_ctx, smp_len),
            _mod_spec(3, t, n_ctx, smp_len),
            pl.BlockSpec((d, N_EXPERTS), lambda i: (0, 0)),
            pl.BlockSpec((1, N_EXPERTS), lambda i: (0, 0)),
            pl.BlockSpec((t, t), lambda i: (0, 0)),
        ],
        out_specs=[
            pl.BlockSpec((t * ROW_SUBLANES, LANES), tok4),
            pl.BlockSpec((t, TOP_K), tok4),
            pl.BlockSpec((t, TOP_K), tok4),
            pl.BlockSpec((t, TOP_K), tok4),
            pl.BlockSpec((1, N_EXPERTS), lambda i: (0, 0)),
        ],
        out_shape=[
            jax.ShapeDtypeStruct((n * ROW_SUBLANES, LANES), F32),
            jax.ShapeDtypeStruct((n, TOP_K), jnp.int32),
            jax.ShapeDtypeStruct((n, TOP_K), F32),
            jax.ShapeDtypeStruct((n, TOP_K), jnp.int32),
            jax.ShapeDtypeStruct((1, N_EXPERTS), jnp.int32),
        ],
        scratch_shapes=[pltpu.VMEM((1, N_EXPERTS), F32)],
        compiler_params=_cparams("arbitrary"),
        name="moe_router",
    )(x, g.reshape(1, d), mods, mods, w_r, b_r.reshape(1, N_EXPERTS), tri)


def _tile_copy(src_ref, src_row, dst_ref, dst_row, sem):
    rows = lambda r: pl.ds(pl.multiple_of(r, ROW_SUBLANES), ROW_SUBLANES)
    return pltpu.make_async_copy(src_ref.at[rows(src_row), :], dst_ref.at[rows(dst_row), :], sem)


def _tiles_wait(src_ref, dst_ref, n_tiles, sem):
    rows = pl.ds(0, n_tiles * ROW_SUBLANES)
    pltpu.make_async_copy(src_ref.at[rows, :], dst_ref.at[rows, :], sem).wait()


ENTRIES_PER_TILE = ROW_TILE * TOP_K
DMA_PRIORITIES = 2


def _dispatch_kernel(fill_start_ref, fill_len_ref, na_ref, dest_ref, h_ref, xs_ref, hbuf, zero_ref, sem):
    i = pl.program_id(0)
    last = pl.num_programs(0) - 1
    slot = lax.rem(i, 2)
    hbuf[slot] = h_ref[...]

    def issue(t, carry):
        for k in range(TOP_K):
            _tile_copy(hbuf.at[slot], t * ROW_SUBLANES, xs_ref, dest_ref[t * TOP_K + k],
                       sem.at[slot]).start(priority=k % DMA_PRIORITIES)
        return carry

    lax.fori_loop(0, ROW_TILE, issue, 0, unroll=4)

    @pl.when(i >= 1)
    def _():
        _tiles_wait(xs_ref, xs_ref, ENTRIES_PER_TILE, sem.at[1 - slot])

    @pl.when(i == last)
    def _():
        _tiles_wait(xs_ref, xs_ref, ENTRIES_PER_TILE, sem.at[slot])
        zero_ref[...] = jnp.zeros_like(zero_ref)
        for e in range(N_EXPERTS):
            start = fill_start_ref[e]
            count = fill_len_ref[e]

            def fill(r, carry):
                _tile_copy(zero_ref, 0, xs_ref, (start + r) * ROW_SUBLANES, sem.at[0]).start()
                return carry

            lax.fori_loop(0, count, fill, 0)

            def fill_drain(r, carry):
                _tile_copy(zero_ref, 0, xs_ref, 0, sem.at[0]).wait()
                return carry

            lax.fori_loop(0, count, fill_drain, 0)

        block_rows = MOE_TILE * ROW_SUBLANES

        def block_copy(blk):
            return pltpu.make_async_copy(
                zero_ref, xs_ref.at[pl.ds(pl.multiple_of(blk * block_rows, block_rows), block_rows), :], sem.at[0])

        n_blocks = xs_ref.shape[0] // block_rows

        def tail(blk, carry):
            block_copy(blk).start()
            return carry

        lax.fori_loop(na_ref[0], n_blocks, tail, 0)

        def tail_drain(blk, carry):
            block_copy(0).wait()
            return carry

        lax.fori_loop(na_ref[0], n_blocks, tail_drain, 0)


def _dispatch(h_tiles, dest_rows, fill_start, fill_len, n_active, cap):
    n = h_tiles.shape[0] // ROW_SUBLANES
    grid_spec = pltpu.PrefetchScalarGridSpec(
        num_scalar_prefetch=3,
        grid=(n // ROW_TILE,),
        in_specs=[
            pl.BlockSpec((ENTRIES_PER_TILE,), lambda i, *_: (i,), memory_space=pltpu.SMEM),
            pl.BlockSpec((ROW_TILE * ROW_SUBLANES, LANES), lambda i, *_: (i, 0)),
        ],
        out_specs=pl.BlockSpec(memory_space=pl.ANY),
        scratch_shapes=[pltpu.VMEM((2, ROW_TILE * ROW_SUBLANES, LANES), F32),
                        pltpu.VMEM((MOE_TILE * ROW_SUBLANES, LANES), F32), pltpu.SemaphoreType.DMA((2,))],
    )
    return pl.pallas_call(
        _dispatch_kernel,
        grid_spec=grid_spec,
        out_shape=jax.ShapeDtypeStruct((cap * ROW_SUBLANES, LANES), F32),
        compiler_params=_cparams("arbitrary"),
        name="moe_dispatch",
    )(fill_start, fill_len, n_active, dest_rows, h_tiles)


def _ffn_kernel(be_ref, na_ref, wslot_ref, next_e_ref, xs_ref, wg_ref, bg_ref, wu_ref, bu_ref, wd_ref, bd_ref,
                ys_ref, wbuf, wg_s, wu_s, wd_s, wsem, *, layer):
    i = pl.program_id(0)
    n_active = na_ref[0]
    tm = xs_ref.shape[0] // ROW_SUBLANES
    changed = jnp.logical_or(i == 0, be_ref[i] != be_ref[jnp.maximum(i - 1, 0)])
    fresh = jnp.logical_and(i < n_active, changed)

    def weight_copies(expert, b):
        return [pltpu.make_async_copy(w_ref.at[layer, expert], wbuf.at[b, m], wsem.at[b])
                for m, w_ref in enumerate((wg_ref, wu_ref, wd_ref))]

    @pl.when(jnp.logical_and(i == 0, n_active > 0))
    def _():
        for copy in weight_copies(be_ref[0], wslot_ref[0]):
            copy.start()

    @pl.when(fresh)
    def _():
        b = wslot_ref[i]
        for copy in weight_copies(be_ref[i], b):
            copy.wait()
        wg_s[...] = wbuf[b, 0].astype(BF16)
        wu_s[...] = wbuf[b, 1].astype(BF16)
        wd_s[...] = wbuf[b, 2].astype(BF16)
        upcoming = next_e_ref[i]

        @pl.when(upcoming >= 0)
        def _():
            for copy in weight_copies(upcoming, 1 - b):
                copy.start()

    @pl.when(i < n_active)
    def _():
        x = _load_row_tiles(xs_ref, tm).astype(BF16)
        g = jnp.minimum(_dot(x, wg_s[...]) + bg_ref[0, 0], SWIGLU_LIMIT)
        u = jnp.clip(_dot(x, wu_s[...]) + bu_ref[0, 0], -SWIGLU_LIMIT, SWIGLU_LIMIT)
        a = g * jax.nn.sigmoid(SWIGLU_ALPHA * g) * (u + 1.0)
        _store_row_tiles(ys_ref, _dot(a.astype(BF16), wd_s[...]) + bd_ref[0, 0])

    @pl.when(i >= na_ref[0])
    def _():
        ys_ref[...] = jnp.zeros_like(ys_ref)


def _expert_ffn(xs, block_e, n_active, weight_slot, next_expert, layer, w_g, b_g, w_u, b_u, w_d, b_d):
    d = D_MODEL
    depth, ne, _, f = w_g.shape
    assert f == d
    nb = block_e.shape[0]
    block = (MOE_TILE * ROW_SUBLANES, LANES)
    bmap = lambda i, be, *_: (layer, be[i], 0, 0)
    hbm = pl.BlockSpec(memory_space=pl.ANY)
    grid_spec = pltpu.PrefetchScalarGridSpec(
        num_scalar_prefetch=4,
        grid=(nb,),
        in_specs=[
            pl.BlockSpec(block, lambda i, be, na, *_: (jnp.minimum(i, na[0] - 1), 0)),
            hbm, pl.BlockSpec((1, 1, 1, f), bmap),
            hbm, pl.BlockSpec((1, 1, 1, f), bmap),
            hbm, pl.BlockSpec((1, 1, 1, d), bmap),
        ],
        out_specs=pl.BlockSpec(block, lambda i, *_: (i, 0)),
        scratch_shapes=[pltpu.VMEM((2, 3, d, f), F32),
                        pltpu.VMEM((d, f), BF16), pltpu.VMEM((d, f), BF16), pltpu.VMEM((f, d), BF16),
                        pltpu.SemaphoreType.DMA((2,))],
    )
    return pl.pallas_call(
        functools.partial(_ffn_kernel, layer=layer),
        grid_spec=grid_spec,
        out_shape=jax.ShapeDtypeStruct(xs.shape, F32),
        compiler_params=_cparams("arbitrary"),
        name="moe_expert_ffn",
    )(block_e, n_active, weight_slot, next_expert, xs,
      w_g, b_g.reshape(depth, ne, 1, f), w_u, b_u.reshape(depth, ne, 1, f), w_d, b_d.reshape(depth, ne, 1, d))


def _combine_kernel(dest_ref, dest_next_ref, x_ref, gate_ref, g2_ref, gf_ref, ys_ref, o_ref, buf, sem, *,
                    final_norm):
    i = pl.program_id(0)
    slot = lax.rem(i, 2)
    t = x_ref.shape[0]
    plane = t * ROW_SUBLANES

    def start_gather(idx_ref, b):
        def body(r, carry):
            for k in range(TOP_K):
                _tile_copy(ys_ref, idx_ref[r * TOP_K + k], buf.at[b], k * plane + r * ROW_SUBLANES,
                           sem.at[b]).start(priority=k % DMA_PRIORITIES)
            return carry

        lax.fori_loop(0, t, body, 0, unroll=4)

    @pl.when(i == 0)
    def _():
        start_gather(dest_ref, 0)

    @pl.when(i + 1 < pl.num_programs(0))
    def _():
        start_gather(dest_next_ref, 1 - slot)

    _tiles_wait(ys_ref, buf.at[slot], TOP_K * t, sem.at[slot])
    gates = gate_ref[...]
    acc = None
    for k in range(TOP_K):
        term = gates[:, k:k + 1] * _load_row_tiles(buf.at[slot, pl.ds(k * plane, plane), :], t)
        acc = term if acc is None else acc + term
    y = x_ref[...] + g2_ref[0] * acc
    if final_norm:
        y = y * lax.rsqrt(jnp.mean(y * y, axis=-1, keepdims=True) + RMS_EPS) * gf_ref[...]
    o_ref[...] = y


def _combine(x, ys, dest_rows, gates, mods, normf_g, seg, final_norm, first_row=0, n_rows=None):
    n, d = x.shape
    n_rows = n if n_rows is None else n_rows
    n_ctx, smp_len = seg
    t = ROW_TILE
    first = first_row // t
    steps = n_rows // t
    rows = lambda i: (i + first, 0)
    return pl.pallas_call(
        functools.partial(_combine_kernel, final_norm=final_norm),
        grid=(steps,),
        in_specs=[
            pl.BlockSpec((ENTRIES_PER_TILE,), lambda i: (i + first,), memory_space=pltpu.SMEM),
            pl.BlockSpec((ENTRIES_PER_TILE,), lambda i: (jnp.minimum(i + 1, steps - 1) + first,),
                         memory_space=pltpu.SMEM),
            pl.BlockSpec((t, d), rows),
            pl.BlockSpec((t, TOP_K), rows),
            _mod_spec(5, t, n_ctx, smp_len, first),
            pl.BlockSpec((1, d), lambda i: (0, 0)),
            pl.BlockSpec(memory_space=pl.ANY),
        ],
        out_specs=pl.BlockSpec((t, d), lambda i: (i, 0)),
        out_shape=jax.ShapeDtypeStruct((n_rows, d), F32),
        scratch_shapes=[pltpu.VMEM((2, TOP_K * t * ROW_SUBLANES, LANES), F32), pltpu.SemaphoreType.DMA((2,))],
        compiler_params=_cparams("arbitrary"),
        name="moe_combine",
    )(dest_rows, dest_rows, x, gates, mods, normf_g.reshape(1, d), ys)


def _moe_layer(x, norm_g, mods, w_r, b_r, layer, w_g, b_g, w_u, b_u, w_d, b_d, normf_g, seg, final_norm):
    n, d = x.shape
    nk = n * TOP_K
    h, top_e, gates, rank, counts = _router(x, norm_g, mods, w_r, b_r, seg)
    counts = counts[0]
    padded = (counts + MOE_TILE - 1) // MOE_TILE * MOE_TILE
    pad_end = jnp.cumsum(padded)
    pad_start = pad_end - padded
    cap = nk + N_EXPERTS * MOE_TILE
    nb = cap // MOE_TILE
    experts = jnp.arange(N_EXPERTS, dtype=jnp.int32)
    start_of = jnp.sum(jnp.where(top_e[..., None] == experts, pad_start, 0), axis=-1)
    dest_rows = ((start_of + rank) * ROW_SUBLANES).reshape(nk).astype(jnp.int32)
    blk_start = jnp.arange(nb, dtype=jnp.int32) * MOE_TILE
    block_e = jnp.minimum(jnp.sum(blk_start[:, None] >= pad_end[None, :], axis=1), N_EXPERTS - 1).astype(jnp.int32)
    n_active = (pad_end[-1] // MOE_TILE).astype(jnp.int32).reshape(1)
    xs = _dispatch(h, dest_rows, (pad_start + counts).astype(jnp.int32), (padded - counts).astype(jnp.int32),
                   n_active, cap)
    changes = jnp.concatenate([jnp.zeros((1,), jnp.int32), (block_e[1:] != block_e[:-1]).astype(jnp.int32)])
    weight_slot = (jnp.cumsum(changes) % 2).astype(jnp.int32)
    later = (experts[None, :] > experts[:, None]) & (counts[None, :] > 0)
    next_of = jnp.min(jnp.where(later, experts[None, :], N_EXPERTS), axis=1)
    next_expert = jnp.where(next_of < N_EXPERTS, next_of, -1).astype(jnp.int32)[block_e]
    ys = _expert_ffn(xs, block_e, n_active, weight_slot, next_expert, layer, w_g, b_g, w_u, b_u, w_d, b_d)
    if not final_norm:
        return _combine(x, ys, dest_rows, gates, mods, normf_g, seg, False)
    n_ctx = seg[0]
    return (_combine(x, ys, dest_rows, gates, mods, normf_g, seg, True, 0, n_ctx),
            _combine(x, ys, dest_rows, gates, mods, normf_g, seg, True, n_ctx, n - n_ctx))


HY_ORDER = 2
HY_BANDS = 16
HY_EMB = 1 + 2 * HY_BANDS
HY_FFN = 64
HY_MIN_DECAY = math.log(1e-2) / 1.5
HY_MAX_DECAY = math.log(1e-2) / 0.3
HY_EMB_PAD = 64


def _filter_mlp_kernel(z_ref, t_ref, w1_ref, b1_ref, fr_ref, w2_ref, b2_ref, w3_ref, dl_ref, o_ref, h_ref):
    @pl.when(pl.program_id(1) == 0)
    def _():
        fr = fr_ref[...]
        h = jnp.sin(fr * (_dot3(z_ref[...], w1_ref[...]) + b1_ref[...]))
        h_ref[...] = jnp.sin(fr * (_dot3(h, w2_ref[...]) + b2_ref[...]))

    o_ref[...] = _dot3(h_ref[...], w3_ref[...]) * jnp.exp(-t_ref[...] * dl_ref[...])


def _hyena_filters(length, w1, b1, freq, w2, b2, w3):
    t = jnp.linspace(0.0, 1.0, length, dtype=F32)[:, None]
    ang = (2.0 * math.pi / length) * jnp.arange(length, dtype=F32)[:, None]
    bands = jnp.linspace(1e-4, HY_BANDS - 1, HY_BANDS, dtype=F32)[None, :]
    z = jnp.concatenate([t, jnp.cos(bands * ang), -jnp.sin(bands * ang)], axis=-1)
    z = jnp.pad(z, ((0, 0), (0, HY_EMB_PAD - HY_EMB)))
    w1p = jnp.pad(w1, ((0, HY_EMB_PAD - HY_EMB), (0, 0)))
    n_out = w3.shape[1]
    deltas = jnp.abs(jnp.linspace(HY_MIN_DECAY, HY_MAX_DECAY, D_MODEL, dtype=F32))
    deltas = jnp.tile(deltas, n_out // D_MODEL)[None, :]
    tl, tn = 256, 1024
    row = lambda i, j: (i, 0)
    fixed = lambda i, j: (0, 0)
    return pl.pallas_call(
        _filter_mlp_kernel,
        grid=(length // tl, n_out // tn),
        in_specs=[
            pl.BlockSpec((tl, HY_EMB_PAD), row),
            pl.BlockSpec((tl, 1), row),
            pl.BlockSpec((HY_EMB_PAD, HY_FFN), fixed),
            pl.BlockSpec((1, HY_FFN), fixed),
            pl.BlockSpec((1, HY_FFN), fixed),
            pl.BlockSpec((HY_FFN, HY_FFN), fixed),
            pl.BlockSpec((1, HY_FFN), fixed),
            pl.BlockSpec((HY_FFN, tn), lambda i, j: (0, j)),
            pl.BlockSpec((1, tn), lambda i, j: (0, j)),
        ],
        out_specs=pl.BlockSpec((tl, tn), lambda i, j: (i, j)),
        out_shape=jax.ShapeDtypeStruct((length, n_out), F32),
        scratch_shapes=[pltpu.VMEM((tl, HY_FFN), F32)],
        compiler_params=_cparams("parallel", "arbitrary"),
        name="hyena_filter_mlp",
    )(z, t, w1p, b1.reshape(1, -1), freq.reshape(1, -1), w2, b2.reshape(1, -1), w3, deltas)


def _short_conv(z, w):
    length = z.shape[0]
    row = lax.broadcasted_iota(jnp.int32, z.shape, 0)
    prev = jnp.where(row == 0, 0.0, pltpu.roll(z, 1, 0))
    nxt = jnp.where(row == length - 1, 0.0, pltpu.roll(z, length - 1, 0))
    return (prev * w[0:1] + z * w[1:2]) + nxt * w[2:3]


def _filter_halves(hf_ref, hb_ref):
    hf = hf_ref[...]
    hb = hb_ref[...]
    hb = jnp.where(lax.broadcasted_iota(jnp.int32, hb.shape, 0) == 0, 0.0, hb)
    norm = jnp.sum(jnp.abs(hf), axis=0, keepdims=True) + jnp.sum(jnp.abs(hb), axis=0, keepdims=True)
    return hf + hb, hf - hb, 1.0 / norm


def _direct_dft_tables(length):
    n_fft = 2 * length
    n_freq = length + 1
    mf = -(-n_freq // 16) * 16
    k = np.arange(mf)[:, None]
    n = np.arange(length)[None, :]
    ang = 2.0 * np.pi * ((k * n) % n_fft) / n_fft
    valid = k < n_freq
    cos = np.where(valid, np.cos(ang), 0.0)
    msin = np.where(valid, -np.sin(ang), 0.0)
    weight = np.where((k == 0) | (k == length), 1.0, 2.0) * valid / n_fft
    fwd = np.concatenate([cos, msin], axis=0)
    inv = np.concatenate([weight * cos, weight * msin], axis=0).T
    return jnp.asarray(fwd, BF16), jnp.asarray(inv, BF16), mf


def _hyena_direct_kernel(zv_ref, z1_ref, z2_ref, wv_ref, w1_ref, w2_ref,
                         hf0_ref, hf1_ref, hb0_ref, hb1_ref, skip_ref, fw_ref, iv_ref,
                         o_ref, kr_ref, ki_ref):
    mf = fw_ref.shape[0] // 2
    dt = o_ref.shape[-1]

    @pl.when(pl.program_id(1) == 0)
    def _():
        for o, (hf_ref, hb_ref) in enumerate(((hf0_ref, hb0_ref), (hf1_ref, hb1_ref))):
            hs, hd, inv_norm = _filter_halves(hf_ref, hb_ref)
            spec = _dot(fw_ref[...], jnp.concatenate([hs, hd], axis=1).astype(BF16))
            kr_ref[o] = spec[:mf, :dt] * inv_norm
            ki_ref[o] = spec[mf:, dt:] * inv_norm

    y = _short_conv(zv_ref[0], wv_ref[...])
    for o, (z_ref, w_ref) in enumerate(((z1_ref, w1_ref), (z2_ref, w2_ref))):
        spec = _dot(fw_ref[...], y.astype(BF16))
        yr, yi = spec[:mf], spec[mf:]
        kr, ki = kr_ref[o], ki_ref[o]
        prod = jnp.concatenate([yr * kr - yi * ki, yr * ki + yi * kr], axis=0)
        yc = _dot(iv_ref[...], prod.astype(BF16))
        y = _short_conv(z_ref[0], w_ref[...]) * (yc + y * skip_ref[o:o + 1])
    o_ref[0] = y.astype(o_ref.dtype)


def _hyena_direct(z, w_short, hfilt, skip, first=0, bsz=None):
    _, length, d3 = z.shape
    bsz = z.shape[0] if bsz is None else bsz
    d = d3 // 3
    dt = 512
    nct = d // dt
    fwd, inv, mf = _direct_dft_tables(length)
    zspec = lambda part: pl.BlockSpec((1, length, dt), lambda c, b: (b + first, 0, part * nct + c))
    wspec = lambda part: pl.BlockSpec((3, dt), lambda c, b: (0, part * nct + c))
    hspec = lambda direction, order: pl.BlockSpec(
        (length, dt), lambda c, b: (0, (direction * HY_ORDER + order) * nct + c))
    fixed = lambda c, b: (0, 0)
    return pl.pallas_call(
        _hyena_direct_kernel,
        grid=(nct, bsz),
        in_specs=[zspec(0), zspec(1), zspec(2), wspec(0), wspec(1), wspec(2),
                  hspec(0, 0), hspec(0, 1), hspec(1, 0), hspec(1, 1),
                  pl.BlockSpec((HY_ORDER, dt), lambda c, b: (0, c)),
                  pl.BlockSpec(fwd.shape, fixed), pl.BlockSpec(inv.shape, fixed)],
        out_specs=pl.BlockSpec((1, length, dt), lambda c, b: (b, 0, c)),
        out_shape=jax.ShapeDtypeStruct((bsz, length, d), BF16),
        scratch_shapes=[pltpu.VMEM((HY_ORDER, mf, dt), F32), pltpu.VMEM((HY_ORDER, mf, dt), F32)],
        compiler_params=_cparams("parallel", "arbitrary"),
        name="hyena_conv_direct",
    )(z, z, z, w_short, w_short, w_short, hfilt, hfilt, hfilt, hfilt, skip, fwd, inv)


FFT_N1 = 64
FFT_N2 = 128
FFT_LANES = 128
FFT_A_PITCH = 2 * FFT_N2 + 8
FFT_U_PITCH = 2 * FFT_N1 + 8
FFT_GROUP = 2
FFT_UNROLL = 16


def _two_stage_tables():
    n_fft = FFT_N1 * FFT_N2
    half = FFT_N2 // 2
    n1 = np.arange(FFT_N1)[:, None, None]
    k2 = np.arange(FFT_N2)[None, :, None]
    n2 = np.arange(half)[None, None, :]
    ang = 2.0 * np.pi * ((k2 * (n1 + FFT_N1 * n2)) % n_fft) / n_fft
    stage_a = np.concatenate([np.cos(ang), -np.sin(ang)], axis=1)
    stage_a_inv = np.transpose(stage_a, (0, 2, 1)) / n_fft
    k1 = np.arange(FFT_N1)[:, None]
    m1 = np.arange(FFT_N1)[None, :]
    phi = 2.0 * np.pi * ((k1 * m1) % FFT_N1) / FFT_N1
    c, s = np.cos(phi), np.sin(phi)
    stage_b = np.block([[c, s], [-s, c]])
    stage_b_inv = np.block([[c, -s], [s, c]])
    return tuple(jnp.asarray(t, BF16) for t in (stage_a, stage_a_inv, stage_b, stage_b_inv))


def _fft_stage_a(y_ref, ma_ref, a_ref):
    half = FFT_N2 // 2

    def body(n1, carry):
        slab = y_ref[pl.ds(n1, half, stride=FFT_N1), :]
        a_ref[pl.ds(pl.multiple_of(n1 * FFT_A_PITCH, 8), 2 * FFT_N2), :] = _dot(ma_ref[n1], slab.astype(BF16))
        return carry

    lax.fori_loop(0, FFT_N1, body, 0, unroll=FFT_UNROLL)


def _fft_stage_b(a_ref, mb_ref, consume):
    def body(j, carry):
        k2 = j * FFT_GROUP
        cols = []
        for g in range(FFT_GROUP):
            re = a_ref[pl.ds(k2 + g, FFT_N1, stride=FFT_A_PITCH), :]
            im = a_ref[pl.ds(FFT_N2 + k2 + g, FFT_N1, stride=FFT_A_PITCH), :]
            cols.append(jnp.concatenate([re, im], axis=0))
        x = _dot(mb_ref[...], jnp.concatenate(cols, axis=1).astype(BF16))
        for g in range(FFT_GROUP):
            consume(k2 + g, x[:, g * FFT_LANES:(g + 1) * FFT_LANES])
        return carry

    lax.fori_loop(0, FFT_N2 // FFT_GROUP, body, 0, unroll=FFT_UNROLL)


def _fft_inverse(z_ref, mbi_ref, mai_ref, u_ref, out_ref):
    half = FFT_N2 // 2

    def stage_b(j, carry):
        k2 = j * FFT_GROUP
        rhs = jnp.concatenate([z_ref[k2 + g] for g in range(FFT_GROUP)], axis=1)
        u = _dot(mbi_ref[...], rhs)
        for g in range(FFT_GROUP):
            u_ref[pl.ds(pl.multiple_of((k2 + g) * FFT_U_PITCH, 8), 2 * FFT_N1), :] = (
                u[:, g * FFT_LANES:(g + 1) * FFT_LANES])
        return carry

    lax.fori_loop(0, FFT_N2 // FFT_GROUP, stage_b, 0, unroll=FFT_UNROLL)

    def stage_a(n1, carry):
        re = u_ref[pl.ds(n1, FFT_N2, stride=FFT_U_PITCH), :]
        im = u_ref[pl.ds(FFT_N1 + n1, FFT_N2, stride=FFT_U_PITCH), :]
        rhs = jnp.concatenate([re, im], axis=0).astype(BF16)
        out_ref[pl.ds(n1, half, stride=FFT_N1), :] = _dot(mai_ref[n1], rhs)
        return carry

    lax.fori_loop(0, FFT_N1, stage_a, 0, unroll=FFT_UNROLL)


FFT_WORK_ROWS = max(FFT_N1 * FFT_A_PITCH, FFT_N2 * FFT_U_PITCH)


def _hyena_spectrum_kernel(hf0_ref, hf1_ref, hb0_ref, hb1_ref, ma_ref, mb_ref, k_ref, y_ref, a_ref):
    for o, (hf_ref, hb_ref) in enumerate(((hf0_ref, hb0_ref), (hf1_ref, hb1_ref))):
        hs, hd, inv_norm = _filter_halves(hf_ref, hb_ref)

        def keep_real(k2, x):
            k_ref[o, k2, 0:FFT_N1, :] = (x[:FFT_N1] * inv_norm).astype(k_ref.dtype)

        def keep_imag(k2, x):
            k_ref[o, k2, FFT_N1:, :] = (x[FFT_N1:] * inv_norm).astype(k_ref.dtype)

        for part, keep in ((hs, keep_real), (hd, keep_imag)):
            y_ref[...] = part
            _fft_stage_a(y_ref, ma_ref, a_ref)
            _fft_stage_b(a_ref, mb_ref, keep)


def _hyena_two_stage_kernel(zv_ref, z1_ref, z2_ref, wv_ref, w1_ref, w2_ref, k_ref, skip_ref,
                            ma_ref, mai_ref, mb_ref, mbi_ref, o_ref, y_ref, c_ref, a_ref, z_ref):
    y_ref[...] = _short_conv(zv_ref[0], wv_ref[...])
    for o, (g_ref, w_ref) in enumerate(((z1_ref, w1_ref), (z2_ref, w2_ref))):
        def multiply(k2, x):
            k = k_ref[o, k2].astype(F32)
            xr, xi = x[:FFT_N1], x[FFT_N1:]
            kr, ki = k[:FFT_N1], k[FFT_N1:]
            z_ref[k2] = jnp.concatenate([xr * kr - xi * ki, xr * ki + xi * kr], axis=0).astype(z_ref.dtype)

        _fft_stage_a(y_ref, ma_ref, a_ref)
        _fft_stage_b(a_ref, mb_ref, multiply)
        _fft_inverse(z_ref, mbi_ref, mai_ref, a_ref, c_ref)
        y = _short_conv(g_ref[0], w_ref[...]) * (c_ref[...] + y_ref[...] * skip_ref[o:o + 1])
        if o + 1 < HY_ORDER:
            y_ref[...] = y
        else:
            o_ref[0] = y.astype(o_ref.dtype)


def _hyena_two_stage(z, w_short, hfilt, skip, first=0, bsz=None):
    _, length, d3 = z.shape
    bsz = z.shape[0] if bsz is None else bsz
    d = d3 // 3
    dt = FFT_LANES
    nct = d // dt
    assert 2 * length == FFT_N1 * FFT_N2
    ma, mai, mb, mbi = _two_stage_tables()
    once = pl.Buffered(1)
    hspec = lambda direction, order: pl.BlockSpec(
        (length, dt), lambda c: (0, (direction * HY_ORDER + order) * nct + c))
    spectrum = pl.pallas_call(
        _hyena_spectrum_kernel,
        grid=(nct,),
        in_specs=[hspec(0, 0), hspec(0, 1), hspec(1, 0), hspec(1, 1),
                  pl.BlockSpec(ma.shape, lambda c: (0, 0, 0), pipeline_mode=once),
                  pl.BlockSpec(mb.shape, lambda c: (0, 0), pipeline_mode=once)],
        out_specs=pl.BlockSpec((HY_ORDER, FFT_N2, 2 * FFT_N1, dt), lambda c: (0, 0, 0, c)),
        out_shape=jax.ShapeDtypeStruct((HY_ORDER, FFT_N2, 2 * FFT_N1, d), BF16),
        scratch_shapes=[pltpu.VMEM((length, dt), F32), pltpu.VMEM((FFT_WORK_ROWS, dt), F32)],
        compiler_params=_cparams("parallel"),
        name="hyena_filter_spectrum",
    )(hfilt, hfilt, hfilt, hfilt, ma, mb)

    zspec = lambda part: pl.BlockSpec((1, length, dt), lambda c, b: (b + first, 0, part * nct + c))
    wspec = lambda part: pl.BlockSpec((3, dt), lambda c, b: (0, part * nct + c))
    fixed3 = lambda c, b: (0, 0, 0)
    fixed2 = lambda c, b: (0, 0)
    return pl.pallas_call(
        _hyena_two_stage_kernel,
        grid=(nct, bsz),
        in_specs=[zspec(0), zspec(1), zspec(2), wspec(0), wspec(1), wspec(2),
                  pl.BlockSpec((HY_ORDER, FFT_N2, 2 * FFT_N1, dt), lambda c, b: (0, 0, 0, c), pipeline_mode=once),
                  pl.BlockSpec((HY_ORDER, dt), lambda c, b: (0, c)),
                  pl.BlockSpec(ma.shape, fixed3, pipeline_mode=once),
                  pl.BlockSpec(mai.shape, fixed3, pipeline_mode=once),
                  pl.BlockSpec(mb.shape, fixed2, pipeline_mode=once),
                  pl.BlockSpec(mbi.shape, fixed2, pipeline_mode=once)],
        out_specs=pl.BlockSpec((1, length, dt), lambda c, b: (b, 0, c)),
        out_shape=jax.ShapeDtypeStruct((bsz, length, d), BF16),
        scratch_shapes=[pltpu.VMEM((length, dt), F32), pltpu.VMEM((length, dt), F32),
                        pltpu.VMEM((FFT_WORK_ROWS, dt), F32), pltpu.VMEM((FFT_N2, 2 * FFT_N1, dt), BF16)],
        compiler_params=_cparams("parallel", "arbitrary"),
        name="hyena_conv_two_stage",
    )(z, z, z, w_short, w_short, w_short, spectrum, skip, ma, mai, mb, mbi)


N_HEADS = 16
HEAD_DIM = D_MODEL // N_HEADS
HEADS_PER_STEP = 2
CTX_HEADS_PER_STEP = 8
GRID_W = 64
WIN_ROWS = 8
WIN_COLS = 16
NEG_INF = -1e30
NAT_Q_ROWS = 4
NAT_K_ROWS = NAT_Q_ROWS + WIN_ROWS - 1
ATTN_SCALE = HEAD_DIM ** -0.5
assert math.frexp(ATTN_SCALE)[0] == 0.5, "the scale is folded into q, which is exact only for a power of two"


def _dot_nt(a, b):
    return lax.dot_general(a, b, (((1,), (1,)), ((), ())), preferred_element_type=F32)


def _qkv_kernel(x_ref, g_ref, sc_ref, sh_ref, w_ref, q_ref, k_ref, v_ref):
    h = _norm_mod(x_ref[...], g_ref[...], sc_ref[0], sh_ref[0])
    qkv = _dot(h.astype(BF16), w_ref[...])
    for part, ref in enumerate((q_ref, k_ref, v_ref)):
        scale = ATTN_SCALE if part == 0 else 1.0
        for head in range(N_HEADS):
            lo = part * D_MODEL + head * HEAD_DIM
            ref[0, head] = (qkv[:, lo:lo + HEAD_DIM] * scale).astype(ref.dtype)


def _qkv_proj(x, g, mods, w_bf16, seg, first_row, n_seq, seq_len, kv_dtype):
    d = x.shape[1]
    n_ctx, smp_len = seg
    t = ROW_TILE
    first_tile = first_row // t
    per_seq = seq_len // t
    out_spec = pl.BlockSpec((1, N_HEADS, t, HEAD_DIM), lambda i: (i // per_seq, 0, i % per_seq, 0))
    shape = (n_seq, N_HEADS, seq_len, HEAD_DIM)
    return pl.pallas_call(
        _qkv_kernel,
        grid=(n_seq * per_seq,),
        in_specs=[
            pl.BlockSpec((t, d), lambda i: (i + first_tile, 0)),
            pl.BlockSpec((1, d), lambda i: (0, 0)),
            _mod_spec(1, t, n_ctx, smp_len, first_tile),
            _mod_spec(0, t, n_ctx, smp_len, first_tile),
            pl.BlockSpec(w_bf16.shape, lambda i: (0, 0)),
        ],
        out_specs=[out_spec, out_spec, out_spec],
        out_shape=[jax.ShapeDtypeStruct(shape, BF16), jax.ShapeDtypeStruct(shape, kv_dtype),
                   jax.ShapeDtypeStruct(shape, kv_dtype)],
        compiler_params=_cparams("parallel"),
        name="qkv_proj",
    )(x, g.reshape(1, d), mods, mods, w_bf16)


def _ctx_attn_kernel(q_ref, k_ref, v_ref, o_ref):
    for j in range(q_ref.shape[1]):
        q = q_ref[0, j]
        k = k_ref[0, j].astype(BF16)
        v = v_ref[0, j].astype(BF16)
        s = _dot_nt(q, k)
        p = jnp.exp(s - jnp.max(s, axis=-1, keepdims=True))
        o = _dot(p.astype(BF16), v) / jnp.sum(p, axis=-1, keepdims=True)
        o_ref[:, j * HEAD_DIM:(j + 1) * HEAD_DIM] = o.astype(o_ref.dtype)


def _ctx_attention(q, k, v):
    bsz, _, s, _ = q.shape
    hp = CTX_HEADS_PER_STEP
    spec = pl.BlockSpec((1, hp, s, HEAD_DIM), lambda b, h: (b, h, 0, 0))
    return pl.pallas_call(
        _ctx_attn_kernel,
        grid=(bsz, N_HEADS // hp),
        in_specs=[spec, spec, spec],
        out_specs=pl.BlockSpec((s, hp * HEAD_DIM), lambda b, h: (b, h)),
        out_shape=jax.ShapeDtypeStruct((bsz * s, D_MODEL), BF16),
        compiler_params=_cparams("parallel", "parallel"),
        name="ctx_attention",
    )(q, k, v)


def _rpb_toeplitz_kernel(r_ref, e_ref, o_ref):
    o_ref[...] = _dot3(r_ref[...], e_ref[...])


def _nat_bias(rpb):
    n_heads, n_r, n_c = rpb.shape
    n_cp = 32
    qc = np.arange(GRID_W)[:, None]
    kc = np.arange(GRID_W)[None, :]
    onehot = (np.clip(kc - qc + WIN_COLS - 1, 0, n_c - 1)[None] == np.arange(n_cp)[:, None, None])
    onehot = jnp.asarray(onehot.reshape(n_cp, GRID_W * GRID_W), F32)
    rows = jnp.pad(rpb.reshape(n_heads * n_r, n_c), ((0, 0), (0, n_cp - n_c)))
    toep = pl.pallas_call(
        _rpb_toeplitz_kernel,
        out_shape=jax.ShapeDtypeStruct((n_heads * n_r, GRID_W * GRID_W), F32),
        compiler_params=_cparams(),
        name="nat_bias_toeplitz",
    )(rows, onehot).reshape(n_heads, n_r, GRID_W, GRID_W)
    q_start = np.clip(qc - WIN_COLS // 2, 0, GRID_W - WIN_COLS)
    col_ok = jnp.asarray((kc >= q_start) & (kc < q_start + WIN_COLS))
    toep = jnp.where(col_ok, toep, NEG_INF)
    toep = jnp.concatenate([toep, jnp.full((n_heads, 1, GRID_W, GRID_W), NEG_INF, F32)], axis=1)
    rows_total = GRID_W
    idx = np.full((3, NAT_Q_ROWS, NAT_K_ROWS), n_r, np.int32)
    for case, r0 in enumerate((0, NAT_Q_ROWS, rows_total - NAT_Q_ROWS)):
        ks = int(np.clip(r0 - WIN_ROWS // 2, 0, rows_total - NAT_K_ROWS))
        for dr in range(NAT_Q_ROWS):
            r = r0 + dr
            rs = int(np.clip(r - WIN_ROWS // 2, 0, rows_total - WIN_ROWS))
            for dk in range(NAT_K_ROWS):
                kr = ks + dk
                if rs <= kr < rs + WIN_ROWS:
                    idx[case, dr, dk] = kr - r + WIN_ROWS - 1
    def assemble(t_ref, o_ref):
        for case in range(3):
            for dr in range(NAT_Q_ROWS):
                for dk in range(NAT_K_ROWS):
                    o_ref[0, case, dr * GRID_W:(dr + 1) * GRID_W, dk * GRID_W:(dk + 1) * GRID_W] = (
                        t_ref[0, int(idx[case, dr, dk])])

    return pl.pallas_call(
        assemble,
        grid=(n_heads,),
        in_specs=[pl.BlockSpec((1, n_r + 1, GRID_W, GRID_W), lambda h: (h, 0, 0, 0))],
        out_specs=pl.BlockSpec((1, 3, NAT_Q_ROWS * GRID_W, NAT_K_ROWS * GRID_W), lambda h: (h, 0, 0, 0)),
        out_shape=jax.ShapeDtypeStruct((n_heads, 3, NAT_Q_ROWS * GRID_W, NAT_K_ROWS * GRID_W), F32),
        compiler_params=_cparams("parallel"),
        name="nat_bias_assemble",
    )(toep)


def _nat_kernel(q_ref, k_ref, v_ref, kc_ref, vc_ref, bias_ref, o_ref):
    n_blocks = q_ref.shape[2] // (NAT_Q_ROWS * GRID_W)
    rows_total = q_ref.shape[2] // GRID_W
    nq = NAT_Q_ROWS * GRID_W
    nk = NAT_K_ROWS * GRID_W
    ctx = [(kc_ref[0, 0, j].astype(BF16), vc_ref[0, 0, j].astype(BF16)) for j in range(HEADS_PER_STEP)]

    def block(blk, carry):
        ks = jnp.clip(blk * NAT_Q_ROWS - WIN_ROWS // 2, 0, rows_total - NAT_K_ROWS)
        case = jnp.where(blk == 0, 0, jnp.where(blk == n_blocks - 1, 2, 1))
        q_rows = pl.ds(pl.multiple_of(blk * nq, nq), nq)
        k_rows = pl.ds(pl.multiple_of(ks * GRID_W, GRID_W), nk)
        for j, (k_ctx, v_ctx) in enumerate(ctx):
            q = q_ref[0, j, q_rows, :]
            s_loc = _dot_nt(q, k_ref[0, j, k_rows, :]) + bias_ref[j, case]
            s_ctx = _dot_nt(q, k_ctx)
            m = jnp.maximum(jnp.max(s_loc, axis=-1, keepdims=True), jnp.max(s_ctx, axis=-1, keepdims=True))
            p_loc = jnp.exp(s_loc - m)
            p_ctx = jnp.exp(s_ctx - m)
            denom = jnp.sum(p_loc, axis=-1, keepdims=True) + jnp.sum(p_ctx, axis=-1, keepdims=True)
            o = (_dot(p_loc.astype(BF16), v_ref[0, j, k_rows, :]) + _dot(p_ctx.astype(BF16), v_ctx)) / denom
            o_ref[q_rows, j * HEAD_DIM:(j + 1) * HEAD_DIM] = o.astype(o_ref.dtype)
        return carry

    lax.fori_loop(0, n_blocks, block, 0, unroll=2)


def _nat_attention(q, k, v, cache_k, cache_v, bias):
    bsz, _, length, _ = q.shape
    hp = HEADS_PER_STEP
    past = cache_k.shape[3]
    spec = pl.BlockSpec((1, hp, length, HEAD_DIM), lambda b, h: (b, h, 0, 0))
    cspec = pl.BlockSpec((1, 1, hp, past, HEAD_DIM), lambda b, h: (b, 0, h, 0, 0))
    return pl.pallas_call(
        _nat_kernel,
        grid=(bsz, N_HEADS // hp),
        in_specs=[spec, spec, spec, cspec, cspec,
                  pl.BlockSpec((hp,) + bias.shape[1:], lambda b, h: (h, 0, 0, 0))],
        out_specs=pl.BlockSpec((length, hp * HEAD_DIM), lambda b, h: (b, h)),
        out_shape=jax.ShapeDtypeStruct((bsz * length, D_MODEL), BF16),
        compiler_params=_cparams("parallel", "parallel"),
        name="nat_attention",
    )(q, k, v, cache_k, cache_v, bias)


def kernel(x_prompt, x_sample, cache_k, cache_v, c, c_ctx, ada_w, ada_b, norm1_g, norm2_g, normf_g, hy_w_in, hy_w_short, hy_f_w1, hy_f_b1, hy_f_freq, hy_f_w2, hy_f_b2, hy_f_w3, hy_skip, hy_w_out, na_w_qkv, na_rpb, na_w_o, moe_w_router, moe_b_router, moe_w_gate, moe_b_gate, moe_w_up, moe_b_up, moe_w_down, moe_b_down):
    b, s, d = x_prompt.shape
    bd, sd, _ = x_sample.shape
    n_ctx = b * s
    n_tok = n_ctx + bd * sd
    assert n_ctx % sd == 0 and s % ROW_TILE == 0 and sd % ROW_TILE == 0
    seg = (n_ctx, sd)
    x_ctx = x_prompt.reshape(n_ctx, d)
    x_smp = x_sample.reshape(bd * sd, d)
    cond = jnp.zeros((N_COND, d), F32).at[0].set(c_ctx).at[1:1 + bd].set(c)
    mods = _modulation(cond, ada_w, ada_b)

    def moe(x, i, final_norm):
        return _moe_layer(x, norm2_g[i], mods[i], moe_w_router[i], moe_b_router[i], i, moe_w_gate, moe_b_gate,
                          moe_w_up, moe_b_up, moe_w_down, moe_b_down, normf_g, seg, final_norm)

    z = _norm_proj(x_ctx, x_smp, norm1_g[0], mods[0], hy_w_in[0].astype(BF16), seg)
    fargs = (hy_f_w1[0], hy_f_b1[0], hy_f_freq[0], hy_f_w2[0], hy_f_b2[0], hy_f_w3[0])
    y_ctx = _hyena_direct(z.reshape(n_tok // s, s, 3 * d), hy_w_short[0], _hyena_filters(s, *fargs),
                          hy_skip[0], first=0, bsz=b)
    y_smp = _hyena_two_stage(z.reshape(n_tok // sd, sd, 3 * d), hy_w_short[0], _hyena_filters(sd, *fargs),
                             hy_skip[0], first=n_ctx // sd, bsz=bd)
    x = _out_proj(y_ctx.reshape(n_ctx, d), y_smp.reshape(bd * sd, d), hy_w_out[0].astype(BF16), x_ctx, x_smp,
                  mods[0], seg)
    x = moe(x, 0, False)

    w_qkv = na_w_qkv[0].astype(BF16)
    q_c, k_c, v_c = _qkv_proj(x, norm1_g[1], mods[1], w_qkv, seg, 0, b, s, F32)
    q_s, k_s, v_s = _qkv_proj(x, norm1_g[1], mods[1], w_qkv, seg, n_ctx, bd, sd, BF16)
    o_ctx = _ctx_attention(q_c, k_c, v_c)
    o_smp = _nat_attention(q_s, k_s, v_s, cache_k, cache_v, _nat_bias(na_rpb[0]))
    x = _out_proj(o_ctx, o_smp, na_w_o[0].astype(BF16), x, None, mods[1], seg)
    y_prompt, y_sample = moe(x, 1, True)

    nh, hd = k_c.shape[1], k_c.shape[3]
    return (y_prompt.reshape(b, s, d), y_sample.reshape(bd, sd, d),
            k_c.reshape(b, 1, nh, s, hd), v_c.reshape(b, 1, nh, s, hd))
```

```python
import functools
import math

import numpy as np
import jax
import jax.numpy as jnp
from jax import lax
from jax.experimental import pallas as pl
from jax.experimental.pallas import tpu as pltpu

F32 = jnp.float32
BF16 = jnp.bfloat16

D_MODEL = 1024
N_MOD = 6
RMS_EPS = 1e-6
N_EXPERTS = 32
TOP_K = 4
SWIGLU_LIMIT = 7.0
SWIGLU_ALPHA = 1.702

N_COND = 8
ROW_TILE = 512
QKV_TILE = 256
MOE_TILE = 512
V7X_VMEM_LIMIT = 56 * 1024 * 1024


def _cparams(*sem, vmem=V7X_VMEM_LIMIT):
    return pltpu.CompilerParams(dimension_semantics=sem, vmem_limit_bytes=vmem)


def _dot(a, b):
    return jnp.dot(a, b, preferred_element_type=F32)


def _split_bf16(x):
    hi = x.astype(BF16)
    lo = (x - hi.astype(F32)).astype(BF16)
    return hi, lo


def _dot3(a, b):
    ah, al = _split_bf16(a)
    bh, bl = _split_bf16(b)
    return _dot(ah, bh) + (_dot(al, bh) + _dot(ah, bl))


def _seg_of_tile(i, tile, n_ctx, smp_len):
    ctx_tiles = n_ctx // tile
    per_smp = smp_len // tile
    return jnp.where(i < ctx_tiles, 0, 1 + (i - ctx_tiles) // per_smp)


def _norm_mod(x, g, sc, sh):
    y = x * lax.rsqrt(jnp.mean(x * x, axis=-1, keepdims=True) + RMS_EPS)
    return (y * g) * (1.0 + sc) + sh


def _mod_kernel(c_ref, w_ref, b_ref, o_ref):
    c = c_ref[...]
    a = c * jax.nn.sigmoid(c)
    o_ref[0] = _dot3(a, w_ref[0]) + b_ref[0]


def _modulation(cond, ada_w, ada_b):
    depth, d, n_out = ada_w.shape
    tn = 1536
    m = pl.pallas_call(
        _mod_kernel,
        grid=(depth, n_out // tn),
        in_specs=[
            pl.BlockSpec((N_COND, d), lambda l, j: (0, 0)),
            pl.BlockSpec((1, d, tn), lambda l, j: (l, 0, j)),
            pl.BlockSpec((1, 1, tn), lambda l, j: (l, 0, j)),
        ],
        out_specs=pl.BlockSpec((1, N_COND, tn), lambda l, j: (l, 0, j)),
        out_shape=jax.ShapeDtypeStruct((depth, N_COND, n_out), F32),
        compiler_params=_cparams("arbitrary", "arbitrary"),
        name="adaln_modulation",
    )(cond, ada_w, ada_b.reshape(depth, 1, n_out))
    m = m.reshape(depth, N_COND, N_MOD, d)
    return jnp.transpose(m, (0, 2, 1, 3)).reshape(depth, N_MOD * N_COND, 1, d)


def _mod_spec(which, tile, n_ctx, smp_len, first_tile=0):
    return pl.BlockSpec(
        (1, 1, D_MODEL),
        lambda i, *_: (which * N_COND + _seg_of_tile(i + first_tile, tile, n_ctx, smp_len), 0, 0))


def _group_specs(block_cols, ctx_tiles, stacked=False):
    base = ctx_tiles if stacked else 0
    return (pl.BlockSpec((ROW_TILE, block_cols), lambda i: (jnp.minimum(i, ctx_tiles - 1), 0)),
            pl.BlockSpec((ROW_TILE, block_cols), lambda i: (jnp.maximum(i - ctx_tiles, 0) + base, 0)))


def _pick_group(a_ref, b_ref, ctx_tiles):
    return jnp.where(pl.program_id(0) < ctx_tiles, a_ref[...], b_ref[...])


def _norm_proj_kernel(xa_ref, xb_ref, g_ref, sc_ref, sh_ref, w_ref, o_ref, *, ctx_tiles):
    h = _norm_mod(_pick_group(xa_ref, xb_ref, ctx_tiles), g_ref[...], sc_ref[0], sh_ref[0])
    o_ref[...] = _dot(h.astype(BF16), w_ref[...]).astype(o_ref.dtype)


def _norm_proj(x_ctx, x_smp, g, mods, w_bf16, seg, out_dtype=F32):
    d = x_ctx.shape[1]
    n = x_ctx.shape[0] + x_smp.shape[0]
    n_out = w_bf16.shape[1]
    n_ctx, smp_len = seg
    ctx_tiles = n_ctx // ROW_TILE
    return pl.pallas_call(
        functools.partial(_norm_proj_kernel, ctx_tiles=ctx_tiles),
        grid=(n // ROW_TILE,),
        in_specs=[
            *_group_specs(d, ctx_tiles),
            pl.BlockSpec((1, d), lambda i: (0, 0)),
            _mod_spec(1, ROW_TILE, n_ctx, smp_len),
            _mod_spec(0, ROW_TILE, n_ctx, smp_len),
            pl.BlockSpec((d, n_out), lambda i: (0, 0)),
        ],
        out_specs=pl.BlockSpec((ROW_TILE, n_out), lambda i: (i, 0)),
        out_shape=jax.ShapeDtypeStruct((n, n_out), out_dtype),
        compiler_params=_cparams("parallel"),
        name="norm_proj",
    )(x_ctx, x_smp, g.reshape(1, d), mods, mods, w_bf16)


def _out_proj_kernel(ya_ref, yb_ref, w_ref, xa_ref, xb_ref, gate_ref, o_ref, *, ctx_tiles):
    y = _pick_group(ya_ref, yb_ref, ctx_tiles)
    o_ref[...] = _pick_group(xa_ref, xb_ref, ctx_tiles) + gate_ref[0] * _dot(y, w_ref[...])


def _out_proj(y_ctx, y_smp, w_bf16, x_ctx, x_smp, mods, seg):
    d = x_ctx.shape[1]
    n_ctx, smp_len = seg
    n = n_ctx + y_smp.shape[0]
    ctx_tiles = n_ctx // ROW_TILE
    stacked = x_smp is None
    x_smp = x_ctx if stacked else x_smp
    return pl.pallas_call(
        functools.partial(_out_proj_kernel, ctx_tiles=ctx_tiles),
        grid=(n // ROW_TILE,),
        in_specs=[
            *_group_specs(d, ctx_tiles),
            pl.BlockSpec((d, d), lambda i: (0, 0)),
            *_group_specs(d, ctx_tiles, stacked),
            _mod_spec(2, ROW_TILE, n_ctx, smp_len),
        ],
        out_specs=pl.BlockSpec((ROW_TILE, d), lambda i: (i, 0)),
        out_shape=jax.ShapeDtypeStruct((n, d), F32),
        compiler_params=_cparams("parallel"),
        name="out_proj",
    )(y_ctx, y_smp, w_bf16, x_ctx, x_smp, mods)


LANES = 128
ROW_SUBLANES = D_MODEL // LANES


def _store_row_tiles(ref, x):
    t = x.shape[0]
    for s in range(ROW_SUBLANES):
        ref[pl.ds(s, t, stride=ROW_SUBLANES), :] = x[:, s * LANES:(s + 1) * LANES]


def _load_row_tiles(ref, t):
    return jnp.concatenate([ref[pl.ds(s, t, stride=ROW_SUBLANES), :] for s in range(ROW_SUBLANES)], axis=1)


def _columns(cols):
    t = cols[0].shape[0]
    lane = lax.broadcasted_iota(jnp.int32, (t, len(cols)), 1)
    out = jnp.broadcast_to(cols[-1], (t, len(cols)))
    for k in range(len(cols) - 2, -1, -1):
        out = jnp.where(lane == k, cols[k], out)
    return out


def _router_kernel(x_ref, g_ref, sc_ref, sh_ref, wr_ref, br_ref, tri_ref,
                   h_ref, e_ref, gate_ref, rank_ref, cnt_ref, run_ref):
    i = pl.program_id(0)

    @pl.when(i == 0)
    def _():
        run_ref[...] = jnp.zeros_like(run_ref)

    h = _norm_mod(x_ref[...], g_ref[...], sc_ref[0], sh_ref[0])
    _store_row_tiles(h_ref, h)
    logits = _dot3(h, wr_ref[...]) + br_ref[...]
    lane = lax.broadcasted_iota(jnp.int32, logits.shape, 1).astype(F32)
    work = logits
    vals, idxs, hots = [], [], []
    for _ in range(TOP_K):
        m = jnp.max(work, axis=-1, keepdims=True)
        idx = jnp.min(jnp.where(work == m, lane, float(N_EXPERTS)), axis=-1, keepdims=True)
        hot = lane == idx
        vals.append(m)
        idxs.append(idx)
        hots.append(hot)
        work = jnp.where(hot, -jnp.inf, work)
    ex = [jnp.exp(v - vals[0]) for v in vals]
    denom = ex[0] + ex[1] + ex[2] + ex[3]
    gate_ref[...] = _columns([e / denom for e in ex])
    e_ref[...] = _columns(idxs).astype(jnp.int32)

    chosen = (hots[0] | hots[1] | hots[2] | hots[3]).astype(F32)
    before = run_ref[...] + _dot(tri_ref[...], chosen.astype(BF16))
    ranks = [jnp.sum(jnp.where(hot, before, 0.0), axis=-1, keepdims=True) for hot in hots]
    rank_ref[...] = _columns(ranks).astype(jnp.int32)
    run_ref[...] += jnp.sum(chosen, axis=0, keepdims=True)
    cnt_ref[...] = run_ref[...].astype(jnp.int32)


def _router(x, g, mods, w_r, b_r, seg):
    n, d = x.shape
    n_ctx, smp_len = seg
    t = ROW_TILE
    tri = jnp.asarray(np.tril(np.ones((t, t), np.float32), -1), BF16)
    tok4 = lambda i: (i, 0)
    return pl.pallas_call(
        _router_kernel,
        grid=(n // t,),
        in_specs=[
            pl.BlockSpec((t, d), lambda i: (i, 0)),
            pl.BlockSpec((1, d), lambda i: (0, 0)),
            _mod_spec(4, t, n_ctx, smp_len),
            _mod_spec(3, t, n_ctx, smp_len),
            pl.BlockSpec((d, N_EXPERTS), lambda i: (0, 0)),
            pl.BlockSpec((1, N_EXPERTS), lambda i: (0, 0)),
            pl.BlockSpec((t, t), lambda i: (0, 0)),
        ],
        out_specs=[
            pl.BlockSpec((t * ROW_SUBLANES, LANES), tok4),
            pl.BlockSpec((t, TOP_K), tok4),
            pl.BlockSpec((t, TOP_K), tok4),
            pl.BlockSpec((t, TOP_K), tok4),
            pl.BlockSpec((1, N_EXPERTS), lambda i: (0, 0)),
        ],
        out_shape=[
            jax.ShapeDtypeStruct((n * ROW_SUBLANES, LANES), F32),
            jax.ShapeDtypeStruct((n, TOP_K), jnp.int32),
            jax.ShapeDtypeStruct((n, TOP_K), F32),
            jax.ShapeDtypeStruct((n, TOP_K), jnp.int32),
            jax.ShapeDtypeStruct((1, N_EXPERTS), jnp.int32),
        ],
        scratch_shapes=[pltpu.VMEM((1, N_EXPERTS), F32)],
        compiler_params=_cparams("arbitrary"),
        name="moe_router",
    )(x, g.reshape(1, d), mods, mods, w_r, b_r.reshape(1, N_EXPERTS), tri)


def _tile_copy(src_ref, src_row, dst_ref, dst_row, sem):
    rows = lambda r: pl.ds(pl.multiple_of(r, ROW_SUBLANES), ROW_SUBLANES)
    return pltpu.make_async_copy(src_ref.at[rows(src_row), :], dst_ref.at[rows(dst_row), :], sem)


def _tiles_wait(src_ref, dst_ref, n_tiles, sem):
    rows = pl.ds(0, n_tiles * ROW_SUBLANES)
    pltpu.make_async_copy(src_ref.at[rows, :], dst_ref.at[rows, :], sem).wait()


ENTRIES_PER_TILE = ROW_TILE * TOP_K
DMA_PRIORITIES = 2


def _dispatch_kernel(fill_start_ref, fill_len_ref, na_ref, dest_ref, h_ref, xs_ref, hbuf, zero_ref, sem):
    i = pl.program_id(0)
    last = pl.num_programs(0) - 1
    slot = lax.rem(i, 2)
    hbuf[slot] = h_ref[...]

    def issue(t, carry):
        for k in range(TOP_K):
            _tile_copy(hbuf.at[slot], t * ROW_SUBLANES, xs_ref, dest_ref[t * TOP_K + k],
                       sem.at[slot]).start(priority=k % DMA_PRIORITIES)
        return carry

    lax.fori_loop(0, ROW_TILE, issue, 0, unroll=4)

    @pl.when(i >= 1)
    def _():
        _tiles_wait(xs_ref, xs_ref, ENTRIES_PER_TILE, sem.at[1 - slot])

    @pl.when(i == last)
    def _():
        _tiles_wait(xs_ref, xs_ref, ENTRIES_PER_TILE, sem.at[slot])
        zero_ref[...] = jnp.zeros_like(zero_ref)

        def pieces(act):
            for e in range(N_EXPERTS):
                count = fill_len_ref[e]
                done = jnp.int32(0)
                for bit in reversed(range(MOE_TILE.bit_length() - 1)):
                    size = 1 << bit
                    rows = size * ROW_SUBLANES
                    first = pl.multiple_of((fill_start_ref[e] + done) * ROW_SUBLANES, ROW_SUBLANES)
                    copy = pltpu.make_async_copy(zero_ref.at[pl.ds(0, rows), :], xs_ref.at[pl.ds(first, rows), :],
                                                 sem.at[0])
                    pl.when((count & size) != 0)(functools.partial(act, copy))
                    done = done + (count & size)

        pieces(lambda copy: copy.start())
        pieces(lambda copy: copy.wait())

        block_rows = MOE_TILE * ROW_SUBLANES

        def block_copy(blk):
            return pltpu.make_async_copy(
                zero_ref, xs_ref.at[pl.ds(pl.multiple_of(blk * block_rows, block_rows), block_rows), :], sem.at[0])

        n_blocks = xs_ref.shape[0] // block_rows

        def tail(blk, carry):
            block_copy(blk).start()
            return carry

        lax.fori_loop(na_ref[0], n_blocks, tail, 0)

        def tail_drain(blk, carry):
            block_copy(0).wait()
            return carry

        lax.fori_loop(na_ref[0], n_blocks, tail_drain, 0)


def _dispatch(h_tiles, dest_rows, fill_start, fill_len, n_active, cap):
    n = h_tiles.shape[0] // ROW_SUBLANES
    grid_spec = pltpu.PrefetchScalarGridSpec(
        num_scalar_prefetch=3,
        grid=(n // ROW_TILE,),
        in_specs=[
            pl.BlockSpec((ENTRIES_PER_TILE,), lambda i, *_: (i,), memory_space=pltpu.SMEM),
            pl.BlockSpec((ROW_TILE * ROW_SUBLANES, LANES), lambda i, *_: (i, 0)),
        ],
        out_specs=pl.BlockSpec(memory_space=pl.ANY),
        scratch_shapes=[pltpu.VMEM((2, ROW_TILE * ROW_SUBLANES, LANES), F32),
                        pltpu.VMEM((MOE_TILE * ROW_SUBLANES, LANES), F32), pltpu.SemaphoreType.DMA((2,))],
    )
    return pl.pallas_call(
        _dispatch_kernel,
        grid_spec=grid_spec,
        out_shape=jax.ShapeDtypeStruct((cap * ROW_SUBLANES, LANES), F32),
        compiler_params=_cparams("arbitrary"),
        name="moe_dispatch",
    )(fill_start, fill_len, n_active, dest_rows, h_tiles)


def _ffn_kernel(be_ref, na_ref, wslot_ref, next_e_ref, xs_ref, wg_ref, bg_ref, wu_ref, bu_ref, wd_ref, bd_ref,
                ys_ref, wbuf, wg_s, wu_s, wd_s, wsem, *, layer):
    i = pl.program_id(0)
    n_active = na_ref[0]
    tm = xs_ref.shape[0] // ROW_SUBLANES
    changed = jnp.logical_or(i == 0, be_ref[i] != be_ref[jnp.maximum(i - 1, 0)])
    fresh = jnp.logical_and(i < n_active, changed)

    def weight_copies(expert, b):
        return [pltpu.make_async_copy(w_ref.at[layer, expert], wbuf.at[b, m], wsem.at[b])
                for m, w_ref in enumerate((wg_ref, wu_ref, wd_ref))]

    @pl.when(jnp.logical_and(i == 0, n_active > 0))
    def _():
        for copy in weight_copies(be_ref[0], wslot_ref[0]):
            copy.start()

    @pl.when(fresh)
    def _():
        b = wslot_ref[i]
        for copy in weight_copies(be_ref[i], b):
            copy.wait()
        wg_s[...] = wbuf[b, 0].astype(BF16)
        wu_s[...] = wbuf[b, 1].astype(BF16)
        wd_s[...] = wbuf[b, 2].astype(BF16)
        upcoming = next_e_ref[i]

        @pl.when(upcoming >= 0)
        def _():
            for copy in weight_copies(upcoming, 1 - b):
                copy.start()

    @pl.when(i < n_active)
    def _():
        x = _load_row_tiles(xs_ref, tm).astype(BF16)
        g = jnp.minimum(_dot(x, wg_s[...]) + bg_ref[0, 0], SWIGLU_LIMIT)
        u = jnp.clip(_dot(x, wu_s[...]) + bu_ref[0, 0], -SWIGLU_LIMIT, SWIGLU_LIMIT)
        a = g * jax.nn.sigmoid(SWIGLU_ALPHA * g) * (u + 1.0)
        _store_row_tiles(ys_ref, _dot(a.astype(BF16), wd_s[...]) + bd_ref[0, 0])

    @pl.when(i >= na_ref[0])
    def _():
        ys_ref[...] = jnp.zeros_like(ys_ref)


def _expert_ffn(xs, block_e, n_active, weight_slot, next_expert, layer, w_g, b_g, w_u, b_u, w_d, b_d):
    d = D_MODEL
    depth, ne, _, f = w_g.shape
    assert f == d
    nb = block_e.shape[0]
    block = (MOE_TILE * ROW_SUBLANES, LANES)
    bmap = lambda i, be, *_: (layer, be[i], 0, 0)
    hbm = pl.BlockSpec(memory_space=pl.ANY)
    grid_spec = pltpu.PrefetchScalarGridSpec(
        num_scalar_prefetch=4,
        grid=(nb,),
        in_specs=[
            pl.BlockSpec(block, lambda i, be, na, *_: (jnp.minimum(i, na[0] - 1), 0)),
            hbm, pl.BlockSpec((1, 1, 1, f), bmap),
            hbm, pl.BlockSpec((1, 1, 1, f), bmap),
            hbm, pl.BlockSpec((1, 1, 1, d), bmap),
        ],
        out_specs=pl.BlockSpec(block, lambda i, *_: (i, 0)),
        scratch_shapes=[pltpu.VMEM((2, 3, d, f), F32),
                        pltpu.VMEM((d, f), BF16), pltpu.VMEM((d, f), BF16), pltpu.VMEM((f, d), BF16),
                        pltpu.SemaphoreType.DMA((2,))],
    )
    return pl.pallas_call(
        functools.partial(_ffn_kernel, layer=layer),
        grid_spec=grid_spec,
        out_shape=jax.ShapeDtypeStruct(xs.shape, F32),
        compiler_params=_cparams("arbitrary"),
        name="moe_expert_ffn",
    )(block_e, n_active, weight_slot, next_expert, xs,
      w_g, b_g.reshape(depth, ne, 1, f), w_u, b_u.reshape(depth, ne, 1, f), w_d, b_d.reshape(depth, ne, 1, d))


def _combine_kernel(dest_ref, dest_next_ref, x_ref, gate_ref, g2_ref, gf_ref, ys_ref, o_ref, buf, sem, *,
                    final_norm):
    i = pl.program_id(0)
    slot = lax.rem(i, 2)
    t = x_ref.shape[0]
    plane = t * ROW_SUBLANES

    def start_gather(idx_ref, b):
        def body(r, carry):
            for k in range(TOP_K):
                _tile_copy(ys_ref, idx_ref[r * TOP_K + k], buf.at[b], k * plane + r * ROW_SUBLANES,
                           sem.at[b]).start(priority=k % DMA_PRIORITIES)
            return carry

        lax.fori_loop(0, t, body, 0, unroll=4)

    @pl.when(i == 0)
    def _():
        start_gather(dest_ref, 0)

    @pl.when(i + 1 < pl.num_programs(0))
    def _():
        start_gather(dest_next_ref, 1 - slot)

    _tiles_wait(ys_ref, buf.at[slot], TOP_K * t, sem.at[slot])
    gates = gate_ref[...]
    acc = None
    for k in range(TOP_K):
        term = gates[:, k:k + 1] * _load_row_tiles(buf.at[slot, pl.ds(k * plane, plane), :], t)
        acc = term if acc is None else acc + term
    y = x_ref[...] + g2_ref[0] * acc
    if final_norm:
        y = y * lax.rsqrt(jnp.mean(y * y, axis=-1, keepdims=True) + RMS_EPS) * gf_ref[...]
    o_ref[...] = y


def _combine(x, ys, dest_rows, gates, mods, normf_g, seg, final_norm, first_row=0, n_rows=None):
    n, d = x.shape
    n_rows = n if n_rows is None else n_rows
    n_ctx, smp_len = seg
    t = ROW_TILE
    first = first_row // t
    steps = n_rows // t
    rows = lambda i: (i + first, 0)
    return pl.pallas_call(
        functools.partial(_combine_kernel, final_norm=final_norm),
        grid=(steps,),
        in_specs=[
            pl.BlockSpec((ENTRIES_PER_TILE,), lambda i: (i + first,), memory_space=pltpu.SMEM),
            pl.BlockSpec((ENTRIES_PER_TILE,), lambda i: (jnp.minimum(i + 1, steps - 1) + first,),
                         memory_space=pltpu.SMEM),
            pl.BlockSpec((t, d), rows),
            pl.BlockSpec((t, TOP_K), rows),
            _mod_spec(5, t, n_ctx, smp_len, first),
            pl.BlockSpec((1, d), lambda i: (0, 0)),
            pl.BlockSpec(memory_space=pl.ANY),
        ],
        out_specs=pl.BlockSpec((t, d), lambda i: (i, 0)),
        out_shape=jax.ShapeDtypeStruct((n_rows, d), F32),
        scratch_shapes=[pltpu.VMEM((2, TOP_K * t * ROW_SUBLANES, LANES), F32), pltpu.SemaphoreType.DMA((2,))],
        compiler_params=_cparams("arbitrary"),
        name="moe_combine",
    )(dest_rows, dest_rows, x, gates, mods, normf_g.reshape(1, d), ys)


def _moe_layer(x, norm_g, mods, w_r, b_r, layer, w_g, b_g, w_u, b_u, w_d, b_d, normf_g, seg, final_norm):
    n, d = x.shape
    nk = n * TOP_K
    h, top_e, gates, rank, counts = _router(x, norm_g, mods, w_r, b_r, seg)
    counts = counts[0]
    padded = (counts + MOE_TILE - 1) // MOE_TILE * MOE_TILE
    pad_end = jnp.cumsum(padded)
    pad_start = pad_end - padded
    cap = nk + N_EXPERTS * MOE_TILE
    nb = cap // MOE_TILE
    experts = jnp.arange(N_EXPERTS, dtype=jnp.int32)
    start_of = jnp.sum(jnp.where(top_e[..., None] == experts, pad_start, 0), axis=-1)
    dest_rows = ((start_of + rank) * ROW_SUBLANES).reshape(nk).astype(jnp.int32)
    blk_start = jnp.arange(nb, dtype=jnp.int32) * MOE_TILE
    block_e = jnp.minimum(jnp.sum(blk_start[:, None] >= pad_end[None, :], axis=1), N_EXPERTS - 1).astype(jnp.int32)
    n_active = (pad_end[-1] // MOE_TILE).astype(jnp.int32).reshape(1)
    xs = _dispatch(h, dest_rows, (pad_start + counts).astype(jnp.int32), (padded - counts).astype(jnp.int32),
                   n_active, cap)
    changes = jnp.concatenate([jnp.zeros((1,), jnp.int32), (block_e[1:] != block_e[:-1]).astype(jnp.int32)])
    weight_slot = (jnp.cumsum(changes) % 2).astype(jnp.int32)
    later = (experts[None, :] > experts[:, None]) & (counts[None, :] > 0)
    next_of = jnp.min(jnp.where(later, experts[None, :], N_EXPERTS), axis=1)
    next_expert = jnp.where(next_of < N_EXPERTS, next_of, -1).astype(jnp.int32)[block_e]
    ys = _expert_ffn(xs, block_e, n_active, weight_slot, next_expert, layer, w_g, b_g, w_u, b_u, w_d, b_d)
    if not final_norm:
        return _combine(x, ys, dest_rows, gates, mods, normf_g, seg, False)
    n_ctx = seg[0]
    return (_combine(x, ys, dest_rows, gates, mods, normf_g, seg, True, 0, n_ctx),
            _combine(x, ys, dest_rows, gates, mods, normf_g, seg, True, n_ctx, n - n_ctx))


HY_ORDER = 2
HY_BANDS = 16
HY_EMB = 1 + 2 * HY_BANDS
HY_FFN = 64
HY_MIN_DECAY = math.log(1e-2) / 1.5
HY_MAX_DECAY = math.log(1e-2) / 0.3
HY_EMB_PAD = 64


def _filter_mlp_kernel(z_ref, t_ref, w1_ref, b1_ref, fr_ref, w2_ref, b2_ref, w3_ref, dl_ref, o_ref, h_ref):
    @pl.when(pl.program_id(1) == 0)
    def _():
        fr = fr_ref[...]
        h = jnp.sin(fr * (_dot3(z_ref[...], w1_ref[...]) + b1_ref[...]))
        h_ref[...] = jnp.sin(fr * (_dot3(h, w2_ref[...]) + b2_ref[...]))

    o_ref[...] = _dot3(h_ref[...], w3_ref[...]) * jnp.exp(-t_ref[...] * dl_ref[...])


def _hyena_filters(length, w1, b1, freq, w2, b2, w3):
    t = jnp.linspace(0.0, 1.0, length, dtype=F32)[:, None]
    ang = (2.0 * math.pi / length) * jnp.arange(length, dtype=F32)[:, None]
    bands = jnp.linspace(1e-4, HY_BANDS - 1, HY_BANDS, dtype=F32)[None, :]
    z = jnp.concatenate([t, jnp.cos(bands * ang), -jnp.sin(bands * ang)], axis=-1)
    z = jnp.pad(z, ((0, 0), (0, HY_EMB_PAD - HY_EMB)))
    w1p = jnp.pad(w1, ((0, HY_EMB_PAD - HY_EMB), (0, 0)))
    n_out = w3.shape[1]
    deltas = jnp.abs(jnp.linspace(HY_MIN_DECAY, HY_MAX_DECAY, D_MODEL, dtype=F32))
    deltas = jnp.tile(deltas, n_out // D_MODEL)[None, :]
    tl, tn = 256, 1024
    row = lambda i, j: (i, 0)
    fixed = lambda i, j: (0, 0)
    return pl.pallas_call(
        _filter_mlp_kernel,
        grid=(length // tl, n_out // tn),
        in_specs=[
            pl.BlockSpec((tl, HY_EMB_PAD), row),
            pl.BlockSpec((tl, 1), row),
            pl.BlockSpec((HY_EMB_PAD, HY_FFN), fixed),
            pl.BlockSpec((1, HY_FFN), fixed),
            pl.BlockSpec((1, HY_FFN), fixed),
            pl.BlockSpec((HY_FFN, HY_FFN), fixed),
            pl.BlockSpec((1, HY_FFN), fixed),
            pl.BlockSpec((HY_FFN, tn), lambda i, j: (0, j)),
            pl.BlockSpec((1, tn), lambda i, j: (0, j)),
        ],
        out_specs=pl.BlockSpec((tl, tn), lambda i, j: (i, j)),
        out_shape=jax.ShapeDtypeStruct((length, n_out), F32),
        scratch_shapes=[pltpu.VMEM((tl, HY_FFN), F32)],
        compiler_params=_cparams("parallel", "arbitrary"),
        name="hyena_filter_mlp",
    )(z, t, w1p, b1.reshape(1, -1), freq.reshape(1, -1), w2, b2.reshape(1, -1), w3, deltas)


def _short_conv(z, w):
    length = z.shape[0]
    row = lax.broadcasted_iota(jnp.int32, z.shape, 0)
    prev = jnp.where(row == 0, 0.0, pltpu.roll(z, 1, 0))
    nxt = jnp.where(row == length - 1, 0.0, pltpu.roll(z, length - 1, 0))
    return (prev * w[0:1] + z * w[1:2]) + nxt * w[2:3]


def _filter_halves(hf_ref, hb_ref):
    hf = hf_ref[...]
    hb = hb_ref[...]
    hb = jnp.where(lax.broadcasted_iota(jnp.int32, hb.shape, 0) == 0, 0.0, hb)
    norm = jnp.sum(jnp.abs(hf), axis=0, keepdims=True) + jnp.sum(jnp.abs(hb), axis=0, keepdims=True)
    return hf + hb, hf - hb, 1.0 / norm


def _direct_dft_tables(length):
    n_fft = 2 * length
    n_freq = length + 1
    mf = -(-n_freq // 16) * 16
    k = np.arange(mf)[:, None]
    n = np.arange(length)[None, :]
    ang = 2.0 * np.pi * ((k * n) % n_fft) / n_fft
    valid = k < n_freq
    cos = np.where(valid, np.cos(ang), 0.0)
    msin = np.where(valid, -np.sin(ang), 0.0)
    weight = np.where((k == 0) | (k == length), 1.0, 2.0) * valid / n_fft
    fwd = np.concatenate([cos, msin], axis=0)
    inv = np.concatenate([weight * cos, weight * msin], axis=0).T
    return jnp.asarray(fwd, BF16), jnp.asarray(inv, BF16), mf


def _hyena_direct_kernel(zv_ref, z1_ref, z2_ref, wv_ref, w1_ref, w2_ref,
                         hf0_ref, hf1_ref, hb0_ref, hb1_ref, skip_ref, fw_ref, iv_ref,
                         o_ref, kr_ref, ki_ref):
    mf = fw_ref.shape[0] // 2
    dt = o_ref.shape[-1]

    @pl.when(pl.program_id(1) == 0)
    def _():
        for o, (hf_ref, hb_ref) in enumerate(((hf0_ref, hb0_ref), (hf1_ref, hb1_ref))):
            hs, hd, inv_norm = _filter_halves(hf_ref, hb_ref)
            spec = _dot(fw_ref[...], jnp.concatenate([hs, hd], axis=1).astype(BF16))
            kr_ref[o] = spec[:mf, :dt] * inv_norm
            ki_ref[o] = spec[mf:, dt:] * inv_norm

    y = _short_conv(zv_ref[0], wv_ref[...])
    for o, (z_ref, w_ref) in enumerate(((z1_ref, w1_ref), (z2_ref, w2_ref))):
        spec = _dot(fw_ref[...], y.astype(BF16))
        yr, yi = spec[:mf], spec[mf:]
        kr, ki = kr_ref[o], ki_ref[o]
        prod = jnp.concatenate([yr * kr - yi * ki, yr * ki + yi * kr], axis=0)
        yc = _dot(iv_ref[...], prod.astype(BF16))
        y = _short_conv(z_ref[0], w_ref[...]) * (yc + y * skip_ref[o:o + 1])
    o_ref[0] = y.astype(o_ref.dtype)


def _hyena_direct(z, w_short, hfilt, skip, first=0, bsz=None):
    _, length, d3 = z.shape
    bsz = z.shape[0] if bsz is None else bsz
    d = d3 // 3
    dt = 512
    nct = d // dt
    fwd, inv, mf = _direct_dft_tables(length)
    zspec = lambda part: pl.BlockSpec((1, length, dt), lambda c, b: (b + first, 0, part * nct + c))
    wspec = lambda part: pl.BlockSpec((3, dt), lambda c, b: (0, part * nct + c))
    hspec = lambda direction, order: pl.BlockSpec(
        (length, dt), lambda c, b: (0, (direction * HY_ORDER + order) * nct + c))
    fixed = lambda c, b: (0, 0)
    return pl.pallas_call(
        _hyena_direct_kernel,
        grid=(nct, bsz),
        in_specs=[zspec(0), zspec(1), zspec(2), wspec(0), wspec(1), wspec(2),
                  hspec(0, 0), hspec(0, 1), hspec(1, 0), hspec(1, 1),
                  pl.BlockSpec((HY_ORDER, dt), lambda c, b: (0, c)),
                  pl.BlockSpec(fwd.shape, fixed), pl.BlockSpec(inv.shape, fixed)],
        out_specs=pl.BlockSpec((1, length, dt), lambda c, b: (b, 0, c)),
        out_shape=jax.ShapeDtypeStruct((bsz, length, d), BF16),
        scratch_shapes=[pltpu.VMEM((HY_ORDER, mf, dt), F32), pltpu.VMEM((HY_ORDER, mf, dt), F32)],
        compiler_params=_cparams("parallel", "arbitrary"),
        name="hyena_conv_direct",
    )(z, z, z, w_short, w_short, w_short, hfilt, hfilt, hfilt, hfilt, skip, fwd, inv)


FFT_N1 = 64
FFT_N2 = 128
FFT_LANES = 128
FFT_A_PITCH = 2 * FFT_N2 + 8
FFT_U_PITCH = 2 * FFT_N1 + 8
FFT_GROUP = 2
FFT_UNROLL = 16


def _two_stage_tables():
    n_fft = FFT_N1 * FFT_N2
    half = FFT_N2 // 2
    n1 = np.arange(FFT_N1)[:, None, None]
    k2 = np.arange(FFT_N2)[None, :, None]
    n2 = np.arange(half)[None, None, :]
    ang = 2.0 * np.pi * ((k2 * (n1 + FFT_N1 * n2)) % n_fft) / n_fft
    stage_a = np.concatenate([np.cos(ang), -np.sin(ang)], axis=1)
    stage_a_inv = np.transpose(stage_a, (0, 2, 1)) / n_fft
    k1 = np.arange(FFT_N1)[:, None]
    m1 = np.arange(FFT_N1)[None, :]
    phi = 2.0 * np.pi * ((k1 * m1) % FFT_N1) / FFT_N1
    c, s = np.cos(phi), np.sin(phi)
    stage_b = np.block([[c, s], [-s, c]])
    stage_b_inv = np.block([[c, -s], [s, c]])
    return tuple(jnp.asarray(t, BF16) for t in (stage_a, stage_a_inv, stage_b, stage_b_inv))


def _fft_stage_a(y_ref, ma_ref, a_ref):
    half = FFT_N2 // 2

    def body(n1, carry):
        slab = y_ref[pl.ds(n1, half, stride=FFT_N1), :]
        a_ref[pl.ds(pl.multiple_of(n1 * FFT_A_PITCH, 8), 2 * FFT_N2), :] = _dot(ma_ref[n1], slab.astype(BF16))
        return carry

    lax.fori_loop(0, FFT_N1, body, 0, unroll=FFT_UNROLL)


def _fft_stage_b(a_ref, mb_ref, consume):
    def body(j, carry):
        k2 = j * FFT_GROUP
        cols = []
        for g in range(FFT_GROUP):
            re = a_ref[pl.ds(k2 + g, FFT_N1, stride=FFT_A_PITCH), :]
            im = a_ref[pl.ds(FFT_N2 + k2 + g, FFT_N1, stride=FFT_A_PITCH), :]
            cols.append(jnp.concatenate([re, im], axis=0))
        x = _dot(mb_ref[...], jnp.concatenate(cols, axis=1).astype(BF16))
        for g in range(FFT_GROUP):
            consume(k2 + g, x[:, g * FFT_LANES:(g + 1) * FFT_LANES])
        return carry

    lax.fori_loop(0, FFT_N2 // FFT_GROUP, body, 0, unroll=FFT_UNROLL)


def _fft_inverse(z_ref, mbi_ref, mai_ref, u_ref, out_ref):
    half = FFT_N2 // 2

    def stage_b(j, carry):
        k2 = j * FFT_GROUP
        rhs = jnp.concatenate([z_ref[k2 + g] for g in range(FFT_GROUP)], axis=1)
        u = _dot(mbi_ref[...], rhs)
        for g in range(FFT_GROUP):
            u_ref[pl.ds(pl.multiple_of((k2 + g) * FFT_U_PITCH, 8), 2 * FFT_N1), :] = (
                u[:, g * FFT_LANES:(g + 1) * FFT_LANES])
        return carry

    lax.fori_loop(0, FFT_N2 // FFT_GROUP, stage_b, 0, unroll=FFT_UNROLL)

    def stage_a(n1, carry):
        re = u_ref[pl.ds(n1, FFT_N2, stride=FFT_U_PITCH), :]
        im = u_ref[pl.ds(FFT_N1 + n1, FFT_N2, stride=FFT_U_PITCH), :]
        rhs = jnp.concatenate([re, im], axis=0).astype(BF16)
        out_ref[pl.ds(n1, half, stride=FFT_N1), :] = _dot(mai_ref[n1], rhs)
        return carry

    lax.fori_loop(0, FFT_N1, stage_a, 0, unroll=FFT_UNROLL)


FFT_WORK_ROWS = max(FFT_N1 * FFT_A_PITCH, FFT_N2 * FFT_U_PITCH)


def _hyena_spectrum_kernel(hf0_ref, hf1_ref, hb0_ref, hb1_ref, ma_ref, mb_ref, k_ref, y_ref, a_ref):
    for o, (hf_ref, hb_ref) in enumerate(((hf0_ref, hb0_ref), (hf1_ref, hb1_ref))):
        hs, hd, inv_norm = _filter_halves(hf_ref, hb_ref)

        def keep_real(k2, x):
            k_ref[o, k2, 0:FFT_N1, :] = (x[:FFT_N1] * inv_norm).astype(k_ref.dtype)

        def keep_imag(k2, x):
            k_ref[o, k2, FFT_N1:, :] = (x[FFT_N1:] * inv_norm).astype(k_ref.dtype)

        for part, keep in ((hs, keep_real), (hd, keep_imag)):
            y_ref[...] = part
            _fft_stage_a(y_ref, ma_ref, a_ref)
            _fft_stage_b(a_ref, mb_ref, keep)


def _hyena_two_stage_kernel(zv_ref, z1_ref, z2_ref, wv_ref, w1_ref, w2_ref, k_ref, skip_ref,
                            ma_ref, mai_ref, mb_ref, mbi_ref, o_ref, y_ref, c_ref, a_ref, z_ref):
    y_ref[...] = _short_conv(zv_ref[0], wv_ref[...])
    for o, (g_ref, w_ref) in enumerate(((z1_ref, w1_ref), (z2_ref, w2_ref))):
        def multiply(k2, x):
            k = k_ref[o, k2].astype(F32)
            xr, xi = x[:FFT_N1], x[FFT_N1:]
            kr, ki = k[:FFT_N1], k[FFT_N1:]
            z_ref[k2] = jnp.concatenate([xr * kr - xi * ki, xr * ki + xi * kr], axis=0).astype(z_ref.dtype)

        _fft_stage_a(y_ref, ma_ref, a_ref)
        _fft_stage_b(a_ref, mb_ref, multiply)
        _fft_inverse(z_ref, mbi_ref, mai_ref, a_ref, c_ref)
        y = _short_conv(g_ref[0], w_ref[...]) * (c_ref[...] + y_ref[...] * skip_ref[o:o + 1])
        if o + 1 < HY_ORDER:
            y_ref[...] = y
        else:
            o_ref[0] = y.astype(o_ref.dtype)


def _hyena_two_stage(z, w_short, hfilt, skip, first=0, bsz=None):
    _, length, d3 = z.shape
    bsz = z.shape[0] if bsz is None else bsz
    d = d3 // 3
    dt = FFT_LANES
    nct = d // dt
    assert 2 * length == FFT_N1 * FFT_N2
    ma, mai, mb, mbi = _two_stage_tables()
    once = pl.Buffered(1)
    hspec = lambda direction, order: pl.BlockSpec(
        (length, dt), lambda c: (0, (direction * HY_ORDER + order) * nct + c))
    spectrum = pl.pallas_call(
        _hyena_spectrum_kernel,
        grid=(nct,),
        in_specs=[hspec(0, 0), hspec(0, 1), hspec(1, 0), hspec(1, 1),
                  pl.BlockSpec(ma.shape, lambda c: (0, 0, 0), pipeline_mode=once),
                  pl.BlockSpec(mb.shape, lambda c: (0, 0), pipeline_mode=once)],
        out_specs=pl.BlockSpec((HY_ORDER, FFT_N2, 2 * FFT_N1, dt), lambda c: (0, 0, 0, c)),
        out_shape=jax.ShapeDtypeStruct((HY_ORDER, FFT_N2, 2 * FFT_N1, d), BF16),
        scratch_shapes=[pltpu.VMEM((length, dt), F32), pltpu.VMEM((FFT_WORK_ROWS, dt), F32)],
        compiler_params=_cparams("parallel"),
        name="hyena_filter_spectrum",
    )(hfilt, hfilt, hfilt, hfilt, ma, mb)

    zspec = lambda part: pl.BlockSpec((1, length, dt), lambda c, b: (b + first, 0, part * nct + c))
    wspec = lambda part: pl.BlockSpec((3, dt), lambda c, b: (0, part * nct + c))
    fixed3 = lambda c, b: (0, 0, 0)
    fixed2 = lambda c, b: (0, 0)
    return pl.pallas_call(
        _hyena_two_stage_kernel,
        grid=(nct, bsz),
        in_specs=[zspec(0), zspec(1), zspec(2), wspec(0), wspec(1), wspec(2),
                  pl.BlockSpec((HY_ORDER, FFT_N2, 2 * FFT_N1, dt), lambda c, b: (0, 0, 0, c), pipeline_mode=once),
                  pl.BlockSpec((HY_ORDER, dt), lambda c, b: (0, c)),
                  pl.BlockSpec(ma.shape, fixed3, pipeline_mode=once),
                  pl.BlockSpec(mai.shape, fixed3, pipeline_mode=once),
                  pl.BlockSpec(mb.shape, fixed2, pipeline_mode=once),
                  pl.BlockSpec(mbi.shape, fixed2, pipeline_mode=once)],
        out_specs=pl.BlockSpec((1, length, dt), lambda c, b: (b, 0, c)),
        out_shape=jax.ShapeDtypeStruct((bsz, length, d), BF16),
        scratch_shapes=[pltpu.VMEM((length, dt), F32), pltpu.VMEM((length, dt), F32),
                        pltpu.VMEM((FFT_WORK_ROWS, dt), F32), pltpu.VMEM((FFT_N2, 2 * FFT_N1, dt), BF16)],
        compiler_params=_cparams("parallel", "arbitrary"),
        name="hyena_conv_two_stage",
    )(z, z, z, w_short, w_short, w_short, spectrum, skip, ma, mai, mb, mbi)


N_HEADS = 16
HEAD_DIM = D_MODEL // N_HEADS
HEADS_PER_STEP = 2
CTX_HEADS_PER_STEP = 8
GRID_W = 64
WIN_ROWS = 8
WIN_COLS = 16
NEG_INF = -1e30
NAT_Q_ROWS = 4
NAT_K_ROWS = NAT_Q_ROWS + WIN_ROWS - 1
ATTN_SCALE = HEAD_DIM ** -0.5
assert math.frexp(ATTN_SCALE)[0] == 0.5, "the scale is folded into q, which is exact only for a power of two"


def _dot_nt(a, b):
    return lax.dot_general(a, b, (((1,), (1,)), ((), ())), preferred_element_type=F32)


def _qkv_kernel(x_ref, g_ref, sc_ref, sh_ref, w_ref, q_ref, k_ref, v_ref):
    h = _norm_mod(x_ref[...], g_ref[...], sc_ref[0], sh_ref[0])
    qkv = _dot(h.astype(BF16), w_ref[...])
    for part, ref in enumerate((q_ref, k_ref, v_ref)):
        scale = ATTN_SCALE if part == 0 else 1.0
        for head in range(N_HEADS):
            lo = part * D_MODEL + head * HEAD_DIM
            ref[0, head] = (qkv[:, lo:lo + HEAD_DIM] * scale).astype(ref.dtype)


def _qkv_proj(x, g, mods, w_bf16, seg, first_row, n_seq, seq_len, kv_dtype):
    d = x.shape[1]
    n_ctx, smp_len = seg
    t = QKV_TILE
    first_tile = first_row // t
    per_seq = seq_len // t
    out_spec = pl.BlockSpec((1, N_HEADS, t, HEAD_DIM), lambda i: (i // per_seq, 0, i % per_seq, 0))
    shape = (n_seq, N_HEADS, seq_len, HEAD_DIM)
    return pl.pallas_call(
        _qkv_kernel,
        grid=(n_seq * per_seq,),
        in_specs=[
            pl.BlockSpec((t, d), lambda i: (i + first_tile, 0)),
            pl.BlockSpec((1, d), lambda i: (0, 0)),
            _mod_spec(1, t, n_ctx, smp_len, first_tile),
            _mod_spec(0, t, n_ctx, smp_len, first_tile),
            pl.BlockSpec(w_bf16.shape, lambda i: (0, 0)),
        ],
        out_specs=[out_spec, out_spec, out_spec],
        out_shape=[jax.ShapeDtypeStruct(shape, BF16), jax.ShapeDtypeStruct(shape, kv_dtype),
                   jax.ShapeDtypeStruct(shape, kv_dtype)],
        compiler_params=_cparams("parallel"),
        name="qkv_proj",
    )(x, g.reshape(1, d), mods, mods, w_bf16)


def _ctx_attn_kernel(q_ref, k_ref, v_ref, o_ref):
    for j in range(q_ref.shape[1]):
        q = q_ref[0, j]
        k = k_ref[0, j].astype(BF16)
        v = v_ref[0, j].astype(BF16)
        s = _dot_nt(q, k)
        p = jnp.exp(s - jnp.max(s, axis=-1, keepdims=True))
        o = _dot(p.astype(BF16), v) / jnp.sum(p, axis=-1, keepdims=True)
        o_ref[:, j * HEAD_DIM:(j + 1) * HEAD_DIM] = o.astype(o_ref.dtype)


def _ctx_attention(q, k, v):
    bsz, _, s, _ = q.shape
    hp = CTX_HEADS_PER_STEP
    spec = pl.BlockSpec((1, hp, s, HEAD_DIM), lambda b, h: (b, h, 0, 0))
    return pl.pallas_call(
        _ctx_attn_kernel,
        grid=(bsz, N_HEADS // hp),
        in_specs=[spec, spec, spec],
        out_specs=pl.BlockSpec((s, hp * HEAD_DIM), lambda b, h: (b, h)),
        out_shape=jax.ShapeDtypeStruct((bsz * s, D_MODEL), BF16),
        compiler_params=_cparams("parallel", "parallel"),
        name="ctx_attention",
    )(q, k, v)


def _rpb_toeplitz_kernel(r_ref, e_ref, o_ref):
    o_ref[...] = _dot3(r_ref[...], e_ref[...])


def _nat_bias(rpb):
    n_heads, n_r, n_c = rpb.shape
    n_cp = 32
    qc = np.arange(GRID_W)[:, None]
    kc = np.arange(GRID_W)[None, :]
    onehot = (np.clip(kc - qc + WIN_COLS - 1, 0, n_c - 1)[None] == np.arange(n_cp)[:, None, None])
    onehot = jnp.asarray(onehot.reshape(n_cp, GRID_W * GRID_W), F32)
    rows = jnp.pad(rpb.reshape(n_heads * n_r, n_c), ((0, 0), (0, n_cp - n_c)))
    toep = pl.pallas_call(
        _rpb_toeplitz_kernel,
        out_shape=jax.ShapeDtypeStruct((n_heads * n_r, GRID_W * GRID_W), F32),
        compiler_params=_cparams(),
        name="nat_bias_toeplitz",
    )(rows, onehot).reshape(n_heads, n_r, GRID_W, GRID_W)
    q_start = np.clip(qc - WIN_COLS // 2, 0, GRID_W - WIN_COLS)
    col_ok = jnp.asarray((kc >= q_start) & (kc < q_start + WIN_COLS))
    toep = jnp.where(col_ok, toep, NEG_INF)
    toep = jnp.concatenate([toep, jnp.full((n_heads, 1, GRID_W, GRID_W), NEG_INF, F32)], axis=1)
    rows_total = GRID_W
    idx = np.full((3, NAT_Q_ROWS, NAT_K_ROWS), n_r, np.int32)
    for case, r0 in enumerate((0, NAT_Q_ROWS, rows_total - NAT_Q_ROWS)):
        ks = int(np.clip(r0 - WIN_ROWS // 2, 0, rows_total - NAT_K_ROWS))
        for dr in range(NAT_Q_ROWS):
            r = r0 + dr
            rs = int(np.clip(r - WIN_ROWS // 2, 0, rows_total - WIN_ROWS))
            for dk in range(NAT_K_ROWS):
                kr = ks + dk
                if rs <= kr < rs + WIN_ROWS:
                    idx[case, dr, dk] = kr - r + WIN_ROWS - 1
    def assemble(t_ref, o_ref):
        for case in range(3):
            for dr in range(NAT_Q_ROWS):
                for dk in range(NAT_K_ROWS):
                    o_ref[0, case, dr * GRID_W:(dr + 1) * GRID_W, dk * GRID_W:(dk + 1) * GRID_W] = (
                        t_ref[0, int(idx[case, dr, dk])])

    return pl.pallas_call(
        assemble,
        grid=(n_heads,),
        in_specs=[pl.BlockSpec((1, n_r + 1, GRID_W, GRID_W), lambda h: (h, 0, 0, 0))],
        out_specs=pl.BlockSpec((1, 3, NAT_Q_ROWS * GRID_W, NAT_K_ROWS * GRID_W), lambda h: (h, 0, 0, 0)),
        out_shape=jax.ShapeDtypeStruct((n_heads, 3, NAT_Q_ROWS * GRID_W, NAT_K_ROWS * GRID_W), F32),
        compiler_params=_cparams("parallel"),
        name="nat_bias_assemble",
    )(toep)


def _nat_kernel(q_ref, k_ref, v_ref, kc_ref, vc_ref, bias_ref, o_ref):
    n_blocks = q_ref.shape[2] // (NAT_Q_ROWS * GRID_W)
    rows_total = q_ref.shape[2] // GRID_W
    nq = NAT_Q_ROWS * GRID_W
    nk = NAT_K_ROWS * GRID_W
    ctx = [(kc_ref[0, 0, j].astype(BF16), vc_ref[0, 0, j].astype(BF16)) for j in range(HEADS_PER_STEP)]

    def block(blk, carry):
        ks = jnp.clip(blk * NAT_Q_ROWS - WIN_ROWS // 2, 0, rows_total - NAT_K_ROWS)
        case = jnp.where(blk == 0, 0, jnp.where(blk == n_blocks - 1, 2, 1))
        q_rows = pl.ds(pl.multiple_of(blk * nq, nq), nq)
        k_rows = pl.ds(pl.multiple_of(ks * GRID_W, GRID_W), nk)
        for j, (k_ctx, v_ctx) in enumerate(ctx):
            q = q_ref[0, j, q_rows, :]
            s_loc = _dot_nt(q, k_ref[0, j, k_rows, :]) + bias_ref[j, case]
            s_ctx = _dot_nt(q, k_ctx)
            m = jnp.maximum(jnp.max(s_loc, axis=-1, keepdims=True), jnp.max(s_ctx, axis=-1, keepdims=True))
            p_loc = jnp.exp(s_loc - m)
            p_ctx = jnp.exp(s_ctx - m)
            denom = jnp.sum(p_loc, axis=-1, keepdims=True) + jnp.sum(p_ctx, axis=-1, keepdims=True)
            o = (_dot(p_loc.astype(BF16), v_ref[0, j, k_rows, :]) + _dot(p_ctx.astype(BF16), v_ctx)) / denom
            o_ref[q_rows, j * HEAD_DIM:(j + 1) * HEAD_DIM] = o.astype(o_ref.dtype)
        return carry

    lax.fori_loop(0, n_blocks, block, 0, unroll=2)


def _nat_attention(q, k, v, cache_k, cache_v, bias):
    bsz, _, length, _ = q.shape
    hp = HEADS_PER_STEP
    past = cache_k.shape[3]
    spec = pl.BlockSpec((1, hp, length, HEAD_DIM), lambda b, h: (b, h, 0, 0))
    cspec = pl.BlockSpec((1, 1, hp, past, HEAD_DIM), lambda b, h: (b, 0, h, 0, 0))
    return pl.pallas_call(
        _nat_kernel,
        grid=(bsz, N_HEADS // hp),
        in_specs=[spec, spec, spec, cspec, cspec,
                  pl.BlockSpec((hp,) + bias.shape[1:], lambda b, h: (h, 0, 0, 0))],
        out_specs=pl.BlockSpec((length, hp * HEAD_DIM), lambda b, h: (b, h)),
        out_shape=jax.ShapeDtypeStruct((bsz * length, D_MODEL), BF16),
        compiler_params=_cparams("parallel", "parallel"),
        name="nat_attention",
    )(q, k, v, cache_k, cache_v, bias)


def kernel(x_prompt, x_sample, cache_k, cache_v, c, c_ctx, ada_w, ada_b, norm1_g, norm2_g, normf_g, hy_w_in, hy_w_short, hy_f_w1, hy_f_b1, hy_f_freq, hy_f_w2, hy_f_b2, hy_f_w3, hy_skip, hy_w_out, na_w_qkv, na_rpb, na_w_o, moe_w_router, moe_b_router, moe_w_gate, moe_b_gate, moe_w_up, moe_b_up, moe_w_down, moe_b_down):
    b, s, d = x_prompt.shape
    bd, sd, _ = x_sample.shape
    n_ctx = b * s
    n_tok = n_ctx + bd * sd
    assert n_ctx % sd == 0 and s % QKV_TILE == 0 and n_ctx % ROW_TILE == 0 and sd % ROW_TILE == 0
    seg = (n_ctx, sd)
    x_ctx = x_prompt.reshape(n_ctx, d)
    x_smp = x_sample.reshape(bd * sd, d)
    cond = jnp.zeros((N_COND, d), F32).at[0].set(c_ctx).at[1:1 + bd].set(c)
    mods = _modulation(cond, ada_w, ada_b)

    def moe(x, i, final_norm):
        return _moe_layer(x, norm2_g[i], mods[i], moe_w_router[i], moe_b_router[i], i, moe_w_gate, moe_b_gate,
                          moe_w_up, moe_b_up, moe_w_down, moe_b_down, normf_g, seg, final_norm)

    z = _norm_proj(x_ctx, x_smp, norm1_g[0], mods[0], hy_w_in[0].astype(BF16), seg)
    fargs = (hy_f_w1[0], hy_f_b1[0], hy_f_freq[0], hy_f_w2[0], hy_f_b2[0], hy_f_w3[0])
    y_ctx = _hyena_direct(z.reshape(n_tok // s, s, 3 * d), hy_w_short[0], _hyena_filters(s, *fargs),
                          hy_skip[0], first=0, bsz=b)
    y_smp = _hyena_two_stage(z.reshape(n_tok // sd, sd, 3 * d), hy_w_short[0], _hyena_filters(sd, *fargs),
                             hy_skip[0], first=n_ctx // sd, bsz=bd)
    x = _out_proj(y_ctx.reshape(n_ctx, d), y_smp.reshape(bd * sd, d), hy_w_out[0].astype(BF16), x_ctx, x_smp,
                  mods[0], seg)
    x = moe(x, 0, False)

    w_qkv = na_w_qkv[0].astype(BF16)
    q_c, k_c, v_c = _qkv_proj(x, norm1_g[1], mods[1], w_qkv, seg, 0, b, s, F32)
    q_s, k_s, v_s = _qkv_proj(x, norm1_g[1], mods[1], w_qkv, seg, n_ctx, bd, sd, BF16)
    o_ctx = _ctx_attention(q_c, k_c, v_c)
    o_smp = _nat_attention(q_s, k_s, v_s, cache_k, cache_v, _nat_bias(na_rpb[0]))
    x = _out_proj(o_ctx, o_smp, na_w_o[0].astype(BF16), x, None, mods[1], seg)
    y_prompt, y_sample = moe(x, 1, True)

    nh, hd = k_c.shape[1], k_c.shape[3]
    return (y_prompt.reshape(b, s, d), y_sample.reshape(bd, sd, d),
            k_c.reshape(b, 1, nh, s, hd), v_c.reshape(b, 1, nh, s, hd))
```

```python
import functools
import math

import numpy as np
import jax
import jax.numpy as jnp
from jax import lax
from jax.experimental import pallas as pl
from jax.experimental.pallas import tpu as pltpu

F32 = jnp.float32
BF16 = jnp.bfloat16

D_MODEL = 1024
N_MOD = 6
RMS_EPS = 1e-6
N_EXPERTS = 32
TOP_K = 4
SWIGLU_LIMIT = 7.0
SWIGLU_ALPHA = 1.702

N_COND = 8
ROW_TILE = 512
QKV_TILE = 256
MOE_TILE = 512
V7X_VMEM_LIMIT = 56 * 1024 * 1024


def _cparams(*sem, vmem=V7X_VMEM_LIMIT):
    return pltpu.CompilerParams(dimension_semantics=sem, vmem_limit_bytes=vmem)


def _dot(a, b):
    return jnp.dot(a, b, preferred_element_type=F32)


def _split_bf16(x):
    hi = x.astype(BF16)
    lo = (x - hi.astype(F32)).astype(BF16)
    return hi, lo


def _dot3(a, b):
    ah, al = _split_bf16(a)
    bh, bl = _split_bf16(b)
    return _dot(ah, bh) + (_dot(al, bh) + _dot(ah, bl))


def _seg_of_tile(i, tile, n_ctx, smp_len):
    ctx_tiles = n_ctx // tile
    per_smp = smp_len // tile
    return jnp.where(i < ctx_tiles, 0, 1 + (i - ctx_tiles) // per_smp)


def _norm_mod(x, g, sc, sh):
    y = x * lax.rsqrt(jnp.mean(x * x, axis=-1, keepdims=True) + RMS_EPS)
    return (y * g) * (1.0 + sc) + sh


def _mod_kernel(c_ref, w_ref, b_ref, o_ref):
    c = c_ref[...]
    a = c * jax.nn.sigmoid(c)
    o_ref[0] = _dot3(a, w_ref[0]) + b_ref[0]


def _modulation(cond, ada_w, ada_b):
    depth, d, n_out = ada_w.shape
    tn = 1536
    m = pl.pallas_call(
        _mod_kernel,
        grid=(depth, n_out // tn),
        in_specs=[
            pl.BlockSpec((N_COND, d), lambda l, j: (0, 0)),
            pl.BlockSpec((1, d, tn), lambda l, j: (l, 0, j)),
            pl.BlockSpec((1, 1, tn), lambda l, j: (l, 0, j)),
        ],
        out_specs=pl.BlockSpec((1, N_COND, tn), lambda l, j: (l, 0, j)),
        out_shape=jax.ShapeDtypeStruct((depth, N_COND, n_out), F32),
        compiler_params=_cparams("arbitrary", "arbitrary"),
        name="adaln_modulation",
    )(cond, ada_w, ada_b.reshape(depth, 1, n_out))
    m = m.reshape(depth, N_COND, N_MOD, d)
    return jnp.transpose(m, (0, 2, 1, 3)).reshape(depth, N_MOD * N_COND, 1, d)


def _mod_spec(which, tile, n_ctx, smp_len, first_tile=0):
    return pl.BlockSpec(
        (1, 1, D_MODEL),
        lambda i, *_: (which * N_COND + _seg_of_tile(i + first_tile, tile, n_ctx, smp_len), 0, 0))


def _group_specs(block_cols, ctx_tiles, stacked=False):
    base = ctx_tiles if stacked else 0
    return (pl.BlockSpec((ROW_TILE, block_cols), lambda i: (jnp.minimum(i, ctx_tiles - 1), 0)),
            pl.BlockSpec((ROW_TILE, block_cols), lambda i: (jnp.maximum(i - ctx_tiles, 0) + base, 0)))


def _pick_group(a_ref, b_ref, ctx_tiles):
    return jnp.where(pl.program_id(0) < ctx_tiles, a_ref[...], b_ref[...])


def _norm_proj_kernel(xa_ref, xb_ref, g_ref, sc_ref, sh_ref, w_ref, o_ref, *, ctx_tiles):
    h = _norm_mod(_pick_group(xa_ref, xb_ref, ctx_tiles), g_ref[...], sc_ref[0], sh_ref[0])
    o_ref[...] = _dot(h.astype(BF16), w_ref[...]).astype(o_ref.dtype)


def _norm_proj(x_ctx, x_smp, g, mods, w_bf16, seg, out_dtype=F32):
    d = x_ctx.shape[1]
    n = x_ctx.shape[0] + x_smp.shape[0]
    n_out = w_bf16.shape[1]
    n_ctx, smp_len = seg
    ctx_tiles = n_ctx // ROW_TILE
    return pl.pallas_call(
        functools.partial(_norm_proj_kernel, ctx_tiles=ctx_tiles),
        grid=(n // ROW_TILE,),
        in_specs=[
            *_group_specs(d, ctx_tiles),
            pl.BlockSpec((1, d), lambda i: (0, 0)),
            _mod_spec(1, ROW_TILE, n_ctx, smp_len),
            _mod_spec(0, ROW_TILE, n_ctx, smp_len),
            pl.BlockSpec((d, n_out), lambda i: (0, 0)),
        ],
        out_specs=pl.BlockSpec((ROW_TILE, n_out), lambda i: (i, 0)),
        out_shape=jax.ShapeDtypeStruct((n, n_out), out_dtype),
        compiler_params=_cparams("parallel"),
        name="norm_proj",
    )(x_ctx, x_smp, g.reshape(1, d), mods, mods, w_bf16)


def _out_proj_kernel(ya_ref, yb_ref, w_ref, xa_ref, xb_ref, gate_ref, o_ref, *, ctx_tiles):
    y = _pick_group(ya_ref, yb_ref, ctx_tiles)
    o_ref[...] = _pick_group(xa_ref, xb_ref, ctx_tiles) + gate_ref[0] * _dot(y, w_ref[...])


def _out_proj(y_ctx, y_smp, w_bf16, x_ctx, x_smp, mods, seg):
    d = x_ctx.shape[1]
    n_ctx, smp_len = seg
    n = n_ctx + y_smp.shape[0]
    ctx_tiles = n_ctx // ROW_TILE
    stacked = x_smp is None
    x_smp = x_ctx if stacked else x_smp
    return pl.pallas_call(
        functools.partial(_out_proj_kernel, ctx_tiles=ctx_tiles),
        grid=(n // ROW_TILE,),
        in_specs=[
            *_group_specs(d, ctx_tiles),
            pl.BlockSpec((d, d), lambda i: (0, 0)),
            *_group_specs(d, ctx_tiles, stacked),
            _mod_spec(2, ROW_TILE, n_ctx, smp_len),
        ],
        out_specs=pl.BlockSpec((ROW_TILE, d), lambda i: (i, 0)),
        out_shape=jax.ShapeDtypeStruct((n, d), F32),
        compiler_params=_cparams("parallel"),
        name="out_proj",
    )(y_ctx, y_smp, w_bf16, x_ctx, x_smp, mods)


LANES = 128
ROW_SUBLANES = D_MODEL // LANES


def _store_row_tiles(ref, x):
    t = x.shape[0]
    for s in range(ROW_SUBLANES):
        ref[pl.ds(s, t, stride=ROW_SUBLANES), :] = x[:, s * LANES:(s + 1) * LANES]


def _load_row_tiles(ref, t):
    return jnp.concatenate([ref[pl.ds(s, t, stride=ROW_SUBLANES), :] for s in range(ROW_SUBLANES)], axis=1)


def _columns(cols):
    t = cols[0].shape[0]
    lane = lax.broadcasted_iota(jnp.int32, (t, len(cols)), 1)
    out = jnp.broadcast_to(cols[-1], (t, len(cols)))
    for k in range(len(cols) - 2, -1, -1):
        out = jnp.where(lane == k, cols[k], out)
    return out


def _router_kernel(x_ref, g_ref, sc_ref, sh_ref, wr_ref, br_ref, tri_ref,
                   h_ref, e_ref, gate_ref, rank_ref, cnt_ref, run_ref):
    i = pl.program_id(0)

    @pl.when(i == 0)
    def _():
        run_ref[...] = jnp.zeros_like(run_ref)

    h = _norm_mod(x_ref[...], g_ref[...], sc_ref[0], sh_ref[0])
    _store_row_tiles(h_ref, h)
    logits = _dot3(h, wr_ref[...]) + br_ref[...]
    lane = lax.broadcasted_iota(jnp.int32, logits.shape, 1).astype(F32)
    work = logits
    vals, idxs, hots = [], [], []
    for _ in range(TOP_K):
        m = jnp.max(work, axis=-1, keepdims=True)
        idx = jnp.min(jnp.where(work == m, lane, float(N_EXPERTS)), axis=-1, keepdims=True)
        hot = lane == idx
        vals.append(m)
        idxs.append(idx)
        hots.append(hot)
        work = jnp.where(hot, -jnp.inf, work)
    ex = [jnp.exp(v - vals[0]) for v in vals]
    denom = ex[0] + ex[1] + ex[2] + ex[3]
    gate_ref[...] = _columns([e / denom for e in ex])
    e_ref[...] = _columns(idxs).astype(jnp.int32)

    chosen = (hots[0] | hots[1] | hots[2] | hots[3]).astype(F32)
    before = run_ref[...] + _dot(tri_ref[...], chosen.astype(BF16))
    ranks = [jnp.sum(jnp.where(hot, before, 0.0), axis=-1, keepdims=True) for hot in hots]
    rank_ref[...] = _columns(ranks).astype(jnp.int32)
    run_ref[...] += jnp.sum(chosen, axis=0, keepdims=True)
    cnt_ref[...] = run_ref[...].astype(jnp.int32)


def _router(x, g, mods, w_r, b_r, seg):
    n, d = x.shape
    n_ctx, smp_len = seg
    t = ROW_TILE
    tri = jnp.asarray(np.tril(np.ones((t, t), np.float32), -1), BF16)
    tok4 = lambda i: (i, 0)
    return pl.pallas_call(
        _router_kernel,
        grid=(n // t,),
        in_specs=[
            pl.BlockSpec((t, d), lambda i: (i, 0)),
            pl.BlockSpec((1, d), lambda i: (0, 0)),
            _mod_spec(4, t, n_ctx, smp_len),
            _mod_spec(3, t, n_ctx, smp_len),
            pl.BlockSpec((d, N_EXPERTS), lambda i: (0, 0)),
            pl.BlockSpec((1, N_EXPERTS), lambda i: (0, 0)),
            pl.BlockSpec((t, t), lambda i: (0, 0)),
        ],
        out_specs=[
            pl.BlockSpec((t * ROW_SUBLANES, LANES), tok4),
            pl.BlockSpec((t, TOP_K), tok4),
            pl.BlockSpec((t, TOP_K), tok4),
            pl.BlockSpec((t, TOP_K), tok4),
            pl.BlockSpec((1, N_EXPERTS), lambda i: (0, 0)),
        ],
        out_shape=[
            jax.ShapeDtypeStruct((n * ROW_SUBLANES, LANES), F32),
            jax.ShapeDtypeStruct((n, TOP_K), jnp.int32),
            jax.ShapeDtypeStruct((n, TOP_K), F32),
            jax.ShapeDtypeStruct((n, TOP_K), jnp.int32),
            jax.ShapeDtypeStruct((1, N_EXPERTS), jnp.int32),
        ],
        scratch_shapes=[pltpu.VMEM((1, N_EXPERTS), F32)],
        compiler_params=_cparams("arbitrary"),
        name="moe_router",
    )(x, g.reshape(1, d), mods, mods, w_r, b_r.reshape(1, N_EXPERTS), tri)


def _tile_copy(src_ref, src_row, dst_ref, dst_row, sem):
    rows = lambda r: pl.ds(pl.multiple_of(r, ROW_SUBLANES), ROW_SUBLANES)
    return pltpu.make_async_copy(src_ref.at[rows(src_row), :], dst_ref.at[rows(dst_row), :], sem)


def _tiles_wait(src_ref, dst_ref, n_tiles, sem):
    rows = pl.ds(0, n_tiles * ROW_SUBLANES)
    pltpu.make_async_copy(src_ref.at[rows, :], dst_ref.at[rows, :], sem).wait()


ENTRIES_PER_TILE = ROW_TILE * TOP_K
DMA_PRIORITIES = 2


def _dispatch_kernel(fill_start_ref, fill_len_ref, na_ref, dest_ref, h_ref, xs_ref, hbuf, zero_ref, sem):
    i = pl.program_id(0)
    last = pl.num_programs(0) - 1
    slot = lax.rem(i, 2)
    hbuf[slot] = h_ref[...]

    def issue(t, carry):
        for k in range(TOP_K):
            _tile_copy(hbuf.at[slot], t * ROW_SUBLANES, xs_ref, dest_ref[t * TOP_K + k],
                       sem.at[slot]).start(priority=k % DMA_PRIORITIES)
        return carry

    lax.fori_loop(0, ROW_TILE, issue, 0, unroll=4)

    @pl.when(i >= 1)
    def _():
        _tiles_wait(xs_ref, xs_ref, ENTRIES_PER_TILE, sem.at[1 - slot])

    @pl.when(i == last)
    def _():
        _tiles_wait(xs_ref, xs_ref, ENTRIES_PER_TILE, sem.at[slot])
        zero_ref[...] = jnp.zeros_like(zero_ref)

        def pieces(act):
            for e in range(N_EXPERTS):
                count = fill_len_ref[e]
                done = jnp.int32(0)
                for bit in reversed(range(MOE_TILE.bit_length() - 1)):
                    size = 1 << bit
                    rows = size * ROW_SUBLANES
                    first = pl.multiple_of((fill_start_ref[e] + done) * ROW_SUBLANES, ROW_SUBLANES)
                    copy = pltpu.make_async_copy(zero_ref.at[pl.ds(0, rows), :], xs_ref.at[pl.ds(first, rows), :],
                                                 sem.at[0])
                    pl.when((count & size) != 0)(functools.partial(act, copy))
                    done = done + (count & size)

        pieces(lambda copy: copy.start())
        pieces(lambda copy: copy.wait())

        block_rows = MOE_TILE * ROW_SUBLANES

        def block_copy(blk):
            return pltpu.make_async_copy(
                zero_ref, xs_ref.at[pl.ds(pl.multiple_of(blk * block_rows, block_rows), block_rows), :], sem.at[0])

        n_blocks = xs_ref.shape[0] // block_rows

        def tail(blk, carry):
            block_copy(blk).start()
            return carry

        lax.fori_loop(na_ref[0], n_blocks, tail, 0)

        def tail_drain(blk, carry):
            block_copy(0).wait()
            return carry

        lax.fori_loop(na_ref[0], n_blocks, tail_drain, 0)


def _dispatch(h_tiles, dest_rows, fill_start, fill_len, n_active, cap):
    n = h_tiles.shape[0] // ROW_SUBLANES
    grid_spec = pltpu.PrefetchScalarGridSpec(
        num_scalar_prefetch=3,
        grid=(n // ROW_TILE,),
        in_specs=[
            pl.BlockSpec((ENTRIES_PER_TILE,), lambda i, *_: (i,), memory_space=pltpu.SMEM),
            pl.BlockSpec((ROW_TILE * ROW_SUBLANES, LANES), lambda i, *_: (i, 0)),
        ],
        out_specs=pl.BlockSpec(memory_space=pl.ANY),
        scratch_shapes=[pltpu.VMEM((2, ROW_TILE * ROW_SUBLANES, LANES), F32),
                        pltpu.VMEM((MOE_TILE * ROW_SUBLANES, LANES), F32), pltpu.SemaphoreType.DMA((2,))],
    )
    return pl.pallas_call(
        _dispatch_kernel,
        grid_spec=grid_spec,
        out_shape=jax.ShapeDtypeStruct((cap * ROW_SUBLANES, LANES), F32),
        compiler_params=_cparams("arbitrary"),
        name="moe_dispatch",
    )(fill_start, fill_len, n_active, dest_rows, h_tiles)


def _ffn_kernel(be_ref, na_ref, wslot_ref, next_e_ref, xs_ref, wg_ref, bg_ref, wu_ref, bu_ref, wd_ref, bd_ref,
                ys_ref, wbuf, wg_s, wu_s, wd_s, wsem, *, layer):
    i = pl.program_id(0)
    n_active = na_ref[0]
    tm = xs_ref.shape[0] // ROW_SUBLANES
    changed = jnp.logical_or(i == 0, be_ref[i] != be_ref[jnp.maximum(i - 1, 0)])
    fresh = jnp.logical_and(i < n_active, changed)

    def weight_copies(expert, b):
        return [pltpu.make_async_copy(w_ref.at[layer, expert], wbuf.at[b, m], wsem.at[b])
                for m, w_ref in enumerate((wg_ref, wu_ref, wd_ref))]

    @pl.when(jnp.logical_and(i == 0, n_active > 0))
    def _():
        for copy in weight_copies(be_ref[0], wslot_ref[0]):
            copy.start()

    @pl.when(fresh)
    def _():
        b = wslot_ref[i]
        for copy in weight_copies(be_ref[i], b):
            copy.wait()
        wg_s[...] = wbuf[b, 0].astype(BF16)
        wu_s[...] = wbuf[b, 1].astype(BF16)
        wd_s[...] = wbuf[b, 2].astype(BF16)
        upcoming = next_e_ref[i]

        @pl.when(upcoming >= 0)
        def _():
            for copy in weight_copies(upcoming, 1 - b):
                copy.start()

    @pl.when(i < n_active)
    def _():
        x = _load_row_tiles(xs_ref, tm).astype(BF16)
        g = jnp.minimum(_dot(x, wg_s[...]) + bg_ref[0, 0], SWIGLU_LIMIT)
        u = jnp.clip(_dot(x, wu_s[...]) + bu_ref[0, 0], -SWIGLU_LIMIT, SWIGLU_LIMIT)
        a = g * jax.nn.sigmoid(SWIGLU_ALPHA * g) * (u + 1.0)
        _store_row_tiles(ys_ref, _dot(a.astype(BF16), wd_s[...]) + bd_ref[0, 0])

    @pl.when(i >= na_ref[0])
    def _():
        ys_ref[...] = jnp.zeros_like(ys_ref)


def _expert_ffn(xs, block_e, n_active, weight_slot, next_expert, layer, w_g, b_g, w_u, b_u, w_d, b_d):
    d = D_MODEL
    depth, ne, _, f = w_g.shape
    assert f == d
    nb = block_e.shape[0]
    block = (MOE_TILE * ROW_SUBLANES, LANES)
    bmap = lambda i, be, *_: (layer, be[i], 0, 0)
    hbm = pl.BlockSpec(memory_space=pl.ANY)
    grid_spec = pltpu.PrefetchScalarGridSpec(
        num_scalar_prefetch=4,
        grid=(nb,),
        in_specs=[
            pl.BlockSpec(block, lambda i, be, na, *_: (jnp.minimum(i, na[0] - 1), 0)),
            hbm, pl.BlockSpec((1, 1, 1, f), bmap),
            hbm, pl.BlockSpec((1, 1, 1, f), bmap),
            hbm, pl.BlockSpec((1, 1, 1, d), bmap),
        ],
        out_specs=pl.BlockSpec(block, lambda i, *_: (i, 0)),
        scratch_shapes=[pltpu.VMEM((2, 3, d, f), F32),
                        pltpu.VMEM((d, f), BF16), pltpu.VMEM((d, f), BF16), pltpu.VMEM((f, d), BF16),
                        pltpu.SemaphoreType.DMA((2,))],
    )
    return pl.pallas_call(
        functools.partial(_ffn_kernel, layer=layer),
        grid_spec=grid_spec,
        out_shape=jax.ShapeDtypeStruct(xs.shape, F32),
        compiler_params=_cparams("arbitrary"),
        name="moe_expert_ffn",
    )(block_e, n_active, weight_slot, next_expert, xs,
      w_g, b_g.reshape(depth, ne, 1, f), w_u, b_u.reshape(depth, ne, 1, f), w_d, b_d.reshape(depth, ne, 1, d))


def _combine_kernel(dest_ref, dest_next_ref, x_ref, gate_ref, g2_ref, gf_ref, ys_ref, o_ref, buf, sem, *,
                    final_norm):
    i = pl.program_id(0)
    slot = lax.rem(i, 2)
    t = x_ref.shape[0]
    plane = t * ROW_SUBLANES

    def start_gather(idx_ref, b):
        def body(r, carry):
            for k in range(TOP_K):
                _tile_copy(ys_ref, idx_ref[r * TOP_K + k], buf.at[b], k * plane + r * ROW_SUBLANES,
                           sem.at[b]).start(priority=k % DMA_PRIORITIES)
            return carry

        lax.fori_loop(0, t, body, 0, unroll=4)

    @pl.when(i == 0)
    def _():
        start_gather(dest_ref, 0)

    @pl.when(i + 1 < pl.num_programs(0))
    def _():
        start_gather(dest_next_ref, 1 - slot)

    _tiles_wait(ys_ref, buf.at[slot], TOP_K * t, sem.at[slot])
    gates = gate_ref[...]
    acc = None
    for k in range(TOP_K):
        term = gates[:, k:k + 1] * _load_row_tiles(buf.at[slot, pl.ds(k * plane, plane), :], t)
        acc = term if acc is None else acc + term
    y = x_ref[...] + g2_ref[0] * acc
    if final_norm:
        y = y * lax.rsqrt(jnp.mean(y * y, axis=-1, keepdims=True) + RMS_EPS) * gf_ref[...]
    o_ref[...] = y


def _combine(x, ys, dest_rows, gates, mods, normf_g, seg, final_norm, first_row=0, n_rows=None):
    n, d = x.shape
    n_rows = n if n_rows is None else n_rows
    n_ctx, smp_len = seg
    t = ROW_TILE
    first = first_row // t
    steps = n_rows // t
    rows = lambda i: (i + first, 0)
    return pl.pallas_call(
        functools.partial(_combine_kernel, final_norm=final_norm),
        grid=(steps,),
        in_specs=[
            pl.BlockSpec((ENTRIES_PER_TILE,), lambda i: (i + first,), memory_space=pltpu.SMEM),
            pl.BlockSpec((ENTRIES_PER_TILE,), lambda i: (jnp.minimum(i + 1, steps - 1) + first,),
                         memory_space=pltpu.SMEM),
            pl.BlockSpec((t, d), rows),
            pl.BlockSpec((t, TOP_K), rows),
            _mod_spec(5, t, n_ctx, smp_len, first),
            pl.BlockSpec((1, d), lambda i: (0, 0)),
            pl.BlockSpec(memory_space=pl.ANY),
        ],
        out_specs=pl.BlockSpec((t, d), lambda i: (i, 0)),
        out_shape=jax.ShapeDtypeStruct((n_rows, d), F32),
        scratch_shapes=[pltpu.VMEM((2, TOP_K * t * ROW_SUBLANES, LANES), F32), pltpu.SemaphoreType.DMA((2,))],
        compiler_params=_cparams("arbitrary"),
        name="moe_combine",
    )(dest_rows, dest_rows, x, gates, mods, normf_g.reshape(1, d), ys)


def _moe_layer(x, norm_g, mods, w_r, b_r, layer, w_g, b_g, w_u, b_u, w_d, b_d, normf_g, seg, final_norm):
    n, d = x.shape
    nk = n * TOP_K
    h, top_e, gates, rank, counts = _router(x, norm_g, mods, w_r, b_r, seg)
    counts = counts[0]
    padded = (counts + MOE_TILE - 1) // MOE_TILE * MOE_TILE
    pad_end = jnp.cumsum(padded)
    pad_start = pad_end - padded
    cap = nk + N_EXPERTS * MOE_TILE
    nb = cap // MOE_TILE
    experts = jnp.arange(N_EXPERTS, dtype=jnp.int32)
    start_of = jnp.sum(jnp.where(top_e[..., None] == experts, pad_start, 0), axis=-1)
    dest_rows = ((start_of + rank) * ROW_SUBLANES).reshape(nk).astype(jnp.int32)
    blk_start = jnp.arange(nb, dtype=jnp.int32) * MOE_TILE
    block_e = jnp.minimum(jnp.sum(blk_start[:, None] >= pad_end[None, :], axis=1), N_EXPERTS - 1).astype(jnp.int32)
    n_active = (pad_end[-1] // MOE_TILE).astype(jnp.int32).reshape(1)
    xs = _dispatch(h, dest_rows, (pad_start + counts).astype(jnp.int32), (padded - counts).astype(jnp.int32),
                   n_active, cap)
    changes = jnp.concatenate([jnp.zeros((1,), jnp.int32), (block_e[1:] != block_e[:-1]).astype(jnp.int32)])
    weight_slot = (jnp.cumsum(changes) % 2).astype(jnp.int32)
    later = (experts[None, :] > experts[:, None]) & (counts[None, :] > 0)
    next_of = jnp.min(jnp.where(later, experts[None, :], N_EXPERTS), axis=1)
    next_expert = jnp.where(next_of < N_EXPERTS, next_of, -1).astype(jnp.int32)[block_e]
    ys = _expert_ffn(xs, block_e, n_active, weight_slot, next_expert, layer, w_g, b_g, w_u, b_u, w_d, b_d)
    if not final_norm:
        return _combine(x, ys, dest_rows, gates, mods, normf_g, seg, False)
    n_ctx = seg[0]
    return (_combine(x, ys, dest_rows, gates, mods, normf_g, seg, True, 0, n_ctx),
            _combine(x, ys, dest_rows, gates, mods, normf_g, seg, True, n_ctx, n - n_ctx))


HY_ORDER = 2
HY_BANDS = 16
HY_EMB = 1 + 2 * HY_BANDS
HY_FFN = 64
HY_MIN_DECAY = math.log(1e-2) / 1.5
HY_MAX_DECAY = math.log(1e-2) / 0.3
HY_EMB_PAD = 64


def _filter_mlp_kernel(z_ref, t_ref, w1_ref, b1_ref, fr_ref, w2_ref, b2_ref, w3_ref, dl_ref, o_ref, h_ref):
    @pl.when(pl.program_id(1) == 0)
    def _():
        fr = fr_ref[...]
        h = jnp.sin(fr * (_dot3(z_ref[...], w1_ref[...]) + b1_ref[...]))
        h_ref[...] = jnp.sin(fr * (_dot3(h, w2_ref[...]) + b2_ref[...]))

    o_ref[...] = _dot3(h_ref[...], w3_ref[...]) * jnp.exp(-t_ref[...] * dl_ref[...])


def _hyena_filters(length, w1, b1, freq, w2, b2, w3):
    t = jnp.linspace(0.0, 1.0, length, dtype=F32)[:, None]
    ang = (2.0 * math.pi / length) * jnp.arange(length, dtype=F32)[:, None]
    bands = jnp.linspace(1e-4, HY_BANDS - 1, HY_BANDS, dtype=F32)[None, :]
    z = jnp.concatenate([t, jnp.cos(bands * ang), -jnp.sin(bands * ang)], axis=-1)
    z = jnp.pad(z, ((0, 0), (0, HY_EMB_PAD - HY_EMB)))
    w1p = jnp.pad(w1, ((0, HY_EMB_PAD - HY_EMB), (0, 0)))
    n_out = w3.shape[1]
    deltas = jnp.abs(jnp.linspace(HY_MIN_DECAY, HY_MAX_DECAY, D_MODEL, dtype=F32))
    deltas = jnp.tile(deltas, n_out // D_MODEL)[None, :]
    tl, tn = 256, 1024
    row = lambda i, j: (i, 0)
    fixed = lambda i, j: (0, 0)
    return pl.pallas_call(
        _filter_mlp_kernel,
        grid=(length // tl, n_out // tn),
        in_specs=[
            pl.BlockSpec((tl, HY_EMB_PAD), row),
            pl.BlockSpec((tl, 1), row),
            pl.BlockSpec((HY_EMB_PAD, HY_FFN), fixed),
            pl.BlockSpec((1, HY_FFN), fixed),
            pl.BlockSpec((1, HY_FFN), fixed),
            pl.BlockSpec((HY_FFN, HY_FFN), fixed),
            pl.BlockSpec((1, HY_FFN), fixed),
            pl.BlockSpec((HY_FFN, tn), lambda i, j: (0, j)),
            pl.BlockSpec((1, tn), lambda i, j: (0, j)),
        ],
        out_specs=pl.BlockSpec((tl, tn), lambda i, j: (i, j)),
        out_shape=jax.ShapeDtypeStruct((length, n_out), F32),
        scratch_shapes=[pltpu.VMEM((tl, HY_FFN), F32)],
        compiler_params=_cparams("parallel", "arbitrary"),
        name="hyena_filter_mlp",
    )(z, t, w1p, b1.reshape(1, -1), freq.reshape(1, -1), w2, b2.reshape(1, -1), w3, deltas)


def _short_conv(z, w):
    length = z.shape[0]
    row = lax.broadcasted_iota(jnp.int32, z.shape, 0)
    prev = jnp.where(row == 0, 0.0, pltpu.roll(z, 1, 0))
    nxt = jnp.where(row == length - 1, 0.0, pltpu.roll(z, length - 1, 0))
    return (prev * w[0:1] + z * w[1:2]) + nxt * w[2:3]


def _filter_halves(hf_ref, hb_ref):
    hf = hf_ref[...]
    hb = hb_ref[...]
    hb = jnp.where(lax.broadcasted_iota(jnp.int32, hb.shape, 0) == 0, 0.0, hb)
    norm = jnp.sum(jnp.abs(hf), axis=0, keepdims=True) + jnp.sum(jnp.abs(hb), axis=0, keepdims=True)
    return hf + hb, hf - hb, 1.0 / norm


def _direct_dft_tables(length):
    n_fft = 2 * length
    n_freq = length + 1
    mf = -(-n_freq // 16) * 16
    k = np.arange(mf)[:, None]
    n = np.arange(length)[None, :]
    ang = 2.0 * np.pi * ((k * n) % n_fft) / n_fft
    valid = k < n_freq
    cos = np.where(valid, np.cos(ang), 0.0)
    msin = np.where(valid, -np.sin(ang), 0.0)
    weight = np.where((k == 0) | (k == length), 1.0, 2.0) * valid / n_fft
    fwd = np.concatenate([cos, msin], axis=0)
    inv = np.concatenate([weight * cos, weight * msin], axis=0).T
    return jnp.asarray(fwd, BF16), jnp.asarray(inv, BF16), mf


def _hyena_direct_kernel(zv_ref, z1_ref, z2_ref, wv_ref, w1_ref, w2_ref,
                         hf0_ref, hf1_ref, hb0_ref, hb1_ref, skip_ref, fw_ref, iv_ref,
                         o_ref, kr_ref, ki_ref):
    mf = fw_ref.shape[0] // 2
    dt = o_ref.shape[-1]

    @pl.when(pl.program_id(1) == 0)
    def _():
        for o, (hf_ref, hb_ref) in enumerate(((hf0_ref, hb0_ref), (hf1_ref, hb1_ref))):
            hs, hd, inv_norm = _filter_halves(hf_ref, hb_ref)
            spec = _dot(fw_ref[...], jnp.concatenate([hs, hd], axis=1).astype(BF16))
            kr_ref[o] = spec[:mf, :dt] * inv_norm
            ki_ref[o] = spec[mf:, dt:] * inv_norm

    y = _short_conv(zv_ref[0], wv_ref[...])
    for o, (z_ref, w_ref) in enumerate(((z1_ref, w1_ref), (z2_ref, w2_ref))):
        spec = _dot(fw_ref[...], y.astype(BF16))
        yr, yi = spec[:mf], spec[mf:]
        kr, ki = kr_ref[o], ki_ref[o]
        prod = jnp.concatenate([yr * kr - yi * ki, yr * ki + yi * kr], axis=0)
        yc = _dot(iv_ref[...], prod.astype(BF16))
        y = _short_conv(z_ref[0], w_ref[...]) * (yc + y * skip_ref[o:o + 1])
    o_ref[0] = y.astype(o_ref.dtype)


def _hyena_direct(z, w_short, hfilt, skip, first=0, bsz=None):
    _, length, d3 = z.shape
    bsz = z.shape[0] if bsz is None else bsz
    d = d3 // 3
    dt = 512
    nct = d // dt
    fwd, inv, mf = _direct_dft_tables(length)
    zspec = lambda part: pl.BlockSpec((1, length, dt), lambda c, b: (b + first, 0, part * nct + c))
    wspec = lambda part: pl.BlockSpec((3, dt), lambda c, b: (0, part * nct + c))
    hspec = lambda direction, order: pl.BlockSpec(
        (length, dt), lambda c, b: (0, (direction * HY_ORDER + order) * nct + c))
    fixed = lambda c, b: (0, 0)
    return pl.pallas_call(
        _hyena_direct_kernel,
        grid=(nct, bsz),
        in_specs=[zspec(0), zspec(1), zspec(2), wspec(0), wspec(1), wspec(2),
                  hspec(0, 0), hspec(0, 1), hspec(1, 0), hspec(1, 1),
                  pl.BlockSpec((HY_ORDER, dt), lambda c, b: (0, c)),
                  pl.BlockSpec(fwd.shape, fixed), pl.BlockSpec(inv.shape, fixed)],
        out_specs=pl.BlockSpec((1, length, dt), lambda c, b: (b, 0, c)),
        out_shape=jax.ShapeDtypeStruct((bsz, length, d), BF16),
        scratch_shapes=[pltpu.VMEM((HY_ORDER, mf, dt), F32), pltpu.VMEM((HY_ORDER, mf, dt), F32)],
        compiler_params=_cparams("parallel", "arbitrary"),
        name="hyena_conv_direct",
    )(z, z, z, w_short, w_short, w_short, hfilt, hfilt, hfilt, hfilt, skip, fwd, inv)


FFT_N1 = 64
FFT_N2 = 128
FFT_LANES = 128
FFT_A_PITCH = 2 * FFT_N2 + 8
FFT_U_PITCH = 2 * FFT_N1 + 8
FFT_GROUP = 2
FFT_UNROLL = 16


def _two_stage_tables():
    n_fft = FFT_N1 * FFT_N2
    half = FFT_N2 // 2
    n1 = np.arange(FFT_N1)[:, None, None]
    k2 = np.arange(FFT_N2)[None, :, None]
    n2 = np.arange(half)[None, None, :]
    ang = 2.0 * np.pi * ((k2 * (n1 + FFT_N1 * n2)) % n_fft) / n_fft
    stage_a = np.concatenate([np.cos(ang), -np.sin(ang)], axis=1)
    stage_a_inv = np.transpose(stage_a, (0, 2, 1)) / n_fft
    k1 = np.arange(FFT_N1)[:, None]
    m1 = np.arange(FFT_N1)[None, :]
    phi = 2.0 * np.pi * ((k1 * m1) % FFT_N1) / FFT_N1
    c, s = np.cos(phi), np.sin(phi)
    stage_b = np.block([[c, s], [-s, c]])
    stage_b_inv = np.block([[c, -s], [s, c]])
    theta = 2.0 * np.pi * np.arange(FFT_N1) / FFT_N1
    mirror = np.broadcast_to(np.stack([np.cos(theta), np.sin(theta)])[:, :, None], (2, FFT_N1, FFT_LANES))
    return (*(jnp.asarray(t, BF16) for t in (stage_a, stage_a_inv, stage_b, stage_b_inv)),
            jnp.asarray(mirror, F32))


def _fft_stage_a(y_ref, ma_ref, a_ref):
    half = FFT_N2 // 2

    def body(n1, carry):
        slab = y_ref[pl.ds(n1, half, stride=FFT_N1), :]
        a_ref[pl.ds(pl.multiple_of(n1 * FFT_A_PITCH, 8), 2 * FFT_N2), :] = _dot(ma_ref[n1], slab.astype(BF16))
        return carry

    lax.fori_loop(0, FFT_N1, body, 0, unroll=FFT_UNROLL)


FFT_K2_USED = FFT_N2 // 2 + FFT_GROUP


def _fft_stage_b(a_ref, mb_ref, consume):
    def body(j, carry):
        k2 = j * FFT_GROUP
        cols = []
        for g in range(FFT_GROUP):
            re = a_ref[pl.ds(k2 + g, FFT_N1, stride=FFT_A_PITCH), :]
            im = a_ref[pl.ds(FFT_N2 + k2 + g, FFT_N1, stride=FFT_A_PITCH), :]
            cols.append(jnp.concatenate([re, im], axis=0))
        x = _dot(mb_ref[...], jnp.concatenate(cols, axis=1).astype(BF16))
        for g in range(FFT_GROUP):
            consume(k2 + g, x[:, g * FFT_LANES:(g + 1) * FFT_LANES])
        return carry

    lax.fori_loop(0, FFT_K2_USED // FFT_GROUP, body, 0, unroll=FFT_K2_USED // FFT_GROUP)


def _fft_inverse(z_ref, mbi_ref, mai_ref, tw_ref, u_ref, out_ref):
    half = FFT_N2 // 2

    def store_u(k2, u):
        u_ref[pl.ds(pl.multiple_of(k2 * FFT_U_PITCH, 8), 2 * FFT_N1), :] = u

    def stage_b(j, mirror):
        k2 = j * FFT_GROUP
        rhs = jnp.concatenate([z_ref[k2 + g] for g in range(FFT_GROUP)], axis=1)
        u = _dot(mbi_ref[...], rhs)
        for g in range(FFT_GROUP):
            ug = u[:, g * FFT_LANES:(g + 1) * FFT_LANES]
            store_u(k2 + g, ug)
            if mirror[g]:
                ur, ui = ug[:FFT_N1], ug[FFT_N1:]
                c, s = tw_ref[0], tw_ref[1]
                store_u(FFT_N2 - (k2 + g), jnp.concatenate([c * ur - s * ui, -(c * ui + s * ur)], axis=0))

    assert FFT_GROUP == 2
    stage_b(0, (False, True))

    def mirrored(j, carry):
        stage_b(j, (True, True))
        return carry

    lax.fori_loop(1, half // FFT_GROUP, mirrored, 0, unroll=half // FFT_GROUP - 1)
    store_u(half, _dot(mbi_ref[...], z_ref[half]))

    def stage_a(n1, carry):
        re = u_ref[pl.ds(n1, FFT_N2, stride=FFT_U_PITCH), :]
        im = u_ref[pl.ds(FFT_N1 + n1, FFT_N2, stride=FFT_U_PITCH), :]
        rhs = jnp.concatenate([re, im], axis=0).astype(BF16)
        out_ref[pl.ds(n1, half, stride=FFT_N1), :] = _dot(mai_ref[n1], rhs)
        return carry

    lax.fori_loop(0, FFT_N1, stage_a, 0, unroll=FFT_UNROLL)


FFT_WORK_ROWS = max(FFT_N1 * FFT_A_PITCH, FFT_N2 * FFT_U_PITCH)


def _hyena_spectrum_kernel(hf0_ref, hf1_ref, hb0_ref, hb1_ref, ma_ref, mb_ref, k_ref, y_ref, a_ref):
    for o, (hf_ref, hb_ref) in enumerate(((hf0_ref, hb0_ref), (hf1_ref, hb1_ref))):
        hs, hd, inv_norm = _filter_halves(hf_ref, hb_ref)

        def keep_real(k2, x):
            k_ref[o, k2, 0:FFT_N1, :] = (x[:FFT_N1] * inv_norm).astype(k_ref.dtype)

        def keep_imag(k2, x):
            k_ref[o, k2, FFT_N1:, :] = (x[FFT_N1:] * inv_norm).astype(k_ref.dtype)

        for part, keep in ((hs, keep_real), (hd, keep_imag)):
            y_ref[...] = part
            _fft_stage_a(y_ref, ma_ref, a_ref)
            _fft_stage_b(a_ref, mb_ref, keep)


def _hyena_two_stage_kernel(zv_ref, z1_ref, z2_ref, wv_ref, w1_ref, w2_ref, k_ref, skip_ref,
                            ma_ref, mai_ref, mb_ref, mbi_ref, tw_ref, o_ref, y_ref, c_ref, a_ref, z_ref):
    y_ref[...] = _short_conv(zv_ref[0], wv_ref[...])
    for o, (g_ref, w_ref) in enumerate(((z1_ref, w1_ref), (z2_ref, w2_ref))):
        def multiply(k2, x):
            k = k_ref[o, k2].astype(F32)
            xr, xi = x[:FFT_N1], x[FFT_N1:]
            kr, ki = k[:FFT_N1], k[FFT_N1:]
            z_ref[k2] = jnp.concatenate([xr * kr - xi * ki, xr * ki + xi * kr], axis=0).astype(z_ref.dtype)

        _fft_stage_a(y_ref, ma_ref, a_ref)
        _fft_stage_b(a_ref, mb_ref, multiply)
        _fft_inverse(z_ref, mbi_ref, mai_ref, tw_ref, a_ref, c_ref)
        y = _short_conv(g_ref[0], w_ref[...]) * (c_ref[...] + y_ref[...] * skip_ref[o:o + 1])
        if o + 1 < HY_ORDER:
            y_ref[...] = y
        else:
            o_ref[0] = y.astype(o_ref.dtype)


def _hyena_two_stage(z, w_short, hfilt, skip, first=0, bsz=None):
    _, length, d3 = z.shape
    bsz = z.shape[0] if bsz is None else bsz
    d = d3 // 3
    dt = FFT_LANES
    nct = d // dt
    assert 2 * length == FFT_N1 * FFT_N2
    ma, mai, mb, mbi, mirror = _two_stage_tables()
    spec_shape = (HY_ORDER, FFT_K2_USED, 2 * FFT_N1)
    once = pl.Buffered(1)
    hspec = lambda direction, order: pl.BlockSpec(
        (length, dt), lambda c: (0, (direction * HY_ORDER + order) * nct + c))
    spectrum = pl.pallas_call(
        _hyena_spectrum_kernel,
        grid=(nct,),
        in_specs=[hspec(0, 0), hspec(0, 1), hspec(1, 0), hspec(1, 1),
                  pl.BlockSpec(ma.shape, lambda c: (0, 0, 0), pipeline_mode=once),
                  pl.BlockSpec(mb.shape, lambda c: (0, 0), pipeline_mode=once)],
        out_specs=pl.BlockSpec(spec_shape + (dt,), lambda c: (0, 0, 0, c)),
        out_shape=jax.ShapeDtypeStruct(spec_shape + (d,), BF16),
        scratch_shapes=[pltpu.VMEM((length, dt), F32), pltpu.VMEM((FFT_WORK_ROWS, dt), F32)],
        compiler_params=_cparams("parallel"),
        name="hyena_filter_spectrum",
    )(hfilt, hfilt, hfilt, hfilt, ma, mb)

    zspec = lambda part: pl.BlockSpec((1, length, dt), lambda c, b: (b + first, 0, part * nct + c))
    wspec = lambda part: pl.BlockSpec((3, dt), lambda c, b: (0, part * nct + c))
    fixed3 = lambda c, b: (0, 0, 0)
    fixed2 = lambda c, b: (0, 0)
    return pl.pallas_call(
        _hyena_two_stage_kernel,
        grid=(nct, bsz),
        in_specs=[zspec(0), zspec(1), zspec(2), wspec(0), wspec(1), wspec(2),
                  pl.BlockSpec(spec_shape + (dt,), lambda c, b: (0, 0, 0, c), pipeline_mode=once),
                  pl.BlockSpec((HY_ORDER, dt), lambda c, b: (0, c)),
                  pl.BlockSpec(ma.shape, fixed3, pipeline_mode=once),
                  pl.BlockSpec(mai.shape, fixed3, pipeline_mode=once),
                  pl.BlockSpec(mb.shape, fixed2, pipeline_mode=once),
                  pl.BlockSpec(mbi.shape, fixed2, pipeline_mode=once),
                  pl.BlockSpec(mirror.shape, fixed3, pipeline_mode=once)],
        out_specs=pl.BlockSpec((1, length, dt), lambda c, b: (b, 0, c)),
        out_shape=jax.ShapeDtypeStruct((bsz, length, d), BF16),
        scratch_shapes=[pltpu.VMEM((length, dt), F32), pltpu.VMEM((length, dt), F32),
                        pltpu.VMEM((FFT_WORK_ROWS, dt), F32), pltpu.VMEM((FFT_K2_USED, 2 * FFT_N1, dt), BF16)],
        compiler_params=_cparams("parallel", "arbitrary"),
        name="hyena_conv_two_stage",
    )(z, z, z, w_short, w_short, w_short, spectrum, skip, ma, mai, mb, mbi, mirror)


N_HEADS = 16
HEAD_DIM = D_MODEL // N_HEADS
HEADS_PER_STEP = 2
CTX_HEADS_PER_STEP = 8
GRID_W = 64
WIN_ROWS = 8
WIN_COLS = 16
NEG_INF = -1e30
NAT_Q_ROWS = 4
NAT_K_ROWS = NAT_Q_ROWS + WIN_ROWS - 1
ATTN_SCALE = HEAD_DIM ** -0.5
assert math.frexp(ATTN_SCALE)[0] == 0.5, "the scale is folded into q, which is exact only for a power of two"


def _dot_nt(a, b):
    return lax.dot_general(a, b, (((1,), (1,)), ((), ())), preferred_element_type=F32)


def _qkv_kernel(x_ref, g_ref, sc_ref, sh_ref, w_ref, q_ref, k_ref, v_ref):
    h = _norm_mod(x_ref[...], g_ref[...], sc_ref[0], sh_ref[0])
    qkv = _dot(h.astype(BF16), w_ref[...])
    for part, ref in enumerate((q_ref, k_ref, v_ref)):
        scale = ATTN_SCALE if part == 0 else 1.0
        for head in range(N_HEADS):
            lo = part * D_MODEL + head * HEAD_DIM
            ref[0, head] = (qkv[:, lo:lo + HEAD_DIM] * scale).astype(ref.dtype)


def _qkv_proj(x, g, mods, w_bf16, seg, first_row, n_seq, seq_len, kv_dtype):
    d = x.shape[1]
    n_ctx, smp_len = seg
    t = QKV_TILE
    first_tile = first_row // t
    per_seq = seq_len // t
    out_spec = pl.BlockSpec((1, N_HEADS, t, HEAD_DIM), lambda i: (i // per_seq, 0, i % per_seq, 0))
    shape = (n_seq, N_HEADS, seq_len, HEAD_DIM)
    return pl.pallas_call(
        _qkv_kernel,
        grid=(n_seq * per_seq,),
        in_specs=[
            pl.BlockSpec((t, d), lambda i: (i + first_tile, 0)),
            pl.BlockSpec((1, d), lambda i: (0, 0)),
            _mod_spec(1, t, n_ctx, smp_len, first_tile),
            _mod_spec(0, t, n_ctx, smp_len, first_tile),
            pl.BlockSpec(w_bf16.shape, lambda i: (0, 0)),
        ],
        out_specs=[out_spec, out_spec, out_spec],
        out_shape=[jax.ShapeDtypeStruct(shape, BF16), jax.ShapeDtypeStruct(shape, kv_dtype),
                   jax.ShapeDtypeStruct(shape, kv_dtype)],
        compiler_params=_cparams("parallel"),
        name="qkv_proj",
    )(x, g.reshape(1, d), mods, mods, w_bf16)


def _ctx_attn_kernel(q_ref, k_ref, v_ref, o_ref):
    for j in range(q_ref.shape[1]):
        q = q_ref[0, j]
        k = k_ref[0, j].astype(BF16)
        v = v_ref[0, j].astype(BF16)
        s = _dot_nt(q, k)
        p = jnp.exp(s - jnp.max(s, axis=-1, keepdims=True))
        o = _dot(p.astype(BF16), v) / jnp.sum(p, axis=-1, keepdims=True)
        o_ref[:, j * HEAD_DIM:(j + 1) * HEAD_DIM] = o.astype(o_ref.dtype)


def _ctx_attention(q, k, v):
    bsz, _, s, _ = q.shape
    hp = CTX_HEADS_PER_STEP
    spec = pl.BlockSpec((1, hp, s, HEAD_DIM), lambda b, h: (b, h, 0, 0))
    return pl.pallas_call(
        _ctx_attn_kernel,
        grid=(bsz, N_HEADS // hp),
        in_specs=[spec, spec, spec],
        out_specs=pl.BlockSpec((s, hp * HEAD_DIM), lambda b, h: (b, h)),
        out_shape=jax.ShapeDtypeStruct((bsz * s, D_MODEL), BF16),
        compiler_params=_cparams("parallel", "parallel"),
        name="ctx_attention",
    )(q, k, v)


def _rpb_toeplitz_kernel(r_ref, e_ref, o_ref):
    o_ref[...] = _dot3(r_ref[...], e_ref[...])


def _nat_bias(rpb):
    n_heads, n_r, n_c = rpb.shape
    n_cp = 32
    qc = np.arange(GRID_W)[:, None]
    kc = np.arange(GRID_W)[None, :]
    onehot = (np.clip(kc - qc + WIN_COLS - 1, 0, n_c - 1)[None] == np.arange(n_cp)[:, None, None])
    onehot = jnp.asarray(onehot.reshape(n_cp, GRID_W * GRID_W), F32)
    rows = jnp.pad(rpb.reshape(n_heads * n_r, n_c), ((0, 0), (0, n_cp - n_c)))
    toep = pl.pallas_call(
        _rpb_toeplitz_kernel,
        out_shape=jax.ShapeDtypeStruct((n_heads * n_r, GRID_W * GRID_W), F32),
        compiler_params=_cparams(),
        name="nat_bias_toeplitz",
    )(rows, onehot).reshape(n_heads, n_r, GRID_W, GRID_W)
    q_start = np.clip(qc - WIN_COLS // 2, 0, GRID_W - WIN_COLS)
    col_ok = jnp.asarray((kc >= q_start) & (kc < q_start + WIN_COLS))
    toep = jnp.where(col_ok, toep, NEG_INF)
    toep = jnp.concatenate([toep, jnp.full((n_heads, 1, GRID_W, GRID_W), NEG_INF, F32)], axis=1)
    rows_total = GRID_W
    idx = np.full((3, NAT_Q_ROWS, NAT_K_ROWS), n_r, np.int32)
    for case, r0 in enumerate((0, NAT_Q_ROWS, rows_total - NAT_Q_ROWS)):
        ks = int(np.clip(r0 - WIN_ROWS // 2, 0, rows_total - NAT_K_ROWS))
        for dr in range(NAT_Q_ROWS):
            r = r0 + dr
            rs = int(np.clip(r - WIN_ROWS // 2, 0, rows_total - WIN_ROWS))
            for dk in range(NAT_K_ROWS):
                kr = ks + dk
                if rs <= kr < rs + WIN_ROWS:
                    idx[case, dr, dk] = kr - r + WIN_ROWS - 1
    def assemble(t_ref, o_ref):
        for case in range(3):
            for dr in range(NAT_Q_ROWS):
                for dk in range(NAT_K_ROWS):
                    o_ref[0, case, dr * GRID_W:(dr + 1) * GRID_W, dk * GRID_W:(dk + 1) * GRID_W] = (
                        t_ref[0, int(idx[case, dr, dk])])

    return pl.pallas_call(
        assemble,
        grid=(n_heads,),
        in_specs=[pl.BlockSpec((1, n_r + 1, GRID_W, GRID_W), lambda h: (h, 0, 0, 0))],
        out_specs=pl.BlockSpec((1, 3, NAT_Q_ROWS * GRID_W, NAT_K_ROWS * GRID_W), lambda h: (h, 0, 0, 0)),
        out_shape=jax.ShapeDtypeStruct((n_heads, 3, NAT_Q_ROWS * GRID_W, NAT_K_ROWS * GRID_W), F32),
        compiler_params=_cparams("parallel"),
        name="nat_bias_assemble",
    )(toep)


def _nat_kernel(q_ref, k_ref, v_ref, kc_ref, vc_ref, bias_ref, o_ref):
    n_blocks = q_ref.shape[2] // (NAT_Q_ROWS * GRID_W)
    rows_total = q_ref.shape[2] // GRID_W
    nq = NAT_Q_ROWS * GRID_W
    nk = NAT_K_ROWS * GRID_W
    ctx = [(kc_ref[0, 0, j].astype(BF16), vc_ref[0, 0, j].astype(BF16)) for j in range(HEADS_PER_STEP)]

    def block(blk, carry):
        ks = jnp.clip(blk * NAT_Q_ROWS - WIN_ROWS // 2, 0, rows_total - NAT_K_ROWS)
        case = jnp.where(blk == 0, 0, jnp.where(blk == n_blocks - 1, 2, 1))
        q_rows = pl.ds(pl.multiple_of(blk * nq, nq), nq)
        k_rows = pl.ds(pl.multiple_of(ks * GRID_W, GRID_W), nk)
        for j, (k_ctx, v_ctx) in enumerate(ctx):
            q = q_ref[0, j, q_rows, :]
            s_loc = _dot_nt(q, k_ref[0, j, k_rows, :]) + bias_ref[j, case]
            s_ctx = _dot_nt(q, k_ctx)
            m = jnp.maximum(jnp.max(s_loc, axis=-1, keepdims=True), jnp.max(s_ctx, axis=-1, keepdims=True))
            p_loc = jnp.exp(s_loc - m)
            p_ctx = jnp.exp(s_ctx - m)
            denom = jnp.sum(p_loc, axis=-1, keepdims=True) + jnp.sum(p_ctx, axis=-1, keepdims=True)
            o = (_dot(p_loc.astype(BF16), v_ref[0, j, k_rows, :]) + _dot(p_ctx.astype(BF16), v_ctx)) / denom
            o_ref[q_rows, j * HEAD_DIM:(j + 1) * HEAD_DIM] = o.astype(o_ref.dtype)
        return carry

    lax.fori_loop(0, n_blocks, block, 0, unroll=2)


def _nat_attention(q, k, v, cache_k, cache_v, bias):
    bsz, _, length, _ = q.shape
    hp = HEADS_PER_STEP
    past = cache_k.shape[3]
    spec = pl.BlockSpec((1, hp, length, HEAD_DIM), lambda b, h: (b, h, 0, 0))
    cspec = pl.BlockSpec((1, 1, hp, past, HEAD_DIM), lambda b, h: (b, 0, h, 0, 0))
    return pl.pallas_call(
        _nat_kernel,
        grid=(bsz, N_HEADS // hp),
        in_specs=[spec, spec, spec, cspec, cspec,
                  pl.BlockSpec((hp,) + bias.shape[1:], lambda b, h: (h, 0, 0, 0))],
        out_specs=pl.BlockSpec((length, hp * HEAD_DIM), lambda b, h: (b, h)),
        out_shape=jax.ShapeDtypeStruct((bsz * length, D_MODEL), BF16),
        compiler_params=_cparams("parallel", "parallel"),
        name="nat_attention",
    )(q, k, v, cache_k, cache_v, bias)


def kernel(x_prompt, x_sample, cache_k, cache_v, c, c_ctx, ada_w, ada_b, norm1_g, norm2_g, normf_g, hy_w_in, hy_w_short, hy_f_w1, hy_f_b1, hy_f_freq, hy_f_w2, hy_f_b2, hy_f_w3, hy_skip, hy_w_out, na_w_qkv, na_rpb, na_w_o, moe_w_router, moe_b_router, moe_w_gate, moe_b_gate, moe_w_up, moe_b_up, moe_w_down, moe_b_down):
    b, s, d = x_prompt.shape
    bd, sd, _ = x_sample.shape
    n_ctx = b * s
    n_tok = n_ctx + bd * sd
    assert n_ctx % sd == 0 and s % QKV_TILE == 0 and n_ctx % ROW_TILE == 0 and sd % ROW_TILE == 0
    seg = (n_ctx, sd)
    x_ctx = x_prompt.reshape(n_ctx, d)
    x_smp = x_sample.reshape(bd * sd, d)
    cond = jnp.zeros((N_COND, d), F32).at[0].set(c_ctx).at[1:1 + bd].set(c)
    mods = _modulation(cond, ada_w, ada_b)

    def moe(x, i, final_norm):
        return _moe_layer(x, norm2_g[i], mods[i], moe_w_router[i], moe_b_router[i], i, moe_w_gate, moe_b_gate,
                          moe_w_up, moe_b_up, moe_w_down, moe_b_down, normf_g, seg, final_norm)

    z = _norm_proj(x_ctx, x_smp, norm1_g[0], mods[0], hy_w_in[0].astype(BF16), seg)
    fargs = (hy_f_w1[0], hy_f_b1[0], hy_f_freq[0], hy_f_w2[0], hy_f_b2[0], hy_f_w3[0])
    y_ctx = _hyena_direct(z.reshape(n_tok // s, s, 3 * d), hy_w_short[0], _hyena_filters(s, *fargs),
                          hy_skip[0], first=0, bsz=b)
    y_smp = _hyena_two_stage(z.reshape(n_tok // sd, sd, 3 * d), hy_w_short[0], _hyena_filters(sd, *fargs),
                             hy_skip[0], first=n_ctx // sd, bsz=bd)
    x = _out_proj(y_ctx.reshape(n_ctx, d), y_smp.reshape(bd * sd, d), hy_w_out[0].astype(BF16), x_ctx, x_smp,
                  mods[0], seg)
    x = moe(x, 0, False)

    w_qkv = na_w_qkv[0].astype(BF16)
    q_c, k_c, v_c = _qkv_proj(x, norm1_g[1], mods[1], w_qkv, seg, 0, b, s, F32)
    q_s, k_s, v_s = _qkv_proj(x, norm1_g[1], mods[1], w_qkv, seg, n_ctx, bd, sd, BF16)
    o_ctx = _ctx_attention(q_c, k_c, v_c)
    o_smp = _nat_attention(q_s, k_s, v_s, cache_k, cache_v, _nat_bias(na_rpb[0]))
    x = _out_proj(o_ctx, o_smp, na_w_o[0].astype(BF16), x, None, mods[1], seg)
    y_prompt, y_sample = moe(x, 1, True)

    nh, hd = k_c.shape[1], k_c.shape[3]
    return (y_prompt.reshape(b, s, d), y_sample.reshape(bd, sd, d),
            k_c.reshape(b, 1, nh, s, hd), v_c.reshape(b, 1, nh, s, hd))
```

```python
import functools
import math

import numpy as np
import jax
import jax.numpy as jnp
from jax import lax
from jax.experimental import pallas as pl
from jax.experimental.pallas import tpu as pltpu

F32 = jnp.float32
BF16 = jnp.bfloat16

D_MODEL = 1024
N_MOD = 6
RMS_EPS = 1e-6
N_EXPERTS = 32
TOP_K = 4
SWIGLU_LIMIT = 7.0
SWIGLU_ALPHA = 1.702

N_COND = 8
ROW_TILE = 512
QKV_TILE = 256
MOE_TILE = 512
V7X_VMEM_LIMIT = 56 * 1024 * 1024


def _cparams(*sem, vmem=V7X_VMEM_LIMIT):
    return pltpu.CompilerParams(dimension_semantics=sem, vmem_limit_bytes=vmem)


def _dot(a, b):
    return jnp.dot(a, b, preferred_element_type=F32)


def _split_bf16(x):
    hi = x.astype(BF16)
    lo = (x - hi.astype(F32)).astype(BF16)
    return hi, lo


def _dot3(a, b):
    ah, al = _split_bf16(a)
    bh, bl = _split_bf16(b)
    return _dot(ah, bh) + (_dot(al, bh) + _dot(ah, bl))


def _seg_of_tile(i, tile, n_ctx, smp_len):
    ctx_tiles = n_ctx // tile
    per_smp = smp_len // tile
    return jnp.where(i < ctx_tiles, 0, 1 + (i - ctx_tiles) // per_smp)


def _norm_mod(x, g, sc, sh):
    y = x * lax.rsqrt(jnp.mean(x * x, axis=-1, keepdims=True) + RMS_EPS)
    return (y * g) * (1.0 + sc) + sh


def _mod_kernel(c_ref, w_ref, b_ref, o_ref):
    c = c_ref[...]
    a = c * jax.nn.sigmoid(c)
    o_ref[0] = _dot3(a, w_ref[0]) + b_ref[0]


def _modulation(cond, ada_w, ada_b):
    depth, d, n_out = ada_w.shape
    tn = 1536
    m = pl.pallas_call(
        _mod_kernel,
        grid=(depth, n_out // tn),
        in_specs=[
            pl.BlockSpec((N_COND, d), lambda l, j: (0, 0)),
            pl.BlockSpec((1, d, tn), lambda l, j: (l, 0, j)),
            pl.BlockSpec((1, 1, tn), lambda l, j: (l, 0, j)),
        ],
        out_specs=pl.BlockSpec((1, N_COND, tn), lambda l, j: (l, 0, j)),
        out_shape=jax.ShapeDtypeStruct((depth, N_COND, n_out), F32),
        compiler_params=_cparams("arbitrary", "arbitrary"),
        name="adaln_modulation",
    )(cond, ada_w, ada_b.reshape(depth, 1, n_out))
    m = m.reshape(depth, N_COND, N_MOD, d)
    return jnp.transpose(m, (0, 2, 1, 3)).reshape(depth, N_MOD * N_COND, 1, d)


def _mod_spec(which, tile, n_ctx, smp_len, first_tile=0):
    return pl.BlockSpec(
        (1, 1, D_MODEL),
        lambda i, *_: (which * N_COND + _seg_of_tile(i + first_tile, tile, n_ctx, smp_len), 0, 0))


def _group_specs(block_cols, ctx_tiles, stacked=False):
    base = ctx_tiles if stacked else 0
    return (pl.BlockSpec((ROW_TILE, block_cols), lambda i: (jnp.minimum(i, ctx_tiles - 1), 0)),
            pl.BlockSpec((ROW_TILE, block_cols), lambda i: (jnp.maximum(i - ctx_tiles, 0) + base, 0)))


def _pick_group(a_ref, b_ref, ctx_tiles):
    return jnp.where(pl.program_id(0) < ctx_tiles, a_ref[...], b_ref[...])


def _norm_proj_kernel(xa_ref, xb_ref, g_ref, sc_ref, sh_ref, w_ref, o_ref, *, ctx_tiles):
    h = _norm_mod(_pick_group(xa_ref, xb_ref, ctx_tiles), g_ref[...], sc_ref[0], sh_ref[0])
    o_ref[...] = _dot(h.astype(BF16), w_ref[...]).astype(o_ref.dtype)


def _norm_proj(x_ctx, x_smp, g, mods, w_bf16, seg, out_dtype=F32):
    d = x_ctx.shape[1]
    n = x_ctx.shape[0] + x_smp.shape[0]
    n_out = w_bf16.shape[1]
    n_ctx, smp_len = seg
    ctx_tiles = n_ctx // ROW_TILE
    return pl.pallas_call(
        functools.partial(_norm_proj_kernel, ctx_tiles=ctx_tiles),
        grid=(n // ROW_TILE,),
        in_specs=[
            *_group_specs(d, ctx_tiles),
            pl.BlockSpec((1, d), lambda i: (0, 0)),
            _mod_spec(1, ROW_TILE, n_ctx, smp_len),
            _mod_spec(0, ROW_TILE, n_ctx, smp_len),
            pl.BlockSpec((d, n_out), lambda i: (0, 0)),
        ],
        out_specs=pl.BlockSpec((ROW_TILE, n_out), lambda i: (i, 0)),
        out_shape=jax.ShapeDtypeStruct((n, n_out), out_dtype),
        compiler_params=_cparams("parallel"),
        name="norm_proj",
    )(x_ctx, x_smp, g.reshape(1, d), mods, mods, w_bf16)


def _out_proj_kernel(ya_ref, yb_ref, w_ref, xa_ref, xb_ref, gate_ref, o_ref, *, ctx_tiles):
    y = _pick_group(ya_ref, yb_ref, ctx_tiles)
    o_ref[...] = _pick_group(xa_ref, xb_ref, ctx_tiles) + gate_ref[0] * _dot(y, w_ref[...])


def _out_proj(y_ctx, y_smp, w_bf16, x_ctx, x_smp, mods, seg):
    d = x_ctx.shape[1]
    n_ctx, smp_len = seg
    n = n_ctx + y_smp.shape[0]
    ctx_tiles = n_ctx // ROW_TILE
    stacked = x_smp is None
    x_smp = x_ctx if stacked else x_smp
    return pl.pallas_call(
        functools.partial(_out_proj_kernel, ctx_tiles=ctx_tiles),
        grid=(n // ROW_TILE,),
        in_specs=[
            *_group_specs(d, ctx_tiles),
            pl.BlockSpec((d, d), lambda i: (0, 0)),
            *_group_specs(d, ctx_tiles, stacked),
            _mod_spec(2, ROW_TILE, n_ctx, smp_len),
        ],
        out_specs=pl.BlockSpec((ROW_TILE, d), lambda i: (i, 0)),
        out_shape=jax.ShapeDtypeStruct((n, d), F32),
        compiler_params=_cparams("parallel"),
        name="out_proj",
    )(y_ctx, y_smp, w_bf16, x_ctx, x_smp, mods)


LANES = 128
ROW_SUBLANES = D_MODEL // LANES


def _store_row_tiles(ref, x):
    t = x.shape[0]
    for s in range(ROW_SUBLANES):
        ref[pl.ds(s, t, stride=ROW_SUBLANES), :] = x[:, s * LANES:(s + 1) * LANES]


def _load_row_tiles(ref, t):
    return jnp.concatenate([ref[pl.ds(s, t, stride=ROW_SUBLANES), :] for s in range(ROW_SUBLANES)], axis=1)


def _columns(cols):
    t = cols[0].shape[0]
    lane = lax.broadcasted_iota(jnp.int32, (t, len(cols)), 1)
    out = jnp.broadcast_to(cols[-1], (t, len(cols)))
    for k in range(len(cols) - 2, -1, -1):
        out = jnp.where(lane == k, cols[k], out)
    return out


def _router_kernel(x_ref, g_ref, sc_ref, sh_ref, wr_ref, br_ref, tri_ref,
                   h_ref, e_ref, gate_ref, rank_ref, cnt_ref, run_ref):
    i = pl.program_id(0)

    @pl.when(i == 0)
    def _():
        run_ref[...] = jnp.zeros_like(run_ref)

    h = _norm_mod(x_ref[...], g_ref[...], sc_ref[0], sh_ref[0])
    _store_row_tiles(h_ref, h)
    logits = _dot3(h, wr_ref[...]) + br_ref[...]
    lane = lax.broadcasted_iota(jnp.int32, logits.shape, 1).astype(F32)
    work = logits
    vals, idxs, hots = [], [], []
    for _ in range(TOP_K):
        m = jnp.max(work, axis=-1, keepdims=True)
        idx = jnp.min(jnp.where(work == m, lane, float(N_EXPERTS)), axis=-1, keepdims=True)
        hot = lane == idx
        vals.append(m)
        idxs.append(idx)
        hots.append(hot)
        work = jnp.where(hot, -jnp.inf, work)
    ex = [jnp.exp(v - vals[0]) for v in vals]
    denom = ex[0] + ex[1] + ex[2] + ex[3]
    gate_ref[...] = _columns([e / denom for e in ex])
    e_ref[...] = _columns(idxs).astype(jnp.int32)

    chosen = (hots[0] | hots[1] | hots[2] | hots[3]).astype(F32)
    before = run_ref[...] + _dot(tri_ref[...], chosen.astype(BF16))
    ranks = [jnp.sum(jnp.where(hot, before, 0.0), axis=-1, keepdims=True) for hot in hots]
    rank_ref[...] = _columns(ranks).astype(jnp.int32)
    run_ref[...] += jnp.sum(chosen, axis=0, keepdims=True)
    cnt_ref[...] = run_ref[...].astype(jnp.int32)


def _router(x, g, mods, w_r, b_r, seg):
    n, d = x.shape
    n_ctx, smp_len = seg
    t = ROW_TILE
    tri = jnp.asarray(np.tril(np.ones((t, t), np.float32), -1), BF16)
    tok4 = lambda i: (i, 0)
    return pl.pallas_call(
        _router_kernel,
        grid=(n // t,),
        in_specs=[
            pl.BlockSpec((t, d), lambda i: (i, 0)),
            pl.BlockSpec((1, d), lambda i: (0, 0)),
            _mod_spec(4, t, n_ctx, smp_len),
            _mod_spec(3, t, n_ctx, smp_len),
            pl.BlockSpec((d, N_EXPERTS), lambda i: (0, 0)),
            pl.BlockSpec((1, N_EXPERTS), lambda i: (0, 0)),
            pl.BlockSpec((t, t), lambda i: (0, 0)),
        ],
        out_specs=[
            pl.BlockSpec((t * ROW_SUBLANES, LANES), tok4),
            pl.BlockSpec((t, TOP_K), tok4),
            pl.BlockSpec((t, TOP_K), tok4),
            pl.BlockSpec((t, TOP_K), tok4),
            pl.BlockSpec((1, N_EXPERTS), lambda i: (0, 0)),
        ],
        out_shape=[
            jax.ShapeDtypeStruct((n * ROW_SUBLANES, LANES), F32),
            jax.ShapeDtypeStruct((n, TOP_K), jnp.int32),
            jax.ShapeDtypeStruct((n, TOP_K), F32),
            jax.ShapeDtypeStruct((n, TOP_K), jnp.int32),
            jax.ShapeDtypeStruct((1, N_EXPERTS), jnp.int32),
        ],
        scratch_shapes=[pltpu.VMEM((1, N_EXPERTS), F32)],
        compiler_params=_cparams("arbitrary"),
        name="moe_router",
    )(x, g.reshape(1, d), mods, mods, w_r, b_r.reshape(1, N_EXPERTS), tri)


def _tile_copy(src_ref, src_row, dst_ref, dst_row, sem):
    rows = lambda r: pl.ds(pl.multiple_of(r, ROW_SUBLANES), ROW_SUBLANES)
    return pltpu.make_async_copy(src_ref.at[rows(src_row), :], dst_ref.at[rows(dst_row), :], sem)


def _tiles_wait(src_ref, dst_ref, n_tiles, sem):
    rows = pl.ds(0, n_tiles * ROW_SUBLANES)
    pltpu.make_async_copy(src_ref.at[rows, :], dst_ref.at[rows, :], sem).wait()


ENTRIES_PER_TILE = ROW_TILE * TOP_K
DMA_PRIORITIES = 2


def _dispatch_kernel(fill_start_ref, fill_len_ref, na_ref, dest_ref, h_ref, xs_ref, hbuf, zero_ref, sem):
    i = pl.program_id(0)
    last = pl.num_programs(0) - 1
    slot = lax.rem(i, 2)
    hbuf[slot] = h_ref[...]

    def issue(t, carry):
        for k in range(TOP_K):
            _tile_copy(hbuf.at[slot], t * ROW_SUBLANES, xs_ref, dest_ref[t * TOP_K + k],
                       sem.at[slot]).start(priority=k % DMA_PRIORITIES)
        return carry

    lax.fori_loop(0, ROW_TILE, issue, 0, unroll=8)

    @pl.when(i >= 1)
    def _():
        _tiles_wait(xs_ref, xs_ref, ENTRIES_PER_TILE, sem.at[1 - slot])

    @pl.when(i == last)
    def _():
        _tiles_wait(xs_ref, xs_ref, ENTRIES_PER_TILE, sem.at[slot])
        zero_ref[...] = jnp.zeros_like(zero_ref)

        def pieces(act):
            for e in range(N_EXPERTS):
                count = fill_len_ref[e]
                done = jnp.int32(0)
                for bit in reversed(range(MOE_TILE.bit_length() - 1)):
                    size = 1 << bit
                    rows = size * ROW_SUBLANES
                    first = pl.multiple_of((fill_start_ref[e] + done) * ROW_SUBLANES, ROW_SUBLANES)
                    copy = pltpu.make_async_copy(zero_ref.at[pl.ds(0, rows), :], xs_ref.at[pl.ds(first, rows), :],
                                                 sem.at[0])
                    pl.when((count & size) != 0)(functools.partial(act, copy))
                    done = done + (count & size)

        pieces(lambda copy: copy.start())
        pieces(lambda copy: copy.wait())

        block_rows = MOE_TILE * ROW_SUBLANES

        def block_copy(blk):
            return pltpu.make_async_copy(
                zero_ref, xs_ref.at[pl.ds(pl.multiple_of(blk * block_rows, block_rows), block_rows), :], sem.at[0])

        n_blocks = xs_ref.shape[0] // block_rows

        def tail(blk, carry):
            block_copy(blk).start()
            return carry

        lax.fori_loop(na_ref[0], n_blocks, tail, 0)

        def tail_drain(blk, carry):
            block_copy(0).wait()
            return carry

        lax.fori_loop(na_ref[0], n_blocks, tail_drain, 0)


def _dispatch(h_tiles, dest_rows, fill_start, fill_len, n_active, cap):
    n = h_tiles.shape[0] // ROW_SUBLANES
    grid_spec = pltpu.PrefetchScalarGridSpec(
        num_scalar_prefetch=3,
        grid=(n // ROW_TILE,),
        in_specs=[
            pl.BlockSpec((ENTRIES_PER_TILE,), lambda i, *_: (i,), memory_space=pltpu.SMEM),
            pl.BlockSpec((ROW_TILE * ROW_SUBLANES, LANES), lambda i, *_: (i, 0)),
        ],
        out_specs=pl.BlockSpec(memory_space=pl.ANY),
        scratch_shapes=[pltpu.VMEM((2, ROW_TILE * ROW_SUBLANES, LANES), F32),
                        pltpu.VMEM((MOE_TILE * ROW_SUBLANES, LANES), F32), pltpu.SemaphoreType.DMA((2,))],
    )
    return pl.pallas_call(
        _dispatch_kernel,
        grid_spec=grid_spec,
        out_shape=jax.ShapeDtypeStruct((cap * ROW_SUBLANES, LANES), F32),
        compiler_params=_cparams("arbitrary"),
        name="moe_dispatch",
    )(fill_start, fill_len, n_active, dest_rows, h_tiles)


def _ffn_kernel(be_ref, na_ref, wslot_ref, next_e_ref, xs_ref, wg_ref, bg_ref, wu_ref, bu_ref, wd_ref, bd_ref,
                ys_ref, wbuf, wg_s, wu_s, wd_s, wsem, *, layer):
    i = pl.program_id(0)
    n_active = na_ref[0]
    tm = xs_ref.shape[0] // ROW_SUBLANES
    changed = jnp.logical_or(i == 0, be_ref[i] != be_ref[jnp.maximum(i - 1, 0)])
    fresh = jnp.logical_and(i < n_active, changed)

    def weight_copies(expert, b):
        return [pltpu.make_async_copy(w_ref.at[layer, expert], wbuf.at[b, m], wsem.at[b])
                for m, w_ref in enumerate((wg_ref, wu_ref, wd_ref))]

    @pl.when(jnp.logical_and(i == 0, n_active > 0))
    def _():
        for copy in weight_copies(be_ref[0], wslot_ref[0]):
            copy.start()

    @pl.when(fresh)
    def _():
        b = wslot_ref[i]
        for copy in weight_copies(be_ref[i], b):
            copy.wait()
        wg_s[...] = wbuf[b, 0].astype(BF16)
        wu_s[...] = wbuf[b, 1].astype(BF16)
        wd_s[...] = wbuf[b, 2].astype(BF16)
        upcoming = next_e_ref[i]

        @pl.when(upcoming >= 0)
        def _():
            for copy in weight_copies(upcoming, 1 - b):
                copy.start()

    @pl.when(i < n_active)
    def _():
        x = _load_row_tiles(xs_ref, tm).astype(BF16)
        g = jnp.minimum(_dot(x, wg_s[...]) + bg_ref[0, 0], SWIGLU_LIMIT)
        u = jnp.clip(_dot(x, wu_s[...]) + bu_ref[0, 0], -SWIGLU_LIMIT, SWIGLU_LIMIT)
        a = g * jax.nn.sigmoid(SWIGLU_ALPHA * g) * (u + 1.0)
        _store_row_tiles(ys_ref, _dot(a.astype(BF16), wd_s[...]) + bd_ref[0, 0])

    @pl.when(i >= na_ref[0])
    def _():
        ys_ref[...] = jnp.zeros_like(ys_ref)


def _expert_ffn(xs, block_e, n_active, weight_slot, next_expert, layer, w_g, b_g, w_u, b_u, w_d, b_d):
    d = D_MODEL
    depth, ne, _, f = w_g.shape
    assert f == d
    nb = block_e.shape[0]
    block = (MOE_TILE * ROW_SUBLANES, LANES)
    bmap = lambda i, be, *_: (layer, be[i], 0, 0)
    hbm = pl.BlockSpec(memory_space=pl.ANY)
    grid_spec = pltpu.PrefetchScalarGridSpec(
        num_scalar_prefetch=4,
        grid=(nb,),
        in_specs=[
            pl.BlockSpec(block, lambda i, be, na, *_: (jnp.minimum(i, na[0] - 1), 0)),
            hbm, pl.BlockSpec((1, 1, 1, f), bmap),
            hbm, pl.BlockSpec((1, 1, 1, f), bmap),
            hbm, pl.BlockSpec((1, 1, 1, d), bmap),
        ],
        out_specs=pl.BlockSpec(block, lambda i, *_: (i, 0)),
        scratch_shapes=[pltpu.VMEM((2, 3, d, f), F32),
                        pltpu.VMEM((d, f), BF16), pltpu.VMEM((d, f), BF16), pltpu.VMEM((f, d), BF16),
                        pltpu.SemaphoreType.DMA((2,))],
    )
    return pl.pallas_call(
        functools.partial(_ffn_kernel, layer=layer),
        grid_spec=grid_spec,
        out_shape=jax.ShapeDtypeStruct(xs.shape, F32),
        compiler_params=_cparams("arbitrary"),
        name="moe_expert_ffn",
    )(block_e, n_active, weight_slot, next_expert, xs,
      w_g, b_g.reshape(depth, ne, 1, f), w_u, b_u.reshape(depth, ne, 1, f), w_d, b_d.reshape(depth, ne, 1, d))


def _combine_kernel(dest_ref, dest_next_ref, x_ref, gate_ref, g2_ref, gf_ref, ys_ref, o_ref, buf, sem, *,
                    final_norm):
    i = pl.program_id(0)
    slot = lax.rem(i, 2)
    t = x_ref.shape[0]
    plane = t * ROW_SUBLANES

    def start_gather(idx_ref, b):
        def body(r, carry):
            for k in range(TOP_K):
                _tile_copy(ys_ref, idx_ref[r * TOP_K + k], buf.at[b], k * plane + r * ROW_SUBLANES,
                           sem.at[b]).start(priority=k % DMA_PRIORITIES)
            return carry

        lax.fori_loop(0, t, body, 0, unroll=8)

    @pl.when(i == 0)
    def _():
        start_gather(dest_ref, 0)

    @pl.when(i + 1 < pl.num_programs(0))
    def _():
        start_gather(dest_next_ref, 1 - slot)

    _tiles_wait(ys_ref, buf.at[slot], TOP_K * t, sem.at[slot])
    gates = gate_ref[...]
    acc = None
    for k in range(TOP_K):
        term = gates[:, k:k + 1] * _load_row_tiles(buf.at[slot, pl.ds(k * plane, plane), :], t)
        acc = term if acc is None else acc + term
    y = x_ref[...] + g2_ref[0] * acc
    if final_norm:
        y = y * lax.rsqrt(jnp.mean(y * y, axis=-1, keepdims=True) + RMS_EPS) * gf_ref[...]
    o_ref[...] = y


def _combine(x, ys, dest_rows, gates, mods, normf_g, seg, final_norm, first_row=0, n_rows=None):
    n, d = x.shape
    n_rows = n if n_rows is None else n_rows
    n_ctx, smp_len = seg
    t = ROW_TILE
    first = first_row // t
    steps = n_rows // t
    rows = lambda i: (i + first, 0)
    return pl.pallas_call(
        functools.partial(_combine_kernel, final_norm=final_norm),
        grid=(steps,),
        in_specs=[
            pl.BlockSpec((ENTRIES_PER_TILE,), lambda i: (i + first,), memory_space=pltpu.SMEM),
            pl.BlockSpec((ENTRIES_PER_TILE,), lambda i: (jnp.minimum(i + 1, steps - 1) + first,),
                         memory_space=pltpu.SMEM),
            pl.BlockSpec((t, d), rows),
            pl.BlockSpec((t, TOP_K), rows),
            _mod_spec(5, t, n_ctx, smp_len, first),
            pl.BlockSpec((1, d), lambda i: (0, 0)),
            pl.BlockSpec(memory_space=pl.ANY),
        ],
        out_specs=pl.BlockSpec((t, d), lambda i: (i, 0)),
        out_shape=jax.ShapeDtypeStruct((n_rows, d), F32),
        scratch_shapes=[pltpu.VMEM((2, TOP_K * t * ROW_SUBLANES, LANES), F32), pltpu.SemaphoreType.DMA((2,))],
        compiler_params=_cparams("arbitrary"),
        name="moe_combine",
    )(dest_rows, dest_rows, x, gates, mods, normf_g.reshape(1, d), ys)


def _moe_layer(x, norm_g, mods, w_r, b_r, layer, w_g, b_g, w_u, b_u, w_d, b_d, normf_g, seg, final_norm):
    n, d = x.shape
    nk = n * TOP_K
    h, top_e, gates, rank, counts = _router(x, norm_g, mods, w_r, b_r, seg)
    counts = counts[0]
    padded = (counts + MOE_TILE - 1) // MOE_TILE * MOE_TILE
    pad_end = jnp.cumsum(padded)
    pad_start = pad_end - padded
    cap = nk + N_EXPERTS * MOE_TILE
    nb = cap // MOE_TILE
    experts = jnp.arange(N_EXPERTS, dtype=jnp.int32)
    start_of = jnp.sum(jnp.where(top_e[..., None] == experts, pad_start, 0), axis=-1)
    dest_rows = ((start_of + rank) * ROW_SUBLANES).reshape(nk).astype(jnp.int32)
    blk_start = jnp.arange(nb, dtype=jnp.int32) * MOE_TILE
    block_e = jnp.minimum(jnp.sum(blk_start[:, None] >= pad_end[None, :], axis=1), N_EXPERTS - 1).astype(jnp.int32)
    n_active = (pad_end[-1] // MOE_TILE).astype(jnp.int32).reshape(1)
    xs = _dispatch(h, dest_rows, (pad_start + counts).astype(jnp.int32), (padded - counts).astype(jnp.int32),
                   n_active, cap)
    changes = jnp.concatenate([jnp.zeros((1,), jnp.int32), (block_e[1:] != block_e[:-1]).astype(jnp.int32)])
    weight_slot = (jnp.cumsum(changes) % 2).astype(jnp.int32)
    later = (experts[None, :] > experts[:, None]) & (counts[None, :] > 0)
    next_of = jnp.min(jnp.where(later, experts[None, :], N_EXPERTS), axis=1)
    next_expert = jnp.where(next_of < N_EXPERTS, next_of, -1).astype(jnp.int32)[block_e]
    ys = _expert_ffn(xs, block_e, n_active, weight_slot, next_expert, layer, w_g, b_g, w_u, b_u, w_d, b_d)
    if not final_norm:
        return _combine(x, ys, dest_rows, gates, mods, normf_g, seg, False)
    n_ctx = seg[0]
    return (_combine(x, ys, dest_rows, gates, mods, normf_g, seg, True, 0, n_ctx),
            _combine(x, ys, dest_rows, gates, mods, normf_g, seg, True, n_ctx, n - n_ctx))


HY_ORDER = 2
HY_BANDS = 16
HY_EMB = 1 + 2 * HY_BANDS
HY_FFN = 64
HY_MIN_DECAY = math.log(1e-2) / 1.5
HY_MAX_DECAY = math.log(1e-2) / 0.3
HY_EMB_PAD = 64


def _filter_mlp_kernel(z_ref, t_ref, w1_ref, b1_ref, fr_ref, w2_ref, b2_ref, w3_ref, dl_ref, o_ref, h_ref):
    @pl.when(pl.program_id(1) == 0)
    def _():
        fr = fr_ref[...]
        h = jnp.sin(fr * (_dot3(z_ref[...], w1_ref[...]) + b1_ref[...]))
        h_ref[...] = jnp.sin(fr * (_dot3(h, w2_ref[...]) + b2_ref[...]))

    o_ref[...] = _dot3(h_ref[...], w3_ref[...]) * jnp.exp(-t_ref[...] * dl_ref[...])


def _hyena_filters(length, w1, b1, freq, w2, b2, w3):
    t = jnp.linspace(0.0, 1.0, length, dtype=F32)[:, None]
    ang = (2.0 * math.pi / length) * jnp.arange(length, dtype=F32)[:, None]
    bands = jnp.linspace(1e-4, HY_BANDS - 1, HY_BANDS, dtype=F32)[None, :]
    z = jnp.concatenate([t, jnp.cos(bands * ang), -jnp.sin(bands * ang)], axis=-1)
    z = jnp.pad(z, ((0, 0), (0, HY_EMB_PAD - HY_EMB)))
    w1p = jnp.pad(w1, ((0, HY_EMB_PAD - HY_EMB), (0, 0)))
    n_out = w3.shape[1]
    deltas = jnp.abs(jnp.linspace(HY_MIN_DECAY, HY_MAX_DECAY, D_MODEL, dtype=F32))
    deltas = jnp.tile(deltas, n_out // D_MODEL)[None, :]
    tl, tn = 256, 1024
    row = lambda i, j: (i, 0)
    fixed = lambda i, j: (0, 0)
    return pl.pallas_call(
        _filter_mlp_kernel,
        grid=(length // tl, n_out // tn),
        in_specs=[
            pl.BlockSpec((tl, HY_EMB_PAD), row),
            pl.BlockSpec((tl, 1), row),
            pl.BlockSpec((HY_EMB_PAD, HY_FFN), fixed),
            pl.BlockSpec((1, HY_FFN), fixed),
            pl.BlockSpec((1, HY_FFN), fixed),
            pl.BlockSpec((HY_FFN, HY_FFN), fixed),
            pl.BlockSpec((1, HY_FFN), fixed),
            pl.BlockSpec((HY_FFN, tn), lambda i, j: (0, j)),
            pl.BlockSpec((1, tn), lambda i, j: (0, j)),
        ],
        out_specs=pl.BlockSpec((tl, tn), lambda i, j: (i, j)),
        out_shape=jax.ShapeDtypeStruct((length, n_out), F32),
        scratch_shapes=[pltpu.VMEM((tl, HY_FFN), F32)],
        compiler_params=_cparams("parallel", "arbitrary"),
        name="hyena_filter_mlp",
    )(z, t, w1p, b1.reshape(1, -1), freq.reshape(1, -1), w2, b2.reshape(1, -1), w3, deltas)


def _short_conv(z, w):
    length = z.shape[0]
    row = lax.broadcasted_iota(jnp.int32, z.shape, 0)
    prev = jnp.where(row == 0, 0.0, pltpu.roll(z, 1, 0))
    nxt = jnp.where(row == length - 1, 0.0, pltpu.roll(z, length - 1, 0))
    return (prev * w[0:1] + z * w[1:2]) + nxt * w[2:3]


def _filter_halves(hf_ref, hb_ref):
    hf = hf_ref[...]
    hb = hb_ref[...]
    hb = jnp.where(lax.broadcasted_iota(jnp.int32, hb.shape, 0) == 0, 0.0, hb)
    norm = jnp.sum(jnp.abs(hf), axis=0, keepdims=True) + jnp.sum(jnp.abs(hb), axis=0, keepdims=True)
    return hf + hb, hf - hb, 1.0 / norm


def _direct_dft_tables(length):
    n_fft = 2 * length
    n_freq = length + 1
    mf = -(-n_freq // 16) * 16
    k = np.arange(mf)[:, None]
    n = np.arange(length)[None, :]
    ang = 2.0 * np.pi * ((k * n) % n_fft) / n_fft
    valid = k < n_freq
    cos = np.where(valid, np.cos(ang), 0.0)
    msin = np.where(valid, -np.sin(ang), 0.0)
    weight = np.where((k == 0) | (k == length), 1.0, 2.0) * valid / n_fft
    fwd = np.concatenate([cos, msin], axis=0)
    inv = np.concatenate([weight * cos, weight * msin], axis=0).T
    return jnp.asarray(fwd, BF16), jnp.asarray(inv, BF16), mf


def _hyena_direct_kernel(zv_ref, z1_ref, z2_ref, wv_ref, w1_ref, w2_ref,
                         hf0_ref, hf1_ref, hb0_ref, hb1_ref, skip_ref, fw_ref, iv_ref,
                         o_ref, kr_ref, ki_ref):
    mf = fw_ref.shape[0] // 2
    dt = o_ref.shape[-1]

    @pl.when(pl.program_id(1) == 0)
    def _():
        for o, (hf_ref, hb_ref) in enumerate(((hf0_ref, hb0_ref), (hf1_ref, hb1_ref))):
            hs, hd, inv_norm = _filter_halves(hf_ref, hb_ref)
            spec = _dot(fw_ref[...], jnp.concatenate([hs, hd], axis=1).astype(BF16))
            kr_ref[o] = spec[:mf, :dt] * inv_norm
            ki_ref[o] = spec[mf:, dt:] * inv_norm

    y = _short_conv(zv_ref[0], wv_ref[...])
    for o, (z_ref, w_ref) in enumerate(((z1_ref, w1_ref), (z2_ref, w2_ref))):
        spec = _dot(fw_ref[...], y.astype(BF16))
        yr, yi = spec[:mf], spec[mf:]
        kr, ki = kr_ref[o], ki_ref[o]
        prod = jnp.concatenate([yr * kr - yi * ki, yr * ki + yi * kr], axis=0)
        yc = _dot(iv_ref[...], prod.astype(BF16))
        y = _short_conv(z_ref[0], w_ref[...]) * (yc + y * skip_ref[o:o + 1])
    o_ref[0] = y.astype(o_ref.dtype)


def _hyena_direct(z, w_short, hfilt, skip, first=0, bsz=None):
    _, length, d3 = z.shape
    bsz = z.shape[0] if bsz is None else bsz
    d = d3 // 3
    dt = 512
    nct = d // dt
    fwd, inv, mf = _direct_dft_tables(length)
    zspec = lambda part: pl.BlockSpec((1, length, dt), lambda c, b: (b + first, 0, part * nct + c))
    wspec = lambda part: pl.BlockSpec((3, dt), lambda c, b: (0, part * nct + c))
    hspec = lambda direction, order: pl.BlockSpec(
        (length, dt), lambda c, b: (0, (direction * HY_ORDER + order) * nct + c))
    fixed = lambda c, b: (0, 0)
    return pl.pallas_call(
        _hyena_direct_kernel,
        grid=(nct, bsz),
        in_specs=[zspec(0), zspec(1), zspec(2), wspec(0), wspec(1), wspec(2),
                  hspec(0, 0), hspec(0, 1), hspec(1, 0), hspec(1, 1),
                  pl.BlockSpec((HY_ORDER, dt), lambda c, b: (0, c)),
                  pl.BlockSpec(fwd.shape, fixed), pl.BlockSpec(inv.shape, fixed)],
        out_specs=pl.BlockSpec((1, length, dt), lambda c, b: (b, 0, c)),
        out_shape=jax.ShapeDtypeStruct((bsz, length, d), BF16),
        scratch_shapes=[pltpu.VMEM((HY_ORDER, mf, dt), F32), pltpu.VMEM((HY_ORDER, mf, dt), F32)],
        compiler_params=_cparams("parallel", "arbitrary"),
        name="hyena_conv_direct",
    )(z, z, z, w_short, w_short, w_short, hfilt, hfilt, hfilt, hfilt, skip, fwd, inv)


FFT_N1 = 64
FFT_N2 = 128
FFT_LANES = 128
FFT_A_PITCH = 2 * FFT_N2 + 8
FFT_U_PITCH = 2 * FFT_N1 + 8
FFT_GROUP = 2
FFT_UNROLL = 32


def _two_stage_tables():
    n_fft = FFT_N1 * FFT_N2
    half = FFT_N2 // 2
    n1 = np.arange(FFT_N1)[:, None, None]
    k2 = np.arange(FFT_N2)[None, :, None]
    n2 = np.arange(half)[None, None, :]
    ang = 2.0 * np.pi * ((k2 * (n1 + FFT_N1 * n2)) % n_fft) / n_fft
    stage_a = np.concatenate([np.cos(ang), -np.sin(ang)], axis=1)
    stage_a_inv = np.transpose(stage_a, (0, 2, 1)) / n_fft
    k1 = np.arange(FFT_N1)[:, None]
    m1 = np.arange(FFT_N1)[None, :]
    phi = 2.0 * np.pi * ((k1 * m1) % FFT_N1) / FFT_N1
    c, s = np.cos(phi), np.sin(phi)
    stage_b = np.block([[c, s], [-s, c]])
    stage_b_inv = np.block([[c, -s], [s, c]])
    theta = 2.0 * np.pi * np.arange(FFT_N1) / FFT_N1
    mirror = np.broadcast_to(np.stack([np.cos(theta), np.sin(theta)])[:, :, None], (2, FFT_N1, FFT_LANES))
    return (*(jnp.asarray(t, BF16) for t in (stage_a, stage_a_inv, stage_b, stage_b_inv)),
            jnp.asarray(mirror, F32))


def _fft_stage_a(y_ref, ma_ref, a_ref):
    half = FFT_N2 // 2

    def body(n1, carry):
        slab = y_ref[pl.ds(n1, half, stride=FFT_N1), :]
        a_ref[pl.ds(pl.multiple_of(n1 * FFT_A_PITCH, 8), 2 * FFT_N2), :] = _dot(ma_ref[n1], slab.astype(BF16))
        return carry

    lax.fori_loop(0, FFT_N1, body, 0, unroll=FFT_UNROLL)


FFT_K2_USED = FFT_N2 // 2 + FFT_GROUP


def _fft_stage_b(a_ref, mb_ref, consume):
    def body(j, carry):
        k2 = j * FFT_GROUP
        cols = []
        for g in range(FFT_GROUP):
            re = a_ref[pl.ds(k2 + g, FFT_N1, stride=FFT_A_PITCH), :]
            im = a_ref[pl.ds(FFT_N2 + k2 + g, FFT_N1, stride=FFT_A_PITCH), :]
            cols.append(jnp.concatenate([re, im], axis=0))
        x = _dot(mb_ref[...], jnp.concatenate(cols, axis=1).astype(BF16))
        for g in range(FFT_GROUP):
            consume(k2 + g, x[:, g * FFT_LANES:(g + 1) * FFT_LANES])
        return carry

    lax.fori_loop(0, FFT_K2_USED // FFT_GROUP, body, 0, unroll=FFT_K2_USED // FFT_GROUP)


def _fft_inverse(z_ref, mbi_ref, mai_ref, tw_ref, u_ref, out_ref):
    half = FFT_N2 // 2

    def store_u(k2, u):
        u_ref[pl.ds(pl.multiple_of(k2 * FFT_U_PITCH, 8), 2 * FFT_N1), :] = u

    def stage_b(j, mirror):
        k2 = j * FFT_GROUP
        rhs = jnp.concatenate([z_ref[k2 + g] for g in range(FFT_GROUP)], axis=1)
        u = _dot(mbi_ref[...], rhs)
        for g in range(FFT_GROUP):
            ug = u[:, g * FFT_LANES:(g + 1) * FFT_LANES]
            store_u(k2 + g, ug)
            if mirror[g]:
                ur, ui = ug[:FFT_N1], ug[FFT_N1:]
                c, s = tw_ref[0], tw_ref[1]
                store_u(FFT_N2 - (k2 + g), jnp.concatenate([c * ur - s * ui, -(c * ui + s * ur)], axis=0))

    assert FFT_GROUP == 2
    stage_b(0, (False, True))

    def mirrored(j, carry):
        stage_b(j, (True, True))
        return carry

    lax.fori_loop(1, half // FFT_GROUP, mirrored, 0, unroll=half // FFT_GROUP - 1)
    store_u(half, _dot(mbi_ref[...], z_ref[half]))

    def stage_a(n1, carry):
        re = u_ref[pl.ds(n1, FFT_N2, stride=FFT_U_PITCH), :]
        im = u_ref[pl.ds(FFT_N1 + n1, FFT_N2, stride=FFT_U_PITCH), :]
        rhs = jnp.concatenate([re, im], axis=0).astype(BF16)
        out_ref[pl.ds(n1, half, stride=FFT_N1), :] = _dot(mai_ref[n1], rhs)
        return carry

    lax.fori_loop(0, FFT_N1, stage_a, 0, unroll=FFT_UNROLL)


FFT_WORK_ROWS = max(FFT_N1 * FFT_A_PITCH, FFT_N2 * FFT_U_PITCH)


def _hyena_spectrum_kernel(hf0_ref, hf1_ref, hb0_ref, hb1_ref, ma_ref, mb_ref, k_ref, y_ref, a_ref):
    for o, (hf_ref, hb_ref) in enumerate(((hf0_ref, hb0_ref), (hf1_ref, hb1_ref))):
        hs, hd, inv_norm = _filter_halves(hf_ref, hb_ref)

        def keep_real(k2, x):
            k_ref[o, k2, 0:FFT_N1, :] = (x[:FFT_N1] * inv_norm).astype(k_ref.dtype)

        def keep_imag(k2, x):
            k_ref[o, k2, FFT_N1:, :] = (x[FFT_N1:] * inv_norm).astype(k_ref.dtype)

        for part, keep in ((hs, keep_real), (hd, keep_imag)):
            y_ref[...] = part
            _fft_stage_a(y_ref, ma_ref, a_ref)
            _fft_stage_b(a_ref, mb_ref, keep)


def _hyena_two_stage_kernel(zv_ref, z1_ref, z2_ref, wv_ref, w1_ref, w2_ref, k_ref, skip_ref,
                            ma_ref, mai_ref, mb_ref, mbi_ref, tw_ref, o_ref, y_ref, c_ref, a_ref, z_ref):
    y_ref[...] = _short_conv(zv_ref[0], wv_ref[...])
    for o, (g_ref, w_ref) in enumerate(((z1_ref, w1_ref), (z2_ref, w2_ref))):
        def multiply(k2, x):
            k = k_ref[o, k2].astype(F32)
            xr, xi = x[:FFT_N1], x[FFT_N1:]
            kr, ki = k[:FFT_N1], k[FFT_N1:]
            z_ref[k2] = jnp.concatenate([xr * kr - xi * ki, xr * ki + xi * kr], axis=0).astype(z_ref.dtype)

        _fft_stage_a(y_ref, ma_ref, a_ref)
        _fft_stage_b(a_ref, mb_ref, multiply)
        _fft_inverse(z_ref, mbi_ref, mai_ref, tw_ref, a_ref, c_ref)
        y = _short_conv(g_ref[0], w_ref[...]) * (c_ref[...] + y_ref[...] * skip_ref[o:o + 1])
        if o + 1 < HY_ORDER:
            y_ref[...] = y
        else:
            o_ref[0] = y.astype(o_ref.dtype)


def _hyena_two_stage(z, w_short, hfilt, skip, first=0, bsz=None):
    _, length, d3 = z.shape
    bsz = z.shape[0] if bsz is None else bsz
    d = d3 // 3
    dt = FFT_LANES
    nct = d // dt
    assert 2 * length == FFT_N1 * FFT_N2
    ma, mai, mb, mbi, mirror = _two_stage_tables()
    spec_shape = (HY_ORDER, FFT_K2_USED, 2 * FFT_N1)
    once = pl.Buffered(1)
    hspec = lambda direction, order: pl.BlockSpec(
        (length, dt), lambda c: (0, (direction * HY_ORDER + order) * nct + c))
    spectrum = pl.pallas_call(
        _hyena_spectrum_kernel,
        grid=(nct,),
        in_specs=[hspec(0, 0), hspec(0, 1), hspec(1, 0), hspec(1, 1),
                  pl.BlockSpec(ma.shape, lambda c: (0, 0, 0), pipeline_mode=once),
                  pl.BlockSpec(mb.shape, lambda c: (0, 0), pipeline_mode=once)],
        out_specs=pl.BlockSpec(spec_shape + (dt,), lambda c: (0, 0, 0, c)),
        out_shape=jax.ShapeDtypeStruct(spec_shape + (d,), BF16),
        scratch_shapes=[pltpu.VMEM((length, dt), F32), pltpu.VMEM((FFT_WORK_ROWS, dt), F32)],
        compiler_params=_cparams("parallel"),
        name="hyena_filter_spectrum",
    )(hfilt, hfilt, hfilt, hfilt, ma, mb)

    zspec = lambda part: pl.BlockSpec((1, length, dt), lambda c, b: (b + first, 0, part * nct + c))
    wspec = lambda part: pl.BlockSpec((3, dt), lambda c, b: (0, part * nct + c))
    fixed3 = lambda c, b: (0, 0, 0)
    fixed2 = lambda c, b: (0, 0)
    return pl.pallas_call(
        _hyena_two_stage_kernel,
        grid=(nct, bsz),
        in_specs=[zspec(0), zspec(1), zspec(2), wspec(0), wspec(1), wspec(2),
                  pl.BlockSpec(spec_shape + (dt,), lambda c, b: (0, 0, 0, c), pipeline_mode=once),
                  pl.BlockSpec((HY_ORDER, dt), lambda c, b: (0, c)),
                  pl.BlockSpec(ma.shape, fixed3, pipeline_mode=once),
                  pl.BlockSpec(mai.shape, fixed3, pipeline_mode=once),
                  pl.BlockSpec(mb.shape, fixed2, pipeline_mode=once),
                  pl.BlockSpec(mbi.shape, fixed2, pipeline_mode=once),
                  pl.BlockSpec(mirror.shape, fixed3, pipeline_mode=once)],
        out_specs=pl.BlockSpec((1, length, dt), lambda c, b: (b, 0, c)),
        out_shape=jax.ShapeDtypeStruct((bsz, length, d), BF16),
        scratch_shapes=[pltpu.VMEM((length, dt), F32), pltpu.VMEM((length, dt), F32),
                        pltpu.VMEM((FFT_WORK_ROWS, dt), F32), pltpu.VMEM((FFT_K2_USED, 2 * FFT_N1, dt), BF16)],
        compiler_params=_cparams("parallel", "arbitrary"),
        name="hyena_conv_two_stage",
    )(z, z, z, w_short, w_short, w_short, spectrum, skip, ma, mai, mb, mbi, mirror)


N_HEADS = 16
HEAD_DIM = D_MODEL // N_HEADS
HEADS_PER_STEP = 2
CTX_HEADS_PER_STEP = 16
GRID_W = 64
WIN_ROWS = 8
WIN_COLS = 16
NEG_INF = -1e30
NAT_Q_ROWS = 4
NAT_K_ROWS = NAT_Q_ROWS + WIN_ROWS - 1
ATTN_SCALE = HEAD_DIM ** -0.5
assert math.frexp(ATTN_SCALE)[0] == 0.5, "the scale is folded into q, which is exact only for a power of two"


def _dot_nt(a, b):
    return lax.dot_general(a, b, (((1,), (1,)), ((), ())), preferred_element_type=F32)


def _qkv_kernel(x_ref, g_ref, sc_ref, sh_ref, w_ref, q_ref, k_ref, v_ref):
    h = _norm_mod(x_ref[...], g_ref[...], sc_ref[0], sh_ref[0])
    qkv = _dot(h.astype(BF16), w_ref[...])
    for part, ref in enumerate((q_ref, k_ref, v_ref)):
        scale = ATTN_SCALE if part == 0 else 1.0
        for head in range(N_HEADS):
            lo = part * D_MODEL + head * HEAD_DIM
            ref[0, head] = (qkv[:, lo:lo + HEAD_DIM] * scale).astype(ref.dtype)


def _qkv_proj(x, g, mods, w_bf16, seg, first_row, n_seq, seq_len, kv_dtype):
    d = x.shape[1]
    n_ctx, smp_len = seg
    t = QKV_TILE
    first_tile = first_row // t
    per_seq = seq_len // t
    out_spec = pl.BlockSpec((1, N_HEADS, t, HEAD_DIM), lambda i: (i // per_seq, 0, i % per_seq, 0))
    shape = (n_seq, N_HEADS, seq_len, HEAD_DIM)
    return pl.pallas_call(
        _qkv_kernel,
        grid=(n_seq * per_seq,),
        in_specs=[
            pl.BlockSpec((t, d), lambda i: (i + first_tile, 0)),
            pl.BlockSpec((1, d), lambda i: (0, 0)),
            _mod_spec(1, t, n_ctx, smp_len, first_tile),
            _mod_spec(0, t, n_ctx, smp_len, first_tile),
            pl.BlockSpec(w_bf16.shape, lambda i: (0, 0)),
        ],
        out_specs=[out_spec, out_spec, out_spec],
        out_shape=[jax.ShapeDtypeStruct(shape, BF16), jax.ShapeDtypeStruct(shape, kv_dtype),
                   jax.ShapeDtypeStruct(shape, kv_dtype)],
        compiler_params=_cparams("parallel"),
        name="qkv_proj",
    )(x, g.reshape(1, d), mods, mods, w_bf16)


def _ctx_attn_kernel(q_ref, k_ref, v_ref, o_ref):
    for j in range(q_ref.shape[1]):
        q = q_ref[0, j]
        k = k_ref[0, j].astype(BF16)
        v = v_ref[0, j].astype(BF16)
        s = _dot_nt(q, k)
        p = jnp.exp(s - jnp.max(s, axis=-1, keepdims=True))
        o = _dot(p.astype(BF16), v) / jnp.sum(p, axis=-1, keepdims=True)
        o_ref[:, j * HEAD_DIM:(j + 1) * HEAD_DIM] = o.astype(o_ref.dtype)


def _ctx_attention(q, k, v):
    bsz, _, s, _ = q.shape
    hp = CTX_HEADS_PER_STEP
    spec = pl.BlockSpec((1, hp, s, HEAD_DIM), lambda b, h: (b, h, 0, 0))
    return pl.pallas_call(
        _ctx_attn_kernel,
        grid=(bsz, N_HEADS // hp),
        in_specs=[spec, spec, spec],
        out_specs=pl.BlockSpec((s, hp * HEAD_DIM), lambda b, h: (b, h)),
        out_shape=jax.ShapeDtypeStruct((bsz * s, D_MODEL), BF16),
        compiler_params=_cparams("parallel", "parallel"),
        name="ctx_attention",
    )(q, k, v)


def _rpb_toeplitz_kernel(r_ref, e_ref, o_ref):
    o_ref[...] = _dot3(r_ref[...], e_ref[...])


def _nat_bias(rpb):
    n_heads, n_r, n_c = rpb.shape
    n_cp = 32
    qc = np.arange(GRID_W)[:, None]
    kc = np.arange(GRID_W)[None, :]
    onehot = (np.clip(kc - qc + WIN_COLS - 1, 0, n_c - 1)[None] == np.arange(n_cp)[:, None, None])
    onehot = jnp.asarray(onehot.reshape(n_cp, GRID_W * GRID_W), F32)
    rows = jnp.pad(rpb.reshape(n_heads * n_r, n_c), ((0, 0), (0, n_cp - n_c)))
    toep = pl.pallas_call(
        _rpb_toeplitz_kernel,
        out_shape=jax.ShapeDtypeStruct((n_heads * n_r, GRID_W * GRID_W), F32),
        compiler_params=_cparams(),
        name="nat_bias_toeplitz",
    )(rows, onehot).reshape(n_heads, n_r, GRID_W, GRID_W)
    q_start = np.clip(qc - WIN_COLS // 2, 0, GRID_W - WIN_COLS)
    col_ok = jnp.asarray((kc >= q_start) & (kc < q_start + WIN_COLS))
    toep = jnp.where(col_ok, toep, NEG_INF)
    toep = jnp.concatenate([toep, jnp.full((n_heads, 1, GRID_W, GRID_W), NEG_INF, F32)], axis=1)
    rows_total = GRID_W
    idx = np.full((3, NAT_Q_ROWS, NAT_K_ROWS), n_r, np.int32)
    for case, r0 in enumerate((0, NAT_Q_ROWS, rows_total - NAT_Q_ROWS)):
        ks = int(np.clip(r0 - WIN_ROWS // 2, 0, rows_total - NAT_K_ROWS))
        for dr in range(NAT_Q_ROWS):
            r = r0 + dr
            rs = int(np.clip(r - WIN_ROWS // 2, 0, rows_total - WIN_ROWS))
            for dk in range(NAT_K_ROWS):
                kr = ks + dk
                if rs <= kr < rs + WIN_ROWS:
                    idx[case, dr, dk] = kr - r + WIN_ROWS - 1
    def assemble(t_ref, o_ref):
        for case in range(3):
            for dr in range(NAT_Q_ROWS):
                for dk in range(NAT_K_ROWS):
                    o_ref[0, case, dr * GRID_W:(dr + 1) * GRID_W, dk * GRID_W:(dk + 1) * GRID_W] = (
                        t_ref[0, int(idx[case, dr, dk])])

    return pl.pallas_call(
        assemble,
        grid=(n_heads,),
        in_specs=[pl.BlockSpec((1, n_r + 1, GRID_W, GRID_W), lambda h: (h, 0, 0, 0))],
        out_specs=pl.BlockSpec((1, 3, NAT_Q_ROWS * GRID_W, NAT_K_ROWS * GRID_W), lambda h: (h, 0, 0, 0)),
        out_shape=jax.ShapeDtypeStruct((n_heads, 3, NAT_Q_ROWS * GRID_W, NAT_K_ROWS * GRID_W), F32),
        compiler_params=_cparams("parallel"),
        name="nat_bias_assemble",
    )(toep)


def _nat_kernel(q_ref, k_ref, v_ref, kc_ref, vc_ref, bias_ref, o_ref):
    n_blocks = q_ref.shape[2] // (NAT_Q_ROWS * GRID_W)
    rows_total = q_ref.shape[2] // GRID_W
    nq = NAT_Q_ROWS * GRID_W
    nk = NAT_K_ROWS * GRID_W
    ctx = [(kc_ref[0, 0, j].astype(BF16), vc_ref[0, 0, j].astype(BF16)) for j in range(HEADS_PER_STEP)]

    def block(blk, carry):
        ks = jnp.clip(blk * NAT_Q_ROWS - WIN_ROWS // 2, 0, rows_total - NAT_K_ROWS)
        case = jnp.where(blk == 0, 0, jnp.where(blk == n_blocks - 1, 2, 1))
        q_rows = pl.ds(pl.multiple_of(blk * nq, nq), nq)
        k_rows = pl.ds(pl.multiple_of(ks * GRID_W, GRID_W), nk)
        for j, (k_ctx, v_ctx) in enumerate(ctx):
            q = q_ref[0, j, q_rows, :]
            s_loc = _dot_nt(q, k_ref[0, j, k_rows, :]) + bias_ref[j, case]
            s_ctx = _dot_nt(q, k_ctx)
            m = jnp.maximum(jnp.max(s_loc, axis=-1, keepdims=True), jnp.max(s_ctx, axis=-1, keepdims=True))
            p_loc = jnp.exp(s_loc - m)
            p_ctx = jnp.exp(s_ctx - m)
            denom = jnp.sum(p_loc, axis=-1, keepdims=True) + jnp.sum(p_ctx, axis=-1, keepdims=True)
            o = (_dot(p_loc.astype(BF16), v_ref[0, j, k_rows, :]) + _dot(p_ctx.astype(BF16), v_ctx)) / denom
            o_ref[q_rows, j * HEAD_DIM:(j + 1) * HEAD_DIM] = o.astype(o_ref.dtype)
        return carry

    lax.fori_loop(0, n_blocks, block, 0, unroll=2)


def _nat_attention(q, k, v, cache_k, cache_v, bias):
    bsz, _, length, _ = q.shape
    hp = HEADS_PER_STEP
    past = cache_k.shape[3]
    spec = pl.BlockSpec((1, hp, length, HEAD_DIM), lambda b, h: (b, h, 0, 0))
    cspec = pl.BlockSpec((1, 1, hp, past, HEAD_DIM), lambda b, h: (b, 0, h, 0, 0))
    return pl.pallas_call(
        _nat_kernel,
        grid=(bsz, N_HEADS // hp),
        in_specs=[spec, spec, spec, cspec, cspec,
                  pl.BlockSpec((hp,) + bias.shape[1:], lambda b, h: (h, 0, 0, 0))],
        out_specs=pl.BlockSpec((length, hp * HEAD_DIM), lambda b, h: (b, h)),
        out_shape=jax.ShapeDtypeStruct((bsz * length, D_MODEL), BF16),
        compiler_params=_cparams("parallel", "parallel"),
        name="nat_attention",
    )(q, k, v, cache_k, cache_v, bias)


def kernel(x_prompt, x_sample, cache_k, cache_v, c, c_ctx, ada_w, ada_b, norm1_g, norm2_g, normf_g, hy_w_in, hy_w_short, hy_f_w1, hy_f_b1, hy_f_freq, hy_f_w2, hy_f_b2, hy_f_w3, hy_skip, hy_w_out, na_w_qkv, na_rpb, na_w_o, moe_w_router, moe_b_router, moe_w_gate, moe_b_gate, moe_w_up, moe_b_up, moe_w_down, moe_b_down):
    b, s, d = x_prompt.shape
    bd, sd, _ = x_sample.shape
    n_ctx = b * s
    n_tok = n_ctx + bd * sd
    assert n_ctx % sd == 0 and s % QKV_TILE == 0 and n_ctx % ROW_TILE == 0 and sd % ROW_TILE == 0
    seg = (n_ctx, sd)
    x_ctx = x_prompt.reshape(n_ctx, d)
    x_smp = x_sample.reshape(bd * sd, d)
    cond = jnp.zeros((N_COND, d), F32).at[0].set(c_ctx).at[1:1 + bd].set(c)
    mods = _modulation(cond, ada_w, ada_b)

    def moe(x, i, final_norm):
        return _moe_layer(x, norm2_g[i], mods[i], moe_w_router[i], moe_b_router[i], i, moe_w_gate, moe_b_gate,
                          moe_w_up, moe_b_up, moe_w_down, moe_b_down, normf_g, seg, final_norm)

    z = _norm_proj(x_ctx, x_smp, norm1_g[0], mods[0], hy_w_in[0].astype(BF16), seg)
    fargs = (hy_f_w1[0], hy_f_b1[0], hy_f_freq[0], hy_f_w2[0], hy_f_b2[0], hy_f_w3[0])
    y_ctx = _hyena_direct(z.reshape(n_tok // s, s, 3 * d), hy_w_short[0], _hyena_filters(s, *fargs),
                          hy_skip[0], first=0, bsz=b)
    y_smp = _hyena_two_stage(z.reshape(n_tok // sd, sd, 3 * d), hy_w_short[0], _hyena_filters(sd, *fargs),
                             hy_skip[0], first=n_ctx // sd, bsz=bd)
    x = _out_proj(y_ctx.reshape(n_ctx, d), y_smp.reshape(bd * sd, d), hy_w_out[0].astype(BF16), x_ctx, x_smp,
                  mods[0], seg)
    x = moe(x, 0, False)

    w_qkv = na_w_qkv[0].astype(BF16)
    q_c, k_c, v_c = _qkv_proj(x, norm1_g[1], mods[1], w_qkv, seg, 0, b, s, F32)
    q_s, k_s, v_s = _qkv_proj(x, norm1_g[1], mods[1], w_qkv, seg, n_ctx, bd, sd, BF16)
    o_ctx = _ctx_attention(q_c, k_c, v_c)
    o_smp = _nat_attention(q_s, k_s, v_s, cache_k, cache_v, _nat_bias(na_rpb[0]))
    x = _out_proj(o_ctx, o_smp, na_w_o[0].astype(BF16), x, None, mods[1], seg)
    y_prompt, y_sample = moe(x, 1, True)

    nh, hd = k_c.shape[1], k_c.shape[3]
    return (y_prompt.reshape(b, s, d), y_sample.reshape(bd, sd, d),
            k_c.reshape(b, 1, nh, s, hd), v_c.reshape(b, 1, nh, s, hd))
```

```python
import functools
import math

import numpy as np
import jax
import jax.numpy as jnp
from jax import lax
from jax.experimental import pallas as pl
from jax.experimental.pallas import tpu as pltpu

F32 = jnp.float32
BF16 = jnp.bfloat16

D_MODEL = 1024
N_MOD = 6
RMS_EPS = 1e-6
N_EXPERTS = 32
TOP_K = 4
SWIGLU_LIMIT = 7.0
SWIGLU_ALPHA = 1.702

N_COND = 8
ROW_TILE = 512
MOE_TILE = 512
V7X_VMEM_LIMIT = 56 * 1024 * 1024


def _cparams(*sem, vmem=V7X_VMEM_LIMIT):
    return pltpu.CompilerParams(dimension_semantics=sem, vmem_limit_bytes=vmem)


def _dot(a, b):
    return jnp.dot(a, b, preferred_element_type=F32)


def _split_bf16(x):
    hi = x.astype(BF16)
    lo = (x - hi.astype(F32)).astype(BF16)
    return hi, lo


def _dot3(a, b):
    ah, al = _split_bf16(a)
    bh, bl = _split_bf16(b)
    return _dot(ah, bh) + (_dot(al, bh) + _dot(ah, bl))


def _seg_of_tile(i, tile, n_ctx, smp_len):
    ctx_tiles = n_ctx // tile
    per_smp = smp_len // tile
    return jnp.where(i < ctx_tiles, 0, 1 + (i - ctx_tiles) // per_smp)


def _norm_mod(x, g, sc, sh):
    y = x * lax.rsqrt(jnp.mean(x * x, axis=-1, keepdims=True) + RMS_EPS)
    return (y * g) * (1.0 + sc) + sh


def _mod_kernel(c_ref, w_ref, b_ref, o_ref):
    c = c_ref[...]
    a = c * jax.nn.sigmoid(c)
    o_ref[0] = _dot3(a, w_ref[0]) + b_ref[0]


def _modulation(cond, ada_w, ada_b):
    depth, d, n_out = ada_w.shape
    tn = 1536
    m = pl.pallas_call(
        _mod_kernel,
        grid=(depth, n_out // tn),
        in_specs=[
            pl.BlockSpec((N_COND, d), lambda l, j: (0, 0)),
            pl.BlockSpec((1, d, tn), lambda l, j: (l, 0, j)),
            pl.BlockSpec((1, 1, tn), lambda l, j: (l, 0, j)),
        ],
        out_specs=pl.BlockSpec((1, N_COND, tn), lambda l, j: (l, 0, j)),
        out_shape=jax.ShapeDtypeStruct((depth, N_COND, n_out), F32),
        compiler_params=_cparams("arbitrary", "arbitrary"),
        name="adaln_modulation",
    )(cond, ada_w, ada_b.reshape(depth, 1, n_out))
    m = m.reshape(depth, N_COND, N_MOD, d)
    return jnp.transpose(m, (0, 2, 1, 3)).reshape(depth, N_MOD * N_COND, 1, d)


def _mod_spec(which, tile, n_ctx, smp_len, first_tile=0):
    return pl.BlockSpec(
        (1, 1, D_MODEL),
        lambda i, *_: (which * N_COND + _seg_of_tile(i + first_tile, tile, n_ctx, smp_len), 0, 0))


def _group_specs(block_cols, ctx_tiles, stacked=False):
    base = ctx_tiles if stacked else 0
    return (pl.BlockSpec((ROW_TILE, block_cols), lambda i: (jnp.minimum(i, ctx_tiles - 1), 0)),
            pl.BlockSpec((ROW_TILE, block_cols), lambda i: (jnp.maximum(i - ctx_tiles, 0) + base, 0)))


def _pick_group(a_ref, b_ref, ctx_tiles):
    return jnp.where(pl.program_id(0) < ctx_tiles, a_ref[...], b_ref[...])


def _norm_proj_kernel(xa_ref, xb_ref, g_ref, sc_ref, sh_ref, w_ref, o_ref, *, ctx_tiles):
    h = _norm_mod(_pick_group(xa_ref, xb_ref, ctx_tiles), g_ref[...], sc_ref[0], sh_ref[0])
    o_ref[...] = _dot(h.astype(BF16), w_ref[...]).astype(o_ref.dtype)


def _norm_proj(x_ctx, x_smp, g, mods, w_bf16, seg, out_dtype=F32):
    d = x_ctx.shape[1]
    n = x_ctx.shape[0] + x_smp.shape[0]
    n_out = w_bf16.shape[1]
    n_ctx, smp_len = seg
    ctx_tiles = n_ctx // ROW_TILE
    return pl.pallas_call(
        functools.partial(_norm_proj_kernel, ctx_tiles=ctx_tiles),
        grid=(n // ROW_TILE,),
        in_specs=[
            *_group_specs(d, ctx_tiles),
            pl.BlockSpec((1, d), lambda i: (0, 0)),
            _mod_spec(1, ROW_TILE, n_ctx, smp_len),
            _mod_spec(0, ROW_TILE, n_ctx, smp_len),
            pl.BlockSpec((d, n_out), lambda i: (0, 0)),
        ],
        out_specs=pl.BlockSpec((ROW_TILE, n_out), lambda i: (i, 0)),
        out_shape=jax.ShapeDtypeStruct((n, n_out), out_dtype),
        compiler_params=_cparams("parallel"),
        name="norm_proj",
    )(x_ctx, x_smp, g.reshape(1, d), mods, mods, w_bf16)


def _out_proj_kernel(ya_ref, yb_ref, w_ref, xa_ref, xb_ref, gate_ref, o_ref, *, ctx_tiles):
    y = _pick_group(ya_ref, yb_ref, ctx_tiles)
    o_ref[...] = _pick_group(xa_ref, xb_ref, ctx_tiles) + gate_ref[0] * _dot(y, w_ref[...])


def _out_proj(y_ctx, y_smp, w_bf16, x_ctx, x_smp, mods, seg):
    d = x_ctx.shape[1]
    n_ctx, smp_len = seg
    n = n_ctx + y_smp.shape[0]
    ctx_tiles = n_ctx // ROW_TILE
    stacked = x_smp is None
    x_smp = x_ctx if stacked else x_smp
    return pl.pallas_call(
        functools.partial(_out_proj_kernel, ctx_tiles=ctx_tiles),
        grid=(n // ROW_TILE,),
        in_specs=[
            *_group_specs(d, ctx_tiles),
            pl.BlockSpec((d, d), lambda i: (0, 0)),
            *_group_specs(d, ctx_tiles, stacked),
            _mod_spec(2, ROW_TILE, n_ctx, smp_len),
        ],
        out_specs=pl.BlockSpec((ROW_TILE, d), lambda i: (i, 0)),
        out_shape=jax.ShapeDtypeStruct((n, d), F32),
        compiler_params=_cparams("parallel"),
        name="out_proj",
    )(y_ctx, y_smp, w_bf16, x_ctx, x_smp, mods)


LANES = 128
ROW_SUBLANES = D_MODEL // LANES


def _store_row_tiles(ref, x):
    t = x.shape[0]
    for s in range(ROW_SUBLANES):
        ref[pl.ds(s, t, stride=ROW_SUBLANES), :] = x[:, s * LANES:(s + 1) * LANES]


def _load_row_tiles(ref, t):
    return jnp.concatenate([ref[pl.ds(s, t, stride=ROW_SUBLANES), :] for s in range(ROW_SUBLANES)], axis=1)


def _columns(cols):
    t = cols[0].shape[0]
    lane = lax.broadcasted_iota(jnp.int32, (t, len(cols)), 1)
    out = jnp.broadcast_to(cols[-1], (t, len(cols)))
    for k in range(len(cols) - 2, -1, -1):
        out = jnp.where(lane == k, cols[k], out)
    return out


def _router_kernel(x_ref, g_ref, sc_ref, sh_ref, wr_ref, br_ref, tri_ref,
                   h_ref, e_ref, gate_ref, rank_ref, cnt_ref, run_ref):
    i = pl.program_id(0)

    @pl.when(i == 0)
    def _():
        run_ref[...] = jnp.zeros_like(run_ref)

    h = _norm_mod(x_ref[...], g_ref[...], sc_ref[0], sh_ref[0])
    _store_row_tiles(h_ref, h)
    logits = _dot3(h, wr_ref[...]) + br_ref[...]
    lane = lax.broadcasted_iota(jnp.int32, logits.shape, 1).astype(F32)
    work = logits
    vals, idxs, hots = [], [], []
    for _ in range(TOP_K):
        m = jnp.max(work, axis=-1, keepdims=True)
        idx = jnp.min(jnp.where(work == m, lane, float(N_EXPERTS)), axis=-1, keepdims=True)
        hot = lane == idx
        vals.append(m)
        idxs.append(idx)
        hots.append(hot)
        work = jnp.where(hot, -jnp.inf, work)
    ex = [jnp.exp(v - vals[0]) for v in vals]
    denom = ex[0] + ex[1] + ex[2] + ex[3]
    gate_ref[...] = _columns([e / denom for e in ex])
    e_ref[...] = _columns(idxs).astype(jnp.int32)

    chosen = (hots[0] | hots[1] | hots[2] | hots[3]).astype(F32)
    before = run_ref[...] + _dot(tri_ref[...], chosen.astype(BF16))
    ranks = [jnp.sum(jnp.where(hot, before, 0.0), axis=-1, keepdims=True) for hot in hots]
    rank_ref[...] = _columns(ranks).astype(jnp.int32)
    run_ref[...] += jnp.sum(chosen, axis=0, keepdims=True)
    cnt_ref[...] = run_ref[...].astype(jnp.int32)


def _router(x, g, mods, w_r, b_r, seg):
    n, d = x.shape
    n_ctx, smp_len = seg
    t = ROW_TILE
    tri = jnp.asarray(np.tril(np.ones((t, t), np.float32), -1), BF16)
    tok4 = lambda i: (i, 0)
    return pl.pallas_call(
        _router_kernel,
        grid=(n // t,),
        in_specs=[
            pl.BlockSpec((t, d), lambda i: (i, 0)),
            pl.BlockSpec((1, d), lambda i: (0, 0)),
            _mod_spec(4, t, n_ctx, smp_len),
            _mod_spec(3, t, n_ctx, smp_len),
            pl.BlockSpec((d, N_EXPERTS), lambda i: (0, 0)),
            pl.BlockSpec((1, N_EXPERTS), lambda i: (0, 0)),
            pl.BlockSpec((t, t), lambda i: (0, 0)),
        ],
        out_specs=[
            pl.BlockSpec((t * ROW_SUBLANES, LANES), tok4),
            pl.BlockSpec((t, TOP_K), tok4),
            pl.BlockSpec((t, TOP_K), tok4),
            pl.BlockSpec((t, TOP_K), tok4),
            pl.BlockSpec((1, N_EXPERTS), lambda i: (0, 0)),
        ],
        out_shape=[
            jax.ShapeDtypeStruct((n * ROW_SUBLANES, LANES), F32),
            jax.ShapeDtypeStruct((n, TOP_K), jnp.int32),
            jax.ShapeDtypeStruct((n, TOP_K), F32),
            jax.ShapeDtypeStruct((n, TOP_K), jnp.int32),
            jax.ShapeDtypeStruct((1, N_EXPERTS), jnp.int32),
        ],
        scratch_shapes=[pltpu.VMEM((1, N_EXPERTS), F32)],
        compiler_params=_cparams("arbitrary"),
        name="moe_router",
    )(x, g.reshape(1, d), mods, mods, w_r, b_r.reshape(1, N_EXPERTS), tri)


def _tile_copy(src_ref, src_row, dst_ref, dst_row, sem):
    rows = lambda r: pl.ds(pl.multiple_of(r, ROW_SUBLANES), ROW_SUBLANES)
    return pltpu.make_async_copy(src_ref.at[rows(src_row), :], dst_ref.at[rows(dst_row), :], sem)


def _tiles_wait(src_ref, dst_ref, n_tiles, sem):
    rows = pl.ds(0, n_tiles * ROW_SUBLANES)
    pltpu.make_async_copy(src_ref.at[rows, :], dst_ref.at[rows, :], sem).wait()


ENTRIES_PER_TILE = ROW_TILE * TOP_K
DMA_PRIORITIES = 2


def _dispatch_kernel(fill_start_ref, fill_len_ref, na_ref, dest_ref, h_ref, xs_ref, hbuf, zero_ref, sem):
    i = pl.program_id(0)
    last = pl.num_programs(0) - 1
    slot = lax.rem(i, 2)
    hbuf[slot] = h_ref[...]

    def issue(t, carry):
        for k in range(TOP_K):
            _tile_copy(hbuf.at[slot], t * ROW_SUBLANES, xs_ref, dest_ref[t * TOP_K + k],
                       sem.at[slot]).start(priority=k % DMA_PRIORITIES)
        return carry

    lax.fori_loop(0, ROW_TILE, issue, 0, unroll=8)

    @pl.when(i >= 1)
    def _():
        _tiles_wait(xs_ref, xs_ref, ENTRIES_PER_TILE, sem.at[1 - slot])

    @pl.when(i == last)
    def _():
        _tiles_wait(xs_ref, xs_ref, ENTRIES_PER_TILE, sem.at[slot])
        zero_ref[...] = jnp.zeros_like(zero_ref)

        def pieces(act):
            for e in range(N_EXPERTS):
                count = fill_len_ref[e]
                done = jnp.int32(0)
                for bit in reversed(range(MOE_TILE.bit_length() - 1)):
                    size = 1 << bit
                    rows = size * ROW_SUBLANES
                    first = pl.multiple_of((fill_start_ref[e] + done) * ROW_SUBLANES, ROW_SUBLANES)
                    copy = pltpu.make_async_copy(zero_ref.at[pl.ds(0, rows), :], xs_ref.at[pl.ds(first, rows), :],
                                                 sem.at[0])
                    pl.when((count & size) != 0)(functools.partial(act, copy))
                    done = done + (count & size)

        pieces(lambda copy: copy.start())
        pieces(lambda copy: copy.wait())

        block_rows = MOE_TILE * ROW_SUBLANES

        def block_copy(blk):
            return pltpu.make_async_copy(
                zero_ref, xs_ref.at[pl.ds(pl.multiple_of(blk * block_rows, block_rows), block_rows), :], sem.at[0])

        n_blocks = xs_ref.shape[0] // block_rows

        def tail(blk, carry):
            block_copy(blk).start()
            return carry

        lax.fori_loop(na_ref[0], n_blocks, tail, 0)

        def tail_drain(blk, carry):
            block_copy(0).wait()
            return carry

        lax.fori_loop(na_ref[0], n_blocks, tail_drain, 0)


def _dispatch(h_tiles, dest_rows, fill_start, fill_len, n_active, cap):
    n = h_tiles.shape[0] // ROW_SUBLANES
    grid_spec = pltpu.PrefetchScalarGridSpec(
        num_scalar_prefetch=3,
        grid=(n // ROW_TILE,),
        in_specs=[
            pl.BlockSpec((ENTRIES_PER_TILE,), lambda i, *_: (i,), memory_space=pltpu.SMEM),
            pl.BlockSpec((ROW_TILE * ROW_SUBLANES, LANES), lambda i, *_: (i, 0)),
        ],
        out_specs=pl.BlockSpec(memory_space=pl.ANY),
        scratch_shapes=[pltpu.VMEM((2, ROW_TILE * ROW_SUBLANES, LANES), F32),
                        pltpu.VMEM((MOE_TILE * ROW_SUBLANES, LANES), F32), pltpu.SemaphoreType.DMA((2,))],
    )
    return pl.pallas_call(
        _dispatch_kernel,
        grid_spec=grid_spec,
        out_shape=jax.ShapeDtypeStruct((cap * ROW_SUBLANES, LANES), F32),
        compiler_params=_cparams("arbitrary"),
        name="moe_dispatch",
    )(fill_start, fill_len, n_active, dest_rows, h_tiles)


def _ffn_kernel(be_ref, na_ref, wslot_ref, next_e_ref, xs_ref, wg_ref, bg_ref, wu_ref, bu_ref, wd_ref, bd_ref,
                ys_ref, wbuf, wg_s, wu_s, wd_s, wsem, *, layer):
    i = pl.program_id(0)
    n_active = na_ref[0]
    tm = xs_ref.shape[0] // ROW_SUBLANES
    changed = jnp.logical_or(i == 0, be_ref[i] != be_ref[jnp.maximum(i - 1, 0)])
    fresh = jnp.logical_and(i < n_active, changed)

    def weight_copies(expert, b):
        return [pltpu.make_async_copy(w_ref.at[layer, expert], wbuf.at[b, m], wsem.at[b])
                for m, w_ref in enumerate((wg_ref, wu_ref, wd_ref))]

    @pl.when(jnp.logical_and(i == 0, n_active > 0))
    def _():
        for copy in weight_copies(be_ref[0], wslot_ref[0]):
            copy.start()

    @pl.when(fresh)
    def _():
        b = wslot_ref[i]
        for copy in weight_copies(be_ref[i], b):
            copy.wait()
        wg_s[...] = wbuf[b, 0].astype(BF16)
        wu_s[...] = wbuf[b, 1].astype(BF16)
        wd_s[...] = wbuf[b, 2].astype(BF16)
        upcoming = next_e_ref[i]

        @pl.when(upcoming >= 0)
        def _():
            for copy in weight_copies(upcoming, 1 - b):
                copy.start()

    @pl.when(i < n_active)
    def _():
        x = _load_row_tiles(xs_ref, tm).astype(BF16)
        g = jnp.minimum(_dot(x, wg_s[...]) + bg_ref[0, 0], SWIGLU_LIMIT)
        u = jnp.clip(_dot(x, wu_s[...]) + bu_ref[0, 0], -SWIGLU_LIMIT, SWIGLU_LIMIT)
        a = g * jax.nn.sigmoid(SWIGLU_ALPHA * g) * (u + 1.0)
        _store_row_tiles(ys_ref, _dot(a.astype(BF16), wd_s[...]) + bd_ref[0, 0])

    @pl.when(i >= na_ref[0])
    def _():
        ys_ref[...] = jnp.zeros_like(ys_ref)


def _expert_ffn(xs, block_e, n_active, weight_slot, next_expert, layer, w_g, b_g, w_u, b_u, w_d, b_d):
    d = D_MODEL
    depth, ne, _, f = w_g.shape
    assert f == d
    nb = block_e.shape[0]
    block = (MOE_TILE * ROW_SUBLANES, LANES)
    bmap = lambda i, be, *_: (layer, be[i], 0, 0)
    hbm = pl.BlockSpec(memory_space=pl.ANY)
    grid_spec = pltpu.PrefetchScalarGridSpec(
        num_scalar_prefetch=4,
        grid=(nb,),
        in_specs=[
            pl.BlockSpec(block, lambda i, be, na, *_: (jnp.minimum(i, na[0] - 1), 0)),
            hbm, pl.BlockSpec((1, 1, 1, f), bmap),
            hbm, pl.BlockSpec((1, 1, 1, f), bmap),
            hbm, pl.BlockSpec((1, 1, 1, d), bmap),
        ],
        out_specs=pl.BlockSpec(block, lambda i, *_: (i, 0)),
        scratch_shapes=[pltpu.VMEM((2, 3, d, f), F32),
                        pltpu.VMEM((d, f), BF16), pltpu.VMEM((d, f), BF16), pltpu.VMEM((f, d), BF16),
                        pltpu.SemaphoreType.DMA((2,))],
    )
    return pl.pallas_call(
        functools.partial(_ffn_kernel, layer=layer),
        grid_spec=grid_spec,
        out_shape=jax.ShapeDtypeStruct(xs.shape, F32),
        compiler_params=_cparams("arbitrary"),
        name="moe_expert_ffn",
    )(block_e, n_active, weight_slot, next_expert, xs,
      w_g, b_g.reshape(depth, ne, 1, f), w_u, b_u.reshape(depth, ne, 1, f), w_d, b_d.reshape(depth, ne, 1, d))


def _combine_kernel(dest_ref, dest_next_ref, x_ref, gate_ref, g2_ref, gf_ref, ys_ref, o_ref, buf, sem, *,
                    final_norm):
    i = pl.program_id(0)
    slot = lax.rem(i, 2)
    t = x_ref.shape[0]
    plane = t * ROW_SUBLANES

    def start_gather(idx_ref, b):
        def body(r, carry):
            for k in range(TOP_K):
                _tile_copy(ys_ref, idx_ref[r * TOP_K + k], buf.at[b], k * plane + r * ROW_SUBLANES,
                           sem.at[b]).start(priority=k % DMA_PRIORITIES)
            return carry

        lax.fori_loop(0, t, body, 0, unroll=8)

    @pl.when(i == 0)
    def _():
        start_gather(dest_ref, 0)

    @pl.when(i + 1 < pl.num_programs(0))
    def _():
        start_gather(dest_next_ref, 1 - slot)

    _tiles_wait(ys_ref, buf.at[slot], TOP_K * t, sem.at[slot])
    gates = gate_ref[...]
    acc = None
    for k in range(TOP_K):
        term = gates[:, k:k + 1] * _load_row_tiles(buf.at[slot, pl.ds(k * plane, plane), :], t)
        acc = term if acc is None else acc + term
    y = x_ref[...] + g2_ref[0] * acc
    if final_norm:
        y = y * lax.rsqrt(jnp.mean(y * y, axis=-1, keepdims=True) + RMS_EPS) * gf_ref[...]
    o_ref[...] = y


def _combine(x, ys, dest_rows, gates, mods, normf_g, seg, final_norm, first_row=0, n_rows=None):
    n, d = x.shape
    n_rows = n if n_rows is None else n_rows
    n_ctx, smp_len = seg
    t = ROW_TILE
    first = first_row // t
    steps = n_rows // t
    rows = lambda i: (i + first, 0)
    return pl.pallas_call(
        functools.partial(_combine_kernel, final_norm=final_norm),
        grid=(steps,),
        in_specs=[
            pl.BlockSpec((ENTRIES_PER_TILE,), lambda i: (i + first,), memory_space=pltpu.SMEM),
            pl.BlockSpec((ENTRIES_PER_TILE,), lambda i: (jnp.minimum(i + 1, steps - 1) + first,),
                         memory_space=pltpu.SMEM),
            pl.BlockSpec((t, d), rows),
            pl.BlockSpec((t, TOP_K), rows),
            _mod_spec(5, t, n_ctx, smp_len, first),
            pl.BlockSpec((1, d), lambda i: (0, 0)),
            pl.BlockSpec(memory_space=pl.ANY),
        ],
        out_specs=pl.BlockSpec((t, d), lambda i: (i, 0)),
        out_shape=jax.ShapeDtypeStruct((n_rows, d), F32),
        scratch_shapes=[pltpu.VMEM((2, TOP_K * t * ROW_SUBLANES, LANES), F32), pltpu.SemaphoreType.DMA((2,))],
        compiler_params=_cparams("arbitrary"),
        name="moe_combine",
    )(dest_rows, dest_rows, x, gates, mods, normf_g.reshape(1, d), ys)


def _moe_layer(x, norm_g, mods, w_r, b_r, layer, w_g, b_g, w_u, b_u, w_d, b_d, normf_g, seg, final_norm):
    n, d = x.shape
    nk = n * TOP_K
    h, top_e, gates, rank, counts = _router(x, norm_g, mods, w_r, b_r, seg)
    counts = counts[0]
    padded = (counts + MOE_TILE - 1) // MOE_TILE * MOE_TILE
    pad_end = jnp.cumsum(padded)
    pad_start = pad_end - padded
    cap = nk + N_EXPERTS * MOE_TILE
    nb = cap // MOE_TILE
    experts = jnp.arange(N_EXPERTS, dtype=jnp.int32)
    start_of = jnp.sum(jnp.where(top_e[..., None] == experts, pad_start, 0), axis=-1)
    dest_rows = ((start_of + rank) * ROW_SUBLANES).reshape(nk).astype(jnp.int32)
    blk_start = jnp.arange(nb, dtype=jnp.int32) * MOE_TILE
    block_e = jnp.minimum(jnp.sum(blk_start[:, None] >= pad_end[None, :], axis=1), N_EXPERTS - 1).astype(jnp.int32)
    n_active = (pad_end[-1] // MOE_TILE).astype(jnp.int32).reshape(1)
    xs = _dispatch(h, dest_rows, (pad_start + counts).astype(jnp.int32), (padded - counts).astype(jnp.int32),
                   n_active, cap)
    changes = jnp.concatenate([jnp.zeros((1,), jnp.int32), (block_e[1:] != block_e[:-1]).astype(jnp.int32)])
    weight_slot = (jnp.cumsum(changes) % 2).astype(jnp.int32)
    later = (experts[None, :] > experts[:, None]) & (counts[None, :] > 0)
    next_of = jnp.min(jnp.where(later, experts[None, :], N_EXPERTS), axis=1)
    next_expert = jnp.where(next_of < N_EXPERTS, next_of, -1).astype(jnp.int32)[block_e]
    ys = _expert_ffn(xs, block_e, n_active, weight_slot, next_expert, layer, w_g, b_g, w_u, b_u, w_d, b_d)
    if not final_norm:
        return _combine(x, ys, dest_rows, gates, mods, normf_g, seg, False)
    n_ctx = seg[0]
    return (_combine(x, ys, dest_rows, gates, mods, normf_g, seg, True, 0, n_ctx),
            _combine(x, ys, dest_rows, gates, mods, normf_g, seg, True, n_ctx, n - n_ctx))


HY_ORDER = 2
HY_BANDS = 16
HY_EMB = 1 + 2 * HY_BANDS
HY_FFN = 64
HY_MIN_DECAY = math.log(1e-2) / 1.5
HY_MAX_DECAY = math.log(1e-2) / 0.3
HY_EMB_PAD = 64


def _filter_mlp_kernel(z_ref, t_ref, w1_ref, b1_ref, fr_ref, w2_ref, b2_ref, w3_ref, dl_ref, o_ref, h_ref):
    @pl.when(pl.program_id(1) == 0)
    def _():
        fr = fr_ref[...]
        h = jnp.sin(fr * (_dot3(z_ref[...], w1_ref[...]) + b1_ref[...]))
        h_ref[...] = jnp.sin(fr * (_dot3(h, w2_ref[...]) + b2_ref[...]))

    o_ref[...] = _dot3(h_ref[...], w3_ref[...]) * jnp.exp(-t_ref[...] * dl_ref[...])


def _hyena_filters(length, w1, b1, freq, w2, b2, w3):
    t = jnp.linspace(0.0, 1.0, length, dtype=F32)[:, None]
    ang = (2.0 * math.pi / length) * jnp.arange(length, dtype=F32)[:, None]
    bands = jnp.linspace(1e-4, HY_BANDS - 1, HY_BANDS, dtype=F32)[None, :]
    z = jnp.concatenate([t, jnp.cos(bands * ang), -jnp.sin(bands * ang)], axis=-1)
    z = jnp.pad(z, ((0, 0), (0, HY_EMB_PAD - HY_EMB)))
    w1p = jnp.pad(w1, ((0, HY_EMB_PAD - HY_EMB), (0, 0)))
    n_out = w3.shape[1]
    deltas = jnp.abs(jnp.linspace(HY_MIN_DECAY, HY_MAX_DECAY, D_MODEL, dtype=F32))
    deltas = jnp.tile(deltas, n_out // D_MODEL)[None, :]
    tl, tn = min(512, length), 2048
    row = lambda i, j: (i, 0)
    fixed = lambda i, j: (0, 0)
    return pl.pallas_call(
        _filter_mlp_kernel,
        grid=(length // tl, n_out // tn),
        in_specs=[
            pl.BlockSpec((tl, HY_EMB_PAD), row),
            pl.BlockSpec((tl, 1), row),
            pl.BlockSpec((HY_EMB_PAD, HY_FFN), fixed),
            pl.BlockSpec((1, HY_FFN), fixed),
            pl.BlockSpec((1, HY_FFN), fixed),
            pl.BlockSpec((HY_FFN, HY_FFN), fixed),
            pl.BlockSpec((1, HY_FFN), fixed),
            pl.BlockSpec((HY_FFN, tn), lambda i, j: (0, j)),
            pl.BlockSpec((1, tn), lambda i, j: (0, j)),
        ],
        out_specs=pl.BlockSpec((tl, tn), lambda i, j: (i, j)),
        out_shape=jax.ShapeDtypeStruct((length, n_out), F32),
        scratch_shapes=[pltpu.VMEM((tl, HY_FFN), F32)],
        compiler_params=_cparams("parallel", "arbitrary"),
        name="hyena_filter_mlp",
    )(z, t, w1p, b1.reshape(1, -1), freq.reshape(1, -1), w2, b2.reshape(1, -1), w3, deltas)


def _short_conv(z, w):
    length = z.shape[0]
    row = lax.broadcasted_iota(jnp.int32, z.shape, 0)
    prev = jnp.where(row == 0, 0.0, pltpu.roll(z, 1, 0))
    nxt = jnp.where(row == length - 1, 0.0, pltpu.roll(z, length - 1, 0))
    return (prev * w[0:1] + z * w[1:2]) + nxt * w[2:3]


def _filter_halves(hf_ref, hb_ref):
    hf = hf_ref[...]
    hb = hb_ref[...]
    hb = jnp.where(lax.broadcasted_iota(jnp.int32, hb.shape, 0) == 0, 0.0, hb)
    norm = jnp.sum(jnp.abs(hf), axis=0, keepdims=True) + jnp.sum(jnp.abs(hb), axis=0, keepdims=True)
    return hf + hb, hf - hb, 1.0 / norm


def _direct_dft_tables(length):
    n_fft = 2 * length
    n_freq = length + 1
    mf = -(-n_freq // 16) * 16
    k = np.arange(mf)[:, None]
    n = np.arange(length)[None, :]
    ang = 2.0 * np.pi * ((k * n) % n_fft) / n_fft
    valid = k < n_freq
    cos = np.where(valid, np.cos(ang), 0.0)
    msin = np.where(valid, -np.sin(ang), 0.0)
    weight = np.where((k == 0) | (k == length), 1.0, 2.0) * valid / n_fft
    fwd = np.concatenate([cos, msin], axis=0)
    inv = np.concatenate([weight * cos, weight * msin], axis=0).T
    return jnp.asarray(fwd, BF16), jnp.asarray(inv, BF16), mf


def _hyena_direct_kernel(zv_ref, z1_ref, z2_ref, wv_ref, w1_ref, w2_ref,
                         hf0_ref, hf1_ref, hb0_ref, hb1_ref, skip_ref, fw_ref, iv_ref,
                         o_ref, kr_ref, ki_ref):
    mf = fw_ref.shape[0] // 2
    dt = o_ref.shape[-1]

    @pl.when(pl.program_id(1) == 0)
    def _():
        for o, (hf_ref, hb_ref) in enumerate(((hf0_ref, hb0_ref), (hf1_ref, hb1_ref))):
            hs, hd, inv_norm = _filter_halves(hf_ref, hb_ref)
            spec = _dot(fw_ref[...], jnp.concatenate([hs, hd], axis=1).astype(BF16))
            kr_ref[o] = spec[:mf, :dt] * inv_norm
            ki_ref[o] = spec[mf:, dt:] * inv_norm

    y = _short_conv(zv_ref[0], wv_ref[...])
    for o, (z_ref, w_ref) in enumerate(((z1_ref, w1_ref), (z2_ref, w2_ref))):
        spec = _dot(fw_ref[...], y.astype(BF16))
        yr, yi = spec[:mf], spec[mf:]
        kr, ki = kr_ref[o], ki_ref[o]
        prod = jnp.concatenate([yr * kr - yi * ki, yr * ki + yi * kr], axis=0)
        yc = _dot(iv_ref[...], prod.astype(BF16))
        y = _short_conv(z_ref[0], w_ref[...]) * (yc + y * skip_ref[o:o + 1])
    o_ref[0] = y.astype(o_ref.dtype)


def _hyena_direct(z, w_short, hfilt, skip, first=0, bsz=None):
    _, length, d3 = z.shape
    bsz = z.shape[0] if bsz is None else bsz
    d = d3 // 3
    dt = 512
    nct = d // dt
    fwd, inv, mf = _direct_dft_tables(length)
    zspec = lambda part: pl.BlockSpec((1, length, dt), lambda c, b: (b + first, 0, part * nct + c))
    wspec = lambda part: pl.BlockSpec((3, dt), lambda c, b: (0, part * nct + c))
    hspec = lambda direction, order: pl.BlockSpec(
        (length, dt), lambda c, b: (0, (direction * HY_ORDER + order) * nct + c))
    fixed = lambda c, b: (0, 0)
    return pl.pallas_call(
        _hyena_direct_kernel,
        grid=(nct, bsz),
        in_specs=[zspec(0), zspec(1), zspec(2), wspec(0), wspec(1), wspec(2),
                  hspec(0, 0), hspec(0, 1), hspec(1, 0), hspec(1, 1),
                  pl.BlockSpec((HY_ORDER, dt), lambda c, b: (0, c)),
                  pl.BlockSpec(fwd.shape, fixed), pl.BlockSpec(inv.shape, fixed)],
        out_specs=pl.BlockSpec((1, length, dt), lambda c, b: (b, 0, c)),
        out_shape=jax.ShapeDtypeStruct((bsz, length, d), BF16),
        scratch_shapes=[pltpu.VMEM((HY_ORDER, mf, dt), F32), pltpu.VMEM((HY_ORDER, mf, dt), F32)],
        compiler_params=_cparams("parallel", "arbitrary"),
        name="hyena_conv_direct",
    )(z, z, z, w_short, w_short, w_short, hfilt, hfilt, hfilt, hfilt, skip, fwd, inv)


FFT_N1 = 64
FFT_N2 = 128
FFT_LANES = 128
FFT_A_PITCH = 2 * FFT_N2 + 8
FFT_U_PITCH = 2 * FFT_N1 + 8
FFT_GROUP = 2
FFT_UNROLL = 32


def _two_stage_tables():
    n_fft = FFT_N1 * FFT_N2
    half = FFT_N2 // 2
    n1 = np.arange(FFT_N1)[:, None, None]
    k2 = np.arange(FFT_N2)[None, :, None]
    n2 = np.arange(half)[None, None, :]
    ang = 2.0 * np.pi * ((k2 * (n1 + FFT_N1 * n2)) % n_fft) / n_fft
    stage_a = np.concatenate([np.cos(ang), -np.sin(ang)], axis=1)
    stage_a_inv = np.transpose(stage_a, (0, 2, 1)) / n_fft
    k1 = np.arange(FFT_N1)[:, None]
    m1 = np.arange(FFT_N1)[None, :]
    phi = 2.0 * np.pi * ((k1 * m1) % FFT_N1) / FFT_N1
    c, s = np.cos(phi), np.sin(phi)
    stage_b = np.block([[c, s], [-s, c]])
    stage_b_inv = np.block([[c, -s], [s, c]])
    theta = 2.0 * np.pi * np.arange(FFT_N1) / FFT_N1
    mirror = np.broadcast_to(np.stack([np.cos(theta), np.sin(theta)])[:, :, None], (2, FFT_N1, FFT_LANES))
    return (*(jnp.asarray(t, BF16) for t in (stage_a, stage_a_inv, stage_b, stage_b_inv)),
            jnp.asarray(mirror, F32))


def _fft_stage_a(y_ref, ma_ref, a_ref):
    half = FFT_N2 // 2

    def body(n1, carry):
        slab = y_ref[pl.ds(n1, half, stride=FFT_N1), :]
        a_ref[pl.ds(pl.multiple_of(n1 * FFT_A_PITCH, 8), 2 * FFT_N2), :] = _dot(ma_ref[n1], slab.astype(BF16))
        return carry

    lax.fori_loop(0, FFT_N1, body, 0, unroll=FFT_UNROLL)


FFT_K2_USED = FFT_N2 // 2 + FFT_GROUP


def _fft_stage_b(a_ref, mb_ref, consume):
    def body(j, carry):
        k2 = j * FFT_GROUP
        cols = []
        for g in range(FFT_GROUP):
            re = a_ref[pl.ds(k2 + g, FFT_N1, stride=FFT_A_PITCH), :]
            im = a_ref[pl.ds(FFT_N2 + k2 + g, FFT_N1, stride=FFT_A_PITCH), :]
            cols.append(jnp.concatenate([re, im], axis=0))
        x = _dot(mb_ref[...], jnp.concatenate(cols, axis=1).astype(BF16))
        for g in range(FFT_GROUP):
            consume(k2 + g, x[:, g * FFT_LANES:(g + 1) * FFT_LANES])
        return carry

    lax.fori_loop(0, FFT_K2_USED // FFT_GROUP, body, 0, unroll=FFT_K2_USED // FFT_GROUP)


def _fft_inverse(z_ref, mbi_ref, mai_ref, tw_ref, u_ref, out_ref):
    half = FFT_N2 // 2

    def store_u(k2, u):
        u_ref[pl.ds(pl.multiple_of(k2 * FFT_U_PITCH, 8), 2 * FFT_N1), :] = u

    def stage_b(j, mirror):
        k2 = j * FFT_GROUP
        rhs = jnp.concatenate([z_ref[k2 + g] for g in range(FFT_GROUP)], axis=1)
        u = _dot(mbi_ref[...], rhs)
        for g in range(FFT_GROUP):
            ug = u[:, g * FFT_LANES:(g + 1) * FFT_LANES]
            store_u(k2 + g, ug)
            if mirror[g]:
                ur, ui = ug[:FFT_N1], ug[FFT_N1:]
                c, s = tw_ref[0], tw_ref[1]
                store_u(FFT_N2 - (k2 + g), jnp.concatenate([c * ur - s * ui, -(c * ui + s * ur)], axis=0))

    assert FFT_GROUP == 2
    stage_b(0, (False, True))

    def mirrored(j, carry):
        stage_b(j, (True, True))
        return carry

    lax.fori_loop(1, half // FFT_GROUP, mirrored, 0, unroll=half // FFT_GROUP - 1)
    store_u(half, _dot(mbi_ref[...], z_ref[half]))

    def stage_a(n1, carry):
        re = u_ref[pl.ds(n1, FFT_N2, stride=FFT_U_PITCH), :]
        im = u_ref[pl.ds(FFT_N1 + n1, FFT_N2, stride=FFT_U_PITCH), :]
        rhs = jnp.concatenate([re, im], axis=0).astype(BF16)
        out_ref[pl.ds(n1, half, stride=FFT_N1), :] = _dot(mai_ref[n1], rhs)
        return carry

    lax.fori_loop(0, FFT_N1, stage_a, 0, unroll=FFT_UNROLL)


FFT_WORK_ROWS = max(FFT_N1 * FFT_A_PITCH, FFT_N2 * FFT_U_PITCH)


def _hyena_spectrum_kernel(hf0_ref, hf1_ref, hb0_ref, hb1_ref, ma_ref, mb_ref, k_ref, y_ref, a_ref):
    for o, (hf_ref, hb_ref) in enumerate(((hf0_ref, hb0_ref), (hf1_ref, hb1_ref))):
        hs, hd, inv_norm = _filter_halves(hf_ref, hb_ref)

        def keep_real(k2, x):
            k_ref[o, k2, 0:FFT_N1, :] = (x[:FFT_N1] * inv_norm).astype(k_ref.dtype)

        def keep_imag(k2, x):
            k_ref[o, k2, FFT_N1:, :] = (x[FFT_N1:] * inv_norm).astype(k_ref.dtype)

        for part, keep in ((hs, keep_real), (hd, keep_imag)):
            y_ref[...] = part
            _fft_stage_a(y_ref, ma_ref, a_ref)
            _fft_stage_b(a_ref, mb_ref, keep)


def _hyena_two_stage_kernel(zv_ref, z1_ref, z2_ref, wv_ref, w1_ref, w2_ref, k_ref, skip_ref,
                            ma_ref, mai_ref, mb_ref, mbi_ref, tw_ref, o_ref, y_ref, c_ref, a_ref, z_ref):
    y_ref[...] = _short_conv(zv_ref[0], wv_ref[...])
    for o, (g_ref, w_ref) in enumerate(((z1_ref, w1_ref), (z2_ref, w2_ref))):
        def multiply(k2, x):
            k = k_ref[o, k2].astype(F32)
            xr, xi = x[:FFT_N1], x[FFT_N1:]
            kr, ki = k[:FFT_N1], k[FFT_N1:]
            z_ref[k2] = jnp.concatenate([xr * kr - xi * ki, xr * ki + xi * kr], axis=0).astype(z_ref.dtype)

        _fft_stage_a(y_ref, ma_ref, a_ref)
        _fft_stage_b(a_ref, mb_ref, multiply)
        _fft_inverse(z_ref, mbi_ref, mai_ref, tw_ref, a_ref, c_ref)
        y = _short_conv(g_ref[0], w_ref[...]) * (c_ref[...] + y_ref[...] * skip_ref[o:o + 1])
        if o + 1 < HY_ORDER:
            y_ref[...] = y
        else:
            o_ref[0] = y.astype(o_ref.dtype)


def _hyena_two_stage(z, w_short, hfilt, skip, first=0, bsz=None):
    _, length, d3 = z.shape
    bsz = z.shape[0] if bsz is None else bsz
    d = d3 // 3
    dt = FFT_LANES
    nct = d // dt
    assert 2 * length == FFT_N1 * FFT_N2
    ma, mai, mb, mbi, mirror = _two_stage_tables()
    spec_shape = (HY_ORDER, FFT_K2_USED, 2 * FFT_N1)
    once = pl.Buffered(1)
    hspec = lambda direction, order: pl.BlockSpec(
        (length, dt), lambda c: (0, (direction * HY_ORDER + order) * nct + c))
    spectrum = pl.pallas_call(
        _hyena_spectrum_kernel,
        grid=(nct,),
        in_specs=[hspec(0, 0), hspec(0, 1), hspec(1, 0), hspec(1, 1),
                  pl.BlockSpec(ma.shape, lambda c: (0, 0, 0), pipeline_mode=once),
                  pl.BlockSpec(mb.shape, lambda c: (0, 0), pipeline_mode=once)],
        out_specs=pl.BlockSpec(spec_shape + (dt,), lambda c: (0, 0, 0, c)),
        out_shape=jax.ShapeDtypeStruct(spec_shape + (d,), BF16),
        scratch_shapes=[pltpu.VMEM((length, dt), F32), pltpu.VMEM((FFT_WORK_ROWS, dt), F32)],
        compiler_params=_cparams("parallel"),
        name="hyena_filter_spectrum",
    )(hfilt, hfilt, hfilt, hfilt, ma, mb)

    zspec = lambda part: pl.BlockSpec((1, length, dt), lambda c, b: (b + first, 0, part * nct + c))
    wspec = lambda part: pl.BlockSpec((3, dt), lambda c, b: (0, part * nct + c))
    fixed3 = lambda c, b: (0, 0, 0)
    fixed2 = lambda c, b: (0, 0)
    return pl.pallas_call(
        _hyena_two_stage_kernel,
        grid=(nct, bsz),
        in_specs=[zspec(0), zspec(1), zspec(2), wspec(0), wspec(1), wspec(2),
                  pl.BlockSpec(spec_shape + (dt,), lambda c, b: (0, 0, 0, c), pipeline_mode=once),
                  pl.BlockSpec((HY_ORDER, dt), lambda c, b: (0, c)),
                  pl.BlockSpec(ma.shape, fixed3, pipeline_mode=once),
                  pl.BlockSpec(mai.shape, fixed3, pipeline_mode=once),
                  pl.BlockSpec(mb.shape, fixed2, pipeline_mode=once),
                  pl.BlockSpec(mbi.shape, fixed2, pipeline_mode=once),
                  pl.BlockSpec(mirror.shape, fixed3, pipeline_mode=once)],
        out_specs=pl.BlockSpec((1, length, dt), lambda c, b: (b, 0, c)),
        out_shape=jax.ShapeDtypeStruct((bsz, length, d), BF16),
        scratch_shapes=[pltpu.VMEM((length, dt), F32), pltpu.VMEM((length, dt), F32),
                        pltpu.VMEM((FFT_WORK_ROWS, dt), F32), pltpu.VMEM((FFT_K2_USED, 2 * FFT_N1, dt), BF16)],
        compiler_params=_cparams("parallel", "arbitrary"),
        name="hyena_conv_two_stage",
    )(z, z, z, w_short, w_short, w_short, spectrum, skip, ma, mai, mb, mbi, mirror)


N_HEADS = 16
HEAD_DIM = D_MODEL // N_HEADS
HEADS_PER_STEP = 2
CTX_HEADS_PER_STEP = 16
GRID_W = 64
WIN_ROWS = 8
WIN_COLS = 16
NEG_INF = -1e30
NAT_Q_ROWS = 4
NAT_K_ROWS = NAT_Q_ROWS + WIN_ROWS - 1
ATTN_SCALE = HEAD_DIM ** -0.5
assert math.frexp(ATTN_SCALE)[0] == 0.5, "the scale is folded into q, which is exact only for a power of two"


def _dot_nt(a, b):
    return lax.dot_general(a, b, (((1,), (1,)), ((), ())), preferred_element_type=F32)


def _qkv_kernel(x_ref, g_ref, sc_ref, sh_ref, w_ref, q_ref, k_ref, v_ref):
    h = _norm_mod(x_ref[...], g_ref[...], sc_ref[0], sh_ref[0])
    qkv = _dot(h.astype(BF16), w_ref[...])
    for part, ref in enumerate((q_ref, k_ref, v_ref)):
        scale = ATTN_SCALE if part == 0 else 1.0
        for head in range(N_HEADS):
            lo = part * D_MODEL + head * HEAD_DIM
            ref[0, head] = (qkv[:, lo:lo + HEAD_DIM] * scale).astype(ref.dtype)


def _qkv_proj(x, g, mods, w_bf16, seg, first_row, n_seq, seq_len, kv_dtype):
    d = x.shape[1]
    n_ctx, smp_len = seg
    t = min(ROW_TILE, seq_len)
    assert seq_len % t == 0 and first_row % t == 0
    first_tile = first_row // t
    per_seq = seq_len // t
    out_spec = pl.BlockSpec((1, N_HEADS, t, HEAD_DIM), lambda i: (i // per_seq, 0, i % per_seq, 0))
    shape = (n_seq, N_HEADS, seq_len, HEAD_DIM)
    return pl.pallas_call(
        _qkv_kernel,
        grid=(n_seq * per_seq,),
        in_specs=[
            pl.BlockSpec((t, d), lambda i: (i + first_tile, 0)),
            pl.BlockSpec((1, d), lambda i: (0, 0)),
            _mod_spec(1, t, n_ctx, smp_len, first_tile),
            _mod_spec(0, t, n_ctx, smp_len, first_tile),
            pl.BlockSpec(w_bf16.shape, lambda i: (0, 0)),
        ],
        out_specs=[out_spec, out_spec, out_spec],
        out_shape=[jax.ShapeDtypeStruct(shape, BF16), jax.ShapeDtypeStruct(shape, kv_dtype),
                   jax.ShapeDtypeStruct(shape, kv_dtype)],
        compiler_params=_cparams("parallel"),
        name="qkv_proj",
    )(x, g.reshape(1, d), mods, mods, w_bf16)


def _ctx_attn_kernel(q_ref, k_ref, v_ref, o_ref):
    for j in range(q_ref.shape[1]):
        q = q_ref[0, j]
        k = k_ref[0, j].astype(BF16)
        v = v_ref[0, j].astype(BF16)
        s = _dot_nt(q, k)
        p = jnp.exp(s - jnp.max(s, axis=-1, keepdims=True))
        o = _dot(p.astype(BF16), v) / jnp.sum(p, axis=-1, keepdims=True)
        o_ref[:, j * HEAD_DIM:(j + 1) * HEAD_DIM] = o.astype(o_ref.dtype)


def _ctx_attention(q, k, v):
    bsz, _, s, _ = q.shape
    hp = CTX_HEADS_PER_STEP
    spec = pl.BlockSpec((1, hp, s, HEAD_DIM), lambda b, h: (b, h, 0, 0))
    return pl.pallas_call(
        _ctx_attn_kernel,
        grid=(bsz, N_HEADS // hp),
        in_specs=[spec, spec, spec],
        out_specs=pl.BlockSpec((s, hp * HEAD_DIM), lambda b, h: (b, h)),
        out_shape=jax.ShapeDtypeStruct((bsz * s, D_MODEL), BF16),
        compiler_params=_cparams("parallel", "parallel"),
        name="ctx_attention",
    )(q, k, v)


def _rpb_toeplitz_kernel(r_ref, e_ref, o_ref):
    o_ref[...] = _dot3(r_ref[...], e_ref[...])


def _nat_bias(rpb):
    n_heads, n_r, n_c = rpb.shape
    n_cp = 32
    qc = np.arange(GRID_W)[:, None]
    kc = np.arange(GRID_W)[None, :]
    onehot = (np.clip(kc - qc + WIN_COLS - 1, 0, n_c - 1)[None] == np.arange(n_cp)[:, None, None])
    onehot = jnp.asarray(onehot.reshape(n_cp, GRID_W * GRID_W), F32)
    rows = jnp.pad(rpb.reshape(n_heads * n_r, n_c), ((0, 0), (0, n_cp - n_c)))
    toep = pl.pallas_call(
        _rpb_toeplitz_kernel,
        out_shape=jax.ShapeDtypeStruct((n_heads * n_r, GRID_W * GRID_W), F32),
        compiler_params=_cparams(),
        name="nat_bias_toeplitz",
    )(rows, onehot).reshape(n_heads, n_r, GRID_W, GRID_W)
    q_start = np.clip(qc - WIN_COLS // 2, 0, GRID_W - WIN_COLS)
    col_ok = jnp.asarray((kc >= q_start) & (kc < q_start + WIN_COLS))
    toep = jnp.where(col_ok, toep, NEG_INF)
    toep = jnp.concatenate([toep, jnp.full((n_heads, 1, GRID_W, GRID_W), NEG_INF, F32)], axis=1)
    rows_total = GRID_W
    idx = np.full((3, NAT_Q_ROWS, NAT_K_ROWS), n_r, np.int32)
    for case, r0 in enumerate((0, NAT_Q_ROWS, rows_total - NAT_Q_ROWS)):
        ks = int(np.clip(r0 - WIN_ROWS // 2, 0, rows_total - NAT_K_ROWS))
        for dr in range(NAT_Q_ROWS):
            r = r0 + dr
            rs = int(np.clip(r - WIN_ROWS // 2, 0, rows_total - WIN_ROWS))
            for dk in range(NAT_K_ROWS):
                kr = ks + dk
                if rs <= kr < rs + WIN_ROWS:
                    idx[case, dr, dk] = kr - r + WIN_ROWS - 1
    def assemble(t_ref, o_ref):
        for case in range(3):
            for dr in range(NAT_Q_ROWS):
                for dk in range(NAT_K_ROWS):
                    o_ref[0, case, dr * GRID_W:(dr + 1) * GRID_W, dk * GRID_W:(dk + 1) * GRID_W] = (
                        t_ref[0, int(idx[case, dr, dk])])

    return pl.pallas_call(
        assemble,
        grid=(n_heads,),
        in_specs=[pl.BlockSpec((1, n_r + 1, GRID_W, GRID_W), lambda h: (h, 0, 0, 0))],
        out_specs=pl.BlockSpec((1, 3, NAT_Q_ROWS * GRID_W, NAT_K_ROWS * GRID_W), lambda h: (h, 0, 0, 0)),
        out_shape=jax.ShapeDtypeStruct((n_heads, 3, NAT_Q_ROWS * GRID_W, NAT_K_ROWS * GRID_W), F32),
        compiler_params=_cparams("parallel"),
        name="nat_bias_assemble",
    )(toep)


def _nat_kernel(q_ref, k_ref, v_ref, kc_ref, vc_ref, bias_ref, o_ref):
    n_blocks = q_ref.shape[2] // (NAT_Q_ROWS * GRID_W)
    rows_total = q_ref.shape[2] // GRID_W
    nq = NAT_Q_ROWS * GRID_W
    nk = NAT_K_ROWS * GRID_W
    ctx = [(kc_ref[0, 0, j].astype(BF16), vc_ref[0, 0, j].astype(BF16)) for j in range(HEADS_PER_STEP)]

    def block(blk, carry):
        ks = jnp.clip(blk * NAT_Q_ROWS - WIN_ROWS // 2, 0, rows_total - NAT_K_ROWS)
        case = jnp.where(blk == 0, 0, jnp.where(blk == n_blocks - 1, 2, 1))
        q_rows = pl.ds(pl.multiple_of(blk * nq, nq), nq)
        k_rows = pl.ds(pl.multiple_of(ks * GRID_W, GRID_W), nk)
        for j, (k_ctx, v_ctx) in enumerate(ctx):
            q = q_ref[0, j, q_rows, :]
            s_loc = _dot_nt(q, k_ref[0, j, k_rows, :]) + bias_ref[j, case]
            s_ctx = _dot_nt(q, k_ctx)
            m = jnp.maximum(jnp.max(s_loc, axis=-1, keepdims=True), jnp.max(s_ctx, axis=-1, keepdims=True))
            p_loc = jnp.exp(s_loc - m)
            p_ctx = jnp.exp(s_ctx - m)
            denom = jnp.sum(p_loc, axis=-1, keepdims=True) + jnp.sum(p_ctx, axis=-1, keepdims=True)
            o = (_dot(p_loc.astype(BF16), v_ref[0, j, k_rows, :]) + _dot(p_ctx.astype(BF16), v_ctx)) / denom
            o_ref[q_rows, j * HEAD_DIM:(j + 1) * HEAD_DIM] = o.astype(o_ref.dtype)
        return carry

    lax.fori_loop(0, n_blocks, block, 0, unroll=4)


def _nat_attention(q, k, v, cache_k, cache_v, bias):
    bsz, _, length, _ = q.shape
    hp = HEADS_PER_STEP
    past = cache_k.shape[3]
    spec = pl.BlockSpec((1, hp, length, HEAD_DIM), lambda b, h: (b, h, 0, 0))
    cspec = pl.BlockSpec((1, 1, hp, past, HEAD_DIM), lambda b, h: (b, 0, h, 0, 0))
    return pl.pallas_call(
        _nat_kernel,
        grid=(bsz, N_HEADS // hp),
        in_specs=[spec, spec, spec, cspec, cspec,
                  pl.BlockSpec((hp,) + bias.shape[1:], lambda b, h: (h, 0, 0, 0))],
        out_specs=pl.BlockSpec((length, hp * HEAD_DIM), lambda b, h: (b, h)),
        out_shape=jax.ShapeDtypeStruct((bsz * length, D_MODEL), BF16),
        compiler_params=_cparams("parallel", "parallel"),
        name="nat_attention",
    )(q, k, v, cache_k, cache_v, bias)


def kernel(x_prompt, x_sample, cache_k, cache_v, c, c_ctx, ada_w, ada_b, norm1_g, norm2_g, normf_g, hy_w_in, hy_w_short, hy_f_w1, hy_f_b1, hy_f_freq, hy_f_w2, hy_f_b2, hy_f_w3, hy_skip, hy_w_out, na_w_qkv, na_rpb, na_w_o, moe_w_router, moe_b_router, moe_w_gate, moe_b_gate, moe_w_up, moe_b_up, moe_w_down, moe_b_down):
    b, s, d = x_prompt.shape
    bd, sd, _ = x_sample.shape
    n_ctx = b * s
    n_tok = n_ctx + bd * sd
    assert n_ctx % sd == 0 and n_ctx % ROW_TILE == 0 and sd % ROW_TILE == 0
    seg = (n_ctx, sd)
    x_ctx = x_prompt.reshape(n_ctx, d)
    x_smp = x_sample.reshape(bd * sd, d)
    cond = jnp.zeros((N_COND, d), F32).at[0].set(c_ctx).at[1:1 + bd].set(c)
    mods = _modulation(cond, ada_w, ada_b)

    def moe(x, i, final_norm):
        return _moe_layer(x, norm2_g[i], mods[i], moe_w_router[i], moe_b_router[i], i, moe_w_gate, moe_b_gate,
                          moe_w_up, moe_b_up, moe_w_down, moe_b_down, normf_g, seg, final_norm)

    z = _norm_proj(x_ctx, x_smp, norm1_g[0], mods[0], hy_w_in[0].astype(BF16), seg)
    fargs = (hy_f_w1[0], hy_f_b1[0], hy_f_freq[0], hy_f_w2[0], hy_f_b2[0], hy_f_w3[0])
    y_ctx = _hyena_direct(z.reshape(n_tok // s, s, 3 * d), hy_w_short[0], _hyena_filters(s, *fargs),
                          hy_skip[0], first=0, bsz=b)
    y_smp = _hyena_two_stage(z.reshape(n_tok // sd, sd, 3 * d), hy_w_short[0], _hyena_filters(sd, *fargs),
                             hy_skip[0], first=n_ctx // sd, bsz=bd)
    x = _out_proj(y_ctx.reshape(n_ctx, d), y_smp.reshape(bd * sd, d), hy_w_out[0].astype(BF16), x_ctx, x_smp,
                  mods[0], seg)
    x = moe(x, 0, False)

    w_qkv = na_w_qkv[0].astype(BF16)
    q_c, k_c, v_c = _qkv_proj(x, norm1_g[1], mods[1], w_qkv, seg, 0, b, s, F32)
    q_s, k_s, v_s = _qkv_proj(x, norm1_g[1], mods[1], w_qkv, seg, n_ctx, bd, sd, BF16)
    o_ctx = _ctx_attention(q_c, k_c, v_c)
    o_smp = _nat_attention(q_s, k_s, v_s, cache_k, cache_v, _nat_bias(na_rpb[0]))
    x = _out_proj(o_ctx, o_smp, na_w_o[0].astype(BF16), x, None, mods[1], seg)
    y_prompt, y_sample = moe(x, 1, True)

    nh, hd = k_c.shape[1], k_c.shape[3]
    return (y_prompt.reshape(b, s, d), y_sample.reshape(bd, sd, d),
            k_c.reshape(b, 1, nh, s, hd), v_c.reshape(b, 1, nh, s, hd))
```

```python
import functools
import math

import numpy as np
import jax
import jax.numpy as jnp
from jax import lax
from jax.experimental import pallas as pl
from jax.experimental.pallas import tpu as pltpu

F32 = jnp.float32
BF16 = jnp.bfloat16

D_MODEL = 1024
N_MOD = 6
RMS_EPS = 1e-6
N_EXPERTS = 32
TOP_K = 4
SWIGLU_LIMIT = 7.0
SWIGLU_ALPHA = 1.702

N_COND = 8
ROW_TILE = 512
MOE_TILE = 512
V7X_VMEM_LIMIT = 56 * 1024 * 1024


def _cparams(*sem, vmem=V7X_VMEM_LIMIT):
    return pltpu.CompilerParams(dimension_semantics=sem, vmem_limit_bytes=vmem)


def _dot(a, b):
    return jnp.dot(a, b, preferred_element_type=F32)


def _split_bf16(x):
    hi = x.astype(BF16)
    lo = (x - hi.astype(F32)).astype(BF16)
    return hi, lo


def _dot3(a, b):
    ah, al = _split_bf16(a)
    bh, bl = _split_bf16(b)
    return _dot(ah, bh) + (_dot(al, bh) + _dot(ah, bl))


def _seg_of_tile(i, tile, n_ctx, smp_len):
    ctx_tiles = n_ctx // tile
    per_smp = smp_len // tile
    return jnp.where(i < ctx_tiles, 0, 1 + (i - ctx_tiles) // per_smp)


def _norm_mod(x, g, sc, sh):
    y = x * lax.rsqrt(jnp.mean(x * x, axis=-1, keepdims=True) + RMS_EPS)
    return (y * g) * (1.0 + sc) + sh


def _mod_kernel(c_ref, w_ref, b_ref, o_ref):
    c = c_ref[...]
    a = c * jax.nn.sigmoid(c)
    o_ref[0] = _dot3(a, w_ref[0]) + b_ref[0]


def _modulation(cond, ada_w, ada_b):
    depth, d, n_out = ada_w.shape
    tn = 1536
    m = pl.pallas_call(
        _mod_kernel,
        grid=(depth, n_out // tn),
        in_specs=[
            pl.BlockSpec((N_COND, d), lambda l, j: (0, 0)),
            pl.BlockSpec((1, d, tn), lambda l, j: (l, 0, j)),
            pl.BlockSpec((1, 1, tn), lambda l, j: (l, 0, j)),
        ],
        out_specs=pl.BlockSpec((1, N_COND, tn), lambda l, j: (l, 0, j)),
        out_shape=jax.ShapeDtypeStruct((depth, N_COND, n_out), F32),
        compiler_params=_cparams("arbitrary", "arbitrary"),
        name="adaln_modulation",
    )(cond, ada_w, ada_b.reshape(depth, 1, n_out))
    m = m.reshape(depth, N_COND, N_MOD, d)
    return jnp.transpose(m, (0, 2, 1, 3)).reshape(depth, N_MOD * N_COND, 1, d)


def _mod_spec(which, tile, n_ctx, smp_len, first_tile=0):
    return pl.BlockSpec(
        (1, 1, D_MODEL),
        lambda i, *_: (which * N_COND + _seg_of_tile(i + first_tile, tile, n_ctx, smp_len), 0, 0))


def _group_specs(block_cols, ctx_tiles, stacked=False):
    base = ctx_tiles if stacked else 0
    return (pl.BlockSpec((ROW_TILE, block_cols), lambda i: (jnp.minimum(i, ctx_tiles - 1), 0)),
            pl.BlockSpec((ROW_TILE, block_cols), lambda i: (jnp.maximum(i - ctx_tiles, 0) + base, 0)))


def _pick_group(a_ref, b_ref, ctx_tiles):
    return jnp.where(pl.program_id(0) < ctx_tiles, a_ref[...], b_ref[...])


def _norm_proj_kernel(xa_ref, xb_ref, g_ref, sc_ref, sh_ref, w_ref, o_ref, *, ctx_tiles):
    h = _norm_mod(_pick_group(xa_ref, xb_ref, ctx_tiles), g_ref[...], sc_ref[0], sh_ref[0])
    o_ref[...] = _dot(h.astype(BF16), w_ref[...]).astype(o_ref.dtype)


def _norm_proj(x_ctx, x_smp, g, mods, w_bf16, seg, out_dtype=F32):
    d = x_ctx.shape[1]
    n = x_ctx.shape[0] + x_smp.shape[0]
    n_out = w_bf16.shape[1]
    n_ctx, smp_len = seg
    ctx_tiles = n_ctx // ROW_TILE
    return pl.pallas_call(
        functools.partial(_norm_proj_kernel, ctx_tiles=ctx_tiles),
        grid=(n // ROW_TILE,),
        in_specs=[
            *_group_specs(d, ctx_tiles),
            pl.BlockSpec((1, d), lambda i: (0, 0)),
            _mod_spec(1, ROW_TILE, n_ctx, smp_len),
            _mod_spec(0, ROW_TILE, n_ctx, smp_len),
            pl.BlockSpec((d, n_out), lambda i: (0, 0)),
        ],
        out_specs=pl.BlockSpec((ROW_TILE, n_out), lambda i: (i, 0)),
        out_shape=jax.ShapeDtypeStruct((n, n_out), out_dtype),
        compiler_params=_cparams("parallel"),
        name="norm_proj",
    )(x_ctx, x_smp, g.reshape(1, d), mods, mods, w_bf16)


def _out_proj_kernel(ya_ref, yb_ref, w_ref, xa_ref, xb_ref, gate_ref, o_ref, *, ctx_tiles):
    y = _pick_group(ya_ref, yb_ref, ctx_tiles)
    o_ref[...] = _pick_group(xa_ref, xb_ref, ctx_tiles) + gate_ref[0] * _dot(y, w_ref[...])


def _out_proj(y_ctx, y_smp, w_bf16, x_ctx, x_smp, mods, seg):
    d = x_ctx.shape[1]
    n_ctx, smp_len = seg
    n = n_ctx + y_smp.shape[0]
    ctx_tiles = n_ctx // ROW_TILE
    stacked = x_smp is None
    x_smp = x_ctx if stacked else x_smp
    return pl.pallas_call(
        functools.partial(_out_proj_kernel, ctx_tiles=ctx_tiles),
        grid=(n // ROW_TILE,),
        in_specs=[
            *_group_specs(d, ctx_tiles),
            pl.BlockSpec((d, d), lambda i: (0, 0)),
            *_group_specs(d, ctx_tiles, stacked),
            _mod_spec(2, ROW_TILE, n_ctx, smp_len),
        ],
        out_specs=pl.BlockSpec((ROW_TILE, d), lambda i: (i, 0)),
        out_shape=jax.ShapeDtypeStruct((n, d), F32),
        compiler_params=_cparams("parallel"),
        name="out_proj",
    )(y_ctx, y_smp, w_bf16, x_ctx, x_smp, mods)


LANES = 128
ROW_SUBLANES = D_MODEL // LANES


def _store_row_tiles(ref, x):
    t = x.shape[0]
    for s in range(ROW_SUBLANES):
        ref[pl.ds(s, t, stride=ROW_SUBLANES), :] = x[:, s * LANES:(s + 1) * LANES]


def _load_row_tiles(ref, t):
    return jnp.concatenate([ref[pl.ds(s, t, stride=ROW_SUBLANES), :] for s in range(ROW_SUBLANES)], axis=1)


def _columns(cols):
    t = cols[0].shape[0]
    lane = lax.broadcasted_iota(jnp.int32, (t, len(cols)), 1)
    out = jnp.broadcast_to(cols[-1], (t, len(cols)))
    for k in range(len(cols) - 2, -1, -1):
        out = jnp.where(lane == k, cols[k], out)
    return out


def _router_kernel(x_ref, g_ref, sc_ref, sh_ref, wr_ref, br_ref, tri_ref,
                   h_ref, e_ref, gate_ref, rank_ref, cnt_ref, run_ref):
    i = pl.program_id(0)

    @pl.when(i == 0)
    def _():
        run_ref[...] = jnp.zeros_like(run_ref)

    h = _norm_mod(x_ref[...], g_ref[...], sc_ref[0], sh_ref[0])
    _store_row_tiles(h_ref, h)
    logits = _dot3(h, wr_ref[...]) + br_ref[...]
    lane = lax.broadcasted_iota(jnp.int32, logits.shape, 1).astype(F32)
    work = logits
    vals, idxs, hots = [], [], []
    for _ in range(TOP_K):
        m = jnp.max(work, axis=-1, keepdims=True)
        idx = jnp.min(jnp.where(work == m, lane, float(N_EXPERTS)), axis=-1, keepdims=True)
        hot = lane == idx
        vals.append(m)
        idxs.append(idx)
        hots.append(hot)
        work = jnp.where(hot, -jnp.inf, work)
    ex = [jnp.exp(v - vals[0]) for v in vals]
    denom = ex[0] + ex[1] + ex[2] + ex[3]
    gate_ref[...] = _columns([e / denom for e in ex])
    e_ref[...] = _columns(idxs).astype(jnp.int32)

    chosen = (hots[0] | hots[1] | hots[2] | hots[3]).astype(F32)
    before = run_ref[...] + _dot(tri_ref[...], chosen.astype(BF16))
    ranks = [jnp.sum(jnp.where(hot, before, 0.0), axis=-1, keepdims=True) for hot in hots]
    rank_ref[...] = _columns(ranks).astype(jnp.int32)
    run_ref[...] += jnp.sum(chosen, axis=0, keepdims=True)
    cnt_ref[...] = run_ref[...].astype(jnp.int32)


def _router(x, g, mods, w_r, b_r, seg):
    n, d = x.shape
    n_ctx, smp_len = seg
    t = ROW_TILE
    tri = jnp.asarray(np.tril(np.ones((t, t), np.float32), -1), BF16)
    tok4 = lambda i: (i, 0)
    return pl.pallas_call(
        _router_kernel,
        grid=(n // t,),
        in_specs=[
            pl.BlockSpec((t, d), lambda i: (i, 0)),
            pl.BlockSpec((1, d), lambda i: (0, 0)),
            _mod_spec(4, t, n_ctx, smp_len),
            _mod_spec(3, t, n_ctx, smp_len),
            pl.BlockSpec((d, N_EXPERTS), lambda i: (0, 0)),
            pl.BlockSpec((1, N_EXPERTS), lambda i: (0, 0)),
            pl.BlockSpec((t, t), lambda i: (0, 0)),
        ],
        out_specs=[
            pl.BlockSpec((t * ROW_SUBLANES, LANES), tok4),
            pl.BlockSpec((t, TOP_K), tok4),
            pl.BlockSpec((t, TOP_K), tok4),
            pl.BlockSpec((t, TOP_K), tok4),
            pl.BlockSpec((1, N_EXPERTS), lambda i: (0, 0)),
        ],
        out_shape=[
            jax.ShapeDtypeStruct((n * ROW_SUBLANES, LANES), F32),
            jax.ShapeDtypeStruct((n, TOP_K), jnp.int32),
            jax.ShapeDtypeStruct((n, TOP_K), F32),
            jax.ShapeDtypeStruct((n, TOP_K), jnp.int32),
            jax.ShapeDtypeStruct((1, N_EXPERTS), jnp.int32),
        ],
        scratch_shapes=[pltpu.VMEM((1, N_EXPERTS), F32)],
        compiler_params=_cparams("arbitrary"),
        name="moe_router",
    )(x, g.reshape(1, d), mods, mods, w_r, b_r.reshape(1, N_EXPERTS), tri)


def _tile_copy(src_ref, src_row, dst_ref, dst_row, sem):
    rows = lambda r: pl.ds(pl.multiple_of(r, ROW_SUBLANES), ROW_SUBLANES)
    return pltpu.make_async_copy(src_ref.at[rows(src_row), :], dst_ref.at[rows(dst_row), :], sem)


def _tiles_wait(src_ref, dst_ref, n_tiles, sem):
    rows = pl.ds(0, n_tiles * ROW_SUBLANES)
    pltpu.make_async_copy(src_ref.at[rows, :], dst_ref.at[rows, :], sem).wait()


ENTRIES_PER_TILE = ROW_TILE * TOP_K
DMA_PRIORITIES = 2


def _dispatch_kernel(fill_start_ref, fill_len_ref, na_ref, dest_ref, h_ref, xs_ref, hbuf, zero_ref, sem):
    i = pl.program_id(0)
    last = pl.num_programs(0) - 1
    slot = lax.rem(i, 2)
    hbuf[slot] = h_ref[...]

    def issue(t, carry):
        for k in range(TOP_K):
            _tile_copy(hbuf.at[slot], t * ROW_SUBLANES, xs_ref, dest_ref[t * TOP_K + k],
                       sem.at[slot]).start(priority=k % DMA_PRIORITIES)
        return carry

    lax.fori_loop(0, ROW_TILE, issue, 0, unroll=8)

    @pl.when(i >= 1)
    def _():
        _tiles_wait(xs_ref, xs_ref, ENTRIES_PER_TILE, sem.at[1 - slot])

    @pl.when(i == last)
    def _():
        _tiles_wait(xs_ref, xs_ref, ENTRIES_PER_TILE, sem.at[slot])
        zero_ref[...] = jnp.zeros_like(zero_ref)

        def pieces(act):
            for e in range(N_EXPERTS):
                count = fill_len_ref[e]
                done = jnp.int32(0)
                for bit in reversed(range(MOE_TILE.bit_length() - 1)):
                    size = 1 << bit
                    rows = size * ROW_SUBLANES
                    first = pl.multiple_of((fill_start_ref[e] + done) * ROW_SUBLANES, ROW_SUBLANES)
                    copy = pltpu.make_async_copy(zero_ref.at[pl.ds(0, rows), :], xs_ref.at[pl.ds(first, rows), :],
                                                 sem.at[0])
                    pl.when((count & size) != 0)(functools.partial(act, copy))
                    done = done + (count & size)

        pieces(lambda copy: copy.start())
        pieces(lambda copy: copy.wait())

        block_rows = MOE_TILE * ROW_SUBLANES

        def block_copy(blk):
            return pltpu.make_async_copy(
                zero_ref, xs_ref.at[pl.ds(pl.multiple_of(blk * block_rows, block_rows), block_rows), :], sem.at[0])

        n_blocks = xs_ref.shape[0] // block_rows

        def tail(blk, carry):
            block_copy(blk).start()
            return carry

        lax.fori_loop(na_ref[0], n_blocks, tail, 0)

        def tail_drain(blk, carry):
            block_copy(0).wait()
            return carry

        lax.fori_loop(na_ref[0], n_blocks, tail_drain, 0)


def _dispatch(h_tiles, dest_rows, fill_start, fill_len, n_active, cap):
    n = h_tiles.shape[0] // ROW_SUBLANES
    grid_spec = pltpu.PrefetchScalarGridSpec(
        num_scalar_prefetch=3,
        grid=(n // ROW_TILE,),
        in_specs=[
            pl.BlockSpec((ENTRIES_PER_TILE,), lambda i, *_: (i,), memory_space=pltpu.SMEM),
            pl.BlockSpec((ROW_TILE * ROW_SUBLANES, LANES), lambda i, *_: (i, 0)),
        ],
        out_specs=pl.BlockSpec(memory_space=pl.ANY),
        scratch_shapes=[pltpu.VMEM((2, ROW_TILE * ROW_SUBLANES, LANES), F32),
                        pltpu.VMEM((MOE_TILE * ROW_SUBLANES, LANES), F32), pltpu.SemaphoreType.DMA((2,))],
    )
    return pl.pallas_call(
        _dispatch_kernel,
        grid_spec=grid_spec,
        out_shape=jax.ShapeDtypeStruct((cap * ROW_SUBLANES, LANES), F32),
        compiler_params=_cparams("arbitrary"),
        name="moe_dispatch",
    )(fill_start, fill_len, n_active, dest_rows, h_tiles)


def _ffn_kernel(be_ref, na_ref, wslot_ref, next_e_ref, xs_ref, wg_ref, bg_ref, wu_ref, bu_ref, wd_ref, bd_ref,
                ys_ref, wbuf, wg_s, wu_s, wd_s, wsem, *, layer):
    i = pl.program_id(0)
    n_active = na_ref[0]
    tm = xs_ref.shape[0] // ROW_SUBLANES
    changed = jnp.logical_or(i == 0, be_ref[i] != be_ref[jnp.maximum(i - 1, 0)])
    fresh = jnp.logical_and(i < n_active, changed)

    def weight_copies(expert, b):
        return [pltpu.make_async_copy(w_ref.at[layer, expert], wbuf.at[b, m], wsem.at[b])
                for m, w_ref in enumerate((wg_ref, wu_ref, wd_ref))]

    @pl.when(jnp.logical_and(i == 0, n_active > 0))
    def _():
        for copy in weight_copies(be_ref[0], wslot_ref[0]):
            copy.start()

    @pl.when(fresh)
    def _():
        b = wslot_ref[i]
        for copy in weight_copies(be_ref[i], b):
            copy.wait()
        wg_s[...] = wbuf[b, 0].astype(BF16)
        wu_s[...] = wbuf[b, 1].astype(BF16)
        wd_s[...] = wbuf[b, 2].astype(BF16)
        upcoming = next_e_ref[i]

        @pl.when(upcoming >= 0)
        def _():
            for copy in weight_copies(upcoming, 1 - b):
                copy.start()

    @pl.when(i < n_active)
    def _():
        x = _load_row_tiles(xs_ref, tm).astype(BF16)
        g = jnp.minimum(_dot(x, wg_s[...]) + bg_ref[0, 0], SWIGLU_LIMIT)
        u = jnp.clip(_dot(x, wu_s[...]) + bu_ref[0, 0], -SWIGLU_LIMIT, SWIGLU_LIMIT)
        a = g * jax.nn.sigmoid(SWIGLU_ALPHA * g) * (u + 1.0)
        _store_row_tiles(ys_ref, _dot(a.astype(BF16), wd_s[...]) + bd_ref[0, 0])

    @pl.when(i >= na_ref[0])
    def _():
        ys_ref[...] = jnp.zeros_like(ys_ref)


def _expert_ffn(xs, block_e, n_active, weight_slot, next_expert, layer, w_g, b_g, w_u, b_u, w_d, b_d):
    d = D_MODEL
    depth, ne, _, f = w_g.shape
    assert f == d
    nb = block_e.shape[0]
    block = (MOE_TILE * ROW_SUBLANES, LANES)
    bmap = lambda i, be, *_: (layer, be[i], 0, 0)
    hbm = pl.BlockSpec(memory_space=pl.ANY)
    grid_spec = pltpu.PrefetchScalarGridSpec(
        num_scalar_prefetch=4,
        grid=(nb,),
        in_specs=[
            pl.BlockSpec(block, lambda i, be, na, *_: (jnp.minimum(i, na[0] - 1), 0)),
            hbm, pl.BlockSpec((1, 1, 1, f), bmap),
            hbm, pl.BlockSpec((1, 1, 1, f), bmap),
            hbm, pl.BlockSpec((1, 1, 1, d), bmap),
        ],
        out_specs=pl.BlockSpec(block, lambda i, *_: (i, 0)),
        scratch_shapes=[pltpu.VMEM((2, 3, d, f), F32),
                        pltpu.VMEM((d, f), BF16), pltpu.VMEM((d, f), BF16), pltpu.VMEM((f, d), BF16),
                        pltpu.SemaphoreType.DMA((2,))],
    )
    return pl.pallas_call(
        functools.partial(_ffn_kernel, layer=layer),
        grid_spec=grid_spec,
        out_shape=jax.ShapeDtypeStruct(xs.shape, F32),
        compiler_params=_cparams("arbitrary"),
        name="moe_expert_ffn",
    )(block_e, n_active, weight_slot, next_expert, xs,
      w_g, b_g.reshape(depth, ne, 1, f), w_u, b_u.reshape(depth, ne, 1, f), w_d, b_d.reshape(depth, ne, 1, d))


def _combine_kernel(dest_ref, dest_next_ref, x_ref, gate_ref, g2_ref, gf_ref, ys_ref, o_ref, buf, sem, *,
                    final_norm):
    i = pl.program_id(0)
    slot = lax.rem(i, 2)
    t = x_ref.shape[0]
    plane = t * ROW_SUBLANES

    def start_gather(idx_ref, b):
        def body(r, carry):
            for k in range(TOP_K):
                _tile_copy(ys_ref, idx_ref[r * TOP_K + k], buf.at[b], k * plane + r * ROW_SUBLANES,
                           sem.at[b]).start(priority=k % DMA_PRIORITIES)
            return carry

        lax.fori_loop(0, t, body, 0, unroll=8)

    @pl.when(i == 0)
    def _():
        start_gather(dest_ref, 0)

    @pl.when(i + 1 < pl.num_programs(0))
    def _():
        start_gather(dest_next_ref, 1 - slot)

    _tiles_wait(ys_ref, buf.at[slot], TOP_K * t, sem.at[slot])
    gates = gate_ref[...]
    acc = None
    for k in range(TOP_K):
        term = gates[:, k:k + 1] * _load_row_tiles(buf.at[slot, pl.ds(k * plane, plane), :], t)
        acc = term if acc is None else acc + term
    y = x_ref[...] + g2_ref[0] * acc
    if final_norm:
        y = y * lax.rsqrt(jnp.mean(y * y, axis=-1, keepdims=True) + RMS_EPS) * gf_ref[...]
    o_ref[...] = y


def _combine(x, ys, dest_rows, gates, mods, normf_g, seg, final_norm, first_row=0, n_rows=None):
    n, d = x.shape
    n_rows = n if n_rows is None else n_rows
    n_ctx, smp_len = seg
    t = ROW_TILE
    first = first_row // t
    steps = n_rows // t
    rows = lambda i: (i + first, 0)
    return pl.pallas_call(
        functools.partial(_combine_kernel, final_norm=final_norm),
        grid=(steps,),
        in_specs=[
            pl.BlockSpec((ENTRIES_PER_TILE,), lambda i: (i + first,), memory_space=pltpu.SMEM),
            pl.BlockSpec((ENTRIES_PER_TILE,), lambda i: (jnp.minimum(i + 1, steps - 1) + first,),
                         memory_space=pltpu.SMEM),
            pl.BlockSpec((t, d), rows),
            pl.BlockSpec((t, TOP_K), rows),
            _mod_spec(5, t, n_ctx, smp_len, first),
            pl.BlockSpec((1, d), lambda i: (0, 0)),
            pl.BlockSpec(memory_space=pl.ANY),
        ],
        out_specs=pl.BlockSpec((t, d), lambda i: (i, 0)),
        out_shape=jax.ShapeDtypeStruct((n_rows, d), F32),
        scratch_shapes=[pltpu.VMEM((2, TOP_K * t * ROW_SUBLANES, LANES), F32), pltpu.SemaphoreType.DMA((2,))],
        compiler_params=_cparams("arbitrary"),
        name="moe_combine",
    )(dest_rows, dest_rows, x, gates, mods, normf_g.reshape(1, d), ys)


def _moe_layer(x, norm_g, mods, w_r, b_r, layer, w_g, b_g, w_u, b_u, w_d, b_d, normf_g, seg, final_norm):
    n, d = x.shape
    nk = n * TOP_K
    h, top_e, gates, rank, counts = _router(x, norm_g, mods, w_r, b_r, seg)
    counts = counts[0]
    padded = (counts + MOE_TILE - 1) // MOE_TILE * MOE_TILE
    pad_end = jnp.cumsum(padded)
    pad_start = pad_end - padded
    cap = nk + N_EXPERTS * MOE_TILE
    nb = cap // MOE_TILE
    experts = jnp.arange(N_EXPERTS, dtype=jnp.int32)
    start_of = jnp.sum(jnp.where(top_e[..., None] == experts, pad_start, 0), axis=-1)
    dest_rows = ((start_of + rank) * ROW_SUBLANES).reshape(nk).astype(jnp.int32)
    blk_start = jnp.arange(nb, dtype=jnp.int32) * MOE_TILE
    block_e = jnp.minimum(jnp.sum(blk_start[:, None] >= pad_end[None, :], axis=1), N_EXPERTS - 1).astype(jnp.int32)
    n_active = (pad_end[-1] // MOE_TILE).astype(jnp.int32).reshape(1)
    xs = _dispatch(h, dest_rows, (pad_start + counts).astype(jnp.int32), (padded - counts).astype(jnp.int32),
                   n_active, cap)
    changes = jnp.concatenate([jnp.zeros((1,), jnp.int32), (block_e[1:] != block_e[:-1]).astype(jnp.int32)])
    weight_slot = (jnp.cumsum(changes) % 2).astype(jnp.int32)
    later = (experts[None, :] > experts[:, None]) & (counts[None, :] > 0)
    next_of = jnp.min(jnp.where(later, experts[None, :], N_EXPERTS), axis=1)
    next_expert = jnp.where(next_of < N_EXPERTS, next_of, -1).astype(jnp.int32)[block_e]
    ys = _expert_ffn(xs, block_e, n_active, weight_slot, next_expert, layer, w_g, b_g, w_u, b_u, w_d, b_d)
    if not final_norm:
        return _combine(x, ys, dest_rows, gates, mods, normf_g, seg, False)
    n_ctx = seg[0]
    return (_combine(x, ys, dest_rows, gates, mods, normf_g, seg, True, 0, n_ctx),
            _combine(x, ys, dest_rows, gates, mods, normf_g, seg, True, n_ctx, n - n_ctx))


HY_ORDER = 2
HY_BANDS = 16
HY_EMB = 1 + 2 * HY_BANDS
HY_FFN = 64
HY_MIN_DECAY = math.log(1e-2) / 1.5
HY_MAX_DECAY = math.log(1e-2) / 0.3
HY_EMB_PAD = 64


def _filter_mlp_kernel(z_ref, t_ref, w1_ref, b1_ref, fr_ref, w2_ref, b2_ref, w3_ref, dl_ref, o_ref, h_ref):
    @pl.when(pl.program_id(1) == 0)
    def _():
        fr = fr_ref[...]
        h = jnp.sin(fr * (_dot3(z_ref[...], w1_ref[...]) + b1_ref[...]))
        h_ref[...] = jnp.sin(fr * (_dot3(h, w2_ref[...]) + b2_ref[...]))

    o_ref[...] = _dot3(h_ref[...], w3_ref[...]) * jnp.exp(-t_ref[...] * dl_ref[...])


def _hyena_filters(length, w1, b1, freq, w2, b2, w3):
    t = jnp.linspace(0.0, 1.0, length, dtype=F32)[:, None]
    ang = (2.0 * math.pi / length) * jnp.arange(length, dtype=F32)[:, None]
    bands = jnp.linspace(1e-4, HY_BANDS - 1, HY_BANDS, dtype=F32)[None, :]
    z = jnp.concatenate([t, jnp.cos(bands * ang), -jnp.sin(bands * ang)], axis=-1)
    z = jnp.pad(z, ((0, 0), (0, HY_EMB_PAD - HY_EMB)))
    w1p = jnp.pad(w1, ((0, HY_EMB_PAD - HY_EMB), (0, 0)))
    n_out = w3.shape[1]
    deltas = jnp.abs(jnp.linspace(HY_MIN_DECAY, HY_MAX_DECAY, D_MODEL, dtype=F32))
    deltas = jnp.tile(deltas, n_out // D_MODEL)[None, :]
    tl, tn = min(512, length), 2048
    row = lambda i, j: (i, 0)
    fixed = lambda i, j: (0, 0)
    return pl.pallas_call(
        _filter_mlp_kernel,
        grid=(length // tl, n_out // tn),
        in_specs=[
            pl.BlockSpec((tl, HY_EMB_PAD), row),
            pl.BlockSpec((tl, 1), row),
            pl.BlockSpec((HY_EMB_PAD, HY_FFN), fixed),
            pl.BlockSpec((1, HY_FFN), fixed),
            pl.BlockSpec((1, HY_FFN), fixed),
            pl.BlockSpec((HY_FFN, HY_FFN), fixed),
            pl.BlockSpec((1, HY_FFN), fixed),
            pl.BlockSpec((HY_FFN, tn), lambda i, j: (0, j)),
            pl.BlockSpec((1, tn), lambda i, j: (0, j)),
        ],
        out_specs=pl.BlockSpec((tl, tn), lambda i, j: (i, j)),
        out_shape=jax.ShapeDtypeStruct((length, n_out), F32),
        scratch_shapes=[pltpu.VMEM((tl, HY_FFN), F32)],
        compiler_params=_cparams("parallel", "arbitrary"),
        name="hyena_filter_mlp",
    )(z, t, w1p, b1.reshape(1, -1), freq.reshape(1, -1), w2, b2.reshape(1, -1), w3, deltas)


def _short_conv(z, w):
    length = z.shape[0]
    row = lax.broadcasted_iota(jnp.int32, z.shape, 0)
    prev = jnp.where(row == 0, 0.0, pltpu.roll(z, 1, 0))
    nxt = jnp.where(row == length - 1, 0.0, pltpu.roll(z, length - 1, 0))
    return (prev * w[0:1] + z * w[1:2]) + nxt * w[2:3]


def _filter_halves(hf_ref, hb_ref):
    hf = hf_ref[...]
    hb = hb_ref[...]
    hb = jnp.where(lax.broadcasted_iota(jnp.int32, hb.shape, 0) == 0, 0.0, hb)
    norm = jnp.sum(jnp.abs(hf), axis=0, keepdims=True) + jnp.sum(jnp.abs(hb), axis=0, keepdims=True)
    return hf + hb, hf - hb, 1.0 / norm


def _direct_dft_tables(length):
    n_fft = 2 * length
    n_freq = length + 1
    mf = -(-n_freq // 16) * 16
    k = np.arange(mf)[:, None]
    n = np.arange(length)[None, :]
    ang = 2.0 * np.pi * ((k * n) % n_fft) / n_fft
    valid = k < n_freq
    cos = np.where(valid, np.cos(ang), 0.0)
    msin = np.where(valid, -np.sin(ang), 0.0)
    weight = np.where((k == 0) | (k == length), 1.0, 2.0) * valid / n_fft
    fwd = np.concatenate([cos, msin], axis=0)
    inv = np.concatenate([weight * cos, weight * msin], axis=0).T
    return jnp.asarray(fwd, BF16), jnp.asarray(inv, BF16), mf


def _hyena_direct_kernel(zv_ref, z1_ref, z2_ref, wv_ref, w1_ref, w2_ref,
                         hf0_ref, hf1_ref, hb0_ref, hb1_ref, skip_ref, fw_ref, iv_ref,
                         o_ref, kr_ref, ki_ref):
    mf = fw_ref.shape[0] // 2
    dt = o_ref.shape[-1]

    @pl.when(pl.program_id(1) == 0)
    def _():
        for o, (hf_ref, hb_ref) in enumerate(((hf0_ref, hb0_ref), (hf1_ref, hb1_ref))):
            hs, hd, inv_norm = _filter_halves(hf_ref, hb_ref)
            spec = _dot(fw_ref[...], jnp.concatenate([hs, hd], axis=1).astype(BF16))
            kr_ref[o] = spec[:mf, :dt] * inv_norm
            ki_ref[o] = spec[mf:, dt:] * inv_norm

    y = _short_conv(zv_ref[0], wv_ref[...])
    for o, (z_ref, w_ref) in enumerate(((z1_ref, w1_ref), (z2_ref, w2_ref))):
        spec = _dot(fw_ref[...], y.astype(BF16))
        yr, yi = spec[:mf], spec[mf:]
        kr, ki = kr_ref[o], ki_ref[o]
        prod = jnp.concatenate([yr * kr - yi * ki, yr * ki + yi * kr], axis=0)
        yc = _dot(iv_ref[...], prod.astype(BF16))
        y = _short_conv(z_ref[0], w_ref[...]) * (yc + y * skip_ref[o:o + 1])
    o_ref[0] = y.astype(o_ref.dtype)


def _hyena_direct(z, w_short, hfilt, skip, first=0, bsz=None):
    _, length, d3 = z.shape
    bsz = z.shape[0] if bsz is None else bsz
    d = d3 // 3
    dt = 512
    nct = d // dt
    fwd, inv, mf = _direct_dft_tables(length)
    zspec = lambda part: pl.BlockSpec((1, length, dt), lambda c, b: (b + first, 0, part * nct + c))
    wspec = lambda part: pl.BlockSpec((3, dt), lambda c, b: (0, part * nct + c))
    hspec = lambda direction, order: pl.BlockSpec(
        (length, dt), lambda c, b: (0, (direction * HY_ORDER + order) * nct + c))
    fixed = lambda c, b: (0, 0)
    return pl.pallas_call(
        _hyena_direct_kernel,
        grid=(nct, bsz),
        in_specs=[zspec(0), zspec(1), zspec(2), wspec(0), wspec(1), wspec(2),
                  hspec(0, 0), hspec(0, 1), hspec(1, 0), hspec(1, 1),
                  pl.BlockSpec((HY_ORDER, dt), lambda c, b: (0, c)),
                  pl.BlockSpec(fwd.shape, fixed), pl.BlockSpec(inv.shape, fixed)],
        out_specs=pl.BlockSpec((1, length, dt), lambda c, b: (b, 0, c)),
        out_shape=jax.ShapeDtypeStruct((bsz, length, d), BF16),
        scratch_shapes=[pltpu.VMEM((HY_ORDER, mf, dt), F32), pltpu.VMEM((HY_ORDER, mf, dt), F32)],
        compiler_params=_cparams("parallel", "arbitrary"),
        name="hyena_conv_direct",
    )(z, z, z, w_short, w_short, w_short, hfilt, hfilt, hfilt, hfilt, skip, fwd, inv)


FFT_N1 = 64
FFT_N2 = 128
FFT_LANES = 128
FFT_A_PITCH = 2 * FFT_N2 + 8
FFT_U_PITCH = 2 * FFT_N1 + 8
FFT_GROUP = 2
FFT_UNROLL = 32


def _two_stage_tables():
    n_fft = FFT_N1 * FFT_N2
    half = FFT_N2 // 2
    n1 = np.arange(FFT_N1)[:, None, None]
    k2 = np.arange(FFT_N2)[None, :, None]
    n2 = np.arange(half)[None, None, :]
    ang = 2.0 * np.pi * ((k2 * (n1 + FFT_N1 * n2)) % n_fft) / n_fft
    stage_a = np.concatenate([np.cos(ang), -np.sin(ang)], axis=1)
    stage_a_inv = np.transpose(stage_a, (0, 2, 1)) / n_fft
    k1 = np.arange(FFT_N1)[:, None]
    m1 = np.arange(FFT_N1)[None, :]
    phi = 2.0 * np.pi * ((k1 * m1) % FFT_N1) / FFT_N1
    c, s = np.cos(phi), np.sin(phi)
    stage_b = np.block([[c, s], [-s, c]])
    stage_b_inv = np.block([[c, -s], [s, c]])
    theta = 2.0 * np.pi * np.arange(FFT_N1) / FFT_N1
    mirror = np.broadcast_to(np.stack([np.cos(theta), np.sin(theta)])[:, :, None], (2, FFT_N1, FFT_LANES))
    return (*(jnp.asarray(t, BF16) for t in (stage_a, stage_a_inv, stage_b, stage_b_inv)),
            jnp.asarray(mirror, F32))


def _fft_stage_a(y_ref, ma_ref, a_ref):
    half = FFT_N2 // 2

    def body(n1, carry):
        slab = y_ref[pl.ds(n1, half, stride=FFT_N1), :]
        a_ref[pl.ds(pl.multiple_of(n1 * FFT_A_PITCH, 8), 2 * FFT_N2), :] = _dot(ma_ref[n1], slab.astype(BF16))
        return carry

    lax.fori_loop(0, FFT_N1, body, 0, unroll=FFT_UNROLL)


FFT_K2_USED = FFT_N2 // 2 + FFT_GROUP


def _fft_stage_b(a_ref, mb_ref, consume):
    def body(j, carry):
        k2 = j * FFT_GROUP
        cols = []
        for g in range(FFT_GROUP):
            re = a_ref[pl.ds(k2 + g, FFT_N1, stride=FFT_A_PITCH), :]
            im = a_ref[pl.ds(FFT_N2 + k2 + g, FFT_N1, stride=FFT_A_PITCH), :]
            cols.append(jnp.concatenate([re, im], axis=0))
        x = _dot(mb_ref[...], jnp.concatenate(cols, axis=1).astype(BF16))
        for g in range(FFT_GROUP):
            consume(k2 + g, x[:, g * FFT_LANES:(g + 1) * FFT_LANES])
        return carry

    lax.fori_loop(0, FFT_K2_USED // FFT_GROUP, body, 0, unroll=FFT_K2_USED // FFT_GROUP)


def _fft_inverse(z_ref, mbi_ref, mai_ref, tw_ref, u_ref, out_ref):
    half = FFT_N2 // 2

    def store_u(k2, u):
        u_ref[pl.ds(pl.multiple_of(k2 * FFT_U_PITCH, 8), 2 * FFT_N1), :] = u

    def stage_b(j, mirror):
        k2 = j * FFT_GROUP
        rhs = jnp.concatenate([z_ref[k2 + g] for g in range(FFT_GROUP)], axis=1)
        u = _dot(mbi_ref[...], rhs)
        for g in range(FFT_GROUP):
            ug = u[:, g * FFT_LANES:(g + 1) * FFT_LANES]
            store_u(k2 + g, ug)
            if mirror[g]:
                ur, ui = ug[:FFT_N1], ug[FFT_N1:]
                c, s = tw_ref[0], tw_ref[1]
                store_u(FFT_N2 - (k2 + g), jnp.concatenate([c * ur - s * ui, -(c * ui + s * ur)], axis=0))

    assert FFT_GROUP == 2
    stage_b(0, (False, True))

    def mirrored(j, carry):
        stage_b(j, (True, True))
        return carry

    lax.fori_loop(1, half // FFT_GROUP, mirrored, 0, unroll=half // FFT_GROUP - 1)
    store_u(half, _dot(mbi_ref[...], z_ref[half]))

    def stage_a(n1, carry):
        re = u_ref[pl.ds(n1, FFT_N2, stride=FFT_U_PITCH), :]
        im = u_ref[pl.ds(FFT_N1 + n1, FFT_N2, stride=FFT_U_PITCH), :]
        rhs = jnp.concatenate([re, im], axis=0).astype(BF16)
        out_ref[pl.ds(n1, half, stride=FFT_N1), :] = _dot(mai_ref[n1], rhs)
        return carry

    lax.fori_loop(0, FFT_N1, stage_a, 0, unroll=FFT_UNROLL)


FFT_WORK_ROWS = max(FFT_N1 * FFT_A_PITCH, FFT_N2 * FFT_U_PITCH)


def _hyena_spectrum_kernel(hf0_ref, hf1_ref, hb0_ref, hb1_ref, ma_ref, mb_ref, k_ref, y_ref, a_ref):
    for o, (hf_ref, hb_ref) in enumerate(((hf0_ref, hb0_ref), (hf1_ref, hb1_ref))):
        hs, hd, inv_norm = _filter_halves(hf_ref, hb_ref)

        def keep_real(k2, x):
            k_ref[o, k2, 0:FFT_N1, :] = (x[:FFT_N1] * inv_norm).astype(k_ref.dtype)

        def keep_imag(k2, x):
            k_ref[o, k2, FFT_N1:, :] = (x[FFT_N1:] * inv_norm).astype(k_ref.dtype)

        for part, keep in ((hs, keep_real), (hd, keep_imag)):
            y_ref[...] = part
            _fft_stage_a(y_ref, ma_ref, a_ref)
            _fft_stage_b(a_ref, mb_ref, keep)


def _hyena_two_stage_kernel(zv_ref, z1_ref, z2_ref, wv_ref, w1_ref, w2_ref, k_ref, skip_ref,
                            ma_ref, mai_ref, mb_ref, mbi_ref, tw_ref, o_ref, y_ref, c_ref, a_ref, z_ref):
    y_ref[...] = _short_conv(zv_ref[0], wv_ref[...])
    for o, (g_ref, w_ref) in enumerate(((z1_ref, w1_ref), (z2_ref, w2_ref))):
        def multiply(k2, x):
            k = k_ref[o, k2].astype(F32)
            xr, xi = x[:FFT_N1], x[FFT_N1:]
            kr, ki = k[:FFT_N1], k[FFT_N1:]
            z_ref[k2] = jnp.concatenate([xr * kr - xi * ki, xr * ki + xi * kr], axis=0).astype(z_ref.dtype)

        _fft_stage_a(y_ref, ma_ref, a_ref)
        _fft_stage_b(a_ref, mb_ref, multiply)
        _fft_inverse(z_ref, mbi_ref, mai_ref, tw_ref, a_ref, c_ref)
        y = _short_conv(g_ref[0], w_ref[...]) * (c_ref[...] + y_ref[...] * skip_ref[o:o + 1])
        if o + 1 < HY_ORDER:
            y_ref[...] = y
        else:
            o_ref[0] = y.astype(o_ref.dtype)


def _hyena_two_stage(z, w_short, hfilt, skip, first=0, bsz=None):
    _, length, d3 = z.shape
    bsz = z.shape[0] if bsz is None else bsz
    d = d3 // 3
    dt = FFT_LANES
    nct = d // dt
    assert 2 * length == FFT_N1 * FFT_N2
    ma, mai, mb, mbi, mirror = _two_stage_tables()
    spec_shape = (HY_ORDER, FFT_K2_USED, 2 * FFT_N1)
    once = pl.Buffered(1)
    hspec = lambda direction, order: pl.BlockSpec(
        (length, dt), lambda c: (0, (direction * HY_ORDER + order) * nct + c))
    spectrum = pl.pallas_call(
        _hyena_spectrum_kernel,
        grid=(nct,),
        in_specs=[hspec(0, 0), hspec(0, 1), hspec(1, 0), hspec(1, 1),
                  pl.BlockSpec(ma.shape, lambda c: (0, 0, 0), pipeline_mode=once),
                  pl.BlockSpec(mb.shape, lambda c: (0, 0), pipeline_mode=once)],
        out_specs=pl.BlockSpec(spec_shape + (dt,), lambda c: (0, 0, 0, c)),
        out_shape=jax.ShapeDtypeStruct(spec_shape + (d,), BF16),
        scratch_shapes=[pltpu.VMEM((length, dt), F32), pltpu.VMEM((FFT_WORK_ROWS, dt), F32)],
        compiler_params=_cparams("parallel"),
        name="hyena_filter_spectrum",
    )(hfilt, hfilt, hfilt, hfilt, ma, mb)

    zspec = lambda part: pl.BlockSpec((1, length, dt), lambda c, b: (b + first, 0, part * nct + c))
    wspec = lambda part: pl.BlockSpec((3, dt), lambda c, b: (0, part * nct + c))
    fixed3 = lambda c, b: (0, 0, 0)
    fixed2 = lambda c, b: (0, 0)
    return pl.pallas_call(
        _hyena_two_stage_kernel,
        grid=(nct, bsz),
        in_specs=[zspec(0), zspec(1), zspec(2), wspec(0), wspec(1), wspec(2),
                  pl.BlockSpec(spec_shape + (dt,), lambda c, b: (0, 0, 0, c), pipeline_mode=once),
                  pl.BlockSpec((HY_ORDER, dt), lambda c, b: (0, c)),
                  pl.BlockSpec(ma.shape, fixed3, pipeline_mode=once),
                  pl.BlockSpec(mai.shape, fixed3, pipeline_mode=once),
                  pl.BlockSpec(mb.shape, fixed2, pipeline_mode=once),
                  pl.BlockSpec(mbi.shape, fixed2, pipeline_mode=once),
                  pl.BlockSpec(mirror.shape, fixed3, pipeline_mode=once)],
        out_specs=pl.BlockSpec((1, length, dt), lambda c, b: (b, 0, c)),
        out_shape=jax.ShapeDtypeStruct((bsz, length, d), BF16),
        scratch_shapes=[pltpu.VMEM((length, dt), F32), pltpu.VMEM((length, dt), F32),
                        pltpu.VMEM((FFT_WORK_ROWS, dt), F32), pltpu.VMEM((FFT_K2_USED, 2 * FFT_N1, dt), BF16)],
        compiler_params=_cparams("parallel", "arbitrary"),
        name="hyena_conv_two_stage",
    )(z, z, z, w_short, w_short, w_short, spectrum, skip, ma, mai, mb, mbi, mirror)


N_HEADS = 16
HEAD_DIM = D_MODEL // N_HEADS
HEADS_PER_STEP = 2
CTX_HEADS_PER_STEP = 16
GRID_W = 64
WIN_ROWS = 8
WIN_COLS = 16
NEG_INF = -1e30
NAT_Q_ROWS = 4
NAT_K_ROWS = NAT_Q_ROWS + WIN_ROWS - 1
ATTN_SCALE = HEAD_DIM ** -0.5
assert math.frexp(ATTN_SCALE)[0] == 0.5, "the scale is folded into q, which is exact only for a power of two"


def _dot_nt(a, b):
    return lax.dot_general(a, b, (((1,), (1,)), ((), ())), preferred_element_type=F32)


def _qkv_kernel(x_ref, g_ref, sc_ref, sh_ref, w_ref, q_ref, k_ref, v_ref):
    h = _norm_mod(x_ref[...], g_ref[...], sc_ref[0], sh_ref[0])
    qkv = _dot(h.astype(BF16), w_ref[...])
    for part, ref in enumerate((q_ref, k_ref, v_ref)):
        scale = ATTN_SCALE if part == 0 else 1.0
        for head in range(N_HEADS):
            lo = part * D_MODEL + head * HEAD_DIM
            ref[0, head] = (qkv[:, lo:lo + HEAD_DIM] * scale).astype(ref.dtype)


def _qkv_proj(x, g, mods, w_bf16, seg, first_row, n_seq, seq_len, kv_dtype):
    d = x.shape[1]
    n_ctx, smp_len = seg
    t = min(ROW_TILE, seq_len)
    assert seq_len % t == 0 and first_row % t == 0
    first_tile = first_row // t
    per_seq = seq_len // t
    out_spec = pl.BlockSpec((1, N_HEADS, t, HEAD_DIM), lambda i: (i // per_seq, 0, i % per_seq, 0))
    shape = (n_seq, N_HEADS, seq_len, HEAD_DIM)
    return pl.pallas_call(
        _qkv_kernel,
        grid=(n_seq * per_seq,),
        in_specs=[
            pl.BlockSpec((t, d), lambda i: (i + first_tile, 0)),
            pl.BlockSpec((1, d), lambda i: (0, 0)),
            _mod_spec(1, t, n_ctx, smp_len, first_tile),
            _mod_spec(0, t, n_ctx, smp_len, first_tile),
            pl.BlockSpec(w_bf16.shape, lambda i: (0, 0)),
        ],
        out_specs=[out_spec, out_spec, out_spec],
        out_shape=[jax.ShapeDtypeStruct(shape, BF16), jax.ShapeDtypeStruct(shape, kv_dtype),
                   jax.ShapeDtypeStruct(shape, kv_dtype)],
        compiler_params=_cparams("parallel"),
        name="qkv_proj",
    )(x, g.reshape(1, d), mods, mods, w_bf16)


def _ctx_attn_kernel(q_ref, k_ref, v_ref, o_ref):
    for j in range(q_ref.shape[1]):
        q = q_ref[0, j]
        k = k_ref[0, j].astype(BF16)
        v = v_ref[0, j].astype(BF16)
        s = _dot_nt(q, k)
        p = jnp.exp(s - jnp.max(s, axis=-1, keepdims=True))
        o = _dot(p.astype(BF16), v) / jnp.sum(p, axis=-1, keepdims=True)
        o_ref[:, j * HEAD_DIM:(j + 1) * HEAD_DIM] = o.astype(o_ref.dtype)


def _ctx_attention(q, k, v):
    bsz, _, s, _ = q.shape
    hp = CTX_HEADS_PER_STEP
    spec = pl.BlockSpec((1, hp, s, HEAD_DIM), lambda b, h: (b, h, 0, 0))
    return pl.pallas_call(
        _ctx_attn_kernel,
        grid=(bsz, N_HEADS // hp),
        in_specs=[spec, spec, spec],
        out_specs=pl.BlockSpec((s, hp * HEAD_DIM), lambda b, h: (b, h)),
        out_shape=jax.ShapeDtypeStruct((bsz * s, D_MODEL), BF16),
        compiler_params=_cparams("parallel", "parallel"),
        name="ctx_attention",
    )(q, k, v)


def _rpb_toeplitz_kernel(r_ref, e_ref, o_ref):
    o_ref[...] = _dot3(r_ref[...], e_ref[...])


def _nat_bias(rpb):
    n_heads, n_r, n_c = rpb.shape
    n_cp = 32
    qc = np.arange(GRID_W)[:, None]
    kc = np.arange(GRID_W)[None, :]
    onehot = (np.clip(kc - qc + WIN_COLS - 1, 0, n_c - 1)[None] == np.arange(n_cp)[:, None, None])
    onehot = jnp.asarray(onehot.reshape(n_cp, GRID_W * GRID_W), F32)
    rows = jnp.pad(rpb.reshape(n_heads * n_r, n_c), ((0, 0), (0, n_cp - n_c)))
    toep = pl.pallas_call(
        _rpb_toeplitz_kernel,
        out_shape=jax.ShapeDtypeStruct((n_heads * n_r, GRID_W * GRID_W), F32),
        compiler_params=_cparams(),
        name="nat_bias_toeplitz",
    )(rows, onehot).reshape(n_heads, n_r, GRID_W, GRID_W)
    q_start = np.clip(qc - WIN_COLS // 2, 0, GRID_W - WIN_COLS)
    col_ok = jnp.asarray((kc >= q_start) & (kc < q_start + WIN_COLS))
    toep = jnp.where(col_ok, toep, NEG_INF)
    toep = jnp.concatenate([toep, jnp.full((n_heads, 1, GRID_W, GRID_W), NEG_INF, F32)], axis=1)
    rows_total = GRID_W
    idx = np.full((3, NAT_Q_ROWS, NAT_K_ROWS), n_r, np.int32)
    for case, r0 in enumerate((0, NAT_Q_ROWS, rows_total - NAT_Q_ROWS)):
        ks = int(np.clip(r0 - WIN_ROWS // 2, 0, rows_total - NAT_K_ROWS))
        for dr in range(NAT_Q_ROWS):
            r = r0 + dr
            rs = int(np.clip(r - WIN_ROWS // 2, 0, rows_total - WIN_ROWS))
            for dk in range(NAT_K_ROWS):
                kr = ks + dk
                if rs <= kr < rs + WIN_ROWS:
                    idx[case, dr, dk] = kr - r + WIN_ROWS - 1
    def assemble(t_ref, o_ref):
        for case in range(3):
            for dr in range(NAT_Q_ROWS):
                for dk in range(NAT_K_ROWS):
                    o_ref[0, case, dr * GRID_W:(dr + 1) * GRID_W, dk * GRID_W:(dk + 1) * GRID_W] = (
                        t_ref[0, int(idx[case, dr, dk])])

    return pl.pallas_call(
        assemble,
        grid=(n_heads,),
        in_specs=[pl.BlockSpec((1, n_r + 1, GRID_W, GRID_W), lambda h: (h, 0, 0, 0))],
        out_specs=pl.BlockSpec((1, 3, NAT_Q_ROWS * GRID_W, NAT_K_ROWS * GRID_W), lambda h: (h, 0, 0, 0)),
        out_shape=jax.ShapeDtypeStruct((n_heads, 3, NAT_Q_ROWS * GRID_W, NAT_K_ROWS * GRID_W), F32),
        compiler_params=_cparams("parallel"),
        name="nat_bias_assemble",
    )(toep)


def _nat_kernel(q_ref, k_ref, v_ref, kc_ref, vc_ref, bias_ref, o_ref):
    n_blocks = q_ref.shape[2] // (NAT_Q_ROWS * GRID_W)
    rows_total = q_ref.shape[2] // GRID_W
    nq = NAT_Q_ROWS * GRID_W
    nk = NAT_K_ROWS * GRID_W
    ctx = [(kc_ref[0, 0, j].astype(BF16), vc_ref[0, 0, j].astype(BF16)) for j in range(HEADS_PER_STEP)]

    def block(blk, carry):
        ks = jnp.clip(blk * NAT_Q_ROWS - WIN_ROWS // 2, 0, rows_total - NAT_K_ROWS)
        case = jnp.where(blk == 0, 0, jnp.where(blk == n_blocks - 1, 2, 1))
        q_rows = pl.ds(pl.multiple_of(blk * nq, nq), nq)
        k_rows = pl.ds(pl.multiple_of(ks * GRID_W, GRID_W), nk)
        for j, (k_ctx, v_ctx) in enumerate(ctx):
            q = q_ref[0, j, q_rows, :]
            n_past = k_ctx.shape[0]
            s = _dot_nt(q, jnp.concatenate([k_ctx, k_ref[0, j, k_rows, :]], axis=0))
            s = jnp.concatenate([s[:, :n_past], s[:, n_past:] + bias_ref[j, case]], axis=1)
            p = jnp.exp(s - jnp.max(s, axis=-1, keepdims=True))
            v_all = jnp.concatenate([v_ctx, v_ref[0, j, k_rows, :]], axis=0)
            o = _dot(p.astype(BF16), v_all) / jnp.sum(p, axis=-1, keepdims=True)
            o_ref[q_rows, j * HEAD_DIM:(j + 1) * HEAD_DIM] = o.astype(o_ref.dtype)
        return carry

    lax.fori_loop(0, n_blocks, block, 0, unroll=4)


def _nat_attention(q, k, v, cache_k, cache_v, bias):
    bsz, _, length, _ = q.shape
    hp = HEADS_PER_STEP
    past = cache_k.shape[3]
    spec = pl.BlockSpec((1, hp, length, HEAD_DIM), lambda b, h: (b, h, 0, 0))
    cspec = pl.BlockSpec((1, 1, hp, past, HEAD_DIM), lambda b, h: (b, 0, h, 0, 0))
    return pl.pallas_call(
        _nat_kernel,
        grid=(bsz, N_HEADS // hp),
        in_specs=[spec, spec, spec, cspec, cspec,
                  pl.BlockSpec((hp,) + bias.shape[1:], lambda b, h: (h, 0, 0, 0))],
        out_specs=pl.BlockSpec((length, hp * HEAD_DIM), lambda b, h: (b, h)),
        out_shape=jax.ShapeDtypeStruct((bsz * length, D_MODEL), BF16),
        compiler_params=_cparams("parallel", "parallel"),
        name="nat_attention",
    )(q, k, v, cache_k, cache_v, bias)


def kernel(x_prompt, x_sample, cache_k, cache_v, c, c_ctx, ada_w, ada_b, norm1_g, norm2_g, normf_g, hy_w_in, hy_w_short, hy_f_w1, hy_f_b1, hy_f_freq, hy_f_w2, hy_f_b2, hy_f_w3, hy_skip, hy_w_out, na_w_qkv, na_rpb, na_w_o, moe_w_router, moe_b_router, moe_w_gate, moe_b_gate, moe_w_up, moe_b_up, moe_w_down, moe_b_down):
    b, s, d = x_prompt.shape
    bd, sd, _ = x_sample.shape
    n_ctx = b * s
    n_tok = n_ctx + bd * sd
    assert n_ctx % sd == 0 and n_ctx % ROW_TILE == 0 and sd % ROW_TILE == 0
    seg = (n_ctx, sd)
    x_ctx = x_prompt.reshape(n_ctx, d)
    x_smp = x_sample.reshape(bd * sd, d)
    cond = jnp.zeros((N_COND, d), F32).at[0].set(c_ctx).at[1:1 + bd].set(c)
    mods = _modulation(cond, ada_w, ada_b)

    def moe(x, i, final_norm):
        return _moe_layer(x, norm2_g[i], mods[i], moe_w_router[i], moe_b_router[i], i, moe_w_gate, moe_b_gate,
                          moe_w_up, moe_b_up, moe_w_down, moe_b_down, normf_g, seg, final_norm)

    z = _norm_proj(x_ctx, x_smp, norm1_g[0], mods[0], hy_w_in[0].astype(BF16), seg)
    fargs = (hy_f_w1[0], hy_f_b1[0], hy_f_freq[0], hy_f_w2[0], hy_f_b2[0], hy_f_w3[0])
    y_ctx = _hyena_direct(z.reshape(n_tok // s, s, 3 * d), hy_w_short[0], _hyena_filters(s, *fargs),
                          hy_skip[0], first=0, bsz=b)
    y_smp = _hyena_two_stage(z.reshape(n_tok // sd, sd, 3 * d), hy_w_short[0], _hyena_filters(sd, *fargs),
                             hy_skip[0], first=n_ctx // sd, bsz=bd)
    x = _out_proj(y_ctx.reshape(n_ctx, d), y_smp.reshape(bd * sd, d), hy_w_out[0].astype(BF16), x_ctx, x_smp,
                  mods[0], seg)
    x = moe(x, 0, False)

    w_qkv = na_w_qkv[0].astype(BF16)
    q_c, k_c, v_c = _qkv_proj(x, norm1_g[1], mods[1], w_qkv, seg, 0, b, s, F32)
    q_s, k_s, v_s = _qkv_proj(x, norm1_g[1], mods[1], w_qkv, seg, n_ctx, bd, sd, BF16)
    o_ctx = _ctx_attention(q_c, k_c, v_c)
    o_smp = _nat_attention(q_s, k_s, v_s, cache_k, cache_v, _nat_bias(na_rpb[0]))
    x = _out_proj(o_ctx, o_smp, na_w_o[0].astype(BF16), x, None, mods[1], seg)
    y_prompt, y_sample = moe(x, 1, True)

    nh, hd = k_c.shape[1], k_c.shape[3]
    return (y_prompt.reshape(b, s, d), y_sample.reshape(bd, sd, d),
            k_c.reshape(b, 1, nh, s, hd), v_c.reshape(b, 1, nh, s, hd))
```
